```python
import math
import jax, jax.numpy as jnp
from jax import lax
import numpy as np

D_MODEL = 2048
BATCH = 2
SEQ = 8192
DEPTH = 1

D_MIX = D_MODEL
D_POOL = D_MIX // 2
D_SSM = D_MIX - D_POOL
POOL_WINDOWS = (2, 4, 8, 16)
N_POOL_GROUPS = len(POOL_WINDOWS)
POOL_GROUP = D_POOL // N_POOL_GROUPS
SSM_GROUP = 16
N_SSM_GROUPS = D_SSM // SSM_GROUP
SSM_STATE = 64
DT_MIN = 1e-3
DT_MAX = 1e-1
N_EXPERT_GROUPS = 4
EXPERTS_PER_GROUP = 8
N_EXPERTS = N_EXPERT_GROUPS * EXPERTS_PER_GROUP
TOP_K_INNER = 2
D_EXPERT = D_MODEL // 4
MOE_BLOCK = 128
LN_EPS = 1e-5

kernel_name = "hybrid_pool_s5_hmoe_deepnorm_encoder"


def layer_norm(h, g, b):
    hf = h.astype(jnp.float32)
    mu = jnp.mean(hf, axis=-1, keepdims=True)
    var = jnp.mean(jnp.square(hf - mu), axis=-1, keepdims=True)
    out = (hf - mu) * lax.rsqrt(var + LN_EPS) * g.astype(jnp.float32) + b.astype(jnp.float32)
    return out.astype(h.dtype)


def pool_mixer(u, pool_w, pool_scale):
    bsz, s, _ = u.shape
    uf = u.astype(jnp.float32)
    cs = jnp.concatenate([jnp.zeros((bsz, 1, D_POOL), jnp.float32), jnp.cumsum(uf, axis=1)], axis=1)
    pos = jnp.arange(s, dtype=jnp.int32)
    diffs = []
    for g, w in enumerate(POOL_WINDOWS):
        sl = slice(g * POOL_GROUP, (g + 1) * POOL_GROUP)
        lo = jnp.clip(pos - w // 2, 0, s)
        hi = jnp.clip(pos - w // 2 + w, 0, s)
        csg = cs[..., sl]
        cnt = (hi - lo).astype(jnp.float32)[None, :, None]
        mean = (csg[:, hi] - csg[:, lo]) / cnt
        diffs.append(mean - uf[..., sl])
    d = jnp.stack(diffs, axis=2).astype(u.dtype)
    y = jnp.einsum('bsgc,gcd->bsgd', d, pool_w).reshape(bsz, s, D_POOL)
    return y * pool_scale


def _complex_affine_combine(left, right):
    a1r, a1i, b1r, b1i = left
    a2r, a2i, b2r, b2i = right
    return (a1r * a2r - a1i * a2i,
            a1r * a2i + a1i * a2r,
            a2r * b1r - a2i * b1i + b2r,
            a2r * b1i + a2i * b1r + b2i)


def s5_bidirectional(u, a_re, a_im, log_dt, b_re, b_im, c_re, c_im, d_skip):
    bsz, s, _ = u.shape
    uf = u.astype(jnp.float32).reshape(bsz, s, N_SSM_GROUPS, SSM_GROUP).transpose(1, 0, 2, 3)
    y = d_skip.astype(jnp.float32).reshape(N_SSM_GROUPS, SSM_GROUP) * uf
    for dr in range(2):
        lr = a_re[dr].astype(jnp.float32)
        li = a_im[dr].astype(jnp.float32)
        dt = jnp.exp(log_dt[dr].astype(jnp.float32))[:, None]
        mag = jnp.exp(lr * dt)
        abar_r = mag * jnp.cos(li * dt)
        abar_i = mag * jnp.sin(li * dt)
        den = lr * lr + li * li
        zr = ((abar_r - 1.0) * lr + abar_i * li) / den
        zi = (abar_i * lr - (abar_r - 1.0) * li) / den
        br = b_re[dr].astype(jnp.float32)
        bi = b_im[dr].astype(jnp.float32)
        bbar_r = zr[..., None] * br - zi[..., None] * bi
        bbar_i = zr[..., None] * bi + zi[..., None] * br
        bu_r = jnp.einsum('sbgc,gpc->sbgp', uf, bbar_r)
        bu_i = jnp.einsum('sbgc,gpc->sbgp', uf, bbar_i)
        ar = jnp.broadcast_to(abar_r[None, None], (s, 1, N_SSM_GROUPS, SSM_STATE))
        ai = jnp.broadcast_to(abar_i[None, None], (s, 1, N_SSM_GROUPS, SSM_STATE))
        _, _, xr, xi = lax.associative_scan(_complex_affine_combine, (ar, ai, bu_r, bu_i),
                                            reverse=(dr == 1), axis=0)
        y = y + jnp.einsum('sbgp,gcp->sbgc', xr, c_re[dr].astype(jnp.float32)) \
              - jnp.einsum('sbgp,gcp->sbgc', xi, c_im[dr].astype(jnp.float32))
    return y.transpose(1, 0, 2, 3).reshape(bsz, s, D_SSM)


def hier_moe(h, router_g_w, router_g_b, router_e_w, router_e_b, w_gate, w_up, w_down):
    bsz, s, d = h.shape
    t = bsz * s
    xt = h.reshape(t, d)
    gl = (xt @ router_g_w).astype(jnp.float32) + router_g_b.astype(jnp.float32)
    gp = jax.nn.softmax(gl, axis=-1)
    grp = jnp.argmax(gl, axis=-1).astype(jnp.int32)
    p_grp = jnp.take_along_axis(gp, grp[:, None], axis=-1)
    el = (xt @ router_e_w).astype(jnp.float32) + router_e_b.astype(jnp.float32)
    el = el.reshape(t, N_EXPERT_GROUPS, EXPERTS_PER_GROUP)
    el_sel = jnp.take_along_axis(el, grp[:, None, None], axis=1)[:, 0]
    ep = jax.nn.softmax(el_sel, axis=-1)
    top_p, top_i = lax.top_k(ep, TOP_K_INNER)
    gate = p_grp * top_p / jnp.sum(top_p, axis=-1, keepdims=True)
    expert_id = grp[:, None] * EXPERTS_PER_GROUP + top_i.astype(jnp.int32)

    m = t * TOP_K_INNER
    e_flat = expert_id.reshape(m)
    tok = jnp.repeat(jnp.arange(t, dtype=jnp.int32), TOP_K_INNER)
    wt = gate.reshape(m)
    order = jnp.argsort(e_flat)
    e_sorted = e_flat[order]
    counts = jnp.bincount(e_flat, length=N_EXPERTS).astype(jnp.int32)
    starts = jnp.cumsum(counts) - counts
    pcounts = (counts + MOE_BLOCK - 1) // MOE_BLOCK * MOE_BLOCK
    pends = jnp.cumsum(pcounts)
    pstarts = pends - pcounts
    rank = jnp.arange(m, dtype=jnp.int32) - starts[e_sorted]
    dest = pstarts[e_sorted] + rank
    n_blocks = (m + MOE_BLOCK - 1) // MOE_BLOCK + N_EXPERTS
    n_slots = n_blocks * MOE_BLOCK
    slot_tok = jnp.full((n_slots,), t, jnp.int32).at[dest].set(tok[order])
    slot_w = jnp.zeros((n_slots,), jnp.float32).at[dest].set(wt[order])
    block_e = jnp.minimum(jnp.searchsorted(pends, jnp.arange(n_blocks, dtype=jnp.int32) * MOE_BLOCK,
                                           side='right'), N_EXPERTS - 1).astype(jnp.int32)
    x_pad = jnp.concatenate([xt, jnp.zeros((1, d), xt.dtype)], axis=0)
    xb = x_pad[slot_tok].reshape(n_blocks, MOE_BLOCK, d)

    def expert_block(args):
        xb_i, e = args
        hh = jax.nn.silu(xb_i @ w_gate[e]) * (xb_i @ w_up[e])
        return hh @ w_down[e]

    yb = lax.map(expert_block, (xb, block_e))
    yb = yb.reshape(n_slots, d).astype(jnp.float32) * slot_w[:, None]
    out = jnp.zeros((t + 1, d), jnp.float32).at[slot_tok].add(yb)[:t]
    return out.reshape(bsz, s, d).astype(h.dtype)


def setup_inputs(seed: int = 0) -> dict:
    key = jax.random.key(seed)
    ks = jax.random.split(key, 26)
    L = DEPTH
    f32 = jnp.float32
    beta = (8.0 * DEPTH) ** -0.25
    nrm = lambda k, shp, sc: jax.random.normal(k, shp, f32) * sc
    a_im_base = math.pi * jnp.arange(SSM_STATE, dtype=f32)
    return {
        "x": nrm(ks[0], (BATCH, SEQ, D_MODEL), 1.0),
        "w_in": nrm(ks[1], (L, D_MODEL, D_MIX), D_MODEL ** -0.5),
        "pool_w": nrm(ks[2], (L, N_POOL_GROUPS, POOL_GROUP, POOL_GROUP), POOL_GROUP ** -0.5),
        "pool_scale": 1.0 + nrm(ks[3], (L, D_POOL), 0.02),
        "ssm_a_re": -0.5 + nrm(ks[4], (L, 2, N_SSM_GROUPS, SSM_STATE), 0.01),
        "ssm_a_im": a_im_base + nrm(ks[5], (L, 2, N_SSM_GROUPS, SSM_STATE), 0.01),
        "ssm_log_dt": jax.random.uniform(ks[6], (L, 2, N_SSM_GROUPS), f32, math.log(DT_MIN), math.log(DT_MAX)),
        "ssm_b_re": nrm(ks[7], (L, 2, N_SSM_GROUPS, SSM_STATE, SSM_GROUP), (2 * SSM_GROUP) ** -0.5),
        "ssm_b_im": nrm(ks[8], (L, 2, N_SSM_GROUPS, SSM_STATE, SSM_GROUP), (2 * SSM_GROUP) ** -0.5),
        "ssm_c_re": nrm(ks[9], (L, 2, N_SSM_GROUPS, SSM_GROUP, SSM_STATE), SSM_STATE ** -0.5),
        "ssm_c_im": nrm(ks[10], (L, 2, N_SSM_GROUPS, SSM_GROUP, SSM_STATE), SSM_STATE ** -0.5),
        "ssm_d": nrm(ks[11], (L, D_SSM), 1.0),
        "glu_w": nrm(ks[12], (L, D_SSM, D_SSM), D_SSM ** -0.5),
        "glu_b": nrm(ks[13], (L, D_SSM), 0.01),
        "w_out": nrm(ks[14], (L, D_MIX, D_MODEL), beta * D_MIX ** -0.5),
        "ln1_g": 1.0 + nrm(ks[15], (L, D_MODEL), 0.02),
        "ln1_b": nrm(ks[16], (L, D_MODEL), 0.02),
        "router_g_w": nrm(ks[17], (L, D_MODEL, N_EXPERT_GROUPS), D_MODEL ** -0.5),
        "router_g_b": nrm(ks[18], (L, N_EXPERT_GROUPS), 0.01),
        "router_e_w": nrm(ks[19], (L, D_MODEL, N_EXPERTS), D_MODEL ** -0.5),
        "router_e_b": nrm(ks[20], (L, N_EXPERTS), 0.01),
        "w_gate": nrm(ks[21], (L, N_EXPERTS, D_MODEL, D_EXPERT), D_MODEL ** -0.5),
        "w_up": nrm(ks[22], (L, N_EXPERTS, D_MODEL, D_EXPERT), D_MODEL ** -0.5),
        "w_down": nrm(ks[23], (L, N_EXPERTS, D_EXPERT, D_MODEL), beta * D_EXPERT ** -0.5),
        "ln2_g": 1.0 + nrm(ks[24], (L, D_MODEL), 0.02),
        "ln2_b": nrm(ks[25], (L, D_MODEL), 0.02),
    }


def reference(x, w_in, pool_w, pool_scale, ssm_a_re, ssm_a_im, ssm_log_dt, ssm_b_re, ssm_b_im,
              ssm_c_re, ssm_c_im, ssm_d, glu_w, glu_b, w_out, ln1_g, ln1_b, router_g_w, router_g_b,
              router_e_w, router_e_b, w_gate, w_up, w_down, ln2_g, ln2_b):
    alpha = (2.0 * DEPTH) ** 0.25
    h = x
    for l in range(DEPTH):
        proj = h @ w_in[l]
        y_pool = pool_mixer(proj[..., :D_POOL], pool_w[l], pool_scale[l])
        y_ssm = s5_bidirectional(proj[..., D_POOL:], ssm_a_re[l], ssm_a_im[l], ssm_log_dt[l],
                                 ssm_b_re[l], ssm_b_im[l], ssm_c_re[l], ssm_c_im[l], ssm_d[l])
        y_ssm = jax.nn.gelu(y_ssm)
        y_ssm = (y_ssm * jax.nn.sigmoid(y_ssm @ glu_w[l].astype(jnp.float32)
                                        + glu_b[l].astype(jnp.float32))).astype(h.dtype)
        mix = jnp.concatenate([y_pool.astype(h.dtype), y_ssm], axis=-1) @ w_out[l]
        h = layer_norm(alpha * h + mix, ln1_g[l], ln1_b[l])
        ffn = hier_moe(h, router_g_w[l], router_g_b[l], router_e_w[l], router_e_b[l],
                       w_gate[l], w_up[l], w_down[l])
        h = layer_norm(alpha * h + ffn, ln2_g[l], ln2_b[l])
    return h
```

```python
import functools
import math

import jax
import jax.numpy as jnp
from jax import lax
from jax.experimental import pallas as pl
from jax.experimental.pallas import tpu as pltpu

F32 = jnp.float32
BF16 = jnp.bfloat16
I32 = jnp.int32

POOL_WINDOWS = (2, 4, 8, 16)
POOL_GROUP = 256
SSM_GROUP = 16
SSM_STATE = 64
N_EXPERT_GROUPS = 4
EXPERTS_PER_GROUP = 8
N_EXPERTS = N_EXPERT_GROUPS * EXPERTS_PER_GROUP
LN_EPS = 1e-5

CHUNK = 16
MOE_BLK = 256
ROUTER_ROWS = 8 + N_EXPERTS
ROUTER_LANES = 128
HALO = 16
VMEM_LIMIT = 56 * 1024 * 1024


def _cparams(sem, vmem=VMEM_LIMIT):
    return pltpu.CompilerParams(dimension_semantics=sem, vmem_limit_bytes=vmem)


def _proj_kernel(x_ref, wp_ref, wst_ref, pool_ref, ut_ref):
    xb = x_ref[...].astype(BF16)
    pool_ref[...] = jnp.dot(xb, wp_ref[...], preferred_element_type=F32).astype(BF16)
    ut = lax.dot_general(wst_ref[...], xb, (((1,), (1,)), ((), ())),
                         preferred_element_type=F32)
    ut_ref[...] = ut.astype(BF16).reshape(ut_ref.shape)


def _proj(x2, w_pool, w_ssm_t, n_groups):
    nrow, ld = x2.shape
    d = w_pool.shape[0]
    dp = w_pool.shape[1]
    ds = w_ssm_t.shape[0]
    L = ld // d
    r = min(256, nrow)
    return pl.pallas_call(
        _proj_kernel,
        out_shape=(jax.ShapeDtypeStruct((nrow, L * dp), BF16),
                   jax.ShapeDtypeStruct((n_groups, L, SSM_GROUP, nrow), BF16)),
        grid=(nrow // r, L),
        in_specs=[pl.BlockSpec((r, d), lambda i, s: (i, s)),
                  pl.BlockSpec((d, dp), lambda i, s: (0, 0)),
                  pl.BlockSpec((ds, d), lambda i, s: (0, 0))],
        out_specs=(pl.BlockSpec((r, dp), lambda i, s: (i, s)),
                   pl.BlockSpec((n_groups, None, SSM_GROUP, r), lambda i, s: (0, s, 0, i))),
        compiler_params=_cparams(("arbitrary", "arbitrary")),
        name="proj",
    )(x2, w_pool, w_ssm_t)


def _pool_kernel(prev_ref, main_ref, next_ref, pw_ref, sc_ref, out_ref, *, seq, ts):
    i = pl.program_id(0)
    ext = jnp.concatenate([prev_ref[...], main_ref[...], next_ref[...]], axis=0)
    k = ts + 2 * HALO
    row = lax.broadcasted_iota(I32, (ts, k), 0)
    col = lax.broadcasted_iota(I32, (ts, k), 1)
    base = (i * ts) % seq
    off = col - HALO - row
    src = base + row + off
    inside = (src >= 0) & (src < seq)
    pos = base + lax.broadcasted_iota(I32, (ts, 1), 0)
    for g, w in enumerate(POOL_WINDOWS):
        sl = slice(g * POOL_GROUP, (g + 1) * POOL_GROUP)
        band = inside & (off >= -(w // 2)) & (off < w - w // 2)
        bm = jnp.where(band, 1.0, 0.0).astype(BF16)
        sums = jnp.dot(bm, ext[:, sl], preferred_element_type=F32)
        lo = jnp.maximum(pos - w // 2, 0)
        hi = jnp.minimum(pos - w // 2 + w, seq)
        inv = 1.0 / (hi - lo).astype(F32)
        dlt = sums * inv - main_ref[:, sl].astype(F32)
        y = jnp.dot(dlt.astype(BF16), pw_ref[g], preferred_element_type=F32) * sc_ref[:, sl]
        out_ref[:, sl] = y.astype(BF16)


def _pool(pp, pool_w, pool_scale, seq):
    t, dp = pp.shape
    ts = min(256, seq)
    nh = t // HALO
    per = ts // HALO
    return pl.pallas_call(
        functools.partial(_pool_kernel, seq=seq, ts=ts),
        out_shape=jax.ShapeDtypeStruct((t, dp), BF16),
        grid=(t // ts,),
        in_specs=[pl.BlockSpec((HALO, dp), lambda i: (jnp.maximum(i * per - 1, 0), 0)),
                  pl.BlockSpec((ts, dp), lambda i: (i, 0)),
                  pl.BlockSpec((HALO, dp), lambda i: (jnp.minimum((i + 1) * per, nh - 1), 0)),
                  pl.BlockSpec(pool_w.shape, lambda i: (0, 0, 0)),
                  pl.BlockSpec((1, dp), lambda i: (0, 0))],
        out_specs=pl.BlockSpec((ts, dp), lambda i: (i, 0)),
        compiler_params=_cparams(("arbitrary",)),
        name="pool",
    )(pp, pp, pp, pool_w, pool_scale)


def _ssm_tables(a_re, a_im, log_dt, b_re, b_im, c_re, c_im, d_skip, L):
    g = a_re.shape[1]
    p = a_re.shape[2]
    c = b_re.shape[3]
    fl = L * c
    lr = a_re.astype(F32)
    li = a_im.astype(F32)
    dt = jnp.exp(log_dt.astype(F32))[..., None]
    mag = jnp.exp(lr * dt)
    abr = mag * jnp.cos(li * dt)
    abi = mag * jnp.sin(li * dt)
    den = lr * lr + li * li
    zr = ((abr - 1.0) * lr + abi * li) / den
    zi = (abi * lr - (abr - 1.0) * li) / den
    br = b_re.astype(F32)
    bi = b_im.astype(F32)
    bbr = zr[..., None] * br - zi[..., None] * bi
    bbi = zr[..., None] * bi + zi[..., None] * br
    cr = c_re.astype(F32)
    ci = c_im.astype(F32)
    kk = jnp.arange(L + 1, dtype=F32)[None, None, :, None]
    pm = jnp.exp(kk * (lr * dt)[:, :, None, :])
    ang = kk * (li * dt)[:, :, None, :]
    pr = pm * jnp.cos(ang)
    pi = pm * jnp.sin(ang)

    cpr = cr[:, :, None] * pr[:, :, :L, None, :] - ci[:, :, None] * pi[:, :, :L, None, :]
    cpi = cr[:, :, None] * pi[:, :, :L, None, :] + ci[:, :, None] * pr[:, :, :L, None, :]
    kern = jnp.einsum('dgkop,dgpc->dgkoc', cpr, bbr) - jnp.einsum('dgkop,dgpc->dgkoc', cpi, bbi)
    tt = jnp.arange(L)[:, None]
    ss = jnp.arange(L)[None, :]
    kf = kern[0][:, jnp.clip(tt - ss, 0, L - 1)]
    kb = kern[1][:, jnp.clip(ss - tt, 0, L - 1)]
    m5 = jnp.where((tt >= ss)[None, :, :, None, None], kf, 0.0) \
        + jnp.where((tt <= ss)[None, :, :, None, None], kb, 0.0)
    eye_t = (tt == ss)[None, :, :, None, None]
    eye_c = jnp.eye(c, dtype=F32)[None, None, None]
    m5 = m5 + jnp.where(eye_t, eye_c * d_skip.astype(F32).reshape(g, 1, 1, c, 1), 0.0)
    mt = m5.transpose(0, 1, 3, 2, 4).reshape(g, fl, fl)

    def w1(d, pw_idx):
        qr = pr[d][:, pw_idx]
        qi = pi[d][:, pw_idx]
        re = qr[..., None] * bbr[d][:, None] - qi[..., None] * bbi[d][:, None]
        im = qr[..., None] * bbi[d][:, None] + qi[..., None] * bbr[d][:, None]
        to = lambda v: v.transpose(0, 2, 1, 3).reshape(g, p, fl)
        return to(re), to(im)
    sidx = jnp.arange(L)
    f_re, f_im = w1(0, L - 1 - sidx)
    b_re_, b_im_ = w1(1, sidx)
    w1t = jnp.concatenate([f_re, f_im, b_re_, b_im_], axis=1)

    def w2(d, pw_idx):
        qr = pr[d][:, pw_idx]
        qi = pi[d][:, pw_idx]
        re = cr[d][:, None] * qr[:, :, None] - ci[d][:, None] * qi[:, :, None]
        im = -(cr[d][:, None] * qi[:, :, None] + ci[d][:, None] * qr[:, :, None])
        to = lambda v: v.reshape(g, fl, p)
        return to(re), to(im)
    rf_re, rf_im = w2(0, sidx + 1)
    rb_re, rb_im = w2(1, L - sidx)
    w2t = jnp.concatenate([rf_re, rf_im, rb_re, rb_im], axis=2)

    al = jnp.concatenate([pr[0][:, L], pi[0][:, L], pr[1][:, L], pi[1][:, L]], axis=1)
    al = jnp.broadcast_to(al[:, :, None], (g, 4 * p, 128))
    return mt.astype(BF16), w1t.astype(BF16), w2t.astype(BF16), al


def _ssm_kernel(ut_ref, w1_ref, mt_ref, w2_ref, a_ref, yt_ref, *, nseg, nlev):
    p = SSM_STATE
    ut = ut_ref[0]
    n = ut.shape[1]
    st = jnp.dot(w1_ref[0], ut, preferred_element_type=F32)
    lane = lax.broadcasted_iota(I32, (p, n), 1) % nseg
    reps = n // 128
    a = a_ref[0]

    def widen(v):
        return v if reps == 1 else jnp.concatenate([v] * reps, axis=1)

    def scan(sr, si, ar, ai, forward):
        def shift(v, d):
            if forward:
                return jnp.where(lane >= d, pltpu.roll(v, d, 1), 0.0)
            return jnp.where(lane < nseg - d, pltpu.roll(v, n - d, 1), 0.0)
        xr, xi = shift(sr, 1), shift(si, 1)
        d = 1
        for _ in range(nlev):
            tr, ti = shift(xr, d), shift(xi, d)
            arw, aiw = widen(ar), widen(ai)
            xr, xi = xr + arw * tr - aiw * ti, xi + arw * ti + aiw * tr
            ar, ai = ar * ar - ai * ai, 2.0 * ar * ai
            d *= 2
        return xr, xi

    hfr, hfi = scan(st[0:p], st[p:2 * p], a[0:p], a[p:2 * p], True)
    hbr, hbi = scan(st[2 * p:3 * p], st[3 * p:4 * p], a[2 * p:3 * p], a[3 * p:4 * p], False)
    h = jnp.concatenate([hfr, hfi, hbr, hbi], axis=0).astype(BF16)
    y = jnp.dot(mt_ref[0], ut, preferred_element_type=F32) \
        + jnp.dot(w2_ref[0], h, preferred_element_type=F32)
    yt_ref[0] = y.astype(BF16)


def _ssm(ut3, mt, w1t, w2t, al, nseg):
    g, fl, n = ut3.shape
    sp = w1t.shape[1]
    nlev = max(1, math.ceil(math.log2(nseg)))
    return pl.pallas_call(
        functools.partial(_ssm_kernel, nseg=nseg, nlev=nlev),
        out_shape=jax.ShapeDtypeStruct((g, fl, n), BF16),
        grid=(g,),
        in_specs=[pl.BlockSpec((1, fl, n), lambda i: (i, 0, 0)),
                  pl.BlockSpec((1, sp, fl), lambda i: (i, 0, 0)),
                  pl.BlockSpec((1, fl, fl), lambda i: (i, 0, 0)),
                  pl.BlockSpec((1, fl, sp), lambda i: (i, 0, 0)),
                  pl.BlockSpec((1, sp, 128), lambda i: (i, 0, 0))],
        out_specs=pl.BlockSpec((1, fl, n), lambda i: (i, 0, 0)),
        compiler_params=_cparams(("arbitrary",)),
        name="ssm",
    )(ut3, w1t, mt, w2t, al)


def _glu_kernel(yt_ref, wt_ref, b_ref, out_ref):
    yt = yt_ref[...]
    nc = yt.shape[2]
    y = yt.reshape(yt.shape[0] * yt.shape[1], nc).astype(F32)
    ya = jax.nn.gelu(y, approximate=True)
    z = jnp.dot(wt_ref[...], ya.astype(BF16), preferred_element_type=F32) + b_ref[...]
    out = ya * (1.0 / (1.0 + jnp.exp(-z)))
    out_ref[...] = out.T.astype(BF16)


def _glu(yt4, glu_w_t, glu_b_col):
    g, L, c, n = yt4.shape
    ds = g * c
    nc = min(512, n)
    return pl.pallas_call(
        _glu_kernel,
        out_shape=jax.ShapeDtypeStruct((n, L * ds), BF16),
        grid=(n // nc, L),
        in_specs=[pl.BlockSpec((g, None, c, nc), lambda j, t: (0, t, 0, j)),
                  pl.BlockSpec((ds, ds), lambda j, t: (0, 0)),
                  pl.BlockSpec((ds, 1), lambda j, t: (0, 0))],
        out_specs=pl.BlockSpec((nc, ds), lambda j, t: (j, t)),
        compiler_params=_cparams(("arbitrary", "arbitrary")),
        name="glu",
    )(yt4, glu_w_t, glu_b_col)


def _layer_norm(r, g, b):
    mu = jnp.mean(r, axis=-1, keepdims=True)
    cen = r - mu
    var = jnp.mean(cen * cen, axis=-1, keepdims=True)
    return cen * lax.rsqrt(var + LN_EPS) * g + b


def _outproj_kernel(yp_ref, ys_ref, x_ref, wo_ref, g_ref, b_ref, rwh_ref, rwl_ref, rb_ref,
                    h_ref, lt_ref, *, alpha, dp):
    mix = jnp.dot(yp_ref[...], wo_ref[0:dp, :], preferred_element_type=F32) \
        + jnp.dot(ys_ref[...], wo_ref[dp:, :], preferred_element_type=F32)
    h = _layer_norm(alpha * x_ref[...] + mix, g_ref[...], b_ref[...])
    h_ref[...] = h
    hh = h.astype(BF16)
    hl = (h - hh.astype(F32)).astype(BF16)
    logits = jnp.dot(hh, rwh_ref[...], preferred_element_type=F32) \
        + jnp.dot(hh, rwl_ref[...], preferred_element_type=F32) \
        + jnp.dot(hl, rwh_ref[...], preferred_element_type=F32) + rb_ref[...]
    lt_ref[...] = logits.T[0:ROUTER_ROWS, :]


def _outproj(y_pool, y_ssm, xt, w_out, ln_g, ln_b, rw_hi, rw_lo, rb, alpha):
    t, d = xt.shape
    dp = y_pool.shape[1]
    tm = min(256, t)
    return pl.pallas_call(
        functools.partial(_outproj_kernel, alpha=alpha, dp=dp),
        out_shape=(jax.ShapeDtypeStruct((t, d), F32),
                   jax.ShapeDtypeStruct((ROUTER_ROWS, t), F32)),
        grid=(t // tm,),
        in_specs=[pl.BlockSpec((tm, dp), lambda i: (i, 0)),
                  pl.BlockSpec((tm, y_ssm.shape[1]), lambda i: (i, 0)),
                  pl.BlockSpec((tm, d), lambda i: (i, 0)),
                  pl.BlockSpec(w_out.shape, lambda i: (0, 0)),
                  pl.BlockSpec((1, d), lambda i: (0, 0)),
                  pl.BlockSpec((1, d), lambda i: (0, 0)),
                  pl.BlockSpec(rw_hi.shape, lambda i: (0, 0)),
                  pl.BlockSpec(rw_lo.shape, lambda i: (0, 0)),
                  pl.BlockSpec((1, ROUTER_LANES), lambda i: (0, 0))],
        out_specs=(pl.BlockSpec((tm, d), lambda i: (i, 0)),
                   pl.BlockSpec((ROUTER_ROWS, tm), lambda i: (0, i))),
        compiler_params=_cparams(("arbitrary",)),
        name="outproj",
    )(y_pool, y_ssm, xt, w_out, ln_g, ln_b, rw_hi, rw_lo, rb)


def _router_kernel(lt_ref, eid_ref, rank_ref, gate_ref, cnt_ref, carry_ref):
    i = pl.program_id(0)
    ne, epg = N_EXPERTS, EXPERTS_PER_GROUP

    @pl.when(i == 0)
    def _():
        carry_ref[...] = jnp.zeros_like(carry_ref)

    lt = lt_ref[...]
    tt = lt.shape[1]
    gl = [lt[j:j + 1, :] for j in range(N_EXPERT_GROUPS)]
    gmax = jnp.maximum(jnp.maximum(gl[0], gl[1]), jnp.maximum(gl[2], gl[3]))
    grp = jnp.where(gl[0] == gmax, 0, jnp.where(gl[1] == gmax, 1, jnp.where(gl[2] == gmax, 2, 3)))
    p_grp = 1.0 / (jnp.exp(gl[0] - gmax) + jnp.exp(gl[1] - gmax)
                   + jnp.exp(gl[2] - gmax) + jnp.exp(gl[3] - gmax))
    eg = [lt[8 + epg * j: 8 + epg * (j + 1), :] for j in range(N_EXPERT_GROUPS)]
    el = jnp.where(grp == 0, eg[0], jnp.where(grp == 1, eg[1], jnp.where(grp == 2, eg[2], eg[3])))
    sub = lax.broadcasted_iota(I32, (epg, tt), 0)
    m1 = jnp.max(el, axis=0, keepdims=True)
    i1 = jnp.min(jnp.where(el == m1, sub, epg), axis=0, keepdims=True)
    rest = jnp.where(sub == i1, -jnp.inf, el)
    m2 = jnp.max(rest, axis=0, keepdims=True)
    i2 = jnp.min(jnp.where(rest == m2, sub, epg), axis=0, keepdims=True)
    r21 = jnp.exp(m2 - m1)
    g1 = p_grp / (1.0 + r21)
    g2 = g1 * r21
    e1 = grp * epg + i1
    e2 = grp * epg + i2

    rows = lax.broadcasted_iota(I32, (ne, tt), 0)
    oh1 = rows == e1
    oh2 = rows == e2
    oh = jnp.where(oh1 | oh2, 1.0, 0.0)
    tri = jnp.where(lax.broadcasted_iota(I32, (tt, tt), 0) < lax.broadcasted_iota(I32, (tt, tt), 1),
                    1.0, 0.0).astype(BF16)
    before = jnp.dot(oh.astype(BF16), tri, preferred_element_type=F32) + carry_ref[:, 0:1]
    r1 = jnp.sum(jnp.where(oh1, before, 0.0), axis=0, keepdims=True)
    r2 = jnp.sum(jnp.where(oh2, before, 0.0), axis=0, keepdims=True)
    carry_ref[...] = carry_ref[...] + jnp.sum(oh, axis=1, keepdims=True)

    eid_ref[...] = jnp.concatenate([e1, e2], axis=0)
    rank_ref[...] = jnp.concatenate([r1, r2], axis=0).astype(I32)
    gate_ref[...] = jnp.concatenate([g1, g2], axis=0)
    cnt_ref[...] = carry_ref[...].astype(I32)


def _router(lt):
    rr, t = lt.shape
    tt = min(512, t)
    return pl.pallas_call(
        _router_kernel,
        out_shape=(jax.ShapeDtypeStruct((2, t), I32),
                   jax.ShapeDtypeStruct((2, t), I32),
                   jax.ShapeDtypeStruct((2, t), F32),
                   jax.ShapeDtypeStruct((N_EXPERTS, 128), I32)),
        grid=(t // tt,),
        in_specs=[pl.BlockSpec((rr, tt), lambda i: (0, i))],
        out_specs=(pl.BlockSpec((2, tt), lambda i: (0, i)),
                   pl.BlockSpec((2, tt), lambda i: (0, i)),
                   pl.BlockSpec((2, tt), lambda i: (0, i)),
                   pl.BlockSpec((N_EXPERTS, 128), lambda i: (0, 0))),
        scratch_shapes=[pltpu.VMEM((N_EXPERTS, 128), F32)],
        compiler_params=_cparams(("arbitrary",)),
        name="router",
    )(lt)


def _moe_kernel(be_ref, nu_ref, st_ref, h_hbm, wg_ref, wu_ref, wd_ref, y_ref,
                xbuf, wgb, wub, wdb, sem):
    b = pl.program_id(0)
    nu = nu_ref[0]
    blk = xbuf.shape[1]

    def row_copy(tok, slot, r):
        return pltpu.make_async_copy(h_hbm.at[pl.ds(tok, 1)], xbuf.at[slot, pl.ds(r, 1)], sem.at[slot])

    def issue(block, slot):
        base = block * blk

        def body(r, carry):
            row_copy(st_ref[base + r], slot, r).start()
            return carry
        lax.fori_loop(0, blk, body, 0)

    @pl.when(b == 0)
    def _():
        issue(0, 0)

    @pl.when(b + 1 < nu)
    def _():
        issue(b + 1, (b + 1) % 2)

    @pl.when(b < nu)
    def _():
        slot = b % 2
        pltpu.make_async_copy(h_hbm.at[pl.ds(0, blk)], xbuf.at[slot], sem.at[slot]).wait()
        changed = (b == 0) | (be_ref[b] != be_ref[jnp.maximum(b - 1, 0)])

        @pl.when(changed)
        def _():
            wgb[...] = wg_ref[...].astype(BF16)
            wub[...] = wu_ref[...].astype(BF16)
            wdb[...] = wd_ref[...].astype(BF16)

        x = xbuf[slot].astype(BF16)
        hg = jnp.dot(x, wgb[...], preferred_element_type=F32)
        hu = jnp.dot(x, wub[...], preferred_element_type=F32)
        hh = hg * (1.0 / (1.0 + jnp.exp(-hg))) * hu
        y_ref[...] = jnp.dot(hh.astype(BF16), wdb[...], preferred_element_type=F32)

    @pl.when(b >= nu)
    def _():
        y_ref[...] = jnp.zeros_like(y_ref)


def _moe(block_e, n_used, slot_tok, h1, w_gate, w_up, w_down, n_blocks):
    t, d = h1.shape
    de = w_gate.shape[2]
    blk = MOE_BLK
    grid_spec = pltpu.PrefetchScalarGridSpec(
        num_scalar_prefetch=3,
        grid=(n_blocks,),
        in_specs=[pl.BlockSpec(memory_space=pl.ANY),
                  pl.BlockSpec((None, d, de), lambda b, be, nu, st: (be[b], 0, 0)),
                  pl.BlockSpec((None, d, de), lambda b, be, nu, st: (be[b], 0, 0)),
                  pl.BlockSpec((None, de, d), lambda b, be, nu, st: (be[b], 0, 0))],
        out_specs=pl.BlockSpec((blk, d), lambda b, be, nu, st: (b, 0)),
        scratch_shapes=[pltpu.VMEM((2, blk, d), F32),
                        pltpu.VMEM((d, de), BF16),
                        pltpu.VMEM((d, de), BF16),
                        pltpu.VMEM((de, d), BF16),
                        pltpu.SemaphoreType.DMA((2,))],
    )
    return pl.pallas_call(
        _moe_kernel,
        out_shape=jax.ShapeDtypeStruct((n_blocks * blk, d), F32),
        grid_spec=grid_spec,
        compiler_params=_cparams(("arbitrary",)),
        name="moe",
    )(block_e, n_used, slot_tok, h1, w_gate, w_up, w_down)


def _combine_kernel(d0_ref, d1_ref, h_ref, gate_ref, yb_hbm, g_ref, b_ref, out_ref, buf, sem, *, alpha):
    i = pl.program_id(0)
    n = pl.num_programs(0)
    tm = h_ref.shape[0]

    def issue(tile, slot):
        base = tile * tm

        def body(r, carry):
            pltpu.make_async_copy(yb_hbm.at[pl.ds(d0_ref[base + r], 1)],
                                  buf.at[slot, 0, pl.ds(r, 1)], sem.at[slot]).start()
            pltpu.make_async_copy(yb_hbm.at[pl.ds(d1_ref[base + r], 1)],
                                  buf.at[slot, 1, pl.ds(r, 1)], sem.at[slot]).start()
            return carry
        lax.fori_loop(0, tm, body, 0)

    @pl.when(i == 0)
    def _():
        issue(0, 0)

    @pl.when(i + 1 < n)
    def _():
        issue(i + 1, (i + 1) % 2)

    slot = i % 2
    for k in range(2):
        pltpu.make_async_copy(yb_hbm.at[pl.ds(0, tm)], buf.at[slot, k], sem.at[slot]).wait()
    ffn = gate_ref[:, 0:1] * buf[slot, 0] + gate_ref[:, 1:2] * buf[slot, 1]
    out_ref[...] = _layer_norm(alpha * h_ref[...] + ffn, g_ref[...], b_ref[...])


def _combine(dest0, dest1, h1, gate_t, yb, ln_g, ln_b, alpha):
    t, d = h1.shape
    tm = min(256, t)
    grid_spec = pltpu.PrefetchScalarGridSpec(
        num_scalar_prefetch=2,
        grid=(t // tm,),
        in_specs=[pl.BlockSpec((tm, d), lambda i, d0, d1: (i, 0)),
                  pl.BlockSpec((tm, 2), lambda i, d0, d1: (i, 0)),
                  pl.BlockSpec(memory_space=pl.ANY),
                  pl.BlockSpec((1, d), lambda i, d0, d1: (0, 0)),
                  pl.BlockSpec((1, d), lambda i, d0, d1: (0, 0))],
        out_specs=pl.BlockSpec((tm, d), lambda i, d0, d1: (i, 0)),
        scratch_shapes=[pltpu.VMEM((2, 2, tm, d), F32),
                        pltpu.SemaphoreType.DMA((2,))],
    )
    return pl.pallas_call(
        functools.partial(_combine_kernel, alpha=alpha),
        out_shape=jax.ShapeDtypeStruct((t, d), F32),
        grid_spec=grid_spec,
        compiler_params=_cparams(("arbitrary",)),
        name="combine",
    )(dest0, dest1, h1, gate_t, yb, ln_g, ln_b)


def _layer(h, w_in, pool_w, pool_scale, a_re, a_im, log_dt, b_re, b_im, c_re, c_im, d_skip,
           glu_w, glu_b, w_out, ln1_g, ln1_b, rg_w, rg_b, re_w, re_b, w_gate, w_up, w_down,
           ln2_g, ln2_b, alpha):
    bsz, seq, d = h.shape
    t = bsz * seq
    L = CHUNK
    dp = pool_w.shape[0] * pool_w.shape[1]
    ds = w_in.shape[1] - dp
    g = ds // SSM_GROUP
    n = t // L

    w_pool = w_in[:, :dp].astype(BF16)
    w_ssm_t = w_in[:, dp:].T.astype(BF16)
    mt, w1t, w2t, al = _ssm_tables(a_re, a_im, log_dt, b_re, b_im, c_re, c_im, d_skip, L)
    rw = jnp.zeros((d, ROUTER_LANES), F32)
    rw = rw.at[:, 0:N_EXPERT_GROUPS].set(rg_w.astype(F32)).at[:, 8:8 + N_EXPERTS].set(re_w.astype(F32))
    rw_hi = rw.astype(BF16)
    rw_lo = (rw - rw_hi.astype(F32)).astype(BF16)
    rb = jnp.zeros((1, ROUTER_LANES), F32)
    rb = rb.at[0, 0:N_EXPERT_GROUPS].set(rg_b.astype(F32)).at[0, 8:8 + N_EXPERTS].set(re_b.astype(F32))

    xt = h.reshape(t, d)
    pool_p, ut = _proj(xt.reshape(n, L * d), w_pool, w_ssm_t, g)
    y_pool = _pool(pool_p.reshape(t, dp), pool_w.astype(BF16), pool_scale.reshape(1, dp).astype(F32), seq)
    yt = _ssm(ut.reshape(g, L * SSM_GROUP, n), mt, w1t, w2t, al, seq // L)
    y_ssm = _glu(yt.reshape(g, L, SSM_GROUP, n), glu_w.T.astype(BF16),
                 glu_b.reshape(ds, 1).astype(F32)).reshape(t, ds)
    h1, lt = _outproj(y_pool, y_ssm, xt, w_out.astype(BF16), ln1_g.reshape(1, d), ln1_b.reshape(1, d),
                      rw_hi, rw_lo, rb, alpha)
    eid, rank, gate, cnt = _router(lt)

    blk = MOE_BLK
    m = 2 * t
    n_blocks = -(-m // blk) + N_EXPERTS
    counts = cnt[:, 0]
    pcounts = (counts + blk - 1) // blk * blk
    pends = jnp.cumsum(pcounts)
    pstarts = pends - pcounts
    dest = pstarts[eid] + rank
    tok = jnp.broadcast_to(jnp.arange(t, dtype=I32)[None, :], (2, t))
    slot_tok = jnp.zeros((n_blocks * blk,), I32).at[dest.reshape(-1)].set(tok.reshape(-1))
    n_used = (pends[-1] // blk).astype(I32)
    bidx = jnp.minimum(jnp.arange(n_blocks, dtype=I32), n_used - 1)
    block_e = jnp.minimum(jnp.searchsorted(pends, bidx * blk, side='right'), N_EXPERTS - 1).astype(I32)

    yb = _moe(block_e, n_used.reshape(1), slot_tok, h1, w_gate, w_up, w_down, n_blocks)
    out = _combine(dest[0], dest[1], h1, gate.T, yb, ln2_g.reshape(1, d), ln2_b.reshape(1, d), alpha)
    return out.reshape(bsz, seq, d)


def kernel(x, w_in, pool_w, pool_scale, ssm_a_re, ssm_a_im, ssm_log_dt, ssm_b_re, ssm_b_im, ssm_c_re, ssm_c_im, ssm_d, glu_w, glu_b, w_out, ln1_g, ln1_b, router_g_w, router_g_b, router_e_w, router_e_b, w_gate, w_up, w_down, ln2_g, ln2_b):
    depth = w_in.shape[0]
    alpha = (2.0 * depth) ** 0.25
    h = x
    for l in range(depth):
        h = _layer(h, w_in[l], pool_w[l], pool_scale[l], ssm_a_re[l], ssm_a_im[l], ssm_log_dt[l],
                   ssm_b_re[l], ssm_b_im[l], ssm_c_re[l], ssm_c_im[l], ssm_d[l], glu_w[l], glu_b[l],
                   w_out[l], ln1_g[l], ln1_b[l], router_g_w[l], router_g_b[l], router_e_w[l],
                   router_e_b[l], w_gate[l], w_up[l], w_down[l], ln2_g[l], ln2_b[l], alpha)
    return h
```

```python
import functools
import math

import numpy as np
import jax
import jax.numpy as jnp
from jax import lax
from jax.experimental import pallas as pl
from jax.experimental.pallas import tpu as pltpu

F32 = jnp.float32
BF16 = jnp.bfloat16
I32 = jnp.int32

POOL_WINDOWS = (2, 4, 8, 16)
POOL_GROUP = 256
SSM_GROUP = 16
SSM_STATE = 64
N_EXPERT_GROUPS = 4
EXPERTS_PER_GROUP = 8
N_EXPERTS = N_EXPERT_GROUPS * EXPERTS_PER_GROUP
LN_EPS = 1e-5

CHUNK = 16
MOE_BLK = 256
ROUTER_ROWS = 8 + N_EXPERTS
ROUTER_LANES = 128
LANES = 128
HALO = 16
VMEM_LIMIT = 56 * 1024 * 1024


def _cparams(sem, vmem=VMEM_LIMIT):
    return pltpu.CompilerParams(dimension_semantics=sem, vmem_limit_bytes=vmem)


def _proj_kernel(x_ref, w_ref, pool_ref, ut_ref, accp_ref, accs_ref, *, sub):
    kk = pl.program_id(1)
    rows, dp = accp_ref.shape
    nj = accs_ref.shape[0]
    for rb in range(rows // sub):
        sl = pl.ds(rb * sub, sub)
        part = jnp.dot(x_ref[sl, :].astype(BF16), w_ref[...], preferred_element_type=F32)

        @pl.when(kk == 0)
        def _():
            accp_ref[sl, :] = part[:, 0:dp]
            for j in range(nj):
                accs_ref[j, sl, :] = part[:, dp + j * LANES: dp + (j + 1) * LANES]

        @pl.when(kk > 0)
        def _():
            accp_ref[sl, :] += part[:, 0:dp]
            for j in range(nj):
                accs_ref[j, sl, :] += part[:, dp + j * LANES: dp + (j + 1) * LANES]

    @pl.when(kk == pl.num_programs(1) - 1)
    def _():
        pool_ref[...] = accp_ref[...].astype(BF16)
        g, L, c, r = ut_ref.shape
        gj = LANES // c
        for s in range(L):
            for j in range(nj):
                piece = accs_ref[j, pl.ds(s, r, stride=L), :]
                ut_ref[j * gj:(j + 1) * gj, s, :, :] = piece.T.astype(BF16).reshape(gj, c, r)


def _proj(xt, w_in, dp, L):
    t, d = xt.shape
    dm = w_in.shape[1]
    g = (dm - dp) // SSM_GROUP
    tm = LANES * L
    kb = 512
    return pl.pallas_call(
        functools.partial(_proj_kernel, sub=256),
        out_shape=(jax.ShapeDtypeStruct((t, dp), BF16),
                   jax.ShapeDtypeStruct((g, L, SSM_GROUP, t // L), BF16)),
        grid=(t // tm, d // kb),
        in_specs=[pl.BlockSpec((tm, kb), lambda i, k: (i, k)),
                  pl.BlockSpec((kb, dm), lambda i, k: (k, 0))],
        out_specs=(pl.BlockSpec((tm, dp), lambda i, k: (i, 0)),
                   pl.BlockSpec((g, L, SSM_GROUP, LANES), lambda i, k: (0, 0, 0, i))),
        scratch_shapes=[pltpu.VMEM((tm, dp), F32),
                        pltpu.VMEM(((dm - dp) // LANES, tm, LANES), F32)],
        compiler_params=_cparams(("arbitrary", "arbitrary")),
        name="proj",
    )(xt, w_in)


def _pool_kernel(prev_ref, main_ref, next_ref, pw_ref, sc_ref, out_ref, *, seq, ts):
    i = pl.program_id(0)
    ext = jnp.concatenate([prev_ref[...], main_ref[...], next_ref[...]], axis=0)
    k = ts + 2 * HALO
    row = lax.broadcasted_iota(I32, (ts, k), 0)
    col = lax.broadcasted_iota(I32, (ts, k), 1)
    base = (i * ts) % seq
    off = col - HALO - row
    src = base + row + off
    inside = (src >= 0) & (src < seq)
    pos = base + lax.broadcasted_iota(I32, (ts, 1), 0)
    for g, w in enumerate(POOL_WINDOWS):
        sl = slice(g * POOL_GROUP, (g + 1) * POOL_GROUP)
        band = inside & (off >= -(w // 2)) & (off < w - w // 2)
        bm = jnp.where(band, 1.0, 0.0).astype(BF16)
        sums = jnp.dot(bm, ext[:, sl], preferred_element_type=F32)
        lo = jnp.maximum(pos - w // 2, 0)
        hi = jnp.minimum(pos - w // 2 + w, seq)
        inv = 1.0 / (hi - lo).astype(F32)
        dlt = sums * inv - main_ref[:, sl].astype(F32)
        y = jnp.dot(dlt.astype(BF16), pw_ref[g], preferred_element_type=F32) * sc_ref[:, sl]
        out_ref[:, sl] = y.astype(BF16)


def _pool(pp, pool_w, pool_scale, seq):
    t, dp = pp.shape
    ts = min(256, seq)
    nh = t // HALO
    per = ts // HALO
    return pl.pallas_call(
        functools.partial(_pool_kernel, seq=seq, ts=ts),
        out_shape=jax.ShapeDtypeStruct((t, dp), BF16),
        grid=(t // ts,),
        in_specs=[pl.BlockSpec((HALO, dp), lambda i: (jnp.maximum(i * per - 1, 0), 0)),
                  pl.BlockSpec((ts, dp), lambda i: (i, 0)),
                  pl.BlockSpec((HALO, dp), lambda i: (jnp.minimum((i + 1) * per, nh - 1), 0)),
                  pl.BlockSpec(pool_w.shape, lambda i: (0, 0, 0)),
                  pl.BlockSpec((1, dp), lambda i: (0, 0))],
        out_specs=pl.BlockSpec((ts, dp), lambda i: (i, 0)),
        compiler_params=_cparams(("arbitrary",)),
        name="pool",
    )(pp, pp, pp, pool_w, pool_scale)


def _ssm_tables(a_re, a_im, log_dt, b_re, b_im, c_re, c_im, d_skip, L):
    g = a_re.shape[1]
    p = a_re.shape[2]
    c = b_re.shape[3]
    fl = L * c
    lr = a_re.astype(F32)
    li = a_im.astype(F32)
    dt = jnp.exp(log_dt.astype(F32))[..., None]
    mag = jnp.exp(lr * dt)
    abr = mag * jnp.cos(li * dt)
    abi = mag * jnp.sin(li * dt)
    den = lr * lr + li * li
    zr = ((abr - 1.0) * lr + abi * li) / den
    zi = (abi * lr - (abr - 1.0) * li) / den
    br = b_re.astype(F32)
    bi = b_im.astype(F32)
    bbr = zr[..., None] * br - zi[..., None] * bi
    bbi = zr[..., None] * bi + zi[..., None] * br
    cr = c_re.astype(F32)
    ci = c_im.astype(F32)
    kk = jnp.arange(L + 1, dtype=F32)[None, None, :, None]
    pm = jnp.exp(kk * (lr * dt)[:, :, None, :])
    ang = kk * (li * dt)[:, :, None, :]
    pr = pm * jnp.cos(ang)
    pi = pm * jnp.sin(ang)

    cpr = cr[:, :, None] * pr[:, :, :L, None, :] - ci[:, :, None] * pi[:, :, :L, None, :]
    cpi = cr[:, :, None] * pi[:, :, :L, None, :] + ci[:, :, None] * pr[:, :, :L, None, :]
    kern = jnp.einsum('dgkop,dgpc->dgkoc', cpr, bbr, precision=lax.Precision.HIGHEST) \
        - jnp.einsum('dgkop,dgpc->dgkoc', cpi, bbi, precision=lax.Precision.HIGHEST)
    lag = np.arange(L)[:, None, None] - np.arange(L)[None, :, None]
    sel_f = jnp.asarray((lag == np.arange(L)[None, None, :]).astype(np.float32))
    sel_b = jnp.asarray((-lag == np.arange(L)[None, None, :]).astype(np.float32))
    skip = jnp.eye(c, dtype=F32)[None] * d_skip.astype(F32).reshape(g, c, 1)
    kern_f = kern[0].at[:, 0].add(skip)
    m5 = jnp.einsum('tsk,gkoc->gtosc', sel_f, kern_f, precision=lax.Precision.HIGHEST) \
        + jnp.einsum('tsk,gkoc->gtosc', sel_b, kern[1], precision=lax.Precision.HIGHEST)
    mt = m5.reshape(g, fl, fl)

    def w1(d, qr, qi):
        re = qr[..., None] * bbr[d][:, None] - qi[..., None] * bbi[d][:, None]
        im = qr[..., None] * bbi[d][:, None] + qi[..., None] * bbr[d][:, None]
        to = lambda v: v.transpose(0, 2, 1, 3).reshape(g, p, fl)
        return to(re), to(im)
    f_re, f_im = w1(0, pr[0][:, L - 1::-1], pi[0][:, L - 1::-1])
    b_re_, b_im_ = w1(1, pr[1][:, :L], pi[1][:, :L])
    w1t = jnp.concatenate([f_re, f_im, b_re_, b_im_], axis=1)

    def w2(d, qr, qi):
        re = cr[d][:, None] * qr[:, :, None] - ci[d][:, None] * qi[:, :, None]
        im = -(cr[d][:, None] * qi[:, :, None] + ci[d][:, None] * qr[:, :, None])
        to = lambda v: v.reshape(g, fl, p)
        return to(re), to(im)
    rf_re, rf_im = w2(0, pr[0][:, 1:], pi[0][:, 1:])
    rb_re, rb_im = w2(1, pr[1][:, :0:-1], pi[1][:, :0:-1])
    w2t = jnp.concatenate([rf_re, rf_im, rb_re, rb_im], axis=2)

    al = jnp.concatenate([pr[0][:, L], pi[0][:, L], pr[1][:, L], pi[1][:, L]], axis=1)
    al = jnp.broadcast_to(al[:, :, None], (g, 4 * p, 128))
    return mt.astype(BF16), w1t.astype(BF16), w2t.astype(BF16), al


def _ssm_kernel(ut_ref, w1_ref, mt_ref, w2_ref, a_ref, yt_ref, *, nseg, nlev):
    p = SSM_STATE
    ut = ut_ref[0]
    n = ut.shape[1]
    st = jnp.dot(w1_ref[0], ut, preferred_element_type=F32)
    lane = lax.broadcasted_iota(I32, (p, n), 1) % nseg
    reps = n // 128
    a = a_ref[0]

    def widen(v):
        return v if reps == 1 else jnp.concatenate([v] * reps, axis=1)

    def scan(sr, si, ar, ai, forward):
        def shift(v, d):
            if forward:
                return jnp.where(lane >= d, pltpu.roll(v, d, 1), 0.0)
            return jnp.where(lane < nseg - d, pltpu.roll(v, n - d, 1), 0.0)
        xr, xi = shift(sr, 1), shift(si, 1)
        d = 1
        for _ in range(nlev):
            tr, ti = shift(xr, d), shift(xi, d)
            arw, aiw = widen(ar), widen(ai)
            xr, xi = xr + arw * tr - aiw * ti, xi + arw * ti + aiw * tr
            ar, ai = ar * ar - ai * ai, 2.0 * ar * ai
            d *= 2
        return xr, xi

    hfr, hfi = scan(st[0:p], st[p:2 * p], a[0:p], a[p:2 * p], True)
    hbr, hbi = scan(st[2 * p:3 * p], st[3 * p:4 * p], a[2 * p:3 * p], a[3 * p:4 * p], False)
    h = jnp.concatenate([hfr, hfi, hbr, hbi], axis=0).astype(BF16)
    y = jnp.dot(mt_ref[0], ut, preferred_element_type=F32) \
        + jnp.dot(w2_ref[0], h, preferred_element_type=F32)
    yt_ref[0] = y.astype(BF16)


def _ssm(ut3, mt, w1t, w2t, al, nseg):
    g, fl, n = ut3.shape
    sp = w1t.shape[1]
    nlev = max(1, math.ceil(math.log2(nseg)))
    return pl.pallas_call(
        functools.partial(_ssm_kernel, nseg=nseg, nlev=nlev),
        out_shape=jax.ShapeDtypeStruct((g, fl, n), BF16),
        grid=(g,),
        in_specs=[pl.BlockSpec((1, fl, n), lambda i: (i, 0, 0)),
                  pl.BlockSpec((1, sp, fl), lambda i: (i, 0, 0)),
                  pl.BlockSpec((1, fl, fl), lambda i: (i, 0, 0)),
                  pl.BlockSpec((1, fl, sp), lambda i: (i, 0, 0)),
                  pl.BlockSpec((1, sp, 128), lambda i: (i, 0, 0))],
        out_specs=pl.BlockSpec((1, fl, n), lambda i: (i, 0, 0)),
        compiler_params=_cparams(("arbitrary",)),
        name="ssm",
    )(ut3, w1t, mt, w2t, al)


def _glu_kernel(yt_ref, w_ref, b_ref, out_ref, nat_ref, *, sub):
    g, L, c, r = yt_ref.shape
    nj = nat_ref.shape[0]
    gj = LANES // c
    for t in range(L):
        for j in range(nj):
            piece = yt_ref[j * gj:(j + 1) * gj, t, :, :].reshape(LANES, r).astype(F32)
            nat_ref[j, pl.ds(t, r, stride=L), :] = piece.T

    def body(k, carry):
        sl = pl.ds(pl.multiple_of(k * sub, sub), sub)
        y = jnp.concatenate([nat_ref[j, sl, :] for j in range(nj)], axis=1)
        ya = jax.nn.gelu(y, approximate=True)
        z = jnp.dot(ya.astype(BF16), w_ref[...], preferred_element_type=F32) + b_ref[...]
        out_ref[sl, :] = (ya * (1.0 / (1.0 + jnp.exp(-z)))).astype(BF16)
        return carry
    lax.fori_loop(0, nat_ref.shape[1] // sub, body, 0)


def _glu(yt4, glu_w, glu_b):
    g, L, c, n = yt4.shape
    ds = g * c
    r = LANES
    return pl.pallas_call(
        functools.partial(_glu_kernel, sub=256),
        out_shape=jax.ShapeDtypeStruct((n * L, ds), BF16),
        grid=(n // r,),
        in_specs=[pl.BlockSpec((g, L, c, r), lambda j: (0, 0, 0, j)),
                  pl.BlockSpec((ds, ds), lambda j: (0, 0)),
                  pl.BlockSpec((1, ds), lambda j: (0, 0))],
        out_specs=pl.BlockSpec((r * L, ds), lambda j: (j, 0)),
        scratch_shapes=[pltpu.VMEM((ds // LANES, r * L, LANES), F32)],
        compiler_params=_cparams(("arbitrary",)),
        name="glu",
    )(yt4, glu_w, glu_b)


def _layer_norm(r, g, b):
    mu = jnp.mean(r, axis=-1, keepdims=True)
    cen = r - mu
    var = jnp.mean(cen * cen, axis=-1, keepdims=True)
    return cen * lax.rsqrt(var + LN_EPS) * g + b


def _outproj_kernel(yp_ref, ys_ref, x_ref, wo_ref, g_ref, b_ref, rwh_ref, rwl_ref, rb_ref,
                    h_ref, lt_ref, *, alpha, dp):
    mix = jnp.dot(yp_ref[...], wo_ref[0:dp, :], preferred_element_type=F32) \
        + jnp.dot(ys_ref[...], wo_ref[dp:, :], preferred_element_type=F32)
    h = _layer_norm(alpha * x_ref[...] + mix, g_ref[...], b_ref[...])
    h_ref[...] = h
    hh = h.astype(BF16)
    hl = (h - hh.astype(F32)).astype(BF16)
    logits = jnp.dot(hh, rwh_ref[...], preferred_element_type=F32) \
        + jnp.dot(hh, rwl_ref[...], preferred_element_type=F32) \
        + jnp.dot(hl, rwh_ref[...], preferred_element_type=F32) + rb_ref[...]
    lt_ref[...] = logits.T[0:ROUTER_ROWS, :]


def _outproj(y_pool, y_ssm, xt, w_out, ln_g, ln_b, rw_hi, rw_lo, rb, alpha):
    t, d = xt.shape
    dp = y_pool.shape[1]
    tm = min(256, t)
    return pl.pallas_call(
        functools.partial(_outproj_kernel, alpha=alpha, dp=dp),
        out_shape=(jax.ShapeDtypeStruct((t, d), F32),
                   jax.ShapeDtypeStruct((ROUTER_ROWS, t), F32)),
        grid=(t // tm,),
        in_specs=[pl.BlockSpec((tm, dp), lambda i: (i, 0)),
                  pl.BlockSpec((tm, y_ssm.shape[1]), lambda i: (i, 0)),
                  pl.BlockSpec((tm, d), lambda i: (i, 0)),
                  pl.BlockSpec(w_out.shape, lambda i: (0, 0)),
                  pl.BlockSpec((1, d), lambda i: (0, 0)),
                  pl.BlockSpec((1, d), lambda i: (0, 0)),
                  pl.BlockSpec(rw_hi.shape, lambda i: (0, 0)),
                  pl.BlockSpec(rw_lo.shape, lambda i: (0, 0)),
                  pl.BlockSpec((1, ROUTER_LANES), lambda i: (0, 0))],
        out_specs=(pl.BlockSpec((tm, d), lambda i: (i, 0)),
                   pl.BlockSpec((ROUTER_ROWS, tm), lambda i: (0, i))),
        compiler_params=_cparams(("arbitrary",)),
        name="outproj",
    )(y_pool, y_ssm, xt, w_out, ln_g, ln_b, rw_hi, rw_lo, rb)


def _router_kernel(lt_ref, eid_ref, rank_ref, gate_ref, cnt_ref, carry_ref):
    i = pl.program_id(0)
    ne, epg = N_EXPERTS, EXPERTS_PER_GROUP

    @pl.when(i == 0)
    def _():
        carry_ref[...] = jnp.zeros_like(carry_ref)

    lt = lt_ref[...]
    tt = lt.shape[1]
    gl = [lt[j:j + 1, :] for j in range(N_EXPERT_GROUPS)]
    gmax = jnp.maximum(jnp.maximum(gl[0], gl[1]), jnp.maximum(gl[2], gl[3]))
    grp = jnp.where(gl[0] == gmax, 0, jnp.where(gl[1] == gmax, 1, jnp.where(gl[2] == gmax, 2, 3)))
    p_grp = 1.0 / (jnp.exp(gl[0] - gmax) + jnp.exp(gl[1] - gmax)
                   + jnp.exp(gl[2] - gmax) + jnp.exp(gl[3] - gmax))
    eg = [lt[8 + epg * j: 8 + epg * (j + 1), :] for j in range(N_EXPERT_GROUPS)]
    el = jnp.where(grp == 0, eg[0], jnp.where(grp == 1, eg[1], jnp.where(grp == 2, eg[2], eg[3])))
    sub = lax.broadcasted_iota(I32, (epg, tt), 0)
    m1 = jnp.max(el, axis=0, keepdims=True)
    i1 = jnp.min(jnp.where(el == m1, sub, epg), axis=0, keepdims=True)
    rest = jnp.where(sub == i1, -jnp.inf, el)
    m2 = jnp.max(rest, axis=0, keepdims=True)
    i2 = jnp.min(jnp.where(rest == m2, sub, epg), axis=0, keepdims=True)
    r21 = jnp.exp(m2 - m1)
    g1 = p_grp / (1.0 + r21)
    g2 = g1 * r21
    e1 = grp * epg + i1
    e2 = grp * epg + i2

    rows = lax.broadcasted_iota(I32, (ne, tt), 0)
    oh1 = rows == e1
    oh2 = rows == e2
    oh = jnp.where(oh1 | oh2, 1.0, 0.0)
    tri = jnp.where(lax.broadcasted_iota(I32, (tt, tt), 0) < lax.broadcasted_iota(I32, (tt, tt), 1),
                    1.0, 0.0).astype(BF16)
    before = jnp.dot(oh.astype(BF16), tri, preferred_element_type=F32) + carry_ref[:, 0:1]
    r1 = jnp.sum(jnp.where(oh1, before, 0.0), axis=0, keepdims=True)
    r2 = jnp.sum(jnp.where(oh2, before, 0.0), axis=0, keepdims=True)
    carry_ref[...] = carry_ref[...] + jnp.sum(oh, axis=1, keepdims=True)

    eid_ref[...] = jnp.concatenate([e1, e2], axis=0)
    rank_ref[...] = jnp.concatenate([r1, r2], axis=0).astype(I32)
    gate_ref[...] = jnp.concatenate([g1, g2], axis=0)
    cnt_ref[...] = carry_ref[...].astype(I32)


def _router(lt):
    rr, t = lt.shape
    tt = min(512, t)
    return pl.pallas_call(
        _router_kernel,
        out_shape=(jax.ShapeDtypeStruct((2, t), I32),
                   jax.ShapeDtypeStruct((2, t), I32),
                   jax.ShapeDtypeStruct((2, t), F32),
                   jax.ShapeDtypeStruct((N_EXPERTS, 128), I32)),
        grid=(t // tt,),
        in_specs=[pl.BlockSpec((rr, tt), lambda i: (0, i))],
        out_specs=(pl.BlockSpec((2, tt), lambda i: (0, i)),
                   pl.BlockSpec((2, tt), lambda i: (0, i)),
                   pl.BlockSpec((2, tt), lambda i: (0, i)),
                   pl.BlockSpec((N_EXPERTS, 128), lambda i: (0, 0))),
        scratch_shapes=[pltpu.VMEM((N_EXPERTS, 128), F32)],
        compiler_params=_cparams(("arbitrary",)),
        name="router",
    )(lt)


def _moe_kernel(be_ref, nu_ref, st_ref, h_hbm, wg_ref, wu_ref, wd_ref, y_ref,
                xbuf, wgb, wub, wdb, sem):
    b = pl.program_id(0)
    nu = nu_ref[0]
    blk = xbuf.shape[1]

    def row_copy(tok, slot, r):
        return pltpu.make_async_copy(h_hbm.at[pl.ds(tok, 1)], xbuf.at[slot, pl.ds(r, 1)], sem.at[slot])

    def issue(block, slot):
        base = block * blk

        def body(r, carry):
            row_copy(st_ref[base + r], slot, r).start()
            return carry
        lax.fori_loop(0, blk, body, 0)

    @pl.when(b == 0)
    def _():
        issue(0, 0)

    @pl.when(b + 1 < nu)
    def _():
        issue(b + 1, (b + 1) % 2)

    @pl.when(b < nu)
    def _():
        slot = b % 2
        pltpu.make_async_copy(h_hbm.at[pl.ds(0, blk)], xbuf.at[slot], sem.at[slot]).wait()
        changed = (b == 0) | (be_ref[b] != be_ref[jnp.maximum(b - 1, 0)])

        @pl.when(changed)
        def _():
            wgb[...] = wg_ref[...].astype(BF16)
            wub[...] = wu_ref[...].astype(BF16)
            wdb[...] = wd_ref[...].astype(BF16)

        x = xbuf[slot].astype(BF16)
        hg = jnp.dot(x, wgb[...], preferred_element_type=F32)
        hu = jnp.dot(x, wub[...], preferred_element_type=F32)
        hh = hg * (1.0 / (1.0 + jnp.exp(-hg))) * hu
        y_ref[...] = jnp.dot(hh.astype(BF16), wdb[...], preferred_element_type=F32)

    @pl.when(b >= nu)
    def _():
        y_ref[...] = jnp.zeros_like(y_ref)


def _moe(block_e, n_used, slot_tok, h1, w_gate, w_up, w_down, n_blocks):
    t, d = h1.shape
    de = w_gate.shape[2]
    blk = MOE_BLK
    grid_spec = pltpu.PrefetchScalarGridSpec(
        num_scalar_prefetch=3,
        grid=(n_blocks,),
        in_specs=[pl.BlockSpec(memory_space=pl.ANY),
                  pl.BlockSpec((None, d, de), lambda b, be, nu, st: (be[b], 0, 0)),
                  pl.BlockSpec((None, d, de), lambda b, be, nu, st: (be[b], 0, 0)),
                  pl.BlockSpec((None, de, d), lambda b, be, nu, st: (be[b], 0, 0))],
        out_specs=pl.BlockSpec((blk, d), lambda b, be, nu, st: (b, 0)),
        scratch_shapes=[pltpu.VMEM((2, blk, d), F32),
                        pltpu.VMEM((d, de), BF16),
                        pltpu.VMEM((d, de), BF16),
                        pltpu.VMEM((de, d), BF16),
                        pltpu.SemaphoreType.DMA((2,))],
    )
    return pl.pallas_call(
        _moe_kernel,
        out_shape=jax.ShapeDtypeStruct((n_blocks * blk, d), F32),
        grid_spec=grid_spec,
        compiler_params=_cparams(("arbitrary",)),
        name="moe",
    )(block_e, n_used, slot_tok, h1, w_gate, w_up, w_down)


def _combine_kernel(d0_ref, d1_ref, h_ref, gate_ref, yb_hbm, g_ref, b_ref, out_ref, buf, sem, *, alpha):
    i = pl.program_id(0)
    n = pl.num_programs(0)
    tm = h_ref.shape[0]

    def issue(tile, slot):
        base = tile * tm

        def body(r, carry):
            pltpu.make_async_copy(yb_hbm.at[pl.ds(d0_ref[base + r], 1)],
                                  buf.at[slot, 0, pl.ds(r, 1)], sem.at[slot]).start()
            pltpu.make_async_copy(yb_hbm.at[pl.ds(d1_ref[base + r], 1)],
                                  buf.at[slot, 1, pl.ds(r, 1)], sem.at[slot]).start()
            return carry
        lax.fori_loop(0, tm, body, 0)

    @pl.when(i == 0)
    def _():
        issue(0, 0)

    @pl.when(i + 1 < n)
    def _():
        issue(i + 1, (i + 1) % 2)

    slot = i % 2
    for k in range(2):
        pltpu.make_async_copy(yb_hbm.at[pl.ds(0, tm)], buf.at[slot, k], sem.at[slot]).wait()
    ffn = gate_ref[:, 0:1] * buf[slot, 0] + gate_ref[:, 1:2] * buf[slot, 1]
    out_ref[...] = _layer_norm(alpha * h_ref[...] + ffn, g_ref[...], b_ref[...])


def _combine(dest0, dest1, h1, gate_t, yb, ln_g, ln_b, alpha):
    t, d = h1.shape
    tm = min(256, t)
    grid_spec = pltpu.PrefetchScalarGridSpec(
        num_scalar_prefetch=2,
        grid=(t // tm,),
        in_specs=[pl.BlockSpec((tm, d), lambda i, d0, d1: (i, 0)),
                  pl.BlockSpec((tm, 2), lambda i, d0, d1: (i, 0)),
                  pl.BlockSpec(memory_space=pl.ANY),
                  pl.BlockSpec((1, d), lambda i, d0, d1: (0, 0)),
                  pl.BlockSpec((1, d), lambda i, d0, d1: (0, 0))],
        out_specs=pl.BlockSpec((tm, d), lambda i, d0, d1: (i, 0)),
        scratch_shapes=[pltpu.VMEM((2, 2, tm, d), F32),
                        pltpu.SemaphoreType.DMA((2,))],
    )
    return pl.pallas_call(
        functools.partial(_combine_kernel, alpha=alpha),
        out_shape=jax.ShapeDtypeStruct((t, d), F32),
        grid_spec=grid_spec,
        compiler_params=_cparams(("arbitrary",)),
        name="combine",
    )(dest0, dest1, h1, gate_t, yb, ln_g, ln_b)


def _layer(h, w_in, pool_w, pool_scale, a_re, a_im, log_dt, b_re, b_im, c_re, c_im, d_skip,
           glu_w, glu_b, w_out, ln1_g, ln1_b, rg_w, rg_b, re_w, re_b, w_gate, w_up, w_down,
           ln2_g, ln2_b, alpha):
    bsz, seq, d = h.shape
    t = bsz * seq
    L = CHUNK
    dp = pool_w.shape[0] * pool_w.shape[1]
    ds = w_in.shape[1] - dp
    g = ds // SSM_GROUP
    n = t // L

    mt, w1t, w2t, al = _ssm_tables(a_re, a_im, log_dt, b_re, b_im, c_re, c_im, d_skip, L)
    zpad = lambda k: jnp.zeros((d, k), F32)
    rw = jnp.concatenate([rg_w.astype(F32), zpad(8 - N_EXPERT_GROUPS), re_w.astype(F32),
                          zpad(ROUTER_LANES - ROUTER_ROWS)], axis=1)
    rw_hi = rw.astype(BF16)
    rw_lo = (rw - rw_hi.astype(F32)).astype(BF16)
    rb = jnp.concatenate([rg_b.astype(F32), jnp.zeros((8 - N_EXPERT_GROUPS,), F32), re_b.astype(F32),
                          jnp.zeros((ROUTER_LANES - ROUTER_ROWS,), F32)]).reshape(1, ROUTER_LANES)

    xt = h.reshape(t, d)
    pool_p, ut = _proj(xt, w_in.astype(BF16), dp, L)
    y_pool = _pool(pool_p, pool_w.astype(BF16), pool_scale.reshape(1, dp).astype(F32), seq)
    yt = _ssm(ut.reshape(g, L * SSM_GROUP, n), mt, w1t, w2t, al, seq // L)
    y_ssm = _glu(yt.reshape(g, L, SSM_GROUP, n), glu_w.astype(BF16), glu_b.reshape(1, ds).astype(F32))
    h1, lt = _outproj(y_pool, y_ssm, xt, w_out.astype(BF16), ln1_g.reshape(1, d), ln1_b.reshape(1, d),
                      rw_hi, rw_lo, rb, alpha)
    eid, rank, gate, cnt = _router(lt)

    blk = MOE_BLK
    m = 2 * t
    n_blocks = -(-m // blk) + N_EXPERTS
    counts = cnt[:, 0]
    pcounts = (counts + blk - 1) // blk * blk
    pends = jnp.cumsum(pcounts)
    pstarts = pends - pcounts
    e_ids = jnp.arange(N_EXPERTS, dtype=I32)
    dest = jnp.sum(jnp.where(eid[..., None] == e_ids, pstarts, 0), axis=-1) + rank
    tok = jnp.broadcast_to(jnp.arange(t, dtype=I32)[None, :], (2, t))
    slot_tok = jnp.zeros((n_blocks * blk,), I32).at[dest.reshape(-1)].set(tok.reshape(-1))
    n_used = (pends[-1] // blk).astype(I32)
    bidx = jnp.minimum(jnp.arange(n_blocks, dtype=I32), n_used - 1)
    block_e = jnp.minimum(jnp.sum((pends[None, :] <= (bidx * blk)[:, None]).astype(I32), axis=1),
                          N_EXPERTS - 1)

    yb = _moe(block_e, n_used.reshape(1), slot_tok, h1, w_gate, w_up, w_down, n_blocks)
    out = _combine(dest[0], dest[1], h1, gate.T, yb, ln2_g.reshape(1, d), ln2_b.reshape(1, d), alpha)
    return out.reshape(bsz, seq, d)


def kernel(x, w_in, pool_w, pool_scale, ssm_a_re, ssm_a_im, ssm_log_dt, ssm_b_re, ssm_b_im, ssm_c_re, ssm_c_im, ssm_d, glu_w, glu_b, w_out, ln1_g, ln1_b, router_g_w, router_g_b, router_e_w, router_e_b, w_gate, w_up, w_down, ln2_g, ln2_b):
    depth = w_in.shape[0]
    alpha = (2.0 * depth) ** 0.25
    h = x
    for l in range(depth):
        h = _layer(h, w_in[l], pool_w[l], pool_scale[l], ssm_a_re[l], ssm_a_im[l], ssm_log_dt[l],
                   ssm_b_re[l], ssm_b_im[l], ssm_c_re[l], ssm_c_im[l], ssm_d[l], glu_w[l], glu_b[l],
                   w_out[l], ln1_g[l], ln1_b[l], router_g_w[l], router_g_b[l], router_e_w[l],
                   router_e_b[l], w_gate[l], w_up[l], w_down[l], ln2_g[l], ln2_b[l], alpha)
    return h
```

```python
import functools
import math

import numpy as np
import jax
import jax.numpy as jnp
from jax import lax
from jax.experimental import pallas as pl
from jax.experimental.pallas import tpu as pltpu

F32 = jnp.float32
BF16 = jnp.bfloat16
I32 = jnp.int32
U32 = jnp.uint32

POOL_WINDOWS = (2, 4, 8, 16)
POOL_GROUP = 256
SSM_GROUP = 16
SSM_STATE = 64
N_EXPERT_GROUPS = 4
EXPERTS_PER_GROUP = 8
N_EXPERTS = N_EXPERT_GROUPS * EXPERTS_PER_GROUP
LN_EPS = 1e-5

CHUNK = 16
MOE_BLK = 256
ROUTER_ROWS = 8 + N_EXPERTS
ROUTER_LANES = 128
LANES = 128
HALO = 16
VMEM_LIMIT = 56 * 1024 * 1024


def _cparams(sem, vmem=VMEM_LIMIT):
    return pltpu.CompilerParams(dimension_semantics=sem, vmem_limit_bytes=vmem)


def _proj_kernel(x_ref, w_ref, pool_ref, ut_ref, accp_ref, accs_ref, *, sub):
    kk = pl.program_id(1)
    rows, dp = accp_ref.shape
    nj = accs_ref.shape[0]
    for rb in range(rows // sub):
        sl = pl.ds(rb * sub, sub)
        part = jnp.dot(x_ref[sl, :].astype(BF16), w_ref[...], preferred_element_type=F32)

        @pl.when(kk == 0)
        def _():
            accp_ref[sl, :] = part[:, 0:dp]
            for j in range(nj):
                accs_ref[j, sl, :] = part[:, dp + j * LANES: dp + (j + 1) * LANES]

        @pl.when(kk > 0)
        def _():
            accp_ref[sl, :] += part[:, 0:dp]
            for j in range(nj):
                accs_ref[j, sl, :] += part[:, dp + j * LANES: dp + (j + 1) * LANES]

    @pl.when(kk == pl.num_programs(1) - 1)
    def _():
        pool_ref[...] = accp_ref[...].astype(BF16)
        g, L, c, r = ut_ref.shape
        gj = LANES // c
        for s in range(L):
            for j in range(nj):
                piece = accs_ref[j, pl.ds(s, r, stride=L), :]
                ut_ref[j * gj:(j + 1) * gj, s, :, :] = piece.T.astype(BF16).reshape(gj, c, r)


def _proj(xt, w_in, dp, L):
    t, d = xt.shape
    dm = w_in.shape[1]
    g = (dm - dp) // SSM_GROUP
    tm = LANES * L
    kb = 512
    return pl.pallas_call(
        functools.partial(_proj_kernel, sub=256),
        out_shape=(jax.ShapeDtypeStruct((t, dp), BF16),
                   jax.ShapeDtypeStruct((g, L, SSM_GROUP, t // L), BF16)),
        grid=(t // tm, d // kb),
        in_specs=[pl.BlockSpec((tm, kb), lambda i, k: (i, k)),
                  pl.BlockSpec((kb, dm), lambda i, k: (k, 0))],
        out_specs=(pl.BlockSpec((tm, dp), lambda i, k: (i, 0)),
                   pl.BlockSpec((g, L, SSM_GROUP, LANES), lambda i, k: (0, 0, 0, i))),
        scratch_shapes=[pltpu.VMEM((tm, dp), F32),
                        pltpu.VMEM(((dm - dp) // LANES, tm, LANES), F32)],
        compiler_params=_cparams(("arbitrary", "arbitrary")),
        name="proj",
    )(xt, w_in)


def _pool_kernel(prev_ref, main_ref, next_ref, pw_ref, sc_ref, out_ref, *, seq, ts):
    i = pl.program_id(0)
    ext = jnp.concatenate([prev_ref[...], main_ref[...], next_ref[...]], axis=0)
    k = ts + 2 * HALO
    row = lax.broadcasted_iota(I32, (ts, k), 0)
    col = lax.broadcasted_iota(I32, (ts, k), 1)
    base = (i * ts) % seq
    off = col - HALO - row
    src = base + row + off
    inside = (src >= 0) & (src < seq)
    pos = base + lax.broadcasted_iota(I32, (ts, 1), 0)
    for g, w in enumerate(POOL_WINDOWS):
        sl = slice(g * POOL_GROUP, (g + 1) * POOL_GROUP)
        band = inside & (off >= -(w // 2)) & (off < w - w // 2)
        bm = jnp.where(band, 1.0, 0.0).astype(BF16)
        sums = jnp.dot(bm, ext[:, sl], preferred_element_type=F32)
        lo = jnp.maximum(pos - w // 2, 0)
        hi = jnp.minimum(pos - w // 2 + w, seq)
        inv = 1.0 / (hi - lo).astype(F32)
        dlt = sums * inv - main_ref[:, sl].astype(F32)
        y = jnp.dot(dlt.astype(BF16), pw_ref[g], preferred_element_type=F32) * sc_ref[:, sl]
        out_ref[:, sl] = y.astype(BF16)


def _pool(pp, pool_w, pool_scale, seq):
    t, dp = pp.shape
    ts = min(256, seq)
    nh = t // HALO
    per = ts // HALO
    return pl.pallas_call(
        functools.partial(_pool_kernel, seq=seq, ts=ts),
        out_shape=jax.ShapeDtypeStruct((t, dp), BF16),
        grid=(t // ts,),
        in_specs=[pl.BlockSpec((HALO, dp), lambda i: (jnp.maximum(i * per - 1, 0), 0)),
                  pl.BlockSpec((ts, dp), lambda i: (i, 0)),
                  pl.BlockSpec((HALO, dp), lambda i: (jnp.minimum((i + 1) * per, nh - 1), 0)),
                  pl.BlockSpec(pool_w.shape, lambda i: (0, 0, 0)),
                  pl.BlockSpec((1, dp), lambda i: (0, 0))],
        out_specs=pl.BlockSpec((ts, dp), lambda i: (i, 0)),
        compiler_params=_cparams(("arbitrary",)),
        name="pool",
    )(pp, pp, pp, pool_w, pool_scale)


def _ssm_tables(a_re, a_im, log_dt, b_re, b_im, c_re, c_im, d_skip, L):
    g = a_re.shape[1]
    p = a_re.shape[2]
    c = b_re.shape[3]
    fl = L * c
    lr = a_re.astype(F32)
    li = a_im.astype(F32)
    dt = jnp.exp(log_dt.astype(F32))[..., None]
    mag = jnp.exp(lr * dt)
    abr = mag * jnp.cos(li * dt)
    abi = mag * jnp.sin(li * dt)
    den = lr * lr + li * li
    zr = ((abr - 1.0) * lr + abi * li) / den
    zi = (abi * lr - (abr - 1.0) * li) / den
    br = b_re.astype(F32)
    bi = b_im.astype(F32)
    bbr = zr[..., None] * br - zi[..., None] * bi
    bbi = zr[..., None] * bi + zi[..., None] * br
    cr = c_re.astype(F32)
    ci = c_im.astype(F32)
    kk = jnp.arange(L + 1, dtype=F32)[None, None, :, None]
    pm = jnp.exp(kk * (lr * dt)[:, :, None, :])
    ang = kk * (li * dt)[:, :, None, :]
    pr = pm * jnp.cos(ang)
    pi = pm * jnp.sin(ang)

    cpr = cr[:, :, None] * pr[:, :, :L, None, :] - ci[:, :, None] * pi[:, :, :L, None, :]
    cpi = cr[:, :, None] * pi[:, :, :L, None, :] + ci[:, :, None] * pr[:, :, :L, None, :]
    kern = jnp.einsum('dgkop,dgpc->dgkoc', cpr, bbr, precision=lax.Precision.HIGHEST) \
        - jnp.einsum('dgkop,dgpc->dgkoc', cpi, bbi, precision=lax.Precision.HIGHEST)
    lag = np.arange(L)[:, None, None] - np.arange(L)[None, :, None]
    sel_f = jnp.asarray((lag == np.arange(L)[None, None, :]).astype(np.float32))
    sel_b = jnp.asarray((-lag == np.arange(L)[None, None, :]).astype(np.float32))
    skip = jnp.eye(c, dtype=F32)[None] * d_skip.astype(F32).reshape(g, c, 1)
    kern_f = kern[0].at[:, 0].add(skip)
    m5 = jnp.einsum('tsk,gkoc->gtosc', sel_f, kern_f, precision=lax.Precision.HIGHEST) \
        + jnp.einsum('tsk,gkoc->gtosc', sel_b, kern[1], precision=lax.Precision.HIGHEST)
    mt = m5.reshape(g, fl, fl)

    def w1(d, qr, qi):
        re = qr[..., None] * bbr[d][:, None] - qi[..., None] * bbi[d][:, None]
        im = qr[..., None] * bbi[d][:, None] + qi[..., None] * bbr[d][:, None]
        to = lambda v: v.transpose(0, 2, 1, 3).reshape(g, p, fl)
        return to(re), to(im)
    f_re, f_im = w1(0, pr[0][:, L - 1::-1], pi[0][:, L - 1::-1])
    b_re_, b_im_ = w1(1, pr[1][:, :L], pi[1][:, :L])
    w1t = jnp.concatenate([f_re, f_im, b_re_, b_im_], axis=1)

    def w2(d, qr, qi):
        re = cr[d][:, None] * qr[:, :, None] - ci[d][:, None] * qi[:, :, None]
        im = -(cr[d][:, None] * qi[:, :, None] + ci[d][:, None] * qr[:, :, None])
        to = lambda v: v.reshape(g, fl, p)
        return to(re), to(im)
    rf_re, rf_im = w2(0, pr[0][:, 1:], pi[0][:, 1:])
    rb_re, rb_im = w2(1, pr[1][:, :0:-1], pi[1][:, :0:-1])
    w2t = jnp.concatenate([rf_re, rf_im, rb_re, rb_im], axis=2)

    al = jnp.concatenate([pr[0][:, L], pi[0][:, L], pr[1][:, L], pi[1][:, L]], axis=1)
    al = jnp.broadcast_to(al[:, :, None], (g, 4 * p, 128))
    return mt.astype(BF16), w1t.astype(BF16), w2t.astype(BF16), al


def _ssm_kernel(ut_ref, w1_ref, mt_ref, w2_ref, a_ref, yt_ref, *, nseg, nlev):
    p = SSM_STATE
    ut = ut_ref[0]
    n = ut.shape[1]
    st = jnp.dot(w1_ref[0], ut, preferred_element_type=F32)
    lane = lax.broadcasted_iota(I32, (p, n), 1) % nseg
    reps = n // 128
    a = a_ref[0]

    def widen(v):
        return v if reps == 1 else jnp.concatenate([v] * reps, axis=1)

    def scan(sr, si, ar, ai, forward):
        def shift(v, d):
            if forward:
                return jnp.where(lane >= d, pltpu.roll(v, d, 1), 0.0)
            return jnp.where(lane < nseg - d, pltpu.roll(v, n - d, 1), 0.0)
        xr, xi = shift(sr, 1), shift(si, 1)
        d = 1
        for _ in range(nlev):
            tr, ti = shift(xr, d), shift(xi, d)
            arw, aiw = widen(ar), widen(ai)
            xr, xi = xr + arw * tr - aiw * ti, xi + arw * ti + aiw * tr
            ar, ai = ar * ar - ai * ai, 2.0 * ar * ai
            d *= 2
        return xr, xi

    hfr, hfi = scan(st[0:p], st[p:2 * p], a[0:p], a[p:2 * p], True)
    hbr, hbi = scan(st[2 * p:3 * p], st[3 * p:4 * p], a[2 * p:3 * p], a[3 * p:4 * p], False)
    h = jnp.concatenate([hfr, hfi, hbr, hbi], axis=0).astype(BF16)
    y = jnp.dot(mt_ref[0], ut, preferred_element_type=F32) \
        + jnp.dot(w2_ref[0], h, preferred_element_type=F32)
    yt_ref[0] = y.astype(BF16)


def _ssm(ut3, mt, w1t, w2t, al, nseg):
    g, fl, n = ut3.shape
    sp = w1t.shape[1]
    nlev = max(1, math.ceil(math.log2(nseg)))
    return pl.pallas_call(
        functools.partial(_ssm_kernel, nseg=nseg, nlev=nlev),
        out_shape=jax.ShapeDtypeStruct((g, fl, n), BF16),
        grid=(g,),
        in_specs=[pl.BlockSpec((1, fl, n), lambda i: (i, 0, 0)),
                  pl.BlockSpec((1, sp, fl), lambda i: (i, 0, 0)),
                  pl.BlockSpec((1, fl, fl), lambda i: (i, 0, 0)),
                  pl.BlockSpec((1, fl, sp), lambda i: (i, 0, 0)),
                  pl.BlockSpec((1, sp, 128), lambda i: (i, 0, 0))],
        out_specs=pl.BlockSpec((1, fl, n), lambda i: (i, 0, 0)),
        compiler_params=_cparams(("arbitrary",)),
        name="ssm",
    )(ut3, w1t, mt, w2t, al)


def _glu_kernel(yt_ref, w_ref, b_ref, out_ref, nat_ref, *, sub):
    g, L, c, r = yt_ref.shape
    nj = nat_ref.shape[0]
    gj = LANES // c
    for t in range(L):
        for j in range(nj):
            piece = yt_ref[j * gj:(j + 1) * gj, t, :, :].reshape(LANES, r).astype(F32)
            nat_ref[j, pl.ds(t, r, stride=L), :] = piece.T

    def body(k, carry):
        sl = pl.ds(pl.multiple_of(k * sub, sub), sub)
        y = jnp.concatenate([nat_ref[j, sl, :] for j in range(nj)], axis=1)
        ya = jax.nn.gelu(y, approximate=True)
        z = jnp.dot(ya.astype(BF16), w_ref[...], preferred_element_type=F32) + b_ref[...]
        out_ref[sl, :] = (ya * (1.0 / (1.0 + jnp.exp(-z)))).astype(BF16)
        return carry
    lax.fori_loop(0, nat_ref.shape[1] // sub, body, 0)


def _glu(yt4, glu_w, glu_b):
    g, L, c, n = yt4.shape
    ds = g * c
    r = LANES
    return pl.pallas_call(
        functools.partial(_glu_kernel, sub=256),
        out_shape=jax.ShapeDtypeStruct((n * L, ds), BF16),
        grid=(n // r,),
        in_specs=[pl.BlockSpec((g, L, c, r), lambda j: (0, 0, 0, j)),
                  pl.BlockSpec((ds, ds), lambda j: (0, 0)),
                  pl.BlockSpec((1, ds), lambda j: (0, 0))],
        out_specs=pl.BlockSpec((r * L, ds), lambda j: (j, 0)),
        scratch_shapes=[pltpu.VMEM((ds // LANES, r * L, LANES), F32)],
        compiler_params=_cparams(("arbitrary",)),
        name="glu",
    )(yt4, glu_w, glu_b)


def _layer_norm(r, g, b):
    mu = jnp.mean(r, axis=-1, keepdims=True)
    cen = r - mu
    var = jnp.mean(cen * cen, axis=-1, keepdims=True)
    return cen * lax.rsqrt(var + LN_EPS) * g + b


def _store_row_packed(ref, m):
    rows, width = m.shape
    half = width // 2
    nc = half // LANES
    lo = lax.bitcast_convert_type(m[:, :half].astype(BF16).astype(F32), U32) >> 16
    hi = lax.bitcast_convert_type(m[:, half:].astype(BF16).astype(F32), U32) & jnp.uint32(0xFFFF0000)
    pk = lo | hi
    for c in range(nc):
        ref[pl.ds(c, rows, stride=nc), :] = pk[:, c * LANES:(c + 1) * LANES]


def _load_row_packed(ref, rows, nc):
    los, his = [], []
    for c in range(nc):
        u = ref[pl.ds(c, rows, stride=nc), :]
        los.append(lax.bitcast_convert_type(u << 16, F32))
        his.append(lax.bitcast_convert_type(u & jnp.uint32(0xFFFF0000), F32))
    return los, his


def _outproj_kernel(yp_ref, ys_ref, x_ref, wo_ref, g_ref, b_ref, rwh_ref, rwl_ref, rb_ref,
                    h_ref, hp_ref, lt_ref, *, alpha, dp):
    mix = jnp.dot(yp_ref[...], wo_ref[0:dp, :], preferred_element_type=F32) \
        + jnp.dot(ys_ref[...], wo_ref[dp:, :], preferred_element_type=F32)
    h = _layer_norm(alpha * x_ref[...] + mix, g_ref[...], b_ref[...])
    h_ref[...] = h
    _store_row_packed(hp_ref, h)
    hh = h.astype(BF16)
    hl = (h - hh.astype(F32)).astype(BF16)
    logits = jnp.dot(hh, rwh_ref[...], preferred_element_type=F32) \
        + jnp.dot(hh, rwl_ref[...], preferred_element_type=F32) \
        + jnp.dot(hl, rwh_ref[...], preferred_element_type=F32) + rb_ref[...]
    lt_ref[...] = logits.T[0:ROUTER_ROWS, :]


def _outproj(y_pool, y_ssm, xt, w_out, ln_g, ln_b, rw_hi, rw_lo, rb, alpha):
    t, d = xt.shape
    dp = y_pool.shape[1]
    tm = min(256, t)
    pr = d // 2 // LANES
    return pl.pallas_call(
        functools.partial(_outproj_kernel, alpha=alpha, dp=dp),
        out_shape=(jax.ShapeDtypeStruct((t, d), F32),
                   jax.ShapeDtypeStruct((t * pr, LANES), U32),
                   jax.ShapeDtypeStruct((ROUTER_ROWS, t), F32)),
        grid=(t // tm,),
        in_specs=[pl.BlockSpec((tm, dp), lambda i: (i, 0)),
                  pl.BlockSpec((tm, y_ssm.shape[1]), lambda i: (i, 0)),
                  pl.BlockSpec((tm, d), lambda i: (i, 0)),
                  pl.BlockSpec(w_out.shape, lambda i: (0, 0)),
                  pl.BlockSpec((1, d), lambda i: (0, 0)),
                  pl.BlockSpec((1, d), lambda i: (0, 0)),
                  pl.BlockSpec(rw_hi.shape, lambda i: (0, 0)),
                  pl.BlockSpec(rw_lo.shape, lambda i: (0, 0)),
                  pl.BlockSpec((1, ROUTER_LANES), lambda i: (0, 0))],
        out_specs=(pl.BlockSpec((tm, d), lambda i: (i, 0)),
                   pl.BlockSpec((tm * pr, LANES), lambda i: (i, 0)),
                   pl.BlockSpec((ROUTER_ROWS, tm), lambda i: (0, i))),
        compiler_params=_cparams(("arbitrary",)),
        name="outproj",
    )(y_pool, y_ssm, xt, w_out, ln_g, ln_b, rw_hi, rw_lo, rb)


def _router_kernel(lt_ref, eid_ref, rank_ref, gate_ref, cnt_ref, carry_ref):
    i = pl.program_id(0)
    ne, epg = N_EXPERTS, EXPERTS_PER_GROUP

    @pl.when(i == 0)
    def _():
        carry_ref[...] = jnp.zeros_like(carry_ref)

    lt = lt_ref[...]
    tt = lt.shape[1]
    gl = [lt[j:j + 1, :] for j in range(N_EXPERT_GROUPS)]
    gmax = jnp.maximum(jnp.maximum(gl[0], gl[1]), jnp.maximum(gl[2], gl[3]))
    grp = jnp.where(gl[0] == gmax, 0, jnp.where(gl[1] == gmax, 1, jnp.where(gl[2] == gmax, 2, 3)))
    p_grp = 1.0 / (jnp.exp(gl[0] - gmax) + jnp.exp(gl[1] - gmax)
                   + jnp.exp(gl[2] - gmax) + jnp.exp(gl[3] - gmax))
    eg = [lt[8 + epg * j: 8 + epg * (j + 1), :] for j in range(N_EXPERT_GROUPS)]
    el = jnp.where(grp == 0, eg[0], jnp.where(grp == 1, eg[1], jnp.where(grp == 2, eg[2], eg[3])))
    sub = lax.broadcasted_iota(I32, (epg, tt), 0)
    m1 = jnp.max(el, axis=0, keepdims=True)
    i1 = jnp.min(jnp.where(el == m1, sub, epg), axis=0, keepdims=True)
    rest = jnp.where(sub == i1, -jnp.inf, el)
    m2 = jnp.max(rest, axis=0, keepdims=True)
    i2 = jnp.min(jnp.where(rest == m2, sub, epg), axis=0, keepdims=True)
    r21 = jnp.exp(m2 - m1)
    g1 = p_grp / (1.0 + r21)
    g2 = g1 * r21
    e1 = grp * epg + i1
    e2 = grp * epg + i2

    rows = lax.broadcasted_iota(I32, (ne, tt), 0)
    oh1 = rows == e1
    oh2 = rows == e2
    oh = jnp.where(oh1 | oh2, 1.0, 0.0)
    tri = jnp.where(lax.broadcasted_iota(I32, (tt, tt), 0) < lax.broadcasted_iota(I32, (tt, tt), 1),
                    1.0, 0.0).astype(BF16)
    before = jnp.dot(oh.astype(BF16), tri, preferred_element_type=F32) + carry_ref[:, 0:1]
    r1 = jnp.sum(jnp.where(oh1, before, 0.0), axis=0, keepdims=True)
    r2 = jnp.sum(jnp.where(oh2, before, 0.0), axis=0, keepdims=True)
    carry_ref[...] = carry_ref[...] + jnp.sum(oh, axis=1, keepdims=True)

    eid_ref[...] = jnp.concatenate([e1, e2], axis=0)
    rank_ref[...] = jnp.concatenate([r1, r2], axis=0).astype(I32)
    gate_ref[...] = jnp.concatenate([g1, g2], axis=0)
    cnt_ref[...] = carry_ref[...].astype(I32)


def _router(lt):
    rr, t = lt.shape
    tt = min(512, t)
    return pl.pallas_call(
        _router_kernel,
        out_shape=(jax.ShapeDtypeStruct((2, t), I32),
                   jax.ShapeDtypeStruct((2, t), I32),
                   jax.ShapeDtypeStruct((2, t), F32),
                   jax.ShapeDtypeStruct((N_EXPERTS, 128), I32)),
        grid=(t // tt,),
        in_specs=[pl.BlockSpec((rr, tt), lambda i: (0, i))],
        out_specs=(pl.BlockSpec((2, tt), lambda i: (0, i)),
                   pl.BlockSpec((2, tt), lambda i: (0, i)),
                   pl.BlockSpec((2, tt), lambda i: (0, i)),
                   pl.BlockSpec((N_EXPERTS, 128), lambda i: (0, 0))),
        scratch_shapes=[pltpu.VMEM((N_EXPERTS, 128), F32)],
        compiler_params=_cparams(("arbitrary",)),
        name="router",
    )(lt)


def _invert_kernel(dest_ref, out_ref):
    n_slots = out_ref.shape[0]
    t = dest_ref.shape[0] // 2

    def clear(i, carry):
        out_ref[i] = 0
        return carry
    lax.fori_loop(0, n_slots, clear, 0, unroll=8)

    def place(i, carry):
        out_ref[dest_ref[i]] = i
        out_ref[dest_ref[t + i]] = i
        return carry
    lax.fori_loop(0, t, place, 0, unroll=8)


def _invert(dest_flat, n_slots):
    return pl.pallas_call(
        _invert_kernel,
        out_shape=jax.ShapeDtypeStruct((n_slots,), I32),
        grid_spec=pltpu.PrefetchScalarGridSpec(
            num_scalar_prefetch=1, grid=(1,), in_specs=[],
            out_specs=pl.BlockSpec(memory_space=pltpu.SMEM)),
        compiler_params=_cparams(("arbitrary",)),
        name="invert",
    )(dest_flat)


def _moe_kernel(be_ref, nu_ref, pc_ref, st_ref, hp_hbm, wg_hbm, wu_hbm, wd_hbm, y_ref,
                gbuf, wgf, wuf, wdf, wgb, wub, wdb, par_ref, gsem, wsem):
    b = pl.program_id(0)
    nu = nu_ref[0]
    pr = gbuf.shape[1] // MOE_BLK
    blk = MOE_BLK

    def row_copy(tok, slot, r):
        src = hp_hbm.at[pl.ds(pl.multiple_of(tok * pr, pr), pr)]
        return pltpu.make_async_copy(src, gbuf.at[slot, pl.ds(pl.multiple_of(r * pr, pr), pr)], gsem.at[slot])

    def issue(block, slot):
        base = block * blk

        def body(r, carry):
            row_copy(st_ref[base + r], slot, r).start()
            return carry
        lax.fori_loop(0, blk, body, 0, unroll=8)

    def weight_copies(e, p):
        return (pltpu.make_async_copy(wg_hbm.at[e], wgf.at[p], wsem.at[p]),
                pltpu.make_async_copy(wu_hbm.at[e], wuf.at[p], wsem.at[p]),
                pltpu.make_async_copy(wd_hbm.at[e], wdf.at[p], wsem.at[p]))

    @pl.when(b == 0)
    def _():
        par_ref[0] = 0
        for cp in weight_copies(be_ref[0], 0):
            cp.start()
        issue(0, 0)

    @pl.when(b + 1 < nu)
    def _():
        issue(b + 1, (b + 1) % 2)

    @pl.when(b < nu)
    def _():
        slot = b % 2
        e = be_ref[b]
        first = (b == 0) | (e != be_ref[jnp.maximum(b - 1, 0)])

        @pl.when(first)
        def _():
            p = par_ref[0]
            for cp in weight_copies(e, p):
                cp.wait()
            nxt = lax.while_loop(
                lambda c: (c < N_EXPERTS) & (pc_ref[jnp.minimum(c, N_EXPERTS - 1)] == 0),
                lambda c: c + 1, e + 1)

            @pl.when(nxt < N_EXPERTS)
            def _():
                for cp in weight_copies(nxt, 1 - p):
                    cp.start()
            wgb[...] = wgf[p].astype(BF16)
            wub[...] = wuf[p].astype(BF16)
            wdb[...] = wdf[p].astype(BF16)
            par_ref[0] = 1 - p

        pltpu.make_async_copy(hp_hbm.at[pl.ds(0, blk * pr)], gbuf.at[slot], gsem.at[slot]).wait()
        los, his = _load_row_packed(gbuf.at[slot], blk, pr)
        x = jnp.concatenate([v.astype(BF16) for v in los + his], axis=1)
        hg = jnp.dot(x, wgb[...], preferred_element_type=F32)
        hu = jnp.dot(x, wub[...], preferred_element_type=F32)
        hh = hg * (1.0 / (1.0 + jnp.exp(-hg))) * hu
        _store_row_packed(y_ref, jnp.dot(hh.astype(BF16), wdb[...], preferred_element_type=F32))

    @pl.when(b >= nu)
    def _():
        y_ref[...] = jnp.zeros_like(y_ref)


def _moe(block_e, n_used, pcounts, slot_tok, hp, w_gate, w_up, w_down, n_blocks):
    ne, d, de = w_gate.shape
    blk = MOE_BLK
    pr = d // 2 // LANES
    grid_spec = pltpu.PrefetchScalarGridSpec(
        num_scalar_prefetch=4,
        grid=(n_blocks,),
        in_specs=[pl.BlockSpec(memory_space=pl.ANY)] * 4,
        out_specs=pl.BlockSpec((blk * pr, LANES), lambda b, *_: (b, 0)),
        scratch_shapes=[pltpu.VMEM((2, blk * pr, LANES), U32),
                        pltpu.VMEM((2, d, de), F32),
                        pltpu.VMEM((2, d, de), F32),
                        pltpu.VMEM((2, de, d), F32),
                        pltpu.VMEM((d, de), BF16),
                        pltpu.VMEM((d, de), BF16),
                        pltpu.VMEM((de, d), BF16),
                        pltpu.SMEM((1,), I32),
                        pltpu.SemaphoreType.DMA((2,)),
                        pltpu.SemaphoreType.DMA((2,))],
    )
    return pl.pallas_call(
        _moe_kernel,
        out_shape=jax.ShapeDtypeStruct((n_blocks * blk * pr, LANES), U32),
        grid_spec=grid_spec,
        compiler_params=_cparams(("arbitrary",)),
        name="moe",
    )(block_e, n_used, pcounts, slot_tok, hp, w_gate, w_up, w_down)


def _combine_kernel(d0_ref, d1_ref, h_ref, gate_ref, yb_hbm, g_ref, b_ref, out_ref, buf, sem, *, alpha):
    i = pl.program_id(0)
    n = pl.num_programs(0)
    tm = h_ref.shape[0]
    pr = buf.shape[2] // tm

    def issue(tile, slot):
        base = tile * tm

        def body(r, carry):
            dst = pl.ds(pl.multiple_of(r * pr, pr), pr)
            for k, dref in enumerate((d0_ref, d1_ref)):
                src = yb_hbm.at[pl.ds(pl.multiple_of(dref[base + r] * pr, pr), pr)]
                pltpu.make_async_copy(src, buf.at[slot, k, dst], sem.at[slot]).start()
            return carry
        lax.fori_loop(0, tm, body, 0, unroll=8)

    @pl.when(i == 0)
    def _():
        issue(0, 0)

    @pl.when(i + 1 < n)
    def _():
        issue(i + 1, (i + 1) % 2)

    slot = i % 2
    for k in range(2):
        pltpu.make_async_copy(yb_hbm.at[pl.ds(0, tm * pr)], buf.at[slot, k], sem.at[slot]).wait()
    lo0, hi0 = _load_row_packed(buf.at[slot, 0], tm, pr)
    lo1, hi1 = _load_row_packed(buf.at[slot, 1], tm, pr)
    g0 = gate_ref[:, 0:1]
    g1 = gate_ref[:, 1:2]
    ffn = jnp.concatenate([g0 * a + g1 * c for a, c in zip(lo0 + hi0, lo1 + hi1)], axis=1)
    out_ref[...] = _layer_norm(alpha * h_ref[...] + ffn, g_ref[...], b_ref[...])


def _combine(dest0, dest1, h1, gate_t, yb, ln_g, ln_b, alpha):
    t, d = h1.shape
    tm = min(256, t)
    grid_spec = pltpu.PrefetchScalarGridSpec(
        num_scalar_prefetch=2,
        grid=(t // tm,),
        in_specs=[pl.BlockSpec((tm, d), lambda i, d0, d1: (i, 0)),
                  pl.BlockSpec((tm, 2), lambda i, d0, d1: (i, 0)),
                  pl.BlockSpec(memory_space=pl.ANY),
                  pl.BlockSpec((1, d), lambda i, d0, d1: (0, 0)),
                  pl.BlockSpec((1, d), lambda i, d0, d1: (0, 0))],
        out_specs=pl.BlockSpec((tm, d), lambda i, d0, d1: (i, 0)),
        scratch_shapes=[pltpu.VMEM((2, 2, tm * (d // 2 // LANES), LANES), U32),
                        pltpu.SemaphoreType.DMA((2,))],
    )
    return pl.pallas_call(
        functools.partial(_combine_kernel, alpha=alpha),
        out_shape=jax.ShapeDtypeStruct((t, d), F32),
        grid_spec=grid_spec,
        compiler_params=_cparams(("arbitrary",)),
        name="combine",
    )(dest0, dest1, h1, gate_t, yb, ln_g, ln_b)


def _layer(h, w_in, pool_w, pool_scale, a_re, a_im, log_dt, b_re, b_im, c_re, c_im, d_skip,
           glu_w, glu_b, w_out, ln1_g, ln1_b, rg_w, rg_b, re_w, re_b, w_gate, w_up, w_down,
           ln2_g, ln2_b, alpha):
    bsz, seq, d = h.shape
    t = bsz * seq
    L = CHUNK
    dp = pool_w.shape[0] * pool_w.shape[1]
    ds = w_in.shape[1] - dp
    g = ds // SSM_GROUP
    n = t // L

    mt, w1t, w2t, al = _ssm_tables(a_re, a_im, log_dt, b_re, b_im, c_re, c_im, d_skip, L)
    zpad = lambda k: jnp.zeros((d, k), F32)
    rw = jnp.concatenate([rg_w.astype(F32), zpad(8 - N_EXPERT_GROUPS), re_w.astype(F32),
                          zpad(ROUTER_LANES - ROUTER_ROWS)], axis=1)
    rw_hi = rw.astype(BF16)
    rw_lo = (rw - rw_hi.astype(F32)).astype(BF16)
    rb = jnp.concatenate([rg_b.astype(F32), jnp.zeros((8 - N_EXPERT_GROUPS,), F32), re_b.astype(F32),
                          jnp.zeros((ROUTER_LANES - ROUTER_ROWS,), F32)]).reshape(1, ROUTER_LANES)

    xt = h.reshape(t, d)
    pool_p, ut = _proj(xt, w_in.astype(BF16), dp, L)
    y_pool = _pool(pool_p, pool_w.astype(BF16), pool_scale.reshape(1, dp).astype(F32), seq)
    yt = _ssm(ut.reshape(g, L * SSM_GROUP, n), mt, w1t, w2t, al, seq // L)
    y_ssm = _glu(yt.reshape(g, L, SSM_GROUP, n), glu_w.astype(BF16), glu_b.reshape(1, ds).astype(F32))
    h1, hp, lt = _outproj(y_pool, y_ssm, xt, w_out.astype(BF16), ln1_g.reshape(1, d), ln1_b.reshape(1, d),
                          rw_hi, rw_lo, rb, alpha)
    eid, rank, gate, cnt = _router(lt)

    blk = MOE_BLK
    m = 2 * t
    n_blocks = -(-m // blk) + N_EXPERTS
    counts = cnt[:, 0]
    pcounts = (counts + blk - 1) // blk * blk
    pends = jnp.cumsum(pcounts)
    pstarts = pends - pcounts
    e_ids = jnp.arange(N_EXPERTS, dtype=I32)
    dest = jnp.sum(jnp.where(eid[..., None] == e_ids, pstarts, 0), axis=-1) + rank
    slot_tok = _invert(dest.reshape(-1), n_blocks * blk)
    n_used = (pends[-1] // blk).astype(I32)
    bidx = jnp.minimum(jnp.arange(n_blocks, dtype=I32), n_used - 1)
    block_e = jnp.minimum(jnp.sum((pends[None, :] <= (bidx * blk)[:, None]).astype(I32), axis=1),
                          N_EXPERTS - 1)

    yb = _moe(block_e, n_used.reshape(1), pcounts, slot_tok, hp, w_gate, w_up, w_down, n_blocks)
    out = _combine(dest[0], dest[1], h1, gate.T, yb, ln2_g.reshape(1, d), ln2_b.reshape(1, d), alpha)
    return out.reshape(bsz, seq, d)


def kernel(x, w_in, pool_w, pool_scale, ssm_a_re, ssm_a_im, ssm_log_dt, ssm_b_re, ssm_b_im, ssm_c_re, ssm_c_im, ssm_d, glu_w, glu_b, w_out, ln1_g, ln1_b, router_g_w, router_g_b, router_e_w, router_e_b, w_gate, w_up, w_down, ln2_g, ln2_b):
    depth = w_in.shape[0]
    alpha = (2.0 * depth) ** 0.25
    h = x
    for l in range(depth):
        h = _layer(h, w_in[l], pool_w[l], pool_scale[l], ssm_a_re[l], ssm_a_im[l], ssm_log_dt[l],
                   ssm_b_re[l], ssm_b_im[l], ssm_c_re[l], ssm_c_im[l], ssm_d[l], glu_w[l], glu_b[l],
                   w_out[l], ln1_g[l], ln1_b[l], router_g_w[l], router_g_b[l], router_e_w[l],
                   router_e_b[l], w_gate[l], w_up[l], w_down[l], ln2_g[l], ln2_b[l], alpha)
    return h
```

```python
import functools
import math

import numpy as np
import jax
import jax.numpy as jnp
from jax import lax
from jax.experimental import pallas as pl
from jax.experimental.pallas import tpu as pltpu

F32 = jnp.float32
BF16 = jnp.bfloat16
I32 = jnp.int32
U32 = jnp.uint32

POOL_WINDOWS = (2, 4, 8, 16)
POOL_GROUP = 256
SSM_GROUP = 16
SSM_STATE = 64
N_EXPERT_GROUPS = 4
EXPERTS_PER_GROUP = 8
N_EXPERTS = N_EXPERT_GROUPS * EXPERTS_PER_GROUP
LN_EPS = 1e-5

CHUNK = 16
MOE_BLK = 256
ROUTER_ROWS = 8 + N_EXPERTS
ROUTER_LANES = 128
LANES = 128
MXU_N = 256
HALO = 16
VMEM_LIMIT = 56 * 1024 * 1024


def _cparams(sem, vmem=VMEM_LIMIT):
    return pltpu.CompilerParams(dimension_semantics=sem, vmem_limit_bytes=vmem)


def _proj_kernel(x_ref, w_ref, pool_ref, ut_ref, accp_ref, accs_ref):
    kk = pl.program_id(1)
    rows, dp = accp_ref.shape
    nj = accs_ref.shape[0]

    @pl.when(kk == 0)
    def _():
        accp_ref[...] = jnp.zeros_like(accp_ref)
        accs_ref[...] = jnp.zeros_like(accs_ref)

    xb = x_ref[...].astype(BF16)
    for n in range(dp // MXU_N):
        cs = slice(n * MXU_N, (n + 1) * MXU_N)
        accp_ref[:, cs] += jnp.dot(xb, w_ref[:, cs], preferred_element_type=F32)
    per = MXU_N // LANES
    for n in range(nj // per):
        part = jnp.dot(xb, w_ref[:, dp + n * MXU_N: dp + (n + 1) * MXU_N], preferred_element_type=F32)
        for q in range(per):
            accs_ref[n * per + q] += part[:, q * LANES:(q + 1) * LANES]

    @pl.when(kk == pl.num_programs(1) - 1)
    def _():
        pool_ref[...] = accp_ref[...].astype(BF16)
        g, L, c, r = ut_ref.shape
        gj = LANES // c
        for s in range(L):
            for j in range(nj):
                piece = accs_ref[j, pl.ds(s, r, stride=L), :]
                ut_ref[j * gj:(j + 1) * gj, s, :, :] = piece.T.astype(BF16).reshape(gj, c, r)


def _proj(xt, w_in, dp, L):
    t, d = xt.shape
    dm = w_in.shape[1]
    g = (dm - dp) // SSM_GROUP
    tm = LANES * L
    kb = 512
    return pl.pallas_call(
        _proj_kernel,
        out_shape=(jax.ShapeDtypeStruct((t, dp), BF16),
                   jax.ShapeDtypeStruct((g, L, SSM_GROUP, t // L), BF16)),
        grid=(t // tm, d // kb),
        in_specs=[pl.BlockSpec((tm, kb), lambda i, k: (i, k)),
                  pl.BlockSpec((kb, dm), lambda i, k: (k, 0))],
        out_specs=(pl.BlockSpec((tm, dp), lambda i, k: (i, 0)),
                   pl.BlockSpec((g, L, SSM_GROUP, LANES), lambda i, k: (0, 0, 0, i))),
        scratch_shapes=[pltpu.VMEM((tm, dp), F32),
                        pltpu.VMEM(((dm - dp) // LANES, tm, LANES), F32)],
        compiler_params=_cparams(("arbitrary", "arbitrary")),
        name="proj",
    )(xt, w_in)


def _pool_kernel(prev_ref, main_ref, next_ref, pw_ref, sc_ref, out_ref, *, seq, ts):
    i = pl.program_id(0)
    ext = jnp.concatenate([prev_ref[...], main_ref[...], next_ref[...]], axis=0)
    k = ts + 2 * HALO
    row = lax.broadcasted_iota(I32, (ts, k), 0)
    col = lax.broadcasted_iota(I32, (ts, k), 1)
    base = (i * ts) % seq
    off = col - HALO - row
    src = base + row + off
    inside = (src >= 0) & (src < seq)
    pos = base + lax.broadcasted_iota(I32, (ts, 1), 0)
    for g, w in enumerate(POOL_WINDOWS):
        sl = slice(g * POOL_GROUP, (g + 1) * POOL_GROUP)
        band = inside & (off >= -(w // 2)) & (off < w - w // 2)
        bm = jnp.where(band, 1.0, 0.0).astype(BF16)
        sums = jnp.dot(bm, ext[:, sl], preferred_element_type=F32)
        lo = jnp.maximum(pos - w // 2, 0)
        hi = jnp.minimum(pos - w // 2 + w, seq)
        inv = 1.0 / (hi - lo).astype(F32)
        dlt = sums * inv - main_ref[:, sl].astype(F32)
        y = jnp.dot(dlt.astype(BF16), pw_ref[g], preferred_element_type=F32) * sc_ref[:, sl]
        out_ref[:, sl] = y.astype(BF16)


def _pool(pp, pool_w, pool_scale, seq):
    t, dp = pp.shape
    ts = min(256, seq)
    nh = t // HALO
    per = ts // HALO
    return pl.pallas_call(
        functools.partial(_pool_kernel, seq=seq, ts=ts),
        out_shape=jax.ShapeDtypeStruct((t, dp), BF16),
        grid=(t // ts,),
        in_specs=[pl.BlockSpec((HALO, dp), lambda i: (jnp.maximum(i * per - 1, 0), 0)),
                  pl.BlockSpec((ts, dp), lambda i: (i, 0)),
                  pl.BlockSpec((HALO, dp), lambda i: (jnp.minimum((i + 1) * per, nh - 1), 0)),
                  pl.BlockSpec(pool_w.shape, lambda i: (0, 0, 0)),
                  pl.BlockSpec((1, dp), lambda i: (0, 0))],
        out_specs=pl.BlockSpec((ts, dp), lambda i: (i, 0)),
        compiler_params=_cparams(("arbitrary",)),
        name="pool",
    )(pp, pp, pp, pool_w, pool_scale)


def _ssm_tables(a_re, a_im, log_dt, b_re, b_im, c_re, c_im, d_skip, L):
    g = a_re.shape[1]
    p = a_re.shape[2]
    c = b_re.shape[3]
    fl = L * c
    lr = a_re.astype(F32)
    li = a_im.astype(F32)
    dt = jnp.exp(log_dt.astype(F32))[..., None]
    mag = jnp.exp(lr * dt)
    abr = mag * jnp.cos(li * dt)
    abi = mag * jnp.sin(li * dt)
    den = lr * lr + li * li
    zr = ((abr - 1.0) * lr + abi * li) / den
    zi = (abi * lr - (abr - 1.0) * li) / den
    br = b_re.astype(F32)
    bi = b_im.astype(F32)
    bbr = zr[..., None] * br - zi[..., None] * bi
    bbi = zr[..., None] * bi + zi[..., None] * br
    cr = c_re.astype(F32)
    ci = c_im.astype(F32)
    kk = jnp.arange(L + 1, dtype=F32)[None, None, :, None]
    pm = jnp.exp(kk * (lr * dt)[:, :, None, :])
    ang = kk * (li * dt)[:, :, None, :]
    pr = pm * jnp.cos(ang)
    pi = pm * jnp.sin(ang)

    cpr = cr[:, :, None] * pr[:, :, :L, None, :] - ci[:, :, None] * pi[:, :, :L, None, :]
    cpi = cr[:, :, None] * pi[:, :, :L, None, :] + ci[:, :, None] * pr[:, :, :L, None, :]
    kern = jnp.einsum('dgkop,dgpc->dgkoc', cpr, bbr, precision=lax.Precision.HIGHEST) \
        - jnp.einsum('dgkop,dgpc->dgkoc', cpi, bbi, precision=lax.Precision.HIGHEST)
    skip = jnp.eye(c, dtype=F32)[None] * d_skip.astype(F32).reshape(g, c, 1)
    lags = jnp.concatenate([kern[0][:, :0:-1], (kern[0][:, 0] + kern[1][:, 0] + skip)[:, None],
                            kern[1][:, 1:]], axis=1)
    kk = lags.transpose(0, 2, 1, 3).reshape(g, c, (2 * L - 1) * c)
    kk = jnp.pad(kk, ((0, 0), (0, 0), (0, 2 * fl - (2 * L - 1) * c)))

    def w1(d, qr, qi):
        re = qr[..., None] * bbr[d][:, None] - qi[..., None] * bbi[d][:, None]
        im = qr[..., None] * bbi[d][:, None] + qi[..., None] * bbr[d][:, None]
        to = lambda v: v.transpose(0, 2, 1, 3).reshape(g, p, fl)
        return to(re), to(im)
    f_re, f_im = w1(0, pr[0][:, L - 1::-1], pi[0][:, L - 1::-1])
    b_re_, b_im_ = w1(1, pr[1][:, :L], pi[1][:, :L])
    w1t = jnp.concatenate([f_re, f_im, b_re_, b_im_], axis=1)

    def w2(d, qr, qi):
        re = cr[d][:, None] * qr[:, :, None] - ci[d][:, None] * qi[:, :, None]
        im = -(cr[d][:, None] * qi[:, :, None] + ci[d][:, None] * qr[:, :, None])
        to = lambda v: v.reshape(g, fl, p)
        return to(re), to(im)
    rf_re, rf_im = w2(0, pr[0][:, 1:], pi[0][:, 1:])
    rb_re, rb_im = w2(1, pr[1][:, :0:-1], pi[1][:, :0:-1])
    w2t = jnp.concatenate([rf_re, rf_im, rb_re, rb_im], axis=2)

    a4 = jnp.stack([pr[0][:, L], pi[0][:, L], pr[1][:, L], pi[1][:, L]], axis=1)
    ap = a4.reshape(g // 2, 2, 4, p).transpose(0, 2, 1, 3).reshape(g // 2, 4, 2 * p)
    ap = jnp.pad(ap, ((0, 0), (0, 4), (0, 0)))
    return kk.astype(BF16), w1t.astype(BF16), w2t.astype(BF16), ap


def _cmul(ar, ai, br, bi):
    return ar * br - ai * bi, ar * bi + ai * br


def _ssm_kernel(ut_ref, w1_ref, kk_ref, w2_ref, a_ref, yt_ref, mt_s, zt_s, ht_s, *, nseg, L):
    p = SSM_STATE
    c = SSM_GROUP
    fl, n = ut_ref.shape[1:]
    nb = n // nseg
    nrow = nseg // 8
    st = [jnp.dot(w1_ref[q], ut_ref[q], preferred_element_type=F32) for q in range(2)]
    z = jnp.concatenate([st[q][k * p:(k + 1) * p] for k in range(4) for q in range(2)], axis=0)
    zt_s[...] = z.T
    for q in range(2):
        for t in range(L):
            mt_s[q, t * c:(t + 1) * c, :] = kk_ref[q, :, (L - 1 - t) * c:(L - 1 - t) * c + fl]

    sub = lax.broadcasted_iota(I32, (8, LANES), 0)
    full = lambda v: jnp.broadcast_to(v, (8, LANES))
    one, zero = jnp.ones((8, LANES), F32), jnp.zeros((8, LANES), F32)

    def powers(row):
        out = [(full(a_ref[0, row:row + 1, :]), full(a_ref[0, row + 1:row + 2, :]))]
        for _ in range(3):
            out.append(_cmul(*out[-1], *out[-1]))
        return out

    def by_bits(pw, idx):
        qr, qi = one, zero
        for bit in range(3):
            on = (idx >> bit) & 1 == 1
            qr, qi = _cmul(qr, qi, jnp.where(on, pw[bit][0], one), jnp.where(on, pw[bit][1], zero))
        return qr, qi

    pw_f, pw_b = powers(0), powers(2)
    qf = by_bits(pw_f, sub)
    qb = by_bits(pw_b, 7 - sub)

    def local(xr, xi, pw, forward):
        for lvl, d in enumerate((1, 2, 4)):
            keep = (sub >= d) if forward else (sub < 8 - d)
            sh = d if forward else 8 - d
            tr = jnp.where(keep, pltpu.roll(xr, sh, 0), 0.0)
            ti = jnp.where(keep, pltpu.roll(xi, sh, 0), 0.0)
            mr, mi = _cmul(pw[lvl][0], pw[lvl][1], tr, ti)
            xr, xi = xr + mr, xi + mi
        return xr, xi

    def step(i, carry):
        new = []
        for b in range(nb):
            for fwd in (True, False):
                cr, ci = carry[2 * (2 * b + (0 if fwd else 1)):][:2]
                row = b * nseg + (i if fwd else nrow - 1 - i) * 8
                rows = pl.ds(pl.multiple_of(row, 8), 8)
                l0 = 0 if fwd else 2 * LANES
                pw, (qr, qi) = (pw_f, qf) if fwd else (pw_b, qb)
                lr, li = local(zt_s[rows, l0:l0 + LANES], zt_s[rows, l0 + LANES:l0 + 2 * LANES], pw, fwd)
                keep = (sub >= 1) if fwd else (sub < 7)
                er = jnp.where(keep, pltpu.roll(lr, 1 if fwd else 7, 0), 0.0)
                ei = jnp.where(keep, pltpu.roll(li, 1 if fwd else 7, 0), 0.0)
                mr, mi = _cmul(qr, qi, cr, ci)
                ht_s[rows, l0:l0 + LANES] = er + mr
                ht_s[rows, l0 + LANES:l0 + 2 * LANES] = ei + mi
                edge = 7 if fwd else 0
                mr, mi = _cmul(pw[3][0], pw[3][1], cr, ci)
                new += [full(lr[edge:edge + 1, :]) + mr, full(li[edge:edge + 1, :]) + mi]
        return tuple(new)

    lax.fori_loop(0, nrow, step, (zero,) * (4 * nb))

    h = ht_s[...].T
    for q in range(2):
        hq = jnp.concatenate([h[(2 * k + q) * p:(2 * k + q + 1) * p] for k in range(4)], axis=0)
        y = jnp.dot(mt_s[q], ut_ref[q], preferred_element_type=F32) \
            + jnp.dot(w2_ref[q], hq.astype(BF16), preferred_element_type=F32)
        yt_ref[q] = y.astype(BF16)


def _ssm(ut3, kk, w1t, w2t, ap, nseg, L):
    g, fl, n = ut3.shape
    sp = w1t.shape[1]
    return pl.pallas_call(
        functools.partial(_ssm_kernel, nseg=nseg, L=L),
        out_shape=jax.ShapeDtypeStruct((g, fl, n), BF16),
        grid=(g // 2,),
        in_specs=[pl.BlockSpec((2, fl, n), lambda i: (i, 0, 0)),
                  pl.BlockSpec((2, sp, fl), lambda i: (i, 0, 0)),
                  pl.BlockSpec((2,) + kk.shape[1:], lambda i: (i, 0, 0)),
                  pl.BlockSpec((2, fl, sp), lambda i: (i, 0, 0)),
                  pl.BlockSpec((1,) + ap.shape[1:], lambda i: (i, 0, 0))],
        out_specs=pl.BlockSpec((2, fl, n), lambda i: (i, 0, 0)),
        scratch_shapes=[pltpu.VMEM((2, fl, fl), BF16),
                        pltpu.VMEM((n, 2 * sp), F32),
                        pltpu.VMEM((n, 2 * sp), F32)],
        compiler_params=_cparams(("arbitrary",)),
        name="ssm",
    )(ut3, w1t, kk, w2t, ap)


def _glu_kernel(yt_ref, w_ref, b_ref, out_ref, nat_ref, *, sub):
    g, L, c, r = yt_ref.shape
    nj = nat_ref.shape[0]
    gj = LANES // c
    for t in range(L):
        for j in range(nj):
            piece = yt_ref[j * gj:(j + 1) * gj, t, :, :].reshape(LANES, r).astype(F32)
            nat_ref[j, pl.ds(t, r, stride=L), :] = piece.T

    def body(k, carry):
        sl = pl.ds(pl.multiple_of(k * sub, sub), sub)
        y = jnp.concatenate([nat_ref[j, sl, :] for j in range(nj)], axis=1)
        ya = jax.nn.gelu(y, approximate=True)
        z = jnp.dot(ya.astype(BF16), w_ref[...], preferred_element_type=F32) + b_ref[...]
        out_ref[sl, :] = (ya * (1.0 / (1.0 + jnp.exp(-z)))).astype(BF16)
        return carry
    lax.fori_loop(0, nat_ref.shape[1] // sub, body, 0)


def _glu(yt4, glu_w, glu_b):
    g, L, c, n = yt4.shape
    ds = g * c
    r = LANES
    return pl.pallas_call(
        functools.partial(_glu_kernel, sub=512),
        out_shape=jax.ShapeDtypeStruct((n * L, ds), BF16),
        grid=(n // r,),
        in_specs=[pl.BlockSpec((g, L, c, r), lambda j: (0, 0, 0, j)),
                  pl.BlockSpec((ds, ds), lambda j: (0, 0)),
                  pl.BlockSpec((1, ds), lambda j: (0, 0))],
        out_specs=pl.BlockSpec((r * L, ds), lambda j: (j, 0)),
        scratch_shapes=[pltpu.VMEM((ds // LANES, r * L, LANES), F32)],
        compiler_params=_cparams(("arbitrary",)),
        name="glu",
    )(yt4, glu_w, glu_b)


def _layer_norm(r, g, b):
    mu = jnp.mean(r, axis=-1, keepdims=True)
    cen = r - mu
    var = jnp.mean(cen * cen, axis=-1, keepdims=True)
    return cen * lax.rsqrt(var + LN_EPS) * g + b


def _store_row_packed(ref, m, row0=0):
    rows, width = m.shape
    half = width // 2
    nc = half // LANES
    lo = lax.bitcast_convert_type(m[:, :half].astype(BF16).astype(F32), U32) >> 16
    hi = lax.bitcast_convert_type(m[:, half:].astype(BF16).astype(F32), U32) & jnp.uint32(0xFFFF0000)
    pk = lo | hi
    for c in range(nc):
        ref[pl.ds(row0 * nc + c, rows, stride=nc), :] = pk[:, c * LANES:(c + 1) * LANES]


def _load_row_packed(ref, rows, nc):
    los, his = [], []
    for c in range(nc):
        u = ref[pl.ds(c, rows, stride=nc), :]
        los.append(lax.bitcast_convert_type(u << 16, F32))
        his.append(lax.bitcast_convert_type(u & jnp.uint32(0xFFFF0000), F32))
    return los, his


def _outproj_kernel(yp_ref, ys_ref, x_ref, wo_ref, g_ref, b_ref, rwh_ref, rwl_ref, rb_ref,
                    h_ref, hp_ref, lt_ref, *, alpha, dp, sub):
    for sb in range(h_ref.shape[0] // sub):
        sl = pl.ds(sb * sub, sub)
        mix = jnp.dot(yp_ref[sl, :], wo_ref[0:dp, :], preferred_element_type=F32) \
            + jnp.dot(ys_ref[sl, :], wo_ref[dp:, :], preferred_element_type=F32)
        h = _layer_norm(alpha * x_ref[sl, :] + mix, g_ref[...], b_ref[...])
        h_ref[sl, :] = h
        _store_row_packed(hp_ref, h, row0=sb * sub)
        hh = h.astype(BF16)
        hl = (h - hh.astype(F32)).astype(BF16)
        logits = jnp.dot(hh, rwh_ref[...], preferred_element_type=F32) \
            + jnp.dot(hh, rwl_ref[...], preferred_element_type=F32) \
            + jnp.dot(hl, rwh_ref[...], preferred_element_type=F32) + rb_ref[...]
        lt_ref[:, sb * sub:(sb + 1) * sub] = logits.T[0:ROUTER_ROWS, :]


def _outproj(y_pool, y_ssm, xt, w_out, ln_g, ln_b, rw_hi, rw_lo, rb, alpha):
    t, d = xt.shape
    dp = y_pool.shape[1]
    tm = min(512, t)
    pr = d // 2 // LANES
    return pl.pallas_call(
        functools.partial(_outproj_kernel, alpha=alpha, dp=dp, sub=256),
        out_shape=(jax.ShapeDtypeStruct((t, d), F32),
                   jax.ShapeDtypeStruct((t * pr, LANES), U32),
                   jax.ShapeDtypeStruct((ROUTER_ROWS, t), F32)),
        grid=(t // tm,),
        in_specs=[pl.BlockSpec((tm, dp), lambda i: (i, 0)),
                  pl.BlockSpec((tm, y_ssm.shape[1]), lambda i: (i, 0)),
                  pl.BlockSpec((tm, d), lambda i: (i, 0)),
                  pl.BlockSpec(w_out.shape, lambda i: (0, 0)),
                  pl.BlockSpec((1, d), lambda i: (0, 0)),
                  pl.BlockSpec((1, d), lambda i: (0, 0)),
                  pl.BlockSpec(rw_hi.shape, lambda i: (0, 0)),
                  pl.BlockSpec(rw_lo.shape, lambda i: (0, 0)),
                  pl.BlockSpec((1, ROUTER_LANES), lambda i: (0, 0))],
        out_specs=(pl.BlockSpec((tm, d), lambda i: (i, 0)),
                   pl.BlockSpec((tm * pr, LANES), lambda i: (i, 0)),
                   pl.BlockSpec((ROUTER_ROWS, tm), lambda i: (0, i))),
        compiler_params=_cparams(("arbitrary",)),
        name="outproj",
    )(y_pool, y_ssm, xt, w_out, ln_g, ln_b, rw_hi, rw_lo, rb)


def _router_kernel(lt_ref, eid_ref, rank_ref, gate_ref, cnt_ref, carry_ref):
    i = pl.program_id(0)
    ne, epg = N_EXPERTS, EXPERTS_PER_GROUP

    @pl.when(i == 0)
    def _():
        carry_ref[...] = jnp.zeros_like(carry_ref)

    lt = lt_ref[...]
    tt = lt.shape[1]
    gl = [lt[j:j + 1, :] for j in range(N_EXPERT_GROUPS)]
    gmax = jnp.maximum(jnp.maximum(gl[0], gl[1]), jnp.maximum(gl[2], gl[3]))
    grp = jnp.where(gl[0] == gmax, 0, jnp.where(gl[1] == gmax, 1, jnp.where(gl[2] == gmax, 2, 3)))
    p_grp = 1.0 / (jnp.exp(gl[0] - gmax) + jnp.exp(gl[1] - gmax)
                   + jnp.exp(gl[2] - gmax) + jnp.exp(gl[3] - gmax))
    eg = [lt[8 + epg * j: 8 + epg * (j + 1), :] for j in range(N_EXPERT_GROUPS)]
    el = jnp.where(grp == 0, eg[0], jnp.where(grp == 1, eg[1], jnp.where(grp == 2, eg[2], eg[3])))
    sub = lax.broadcasted_iota(I32, (epg, tt), 0)
    m1 = jnp.max(el, axis=0, keepdims=True)
    i1 = jnp.min(jnp.where(el == m1, sub, epg), axis=0, keepdims=True)
    rest = jnp.where(sub == i1, -jnp.inf, el)
    m2 = jnp.max(rest, axis=0, keepdims=True)
    i2 = jnp.min(jnp.where(rest == m2, sub, epg), axis=0, keepdims=True)
    r21 = jnp.exp(m2 - m1)
    g1 = p_grp / (1.0 + r21)
    g2 = g1 * r21
    e1 = grp * epg + i1
    e2 = grp * epg + i2

    rows = lax.broadcasted_iota(I32, (ne, tt), 0)
    oh1 = rows == e1
    oh2 = rows == e2
    oh = jnp.where(oh1 | oh2, 1.0, 0.0)
    tri = jnp.where(lax.broadcasted_iota(I32, (tt, tt), 0) < lax.broadcasted_iota(I32, (tt, tt), 1),
                    1.0, 0.0).astype(BF16)
    before = jnp.dot(oh.astype(BF16), tri, preferred_element_type=F32) + carry_ref[:, 0:1]
    r1 = jnp.sum(jnp.where(oh1, before, 0.0), axis=0, keepdims=True)
    r2 = jnp.sum(jnp.where(oh2, before, 0.0), axis=0, keepdims=True)
    carry_ref[...] = carry_ref[...] + jnp.sum(oh, axis=1, keepdims=True)

    eid_ref[...] = jnp.concatenate([e1, e2], axis=0)
    rank_ref[...] = jnp.concatenate([r1, r2], axis=0).astype(I32)
    gate_ref[...] = jnp.concatenate([g1, g2], axis=0)
    cnt_ref[...] = carry_ref[...].astype(I32)


def _router(lt):
    rr, t = lt.shape
    tt = min(512, t)
    return pl.pallas_call(
        _router_kernel,
        out_shape=(jax.ShapeDtypeStruct((2, t), I32),
                   jax.ShapeDtypeStruct((2, t), I32),
                   jax.ShapeDtypeStruct((2, t), F32),
                   jax.ShapeDtypeStruct((N_EXPERTS, 128), I32)),
        grid=(t // tt,),
        in_specs=[pl.BlockSpec((rr, tt), lambda i: (0, i))],
        out_specs=(pl.BlockSpec((2, tt), lambda i: (0, i)),
                   pl.BlockSpec((2, tt), lambda i: (0, i)),
                   pl.BlockSpec((2, tt), lambda i: (0, i)),
                   pl.BlockSpec((N_EXPERTS, 128), lambda i: (0, 0))),
        scratch_shapes=[pltpu.VMEM((N_EXPERTS, 128), F32)],
        compiler_params=_cparams(("arbitrary",)),
        name="router",
    )(lt)


def _invert_kernel(dest_ref, zeros_hbm, out_ref):
    pltpu.sync_copy(zeros_hbm, out_ref)
    t = dest_ref.shape[0] // 2

    def place(i, carry):
        out_ref[dest_ref[i]] = i
        out_ref[dest_ref[t + i]] = i
        return carry
    lax.fori_loop(0, t, place, 0, unroll=8)


def _invert(dest_flat, n_slots):
    return pl.pallas_call(
        _invert_kernel,
        out_shape=jax.ShapeDtypeStruct((n_slots,), I32),
        grid_spec=pltpu.PrefetchScalarGridSpec(
            num_scalar_prefetch=1, grid=(1,),
            in_specs=[pl.BlockSpec(memory_space=pl.ANY)],
            out_specs=pl.BlockSpec(memory_space=pltpu.SMEM)),
        compiler_params=_cparams(("arbitrary",)),
        name="invert",
    )(dest_flat, jnp.zeros((n_slots,), I32))


def _moe_kernel(be_ref, nu_ref, pc_ref, st_ref, hp_hbm, wg_hbm, wu_hbm, wd_hbm, y_ref,
                gbuf, wgf, wuf, wdf, wgb, wub, wdb, par_ref, gsem, wsem):
    b = pl.program_id(0)
    nu = nu_ref[0]
    pr = gbuf.shape[1] // MOE_BLK
    blk = MOE_BLK

    def row_copy(tok, slot, r):
        src = hp_hbm.at[pl.ds(pl.multiple_of(tok * pr, pr), pr)]
        return pltpu.make_async_copy(src, gbuf.at[slot, pl.ds(pl.multiple_of(r * pr, pr), pr)], gsem.at[slot])

    def issue(block, slot):
        base = block * blk

        def body(r, carry):
            row_copy(st_ref[base + r], slot, r).start()
            return carry
        lax.fori_loop(0, blk, body, 0, unroll=8)

    def weight_copies(e, p):
        return (pltpu.make_async_copy(wg_hbm.at[e], wgf.at[p], wsem.at[p]),
                pltpu.make_async_copy(wu_hbm.at[e], wuf.at[p], wsem.at[p]),
                pltpu.make_async_copy(wd_hbm.at[e], wdf.at[p], wsem.at[p]))

    @pl.when(b == 0)
    def _():
        par_ref[0] = 0
        for cp in weight_copies(be_ref[0], 0):
            cp.start(priority=1)
        issue(0, 0)

    @pl.when(b + 1 < nu)
    def _():
        issue(b + 1, (b + 1) % 2)

    @pl.when(b < nu)
    def _():
        slot = b % 2
        e = be_ref[b]
        first = (b == 0) | (e != be_ref[jnp.maximum(b - 1, 0)])

        @pl.when(first)
        def _():
            p = par_ref[0]
            for cp in weight_copies(e, p):
                cp.wait()
            nxt = lax.while_loop(
                lambda c: (c < N_EXPERTS) & (pc_ref[jnp.minimum(c, N_EXPERTS - 1)] == 0),
                lambda c: c + 1, e + 1)

            @pl.when(nxt < N_EXPERTS)
            def _():
                for cp in weight_copies(nxt, 1 - p):
                    cp.start(priority=1)
            wgb[...] = wgf[p].astype(BF16)
            wub[...] = wuf[p].astype(BF16)
            wdb[...] = wdf[p].astype(BF16)
            par_ref[0] = 1 - p

        pltpu.make_async_copy(hp_hbm.at[pl.ds(0, blk * pr)], gbuf.at[slot], gsem.at[slot]).wait()
        los, his = _load_row_packed(gbuf.at[slot], blk, pr)
        x = jnp.concatenate([v.astype(BF16) for v in los + his], axis=1)
        hg = jnp.dot(x, wgb[...], preferred_element_type=F32)
        hu = jnp.dot(x, wub[...], preferred_element_type=F32)
        hh = hg * (1.0 / (1.0 + jnp.exp(-hg))) * hu
        _store_row_packed(y_ref, jnp.dot(hh.astype(BF16), wdb[...], preferred_element_type=F32))

    @pl.when(b >= nu)
    def _():
        y_ref[...] = jnp.zeros_like(y_ref)


def _moe(block_e, n_used, pcounts, slot_tok, hp, w_gate, w_up, w_down, n_blocks):
    ne, d, de = w_gate.shape
    blk = MOE_BLK
    pr = d // 2 // LANES
    grid_spec = pltpu.PrefetchScalarGridSpec(
        num_scalar_prefetch=4,
        grid=(n_blocks,),
        in_specs=[pl.BlockSpec(memory_space=pl.ANY)] * 4,
        out_specs=pl.BlockSpec((blk * pr, LANES), lambda b, *_: (b, 0)),
        scratch_shapes=[pltpu.VMEM((2, blk * pr, LANES), U32),
                        pltpu.VMEM((2, d, de), F32),
                        pltpu.VMEM((2, d, de), F32),
                        pltpu.VMEM((2, de, d), F32),
                        pltpu.VMEM((d, de), BF16),
                        pltpu.VMEM((d, de), BF16),
                        pltpu.VMEM((de, d), BF16),
                        pltpu.SMEM((1,), I32),
                        pltpu.SemaphoreType.DMA((2,)),
                        pltpu.SemaphoreType.DMA((2,))],
    )
    return pl.pallas_call(
        _moe_kernel,
        out_shape=jax.ShapeDtypeStruct((n_blocks * blk * pr, LANES), U32),
        grid_spec=grid_spec,
        compiler_params=_cparams(("arbitrary",)),
        name="moe",
    )(block_e, n_used, pcounts, slot_tok, hp, w_gate, w_up, w_down)


def _combine_kernel(d0_ref, d1_ref, h_ref, gate_ref, yb_hbm, g_ref, b_ref, out_ref, buf, sem, *, alpha):
    i = pl.program_id(0)
    n = pl.num_programs(0)
    tm = h_ref.shape[0]
    pr = buf.shape[2] // tm

    def issue(tile, slot):
        base = tile * tm

        def body(r, carry):
            dst = pl.ds(pl.multiple_of(r * pr, pr), pr)
            for k, dref in enumerate((d0_ref, d1_ref)):
                src = yb_hbm.at[pl.ds(pl.multiple_of(dref[base + r] * pr, pr), pr)]
                pltpu.make_async_copy(src, buf.at[slot, k, dst], sem.at[slot]).start()
            return carry
        lax.fori_loop(0, tm, body, 0, unroll=8)

    @pl.when(i == 0)
    def _():
        issue(0, 0)

    @pl.when(i + 1 < n)
    def _():
        issue(i + 1, (i + 1) % 2)

    slot = i % 2
    for k in range(2):
        pltpu.make_async_copy(yb_hbm.at[pl.ds(0, tm * pr)], buf.at[slot, k], sem.at[slot]).wait()
    lo0, hi0 = _load_row_packed(buf.at[slot, 0], tm, pr)
    lo1, hi1 = _load_row_packed(buf.at[slot, 1], tm, pr)
    g0 = gate_ref[:, 0:1]
    g1 = gate_ref[:, 1:2]
    ffn = jnp.concatenate([g0 * a + g1 * c for a, c in zip(lo0 + hi0, lo1 + hi1)], axis=1)
    out_ref[...] = _layer_norm(alpha * h_ref[...] + ffn, g_ref[...], b_ref[...])


def _combine(dest0, dest1, h1, gate_t, yb, ln_g, ln_b, alpha):
    t, d = h1.shape
    tm = min(256, t)
    grid_spec = pltpu.PrefetchScalarGridSpec(
        num_scalar_prefetch=2,
        grid=(t // tm,),
        in_specs=[pl.BlockSpec((tm, d), lambda i, d0, d1: (i, 0)),
                  pl.BlockSpec((tm, 2), lambda i, d0, d1: (i, 0)),
                  pl.BlockSpec(memory_space=pl.ANY),
                  pl.BlockSpec((1, d), lambda i, d0, d1: (0, 0)),
                  pl.BlockSpec((1, d), lambda i, d0, d1: (0, 0))],
        out_specs=pl.BlockSpec((tm, d), lambda i, d0, d1: (i, 0)),
        scratch_shapes=[pltpu.VMEM((2, 2, tm * (d // 2 // LANES), LANES), U32),
                        pltpu.SemaphoreType.DMA((2,))],
    )
    return pl.pallas_call(
        functools.partial(_combine_kernel, alpha=alpha),
        out_shape=jax.ShapeDtypeStruct((t, d), F32),
        grid_spec=grid_spec,
        compiler_params=_cparams(("arbitrary",)),
        name="combine",
    )(dest0, dest1, h1, gate_t, yb, ln_g, ln_b)


def _layer(h, w_in, pool_w, pool_scale, a_re, a_im, log_dt, b_re, b_im, c_re, c_im, d_skip,
           glu_w, glu_b, w_out, ln1_g, ln1_b, rg_w, rg_b, re_w, re_b, w_gate, w_up, w_down,
           ln2_g, ln2_b, alpha):
    bsz, seq, d = h.shape
    t = bsz * seq
    L = CHUNK
    dp = pool_w.shape[0] * pool_w.shape[1]
    ds = w_in.shape[1] - dp
    g = ds // SSM_GROUP
    n = t // L

    kk, w1t, w2t, al = _ssm_tables(a_re, a_im, log_dt, b_re, b_im, c_re, c_im, d_skip, L)
    zpad = lambda k: jnp.zeros((d, k), F32)
    rw = jnp.concatenate([rg_w.astype(F32), zpad(8 - N_EXPERT_GROUPS), re_w.astype(F32),
                          zpad(ROUTER_LANES - ROUTER_ROWS)], axis=1)
    rw_hi = rw.astype(BF16)
    rw_lo = (rw - rw_hi.astype(F32)).astype(BF16)
    rb = jnp.concatenate([rg_b.astype(F32), jnp.zeros((8 - N_EXPERT_GROUPS,), F32), re_b.astype(F32),
                          jnp.zeros((ROUTER_LANES - ROUTER_ROWS,), F32)]).reshape(1, ROUTER_LANES)

    xt = h.reshape(t, d)
    pool_p, ut = _proj(xt, w_in.astype(BF16), dp, L)
    y_pool = _pool(pool_p, pool_w.astype(BF16), pool_scale.reshape(1, dp).astype(F32), seq)
    yt = _ssm(ut.reshape(g, L * SSM_GROUP, n), kk, w1t, w2t, al, seq // L, L)
    y_ssm = _glu(yt.reshape(g, L, SSM_GROUP, n), glu_w.astype(BF16), glu_b.reshape(1, ds).astype(F32))
    h1, hp, lt = _outproj(y_pool, y_ssm, xt, w_out.astype(BF16), ln1_g.reshape(1, d), ln1_b.reshape(1, d),
                          rw_hi, rw_lo, rb, alpha)
    eid, rank, gate, cnt = _router(lt)

    blk = MOE_BLK
    m = 2 * t
    n_blocks = -(-m // blk) + N_EXPERTS
    counts = cnt[:, 0]
    pcounts = (counts + blk - 1) // blk * blk
    pends = jnp.cumsum(pcounts)
    pstarts = pends - pcounts
    e_ids = jnp.arange(N_EXPERTS, dtype=I32)
    dest = jnp.sum(jnp.where(eid[..., None] == e_ids, pstarts, 0), axis=-1) + rank
    slot_tok = _invert(dest.reshape(-1), n_blocks * blk)
    n_used = (pends[-1] // blk).astype(I32)
    bidx = jnp.minimum(jnp.arange(n_blocks, dtype=I32), n_used - 1)
    block_e = jnp.minimum(jnp.sum((pends[None, :] <= (bidx * blk)[:, None]).astype(I32), axis=1),
                          N_EXPERTS - 1)

    yb = _moe(block_e, n_used.reshape(1), pcounts, slot_tok, hp, w_gate, w_up, w_down, n_blocks)
    out = _combine(dest[0], dest[1], h1, gate.T, yb, ln2_g.reshape(1, d), ln2_b.reshape(1, d), alpha)
    return out.reshape(bsz, seq, d)


def kernel(x, w_in, pool_w, pool_scale, ssm_a_re, ssm_a_im, ssm_log_dt, ssm_b_re, ssm_b_im, ssm_c_re, ssm_c_im, ssm_d, glu_w, glu_b, w_out, ln1_g, ln1_b, router_g_w, router_g_b, router_e_w, router_e_b, w_gate, w_up, w_down, ln2_g, ln2_b):
    depth = w_in.shape[0]
    alpha = (2.0 * depth) ** 0.25
    h = x
    for l in range(depth):
        h = _layer(h, w_in[l], pool_w[l], pool_scale[l], ssm_a_re[l], ssm_a_im[l], ssm_log_dt[l],
                   ssm_b_re[l], ssm_b_im[l], ssm_c_re[l], ssm_c_im[l], ssm_d[l], glu_w[l], glu_b[l],
                   w_out[l], ln1_g[l], ln1_b[l], router_g_w[l], router_g_b[l], router_e_w[l],
                   router_e_b[l], w_gate[l], w_up[l], w_down[l], ln2_g[l], ln2_b[l], alpha)
    return h
```

```python
import functools
import math

import numpy as np
import jax
import jax.numpy as jnp
from jax import lax
from jax.experimental import pallas as pl
from jax.experimental.pallas import tpu as pltpu

F32 = jnp.float32
BF16 = jnp.bfloat16
I32 = jnp.int32
U32 = jnp.uint32

POOL_WINDOWS = (2, 4, 8, 16)
POOL_GROUP = 256
SSM_GROUP = 16
SSM_STATE = 64
N_EXPERT_GROUPS = 4
EXPERTS_PER_GROUP = 8
N_EXPERTS = N_EXPERT_GROUPS * EXPERTS_PER_GROUP
LN_EPS = 1e-5

CHUNK = 16
MOE_BLK = 256
ROUTER_ROWS = 8 + N_EXPERTS
ROUTER_LANES = 128
LANES = 128
MXU_N = 256
HALO = 16
VMEM_LIMIT = 56 * 1024 * 1024


def _cparams(sem, vmem=VMEM_LIMIT):
    return pltpu.CompilerParams(dimension_semantics=sem, vmem_limit_bytes=vmem)


def _proj_kernel(x_ref, w_ref, pool_ref, ut_ref, accp_ref, accs_ref):
    kk = pl.program_id(1)
    rows, dp = accp_ref.shape
    nj = accs_ref.shape[0]

    @pl.when(kk == 0)
    def _():
        accp_ref[...] = jnp.zeros_like(accp_ref)
        accs_ref[...] = jnp.zeros_like(accs_ref)

    xb = x_ref[...].astype(BF16)
    for n in range(dp // MXU_N):
        cs = slice(n * MXU_N, (n + 1) * MXU_N)
        accp_ref[:, cs] += jnp.dot(xb, w_ref[:, cs], preferred_element_type=F32)
    per = MXU_N // LANES
    for n in range(nj // per):
        part = jnp.dot(xb, w_ref[:, dp + n * MXU_N: dp + (n + 1) * MXU_N], preferred_element_type=F32)
        for q in range(per):
            accs_ref[n * per + q] += part[:, q * LANES:(q + 1) * LANES]

    @pl.when(kk == pl.num_programs(1) - 1)
    def _():
        pool_ref[...] = accp_ref[...].astype(BF16)
        g, L, c, r = ut_ref.shape
        gj = LANES // c
        for s in range(L):
            for j in range(nj):
                piece = accs_ref[j, pl.ds(s, r, stride=L), :]
                ut_ref[j * gj:(j + 1) * gj, s, :, :] = piece.T.astype(BF16).reshape(gj, c, r)


def _proj(xt, w_in, dp, L):
    t, d = xt.shape
    dm = w_in.shape[1]
    g = (dm - dp) // SSM_GROUP
    tm = LANES * L
    kb = 512
    return pl.pallas_call(
        _proj_kernel,
        out_shape=(jax.ShapeDtypeStruct((t, dp), BF16),
                   jax.ShapeDtypeStruct((g, L, SSM_GROUP, t // L), BF16)),
        grid=(t // tm, d // kb),
        in_specs=[pl.BlockSpec((tm, kb), lambda i, k: (i, k)),
                  pl.BlockSpec((kb, dm), lambda i, k: (k, 0))],
        out_specs=(pl.BlockSpec((tm, dp), lambda i, k: (i, 0)),
                   pl.BlockSpec((g, L, SSM_GROUP, LANES), lambda i, k: (0, 0, 0, i))),
        scratch_shapes=[pltpu.VMEM((tm, dp), F32),
                        pltpu.VMEM(((dm - dp) // LANES, tm, LANES), F32)],
        compiler_params=_cparams(("arbitrary", "arbitrary")),
        name="proj",
    )(xt, w_in)


def _pool_kernel(prev_ref, main_ref, next_ref, pw_ref, sc_ref, out_ref, *, seq, ts):
    i = pl.program_id(0)
    ext = jnp.concatenate([prev_ref[...], main_ref[...], next_ref[...]], axis=0)
    k = ts + 2 * HALO
    row = lax.broadcasted_iota(I32, (ts, k), 0)
    col = lax.broadcasted_iota(I32, (ts, k), 1)
    base = (i * ts) % seq
    off = col - HALO - row
    src = base + row + off
    inside = (src >= 0) & (src < seq)
    pos = base + lax.broadcasted_iota(I32, (ts, 1), 0)
    for g, w in enumerate(POOL_WINDOWS):
        sl = slice(g * POOL_GROUP, (g + 1) * POOL_GROUP)
        band = inside & (off >= -(w // 2)) & (off < w - w // 2)
        bm = jnp.where(band, 1.0, 0.0).astype(BF16)
        sums = jnp.dot(bm, ext[:, sl], preferred_element_type=F32)
        lo = jnp.maximum(pos - w // 2, 0)
        hi = jnp.minimum(pos - w // 2 + w, seq)
        inv = 1.0 / (hi - lo).astype(F32)
        dlt = sums * inv - main_ref[:, sl].astype(F32)
        y = jnp.dot(dlt.astype(BF16), pw_ref[g], preferred_element_type=F32) * sc_ref[:, sl]
        out_ref[:, sl] = y.astype(BF16)


def _pool(pp, pool_w, pool_scale, seq):
    t, dp = pp.shape
    ts = min(256, seq)
    nh = t // HALO
    per = ts // HALO
    return pl.pallas_call(
        functools.partial(_pool_kernel, seq=seq, ts=ts),
        out_shape=jax.ShapeDtypeStruct((t, dp), BF16),
        grid=(t // ts,),
        in_specs=[pl.BlockSpec((HALO, dp), lambda i: (jnp.maximum(i * per - 1, 0), 0)),
                  pl.BlockSpec((ts, dp), lambda i: (i, 0)),
                  pl.BlockSpec((HALO, dp), lambda i: (jnp.minimum((i + 1) * per, nh - 1), 0)),
                  pl.BlockSpec(pool_w.shape, lambda i: (0, 0, 0)),
                  pl.BlockSpec((1, dp), lambda i: (0, 0))],
        out_specs=pl.BlockSpec((ts, dp), lambda i: (i, 0)),
        compiler_params=_cparams(("arbitrary",)),
        name="pool",
    )(pp, pp, pp, pool_w, pool_scale)


def _ssm_tables(a_re, a_im, log_dt, b_re, b_im, c_re, c_im, d_skip, L):
    g = a_re.shape[1]
    p = a_re.shape[2]
    c = b_re.shape[3]
    fl = L * c
    lr = a_re.astype(F32)
    li = a_im.astype(F32)
    dt = jnp.exp(log_dt.astype(F32))[..., None]
    mag = jnp.exp(lr * dt)
    abr = mag * jnp.cos(li * dt)
    abi = mag * jnp.sin(li * dt)
    den = lr * lr + li * li
    zr = ((abr - 1.0) * lr + abi * li) / den
    zi = (abi * lr - (abr - 1.0) * li) / den
    br = b_re.astype(F32)
    bi = b_im.astype(F32)
    bbr = zr[..., None] * br - zi[..., None] * bi
    bbi = zr[..., None] * bi + zi[..., None] * br
    cr = c_re.astype(F32)
    ci = c_im.astype(F32)
    kk = jnp.arange(L + 1, dtype=F32)[None, None, :, None]
    pm = jnp.exp(kk * (lr * dt)[:, :, None, :])
    ang = kk * (li * dt)[:, :, None, :]
    pr = pm * jnp.cos(ang)
    pi = pm * jnp.sin(ang)

    cpr = cr[:, :, None] * pr[:, :, :L, None, :] - ci[:, :, None] * pi[:, :, :L, None, :]
    cpi = cr[:, :, None] * pi[:, :, :L, None, :] + ci[:, :, None] * pr[:, :, :L, None, :]
    kern = jnp.einsum('dgkop,dgpc->dgkoc', cpr, bbr, precision=lax.Precision.HIGHEST) \
        - jnp.einsum('dgkop,dgpc->dgkoc', cpi, bbi, precision=lax.Precision.HIGHEST)
    skip = jnp.eye(c, dtype=F32)[None] * d_skip.astype(F32).reshape(g, c, 1)
    lags = jnp.concatenate([kern[0][:, :0:-1], (kern[0][:, 0] + kern[1][:, 0] + skip)[:, None],
                            kern[1][:, 1:]], axis=1)
    kk = lags.transpose(0, 2, 1, 3).reshape(g, c, (2 * L - 1) * c)
    kk = jnp.pad(kk, ((0, 0), (0, 0), (0, 2 * fl - (2 * L - 1) * c)))

    def w1(d, qr, qi):
        re = qr[..., None] * bbr[d][:, None] - qi[..., None] * bbi[d][:, None]
        im = qr[..., None] * bbi[d][:, None] + qi[..., None] * bbr[d][:, None]
        to = lambda v: v.transpose(0, 2, 1, 3).reshape(g, p, fl)
        return to(re), to(im)
    f_re, f_im = w1(0, pr[0][:, L - 1::-1], pi[0][:, L - 1::-1])
    b_re_, b_im_ = w1(1, pr[1][:, :L], pi[1][:, :L])
    w1t = jnp.concatenate([f_re, f_im, b_re_, b_im_], axis=1)

    def w2(d, qr, qi):
        re = cr[d][:, None] * qr[:, :, None] - ci[d][:, None] * qi[:, :, None]
        im = -(cr[d][:, None] * qi[:, :, None] + ci[d][:, None] * qr[:, :, None])
        to = lambda v: v.reshape(g, fl, p)
        return to(re), to(im)
    rf_re, rf_im = w2(0, pr[0][:, 1:], pi[0][:, 1:])
    rb_re, rb_im = w2(1, pr[1][:, :0:-1], pi[1][:, :0:-1])
    w2t = jnp.concatenate([rf_re, rf_im, rb_re, rb_im], axis=2)

    a4 = jnp.stack([pr[0][:, L], pi[0][:, L], pr[1][:, L], pi[1][:, L]], axis=1)
    ap = a4.reshape(g // 2, 2, 4, p).transpose(0, 2, 1, 3).reshape(g // 2, 4, 2 * p)
    ap = jnp.pad(ap, ((0, 0), (0, 4), (0, 0)))
    return kk.astype(BF16), w1t.astype(BF16), w2t.astype(BF16), ap


def _cmul(ar, ai, br, bi):
    return ar * br - ai * bi, ar * bi + ai * br


def _ssm_kernel(ut_ref, w1_ref, kk_ref, w2_ref, a_ref, yt_ref, mt_s, zt_s, ht_s, *, nseg, L):
    p = SSM_STATE
    c = SSM_GROUP
    fl, n = ut_ref.shape[1:]
    nb = n // nseg
    nrow = nseg // 8
    st = [jnp.dot(w1_ref[q], ut_ref[q], preferred_element_type=F32) for q in range(2)]
    z = jnp.concatenate([st[q][k * p:(k + 1) * p] for k in range(4) for q in range(2)], axis=0)
    zt_s[...] = z.T
    for q in range(2):
        for t in range(L):
            mt_s[q, t * c:(t + 1) * c, :] = kk_ref[q, :, (L - 1 - t) * c:(L - 1 - t) * c + fl]

    sub = lax.broadcasted_iota(I32, (8, LANES), 0)
    full = lambda v: jnp.broadcast_to(v, (8, LANES))
    one, zero = jnp.ones((8, LANES), F32), jnp.zeros((8, LANES), F32)

    def powers(row):
        out = [(full(a_ref[0, row:row + 1, :]), full(a_ref[0, row + 1:row + 2, :]))]
        for _ in range(3):
            out.append(_cmul(*out[-1], *out[-1]))
        return out

    def by_bits(pw, idx):
        qr, qi = one, zero
        for bit in range(3):
            on = (idx >> bit) & 1 == 1
            qr, qi = _cmul(qr, qi, jnp.where(on, pw[bit][0], one), jnp.where(on, pw[bit][1], zero))
        return qr, qi

    pw_f, pw_b = powers(0), powers(2)
    qf = by_bits(pw_f, sub)
    qb = by_bits(pw_b, 7 - sub)

    def local(xr, xi, pw, forward):
        for lvl, d in enumerate((1, 2, 4)):
            keep = (sub >= d) if forward else (sub < 8 - d)
            sh = d if forward else 8 - d
            tr = jnp.where(keep, pltpu.roll(xr, sh, 0), 0.0)
            ti = jnp.where(keep, pltpu.roll(xi, sh, 0), 0.0)
            mr, mi = _cmul(pw[lvl][0], pw[lvl][1], tr, ti)
            xr, xi = xr + mr, xi + mi
        return xr, xi

    def step(i, carry):
        new = []
        for b in range(nb):
            for fwd in (True, False):
                cr, ci = carry[2 * (2 * b + (0 if fwd else 1)):][:2]
                row = b * nseg + (i if fwd else nrow - 1 - i) * 8
                rows = pl.ds(pl.multiple_of(row, 8), 8)
                l0 = 0 if fwd else 2 * LANES
                pw, (qr, qi) = (pw_f, qf) if fwd else (pw_b, qb)
                lr, li = local(zt_s[rows, l0:l0 + LANES], zt_s[rows, l0 + LANES:l0 + 2 * LANES], pw, fwd)
                keep = (sub >= 1) if fwd else (sub < 7)
                er = jnp.where(keep, pltpu.roll(lr, 1 if fwd else 7, 0), 0.0)
                ei = jnp.where(keep, pltpu.roll(li, 1 if fwd else 7, 0), 0.0)
                mr, mi = _cmul(qr, qi, cr, ci)
                ht_s[rows, l0:l0 + LANES] = er + mr
                ht_s[rows, l0 + LANES:l0 + 2 * LANES] = ei + mi
                edge = 7 if fwd else 0
                mr, mi = _cmul(pw[3][0], pw[3][1], cr, ci)
                new += [full(lr[edge:edge + 1, :]) + mr, full(li[edge:edge + 1, :]) + mi]
        return tuple(new)

    lax.fori_loop(0, nrow, step, (zero,) * (4 * nb))

    h = ht_s[...].T
    for q in range(2):
        hq = jnp.concatenate([h[(2 * k + q) * p:(2 * k + q + 1) * p] for k in range(4)], axis=0)
        y = jnp.dot(mt_s[q], ut_ref[q], preferred_element_type=F32) \
            + jnp.dot(w2_ref[q], hq.astype(BF16), preferred_element_type=F32)
        yt_ref[q] = y.astype(BF16)


def _ssm(ut3, kk, w1t, w2t, ap, nseg, L):
    g, fl, n = ut3.shape
    sp = w1t.shape[1]
    return pl.pallas_call(
        functools.partial(_ssm_kernel, nseg=nseg, L=L),
        out_shape=jax.ShapeDtypeStruct((g, fl, n), BF16),
        grid=(g // 2,),
        in_specs=[pl.BlockSpec((2, fl, n), lambda i: (i, 0, 0)),
                  pl.BlockSpec((2, sp, fl), lambda i: (i, 0, 0)),
                  pl.BlockSpec((2,) + kk.shape[1:], lambda i: (i, 0, 0)),
                  pl.BlockSpec((2, fl, sp), lambda i: (i, 0, 0)),
                  pl.BlockSpec((1,) + ap.shape[1:], lambda i: (i, 0, 0))],
        out_specs=pl.BlockSpec((2, fl, n), lambda i: (i, 0, 0)),
        scratch_shapes=[pltpu.VMEM((2, fl, fl), BF16),
                        pltpu.VMEM((n, 2 * sp), F32),
                        pltpu.VMEM((n, 2 * sp), F32)],
        compiler_params=_cparams(("arbitrary",)),
        name="ssm",
    )(ut3, w1t, kk, w2t, ap)


def _glu_kernel(yt_ref, w_ref, b_ref, out_ref, nat_ref, *, sub):
    g, L, c, r = yt_ref.shape
    nj = nat_ref.shape[0]
    gj = LANES // c
    for t in range(L):
        for j in range(nj):
            piece = yt_ref[j * gj:(j + 1) * gj, t, :, :].reshape(LANES, r).astype(F32)
            nat_ref[j, pl.ds(t, r, stride=L), :] = piece.T

    def body(k, carry):
        sl = pl.ds(pl.multiple_of(k * sub, sub), sub)
        y = jnp.concatenate([nat_ref[j, sl, :] for j in range(nj)], axis=1)
        ya = jax.nn.gelu(y, approximate=True)
        z = jnp.dot(ya.astype(BF16), w_ref[...], preferred_element_type=F32) + b_ref[...]
        out_ref[sl, :] = (ya * (1.0 / (1.0 + jnp.exp(-z)))).astype(BF16)
        return carry
    lax.fori_loop(0, nat_ref.shape[1] // sub, body, 0)


def _glu(yt4, glu_w, glu_b):
    g, L, c, n = yt4.shape
    ds = g * c
    r = LANES
    return pl.pallas_call(
        functools.partial(_glu_kernel, sub=512),
        out_shape=jax.ShapeDtypeStruct((n * L, ds), BF16),
        grid=(n // r,),
        in_specs=[pl.BlockSpec((g, L, c, r), lambda j: (0, 0, 0, j)),
                  pl.BlockSpec((ds, ds), lambda j: (0, 0)),
                  pl.BlockSpec((1, ds), lambda j: (0, 0))],
        out_specs=pl.BlockSpec((r * L, ds), lambda j: (j, 0)),
        scratch_shapes=[pltpu.VMEM((ds // LANES, r * L, LANES), F32)],
        compiler_params=_cparams(("arbitrary",)),
        name="glu",
    )(yt4, glu_w, glu_b)


def _layer_norm(r, g, b):
    mu = jnp.mean(r, axis=-1, keepdims=True)
    cen = r - mu
    var = jnp.mean(cen * cen, axis=-1, keepdims=True)
    return cen * lax.rsqrt(var + LN_EPS) * g + b


def _store_row_packed(ref, m, row0=0):
    rows, width = m.shape
    half = width // 2
    nc = half // LANES
    lo = lax.bitcast_convert_type(m[:, :half].astype(BF16).astype(F32), U32) >> 16
    hi = lax.bitcast_convert_type(m[:, half:].astype(BF16).astype(F32), U32) & jnp.uint32(0xFFFF0000)
    pk = lo | hi
    for c in range(nc):
        ref[pl.ds(row0 * nc + c, rows, stride=nc), :] = pk[:, c * LANES:(c + 1) * LANES]


def _load_row_packed(ref, rows, nc, row0=0):
    los, his = [], []
    for c in range(nc):
        u = ref[pl.ds(row0 * nc + c, rows, stride=nc), :]
        los.append(lax.bitcast_convert_type(u << 16, F32))
        his.append(lax.bitcast_convert_type(u & jnp.uint32(0xFFFF0000), F32))
    return los, his


def _outproj_kernel(yp_ref, ys_ref, x_ref, wo_ref, g_ref, b_ref, rw_ref, rb_ref,
                    h_ref, hp_ref, lt_ref, *, alpha, dp, sub):
    for sb in range(h_ref.shape[0] // sub):
        sl = pl.ds(sb * sub, sub)
        mix = jnp.dot(yp_ref[sl, :], wo_ref[0:dp, :], preferred_element_type=F32) \
            + jnp.dot(ys_ref[sl, :], wo_ref[dp:, :], preferred_element_type=F32)
        h = _layer_norm(alpha * x_ref[sl, :] + mix, g_ref[...], b_ref[...])
        h_ref[sl, :] = h
        _store_row_packed(hp_ref, h, row0=sb * sub)
        hh = h.astype(BF16)
        hl = (h - hh.astype(F32)).astype(BF16)
        part = jnp.dot(hh, rw_ref[...], preferred_element_type=F32) \
            + jnp.dot(hl, rw_ref[...], preferred_element_type=F32)
        logits = part + pltpu.roll(part, ROUTER_LANES // 2, 1) + rb_ref[...]
        lt_ref[:, sb * sub:(sb + 1) * sub] = logits.T[0:ROUTER_ROWS, :]


def _outproj(y_pool, y_ssm, xt, w_out, ln_g, ln_b, rw, rb, alpha):
    t, d = xt.shape
    dp = y_pool.shape[1]
    tm = min(512, t)
    pr = d // 2 // LANES
    return pl.pallas_call(
        functools.partial(_outproj_kernel, alpha=alpha, dp=dp, sub=256),
        out_shape=(jax.ShapeDtypeStruct((t, d), F32),
                   jax.ShapeDtypeStruct((t * pr, LANES), U32),
                   jax.ShapeDtypeStruct((ROUTER_ROWS, t), F32)),
        grid=(t // tm,),
        in_specs=[pl.BlockSpec((tm, dp), lambda i: (i, 0)),
                  pl.BlockSpec((tm, y_ssm.shape[1]), lambda i: (i, 0)),
                  pl.BlockSpec((tm, d), lambda i: (i, 0)),
                  pl.BlockSpec(w_out.shape, lambda i: (0, 0)),
                  pl.BlockSpec((1, d), lambda i: (0, 0)),
                  pl.BlockSpec((1, d), lambda i: (0, 0)),
                  pl.BlockSpec(rw.shape, lambda i: (0, 0)),
                  pl.BlockSpec((1, ROUTER_LANES), lambda i: (0, 0))],
        out_specs=(pl.BlockSpec((tm, d), lambda i: (i, 0)),
                   pl.BlockSpec((tm * pr, LANES), lambda i: (i, 0)),
                   pl.BlockSpec((ROUTER_ROWS, tm), lambda i: (0, i))),
        compiler_params=_cparams(("arbitrary",)),
        name="outproj",
    )(y_pool, y_ssm, xt, w_out, ln_g, ln_b, rw, rb)


def _router_kernel(lt_ref, eid_ref, rank_ref, gate_ref, cnt_ref, carry_ref):
    i = pl.program_id(0)
    ne, epg = N_EXPERTS, EXPERTS_PER_GROUP

    @pl.when(i == 0)
    def _():
        carry_ref[...] = jnp.zeros_like(carry_ref)

    lt = lt_ref[...]
    tt = lt.shape[1]
    gl = [lt[j:j + 1, :] for j in range(N_EXPERT_GROUPS)]
    gmax = jnp.maximum(jnp.maximum(gl[0], gl[1]), jnp.maximum(gl[2], gl[3]))
    grp = jnp.where(gl[0] == gmax, 0, jnp.where(gl[1] == gmax, 1, jnp.where(gl[2] == gmax, 2, 3)))
    p_grp = 1.0 / (jnp.exp(gl[0] - gmax) + jnp.exp(gl[1] - gmax)
                   + jnp.exp(gl[2] - gmax) + jnp.exp(gl[3] - gmax))
    eg = [lt[8 + epg * j: 8 + epg * (j + 1), :] for j in range(N_EXPERT_GROUPS)]
    el = jnp.where(grp == 0, eg[0], jnp.where(grp == 1, eg[1], jnp.where(grp == 2, eg[2], eg[3])))
    sub = lax.broadcasted_iota(I32, (epg, tt), 0)
    m1 = jnp.max(el, axis=0, keepdims=True)
    i1 = jnp.min(jnp.where(el == m1, sub, epg), axis=0, keepdims=True)
    rest = jnp.where(sub == i1, -jnp.inf, el)
    m2 = jnp.max(rest, axis=0, keepdims=True)
    i2 = jnp.min(jnp.where(rest == m2, sub, epg), axis=0, keepdims=True)
    r21 = jnp.exp(m2 - m1)
    g1 = p_grp / (1.0 + r21)
    g2 = g1 * r21
    e1 = grp * epg + i1
    e2 = grp * epg + i2

    rows = lax.broadcasted_iota(I32, (ne, tt), 0)
    oh1 = rows == e1
    oh2 = rows == e2
    oh = jnp.where(oh1 | oh2, 1.0, 0.0)
    tri = jnp.where(lax.broadcasted_iota(I32, (tt, tt), 0) < lax.broadcasted_iota(I32, (tt, tt), 1),
                    1.0, 0.0).astype(BF16)
    before = jnp.dot(oh.astype(BF16), tri, preferred_element_type=F32) + carry_ref[:, 0:1]
    r1 = jnp.sum(jnp.where(oh1, before, 0.0), axis=0, keepdims=True)
    r2 = jnp.sum(jnp.where(oh2, before, 0.0), axis=0, keepdims=True)
    carry_ref[...] = carry_ref[...] + jnp.sum(oh, axis=1, keepdims=True)

    eid_ref[...] = jnp.concatenate([e1, e2], axis=0)
    rank_ref[...] = jnp.concatenate([r1, r2], axis=0).astype(I32)
    gate_ref[...] = jnp.concatenate([g1, g2], axis=0)
    cnt_ref[...] = carry_ref[...].astype(I32)


def _router(lt):
    rr, t = lt.shape
    tt = min(512, t)
    return pl.pallas_call(
        _router_kernel,
        out_shape=(jax.ShapeDtypeStruct((2, t), I32),
                   jax.ShapeDtypeStruct((2, t), I32),
                   jax.ShapeDtypeStruct((2, t), F32),
                   jax.ShapeDtypeStruct((N_EXPERTS, 128), I32)),
        grid=(t // tt,),
        in_specs=[pl.BlockSpec((rr, tt), lambda i: (0, i))],
        out_specs=(pl.BlockSpec((2, tt), lambda i: (0, i)),
                   pl.BlockSpec((2, tt), lambda i: (0, i)),
                   pl.BlockSpec((2, tt), lambda i: (0, i)),
                   pl.BlockSpec((N_EXPERTS, 128), lambda i: (0, 0))),
        scratch_shapes=[pltpu.VMEM((N_EXPERTS, 128), F32)],
        compiler_params=_cparams(("arbitrary",)),
        name="router",
    )(lt)


def _invert_kernel(dest_ref, zeros_hbm, out_ref):
    pltpu.sync_copy(zeros_hbm, out_ref)
    t = dest_ref.shape[0] // 2

    def place(i, carry):
        out_ref[dest_ref[i]] = i
        out_ref[dest_ref[t + i]] = i
        return carry
    lax.fori_loop(0, t, place, 0, unroll=8)


def _invert(dest_flat, n_slots):
    return pl.pallas_call(
        _invert_kernel,
        out_shape=jax.ShapeDtypeStruct((n_slots,), I32),
        grid_spec=pltpu.PrefetchScalarGridSpec(
            num_scalar_prefetch=1, grid=(1,),
            in_specs=[pl.BlockSpec(memory_space=pl.ANY)],
            out_specs=pl.BlockSpec(memory_space=pltpu.SMEM)),
        compiler_params=_cparams(("arbitrary",)),
        name="invert",
    )(dest_flat, jnp.zeros((n_slots,), I32))


def _moe_kernel(be_ref, nu_ref, pc_ref, st_ref, hp_hbm, wg_hbm, wu_hbm, wd_hbm, y_ref,
                gbuf, wgf, wuf, wdf, wgb, wub, wdb, par_ref, gsem, wsem):
    b = pl.program_id(0)
    nu = nu_ref[0]
    pr = gbuf.shape[1] // MOE_BLK
    blk = MOE_BLK

    def row_copy(tok, slot, r):
        src = hp_hbm.at[pl.ds(pl.multiple_of(tok * pr, pr), pr)]
        off = r * pr if isinstance(r, int) else pl.multiple_of(r * pr, pr)
        return pltpu.make_async_copy(src, gbuf.at[slot, pl.ds(off, pr)], gsem.at[slot])

    def issue(block, slot):
        base = block * blk

        def body(r, carry):
            row_copy(st_ref[base + r], slot, r).start()
            return carry
        lax.fori_loop(0, blk, body, 0, unroll=8)

    def weight_copies(e, p):
        return (pltpu.make_async_copy(wg_hbm.at[e], wgf.at[p], wsem.at[p]),
                pltpu.make_async_copy(wu_hbm.at[e], wuf.at[p], wsem.at[p]),
                pltpu.make_async_copy(wd_hbm.at[e], wdf.at[p], wsem.at[p]))

    @pl.when(b == 0)
    def _():
        par_ref[0] = 0
        for cp in weight_copies(be_ref[0], 0):
            cp.start(priority=1)
        issue(0, 0)

    def issue_part(k):
        base = (b + 1) * blk
        for r in range(k * blk // 4, (k + 1) * blk // 4):
            row_copy(st_ref[base + r], (b + 1) % 2, r).start()

    def expert_mlp(slot, with_issue):
        los, his = _load_row_packed(gbuf.at[slot], blk, pr)
        x = jnp.concatenate([v.astype(BF16) for v in los + his], axis=1)
        if with_issue:
            issue_part(0)
        hg = jnp.dot(x, wgb[...], preferred_element_type=F32)
        if with_issue:
            issue_part(1)
        hu = jnp.dot(x, wub[...], preferred_element_type=F32)
        if with_issue:
            issue_part(2)
        hh = hg * (1.0 / (1.0 + jnp.exp(-hg))) * hu
        y = jnp.dot(hh.astype(BF16), wdb[...], preferred_element_type=F32)
        if with_issue:
            issue_part(3)
        _store_row_packed(y_ref, y)

    @pl.when(b < nu)
    def _():
        slot = b % 2
        e = be_ref[b]
        first = (b == 0) | (e != be_ref[jnp.maximum(b - 1, 0)])

        @pl.when(first)
        def _():
            p = par_ref[0]
            for cp in weight_copies(e, p):
                cp.wait()
            nxt = lax.while_loop(
                lambda c: (c < N_EXPERTS) & (pc_ref[jnp.minimum(c, N_EXPERTS - 1)] == 0),
                lambda c: c + 1, e + 1)

            @pl.when(nxt < N_EXPERTS)
            def _():
                for cp in weight_copies(nxt, 1 - p):
                    cp.start(priority=1)
            wgb[...] = wgf[p].astype(BF16)
            wub[...] = wuf[p].astype(BF16)
            wdb[...] = wdf[p].astype(BF16)
            par_ref[0] = 1 - p

        pltpu.make_async_copy(hp_hbm.at[pl.ds(0, blk * pr)], gbuf.at[slot], gsem.at[slot]).wait()

        @pl.when(b + 1 < nu)
        def _():
            expert_mlp(slot, True)

        @pl.when(b + 1 >= nu)
        def _():
            expert_mlp(slot, False)

    @pl.when(b >= nu)
    def _():
        y_ref[...] = jnp.zeros_like(y_ref)


def _moe(block_e, n_used, pcounts, slot_tok, hp, w_gate, w_up, w_down, n_blocks):
    ne, d, de = w_gate.shape
    blk = MOE_BLK
    pr = d // 2 // LANES
    grid_spec = pltpu.PrefetchScalarGridSpec(
        num_scalar_prefetch=4,
        grid=(n_blocks,),
        in_specs=[pl.BlockSpec(memory_space=pl.ANY)] * 4,
        out_specs=pl.BlockSpec((blk * pr, LANES), lambda b, *_: (b, 0)),
        scratch_shapes=[pltpu.VMEM((2, blk * pr, LANES), U32),
                        pltpu.VMEM((2, d, de), F32),
                        pltpu.VMEM((2, d, de), F32),
                        pltpu.VMEM((2, de, d), F32),
                        pltpu.VMEM((d, de), BF16),
                        pltpu.VMEM((d, de), BF16),
                        pltpu.VMEM((de, d), BF16),
                        pltpu.SMEM((1,), I32),
                        pltpu.SemaphoreType.DMA((2,)),
                        pltpu.SemaphoreType.DMA((2,))],
    )
    return pl.pallas_call(
        _moe_kernel,
        out_shape=jax.ShapeDtypeStruct((n_blocks * blk * pr, LANES), U32),
        grid_spec=grid_spec,
        compiler_params=_cparams(("arbitrary",)),
        name="moe",
    )(block_e, n_used, pcounts, slot_tok, hp, w_gate, w_up, w_down)


def _combine_kernel(d0_ref, d1_ref, h_ref, gate_ref, yb_hbm, g_ref, b_ref, out_ref, buf, sem, *, alpha):
    i = pl.program_id(0)
    n = pl.num_programs(0)
    tm = h_ref.shape[0]
    pr = buf.shape[2] // tm

    def issue(tile, slot):
        base = tile * tm

        def body(r, carry):
            dst = pl.ds(pl.multiple_of(r * pr, pr), pr)
            for k, dref in enumerate((d0_ref, d1_ref)):
                src = yb_hbm.at[pl.ds(pl.multiple_of(dref[base + r] * pr, pr), pr)]
                pltpu.make_async_copy(src, buf.at[slot, k, dst], sem.at[slot]).start()
            return carry
        lax.fori_loop(0, tm, body, 0, unroll=8)

    @pl.when(i == 0)
    def _():
        issue(0, 0)

    slot = i % 2
    for k in range(2):
        pltpu.make_async_copy(yb_hbm.at[pl.ds(0, tm * pr)], buf.at[slot, k], sem.at[slot]).wait()

    nq = 4
    sub = tm // nq

    def issue_part(q):
        base = (i + 1) * tm
        for r in range(q * sub, (q + 1) * sub):
            for k, dref in enumerate((d0_ref, d1_ref)):
                src = yb_hbm.at[pl.ds(pl.multiple_of(dref[base + r] * pr, pr), pr)]
                pltpu.make_async_copy(src, buf.at[(i + 1) % 2, k, pl.ds(r * pr, pr)], sem.at[(i + 1) % 2]).start()

    def finish(with_issue):
        for q in range(nq):
            if with_issue:
                issue_part(q)
            rows = pl.ds(q * sub, sub)
            lo0, hi0 = _load_row_packed(buf.at[slot, 0], sub, pr, row0=q * sub)
            lo1, hi1 = _load_row_packed(buf.at[slot, 1], sub, pr, row0=q * sub)
            g0 = gate_ref[rows, 0:1]
            g1 = gate_ref[rows, 1:2]
            ffn = jnp.concatenate([g0 * a + g1 * c for a, c in zip(lo0 + hi0, lo1 + hi1)], axis=1)
            out_ref[rows, :] = _layer_norm(alpha * h_ref[rows, :] + ffn, g_ref[...], b_ref[...])

    @pl.when(i + 1 < n)
    def _():
        finish(True)

    @pl.when(i + 1 >= n)
    def _():
        finish(False)


def _combine(dest0, dest1, h1, gate_t, yb, ln_g, ln_b, alpha):
    t, d = h1.shape
    tm = min(256, t)
    grid_spec = pltpu.PrefetchScalarGridSpec(
        num_scalar_prefetch=2,
        grid=(t // tm,),
        in_specs=[pl.BlockSpec((tm, d), lambda i, d0, d1: (i, 0)),
                  pl.BlockSpec((tm, 2), lambda i, d0, d1: (i, 0)),
                  pl.BlockSpec(memory_space=pl.ANY),
                  pl.BlockSpec((1, d), lambda i, d0, d1: (0, 0)),
                  pl.BlockSpec((1, d), lambda i, d0, d1: (0, 0))],
        out_specs=pl.BlockSpec((tm, d), lambda i, d0, d1: (i, 0)),
        scratch_shapes=[pltpu.VMEM((2, 2, tm * (d // 2 // LANES), LANES), U32),
                        pltpu.SemaphoreType.DMA((2,))],
    )
    return pl.pallas_call(
        functools.partial(_combine_kernel, alpha=alpha),
        out_shape=jax.ShapeDtypeStruct((t, d), F32),
        grid_spec=grid_spec,
        compiler_params=_cparams(("arbitrary",)),
        name="combine",
    )(dest0, dest1, h1, gate_t, yb, ln_g, ln_b)


def _layer(h, w_in, pool_w, pool_scale, a_re, a_im, log_dt, b_re, b_im, c_re, c_im, d_skip,
           glu_w, glu_b, w_out, ln1_g, ln1_b, rg_w, rg_b, re_w, re_b, w_gate, w_up, w_down,
           ln2_g, ln2_b, alpha):
    bsz, seq, d = h.shape
    t = bsz * seq
    L = CHUNK
    dp = pool_w.shape[0] * pool_w.shape[1]
    ds = w_in.shape[1] - dp
    g = ds // SSM_GROUP
    n = t // L

    kk, w1t, w2t, al = _ssm_tables(a_re, a_im, log_dt, b_re, b_im, c_re, c_im, d_skip, L)
    half = ROUTER_LANES // 2
    zpad = lambda k: jnp.zeros((d, k), F32)
    rw = jnp.concatenate([rg_w.astype(F32), zpad(8 - N_EXPERT_GROUPS), re_w.astype(F32),
                          zpad(half - ROUTER_ROWS)], axis=1)
    rw_hi = rw.astype(BF16)
    rw_lo = (rw - rw_hi.astype(F32)).astype(BF16)
    rw2 = jnp.concatenate([rw_hi, rw_lo], axis=1)
    rb = jnp.concatenate([rg_b.astype(F32), jnp.zeros((8 - N_EXPERT_GROUPS,), F32), re_b.astype(F32),
                          jnp.zeros((ROUTER_LANES - ROUTER_ROWS,), F32)]).reshape(1, ROUTER_LANES)

    xt = h.reshape(t, d)
    pool_p, ut = _proj(xt, w_in.astype(BF16), dp, L)
    y_pool = _pool(pool_p, pool_w.astype(BF16), pool_scale.reshape(1, dp).astype(F32), seq)
    yt = _ssm(ut.reshape(g, L * SSM_GROUP, n), kk, w1t, w2t, al, seq // L, L)
    y_ssm = _glu(yt.reshape(g, L, SSM_GROUP, n), glu_w.astype(BF16), glu_b.reshape(1, ds).astype(F32))
    h1, hp, lt = _outproj(y_pool, y_ssm, xt, w_out.astype(BF16), ln1_g.reshape(1, d), ln1_b.reshape(1, d),
                          rw2, rb, alpha)
    eid, rank, gate, cnt = _router(lt)

    blk = MOE_BLK
    m = 2 * t
    n_blocks = -(-m // blk) + N_EXPERTS
    counts = cnt[:, 0]
    pcounts = (counts + blk - 1) // blk * blk
    pends = jnp.cumsum(pcounts)
    pstarts = pends - pcounts
    e_ids = jnp.arange(N_EXPERTS, dtype=I32)
    dest = jnp.sum(jnp.where(eid[..., None] == e_ids, pstarts, 0), axis=-1) + rank
    slot_tok = _invert(dest.reshape(-1), n_blocks * blk)
    n_used = (pends[-1] // blk).astype(I32)
    bidx = jnp.minimum(jnp.arange(n_blocks, dtype=I32), n_used - 1)
    block_e = jnp.minimum(jnp.sum((pends[None, :] <= (bidx * blk)[:, None]).astype(I32), axis=1),
                          N_EXPERTS - 1)

    yb = _moe(block_e, n_used.reshape(1), pcounts, slot_tok, hp, w_gate, w_up, w_down, n_blocks)
    out = _combine(dest[0], dest[1], h1, gate.T, yb, ln2_g.reshape(1, d), ln2_b.reshape(1, d), alpha)
    return out.reshape(bsz, seq, d)


def kernel(x, w_in, pool_w, pool_scale, ssm_a_re, ssm_a_im, ssm_log_dt, ssm_b_re, ssm_b_im, ssm_c_re, ssm_c_im, ssm_d, glu_w, glu_b, w_out, ln1_g, ln1_b, router_g_w, router_g_b, router_e_w, router_e_b, w_gate, w_up, w_down, ln2_g, ln2_b):
    depth = w_in.shape[0]
    alpha = (2.0 * depth) ** 0.25
    h = x
    for l in range(depth):
        h = _layer(h, w_in[l], pool_w[l], pool_scale[l], ssm_a_re[l], ssm_a_im[l], ssm_log_dt[l],
                   ssm_b_re[l], ssm_b_im[l], ssm_c_re[l], ssm_c_im[l], ssm_d[l], glu_w[l], glu_b[l],
                   w_out[l], ln1_g[l], ln1_b[l], router_g_w[l], router_g_b[l], router_e_w[l],
                   router_e_b[l], w_gate[l], w_up[l], w_down[l], ln2_g[l], ln2_b[l], alpha)
    return h
```

```python
import functools
import math

import numpy as np
import jax
import jax.numpy as jnp
from jax import lax
from jax.experimental import pallas as pl
from jax.experimental.pallas import tpu as pltpu

F32 = jnp.float32
BF16 = jnp.bfloat16
I32 = jnp.int32
U32 = jnp.uint32

POOL_WINDOWS = (2, 4, 8, 16)
POOL_GROUP = 256
SSM_GROUP = 16
SSM_STATE = 64
N_EXPERT_GROUPS = 4
EXPERTS_PER_GROUP = 8
N_EXPERTS = N_EXPERT_GROUPS * EXPERTS_PER_GROUP
LN_EPS = 1e-5

CHUNK = 16
MOE_BLK = 256
GATHER_BUFS = 3
ROUTER_ROWS = 8 + N_EXPERTS
ROUTER_LANES = 128
LANES = 128
MXU_N = 256
HALO = 16
VMEM_LIMIT = 56 * 1024 * 1024


def _cparams(sem, vmem=VMEM_LIMIT):
    return pltpu.CompilerParams(dimension_semantics=sem, vmem_limit_bytes=vmem)


def _proj_kernel(x_ref, w_ref, pool_ref, ut_ref, accp_ref, accs_ref):
    kk = pl.program_id(1)
    rows, dp = accp_ref.shape
    nj = accs_ref.shape[0]

    @pl.when(kk == 0)
    def _():
        accp_ref[...] = jnp.zeros_like(accp_ref)
        accs_ref[...] = jnp.zeros_like(accs_ref)

    xb = x_ref[...].astype(BF16)
    for n in range(dp // MXU_N):
        cs = slice(n * MXU_N, (n + 1) * MXU_N)
        accp_ref[:, cs] += jnp.dot(xb, w_ref[:, cs], preferred_element_type=F32)
    per = MXU_N // LANES
    for n in range(nj // per):
        part = jnp.dot(xb, w_ref[:, dp + n * MXU_N: dp + (n + 1) * MXU_N], preferred_element_type=F32)
        for q in range(per):
            accs_ref[n * per + q] += part[:, q * LANES:(q + 1) * LANES]

    @pl.when(kk == pl.num_programs(1) - 1)
    def _():
        pool_ref[...] = accp_ref[...].astype(BF16)
        g, L, c, r = ut_ref.shape
        gj = LANES // c
        for s in range(L):
            for j in range(nj):
                piece = accs_ref[j, pl.ds(s, r, stride=L), :]
                ut_ref[j * gj:(j + 1) * gj, s, :, :] = piece.T.astype(BF16).reshape(gj, c, r)


def _proj(xt, w_in, dp, L):
    t, d = xt.shape
    dm = w_in.shape[1]
    g = (dm - dp) // SSM_GROUP
    tm = LANES * L
    kb = 512
    return pl.pallas_call(
        _proj_kernel,
        out_shape=(jax.ShapeDtypeStruct((t, dp), BF16),
                   jax.ShapeDtypeStruct((g, L, SSM_GROUP, t // L), BF16)),
        grid=(t // tm, d // kb),
        in_specs=[pl.BlockSpec((tm, kb), lambda i, k: (i, k)),
                  pl.BlockSpec((kb, dm), lambda i, k: (k, 0))],
        out_specs=(pl.BlockSpec((tm, dp), lambda i, k: (i, 0)),
                   pl.BlockSpec((g, L, SSM_GROUP, LANES), lambda i, k: (0, 0, 0, i))),
        scratch_shapes=[pltpu.VMEM((tm, dp), F32),
                        pltpu.VMEM(((dm - dp) // LANES, tm, LANES), F32)],
        compiler_params=_cparams(("arbitrary", "arbitrary")),
        name="proj",
    )(xt, w_in)


def _pool_kernel(prev_ref, main_ref, next_ref, pw_ref, sc_ref, out_ref, *, seq, ts):
    i = pl.program_id(0)
    ext = jnp.concatenate([prev_ref[...], main_ref[...], next_ref[...]], axis=0)
    k = ts + 2 * HALO
    row = lax.broadcasted_iota(I32, (ts, k), 0)
    col = lax.broadcasted_iota(I32, (ts, k), 1)
    base = (i * ts) % seq
    off = col - HALO - row
    src = base + row + off
    off = jnp.where((src >= 0) & (src < seq), off, 2 * HALO)
    pos = base + lax.broadcasted_iota(I32, (ts, 1), 0)
    for g, w in enumerate(POOL_WINDOWS):
        sl = slice(g * POOL_GROUP, (g + 1) * POOL_GROUP)
        band = (off + w // 2).astype(U32) < w
        bm = jnp.where(band, 1.0, 0.0).astype(BF16)
        sums = jnp.dot(bm, ext[:, sl], preferred_element_type=F32)
        lo = jnp.maximum(pos - w // 2, 0)
        hi = jnp.minimum(pos - w // 2 + w, seq)
        inv = 1.0 / (hi - lo).astype(F32)
        dlt = sums * inv - main_ref[:, sl].astype(F32)
        y = jnp.dot(dlt.astype(BF16), pw_ref[g], preferred_element_type=F32) * sc_ref[:, sl]
        out_ref[:, sl] = y.astype(BF16)


def _pool(pp, pool_w, pool_scale, seq):
    t, dp = pp.shape
    ts = min(256, seq)
    nh = t // HALO
    per = ts // HALO
    return pl.pallas_call(
        functools.partial(_pool_kernel, seq=seq, ts=ts),
        out_shape=jax.ShapeDtypeStruct((t, dp), BF16),
        grid=(t // ts,),
        in_specs=[pl.BlockSpec((HALO, dp), lambda i: (jnp.maximum(i * per - 1, 0), 0)),
                  pl.BlockSpec((ts, dp), lambda i: (i, 0)),
                  pl.BlockSpec((HALO, dp), lambda i: (jnp.minimum((i + 1) * per, nh - 1), 0)),
                  pl.BlockSpec(pool_w.shape, lambda i: (0, 0, 0)),
                  pl.BlockSpec((1, dp), lambda i: (0, 0))],
        out_specs=pl.BlockSpec((ts, dp), lambda i: (i, 0)),
        compiler_params=_cparams(("arbitrary",)),
        name="pool",
    )(pp, pp, pp, pool_w, pool_scale)


def _ssm_tables(a_re, a_im, log_dt, b_re, b_im, c_re, c_im, d_skip, L):
    g = a_re.shape[1]
    p = a_re.shape[2]
    c = b_re.shape[3]
    fl = L * c
    lr = a_re.astype(F32)
    li = a_im.astype(F32)
    dt = jnp.exp(log_dt.astype(F32))[..., None]
    mag = jnp.exp(lr * dt)
    abr = mag * jnp.cos(li * dt)
    abi = mag * jnp.sin(li * dt)
    den = lr * lr + li * li
    zr = ((abr - 1.0) * lr + abi * li) / den
    zi = (abi * lr - (abr - 1.0) * li) / den
    br = b_re.astype(F32)
    bi = b_im.astype(F32)
    bbr = zr[..., None] * br - zi[..., None] * bi
    bbi = zr[..., None] * bi + zi[..., None] * br
    cr = c_re.astype(F32)
    ci = c_im.astype(F32)
    kk = jnp.arange(L + 1, dtype=F32)[None, None, :, None]
    pm = jnp.exp(kk * (lr * dt)[:, :, None, :])
    ang = kk * (li * dt)[:, :, None, :]
    pr = pm * jnp.cos(ang)
    pi = pm * jnp.sin(ang)

    cpr = cr[:, :, None] * pr[:, :, :L, None, :] - ci[:, :, None] * pi[:, :, :L, None, :]
    cpi = cr[:, :, None] * pi[:, :, :L, None, :] + ci[:, :, None] * pr[:, :, :L, None, :]
    kern = jnp.einsum('dgkop,dgpc->dgkoc', cpr, bbr, precision=lax.Precision.HIGHEST) \
        - jnp.einsum('dgkop,dgpc->dgkoc', cpi, bbi, precision=lax.Precision.HIGHEST)
    skip = jnp.eye(c, dtype=F32)[None] * d_skip.astype(F32).reshape(g, c, 1)
    lags = jnp.concatenate([kern[0][:, :0:-1], (kern[0][:, 0] + kern[1][:, 0] + skip)[:, None],
                            kern[1][:, 1:]], axis=1)
    kk = lags.transpose(0, 2, 1, 3).reshape(g, c, (2 * L - 1) * c)
    kk = jnp.pad(kk, ((0, 0), (0, 0), (0, 2 * fl - (2 * L - 1) * c)))

    def w1(d, qr, qi):
        re = qr[..., None] * bbr[d][:, None] - qi[..., None] * bbi[d][:, None]
        im = qr[..., None] * bbi[d][:, None] + qi[..., None] * bbr[d][:, None]
        to = lambda v: v.transpose(0, 2, 1, 3).reshape(g, p, fl)
        return to(re), to(im)
    f_re, f_im = w1(0, pr[0][:, L - 1::-1], pi[0][:, L - 1::-1])
    b_re_, b_im_ = w1(1, pr[1][:, :L], pi[1][:, :L])
    w1t = jnp.concatenate([f_re, f_im, b_re_, b_im_], axis=1)

    def w2(d, qr, qi):
        re = cr[d][:, None] * qr[:, :, None] - ci[d][:, None] * qi[:, :, None]
        im = -(cr[d][:, None] * qi[:, :, None] + ci[d][:, None] * qr[:, :, None])
        to = lambda v: v.reshape(g, fl, p)
        return to(re), to(im)
    rf_re, rf_im = w2(0, pr[0][:, 1:], pi[0][:, 1:])
    rb_re, rb_im = w2(1, pr[1][:, :0:-1], pi[1][:, :0:-1])
    w2t = jnp.concatenate([rf_re, rf_im, rb_re, rb_im], axis=2)

    a4 = jnp.stack([pr[0][:, L], pi[0][:, L], pr[1][:, L], pi[1][:, L]], axis=1)
    ap = a4.reshape(g // 2, 2, 4, p).transpose(0, 2, 1, 3).reshape(g // 2, 4, 2 * p)
    ap = jnp.pad(ap, ((0, 0), (0, 4), (0, 0)))
    return kk.astype(BF16), w1t.astype(BF16), w2t.astype(BF16), ap


def _cmul(ar, ai, br, bi):
    return ar * br - ai * bi, ar * bi + ai * br


def _ssm_kernel(ut_ref, w1_ref, kk_ref, w2_ref, a_ref, yt_ref, mt_s, zt_s, ht_s, *, nseg, L):
    p = SSM_STATE
    c = SSM_GROUP
    fl, n = ut_ref.shape[1:]
    nb = n // nseg
    nrow = nseg // 8
    st = [jnp.dot(w1_ref[q], ut_ref[q], preferred_element_type=F32) for q in range(2)]
    z = jnp.concatenate([st[q][k * p:(k + 1) * p] for k in range(4) for q in range(2)], axis=0)
    zt_s[...] = z.T
    for q in range(2):
        for t in range(L):
            mt_s[q, t * c:(t + 1) * c, :] = kk_ref[q, :, (L - 1 - t) * c:(L - 1 - t) * c + fl]

    sub = lax.broadcasted_iota(I32, (8, LANES), 0)
    full = lambda v: jnp.broadcast_to(v, (8, LANES))
    one, zero = jnp.ones((8, LANES), F32), jnp.zeros((8, LANES), F32)

    def powers(row):
        out = [(full(a_ref[0, row:row + 1, :]), full(a_ref[0, row + 1:row + 2, :]))]
        for _ in range(3):
            out.append(_cmul(*out[-1], *out[-1]))
        return out

    def by_bits(pw, idx):
        qr, qi = one, zero
        for bit in range(3):
            on = (idx >> bit) & 1 == 1
            qr, qi = _cmul(qr, qi, jnp.where(on, pw[bit][0], one), jnp.where(on, pw[bit][1], zero))
        return qr, qi

    pw_f, pw_b = powers(0), powers(2)
    qf = by_bits(pw_f, sub)
    qb = by_bits(pw_b, 7 - sub)

    def local(xr, xi, pw, forward):
        for lvl, d in enumerate((1, 2, 4)):
            keep = (sub >= d) if forward else (sub < 8 - d)
            sh = d if forward else 8 - d
            tr = jnp.where(keep, pltpu.roll(xr, sh, 0), 0.0)
            ti = jnp.where(keep, pltpu.roll(xi, sh, 0), 0.0)
            mr, mi = _cmul(pw[lvl][0], pw[lvl][1], tr, ti)
            xr, xi = xr + mr, xi + mi
        return xr, xi

    def step(i, carry):
        new = []
        for b in range(nb):
            for fwd in (True, False):
                cr, ci = carry[2 * (2 * b + (0 if fwd else 1)):][:2]
                row = b * nseg + (i if fwd else nrow - 1 - i) * 8
                rows = pl.ds(pl.multiple_of(row, 8), 8)
                l0 = 0 if fwd else 2 * LANES
                pw, (qr, qi) = (pw_f, qf) if fwd else (pw_b, qb)
                lr, li = local(zt_s[rows, l0:l0 + LANES], zt_s[rows, l0 + LANES:l0 + 2 * LANES], pw, fwd)
                keep = (sub >= 1) if fwd else (sub < 7)
                er = jnp.where(keep, pltpu.roll(lr, 1 if fwd else 7, 0), 0.0)
                ei = jnp.where(keep, pltpu.roll(li, 1 if fwd else 7, 0), 0.0)
                mr, mi = _cmul(qr, qi, cr, ci)
                ht_s[rows, l0:l0 + LANES] = er + mr
                ht_s[rows, l0 + LANES:l0 + 2 * LANES] = ei + mi
                edge = 7 if fwd else 0
                mr, mi = _cmul(pw[3][0], pw[3][1], cr, ci)
                new += [full(lr[edge:edge + 1, :]) + mr, full(li[edge:edge + 1, :]) + mi]
        return tuple(new)

    lax.fori_loop(0, nrow, step, (zero,) * (4 * nb))

    h = ht_s[...].T
    for q in range(2):
        hq = jnp.concatenate([h[(2 * k + q) * p:(2 * k + q + 1) * p] for k in range(4)], axis=0)
        y = jnp.dot(mt_s[q], ut_ref[q], preferred_element_type=F32) \
            + jnp.dot(w2_ref[q], hq.astype(BF16), preferred_element_type=F32)
        yt_ref[q] = y.astype(BF16)


def _ssm(ut3, kk, w1t, w2t, ap, nseg, L):
    g, fl, n = ut3.shape
    sp = w1t.shape[1]
    return pl.pallas_call(
        functools.partial(_ssm_kernel, nseg=nseg, L=L),
        out_shape=jax.ShapeDtypeStruct((g, fl, n), BF16),
        grid=(g // 2,),
        in_specs=[pl.BlockSpec((2, fl, n), lambda i: (i, 0, 0)),
                  pl.BlockSpec((2, sp, fl), lambda i: (i, 0, 0)),
                  pl.BlockSpec((2,) + kk.shape[1:], lambda i: (i, 0, 0)),
                  pl.BlockSpec((2, fl, sp), lambda i: (i, 0, 0)),
                  pl.BlockSpec((1,) + ap.shape[1:], lambda i: (i, 0, 0))],
        out_specs=pl.BlockSpec((2, fl, n), lambda i: (i, 0, 0)),
        scratch_shapes=[pltpu.VMEM((2, fl, fl), BF16),
                        pltpu.VMEM((n, 2 * sp), F32),
                        pltpu.VMEM((n, 2 * sp), F32)],
        compiler_params=_cparams(("arbitrary",)),
        name="ssm",
    )(ut3, w1t, kk, w2t, ap)


def _glu_kernel(yt_ref, w_ref, b_ref, out_ref, nat_ref, *, sub):
    g, L, c, r = yt_ref.shape
    nj = nat_ref.shape[0]
    gj = LANES // c
    for t in range(L):
        for j in range(nj):
            piece = yt_ref[j * gj:(j + 1) * gj, t, :, :].reshape(LANES, r).astype(F32)
            nat_ref[j, pl.ds(t, r, stride=L), :] = piece.T

    def body(k, carry):
        sl = pl.ds(pl.multiple_of(k * sub, sub), sub)
        y = jnp.concatenate([nat_ref[j, sl, :] for j in range(nj)], axis=1)
        ya = jax.nn.gelu(y, approximate=True)
        z = jnp.dot(ya.astype(BF16), w_ref[...], preferred_element_type=F32) + b_ref[...]
        out_ref[sl, :] = (ya * (1.0 / (1.0 + jnp.exp(-z)))).astype(BF16)
        return carry
    lax.fori_loop(0, nat_ref.shape[1] // sub, body, 0)


def _glu(yt4, glu_w, glu_b):
    g, L, c, n = yt4.shape
    ds = g * c
    r = LANES
    return pl.pallas_call(
        functools.partial(_glu_kernel, sub=512),
        out_shape=jax.ShapeDtypeStruct((n * L, ds), BF16),
        grid=(n // r,),
        in_specs=[pl.BlockSpec((g, L, c, r), lambda j: (0, 0, 0, j)),
                  pl.BlockSpec((ds, ds), lambda j: (0, 0)),
                  pl.BlockSpec((1, ds), lambda j: (0, 0))],
        out_specs=pl.BlockSpec((r * L, ds), lambda j: (j, 0)),
        scratch_shapes=[pltpu.VMEM((ds // LANES, r * L, LANES), F32)],
        compiler_params=_cparams(("arbitrary",)),
        name="glu",
    )(yt4, glu_w, glu_b)


def _layer_norm(r, g, b):
    mu = jnp.mean(r, axis=-1, keepdims=True)
    cen = r - mu
    var = jnp.mean(cen * cen, axis=-1, keepdims=True)
    return cen * lax.rsqrt(var + LN_EPS) * g + b


def _store_row_packed(ref, m, row0=0):
    rows, width = m.shape
    half = width // 2
    nc = half // LANES
    lo = lax.bitcast_convert_type(m[:, :half].astype(BF16).astype(F32), U32) >> 16
    hi = lax.bitcast_convert_type(m[:, half:].astype(BF16).astype(F32), U32) & jnp.uint32(0xFFFF0000)
    pk = lo | hi
    for c in range(nc):
        ref[pl.ds(row0 * nc + c, rows, stride=nc), :] = pk[:, c * LANES:(c + 1) * LANES]


def _load_row_packed(ref, rows, nc, row0=0):
    los, his = [], []
    for c in range(nc):
        u = ref[pl.ds(row0 * nc + c, rows, stride=nc), :]
        los.append(lax.bitcast_convert_type(u << 16, F32))
        his.append(lax.bitcast_convert_type(u & jnp.uint32(0xFFFF0000), F32))
    return los, his


def _outproj_kernel(yp_ref, ys_ref, x_ref, wo_ref, g_ref, b_ref, rw_ref, rb_ref,
                    h_ref, hp_ref, lt_ref, *, alpha, dp, sub):
    for sb in range(h_ref.shape[0] // sub):
        sl = pl.ds(sb * sub, sub)
        mix = jnp.dot(yp_ref[sl, :], wo_ref[0:dp, :], preferred_element_type=F32) \
            + jnp.dot(ys_ref[sl, :], wo_ref[dp:, :], preferred_element_type=F32)
        h = _layer_norm(alpha * x_ref[sl, :] + mix, g_ref[...], b_ref[...])
        h_ref[sl, :] = h
        _store_row_packed(hp_ref, h, row0=sb * sub)
        hh = h.astype(BF16)
        hl = (h - hh.astype(F32)).astype(BF16)
        part = jnp.dot(hh, rw_ref[...], preferred_element_type=F32) \
            + jnp.dot(hl, rw_ref[...], preferred_element_type=F32)
        logits = part + pltpu.roll(part, ROUTER_LANES // 2, 1) + rb_ref[...]
        lt_ref[:, sb * sub:(sb + 1) * sub] = logits.T[0:ROUTER_ROWS, :]


def _outproj(y_pool, y_ssm, xt, w_out, ln_g, ln_b, rw, rb, alpha):
    t, d = xt.shape
    dp = y_pool.shape[1]
    tm = min(512, t)
    pr = d // 2 // LANES
    return pl.pallas_call(
        functools.partial(_outproj_kernel, alpha=alpha, dp=dp, sub=256),
        out_shape=(jax.ShapeDtypeStruct((t, d), F32),
                   jax.ShapeDtypeStruct((t * pr, LANES), U32),
                   jax.ShapeDtypeStruct((ROUTER_ROWS, t), F32)),
        grid=(t // tm,),
        in_specs=[pl.BlockSpec((tm, dp), lambda i: (i, 0)),
                  pl.BlockSpec((tm, y_ssm.shape[1]), lambda i: (i, 0)),
                  pl.BlockSpec((tm, d), lambda i: (i, 0)),
                  pl.BlockSpec(w_out.shape, lambda i: (0, 0)),
                  pl.BlockSpec((1, d), lambda i: (0, 0)),
                  pl.BlockSpec((1, d), lambda i: (0, 0)),
                  pl.BlockSpec(rw.shape, lambda i: (0, 0)),
                  pl.BlockSpec((1, ROUTER_LANES), lambda i: (0, 0))],
        out_specs=(pl.BlockSpec((tm, d), lambda i: (i, 0)),
                   pl.BlockSpec((tm * pr, LANES), lambda i: (i, 0)),
                   pl.BlockSpec((ROUTER_ROWS, tm), lambda i: (0, i))),
        compiler_params=_cparams(("arbitrary",)),
        name="outproj",
    )(y_pool, y_ssm, xt, w_out, ln_g, ln_b, rw, rb)


def _router_kernel(lt_ref, eid_ref, rank_ref, gate_ref, cnt_ref, carry_ref):
    i = pl.program_id(0)
    ne, epg = N_EXPERTS, EXPERTS_PER_GROUP

    @pl.when(i == 0)
    def _():
        carry_ref[...] = jnp.zeros_like(carry_ref)

    lt = lt_ref[...]
    tt = lt.shape[1]
    gl = [lt[j:j + 1, :] for j in range(N_EXPERT_GROUPS)]
    gmax = jnp.maximum(jnp.maximum(gl[0], gl[1]), jnp.maximum(gl[2], gl[3]))
    grp = jnp.where(gl[0] == gmax, 0, jnp.where(gl[1] == gmax, 1, jnp.where(gl[2] == gmax, 2, 3)))
    p_grp = 1.0 / (jnp.exp(gl[0] - gmax) + jnp.exp(gl[1] - gmax)
                   + jnp.exp(gl[2] - gmax) + jnp.exp(gl[3] - gmax))
    eg = [lt[8 + epg * j: 8 + epg * (j + 1), :] for j in range(N_EXPERT_GROUPS)]
    el = jnp.where(grp == 0, eg[0], jnp.where(grp == 1, eg[1], jnp.where(grp == 2, eg[2], eg[3])))
    sub = lax.broadcasted_iota(I32, (epg, tt), 0)
    m1 = jnp.max(el, axis=0, keepdims=True)
    i1 = jnp.min(jnp.where(el == m1, sub, epg), axis=0, keepdims=True)
    rest = jnp.where(sub == i1, -jnp.inf, el)
    m2 = jnp.max(rest, axis=0, keepdims=True)
    i2 = jnp.min(jnp.where(rest == m2, sub, epg), axis=0, keepdims=True)
    r21 = jnp.exp(m2 - m1)
    g1 = p_grp / (1.0 + r21)
    g2 = g1 * r21
    e1 = grp * epg + i1
    e2 = grp * epg + i2

    rows = lax.broadcasted_iota(I32, (ne, tt), 0)
    oh1 = rows == e1
    oh2 = rows == e2
    oh = jnp.where(oh1 | oh2, 1.0, 0.0)
    tri = jnp.where(lax.broadcasted_iota(I32, (tt, tt), 0) < lax.broadcasted_iota(I32, (tt, tt), 1),
                    1.0, 0.0).astype(BF16)
    before = jnp.dot(oh.astype(BF16), tri, preferred_element_type=F32) + carry_ref[:, 0:1]
    r1 = jnp.sum(jnp.where(oh1, before, 0.0), axis=0, keepdims=True)
    r2 = jnp.sum(jnp.where(oh2, before, 0.0), axis=0, keepdims=True)
    carry_ref[...] = carry_ref[...] + jnp.sum(oh, axis=1, keepdims=True)

    eid_ref[...] = jnp.concatenate([e1, e2], axis=0)
    rank_ref[...] = jnp.concatenate([r1, r2], axis=0).astype(I32)
    gate_ref[...] = jnp.concatenate([g1, g2], axis=0)
    cnt_ref[...] = carry_ref[...].astype(I32)


def _router(lt):
    rr, t = lt.shape
    tt = min(512, t)
    return pl.pallas_call(
        _router_kernel,
        out_shape=(jax.ShapeDtypeStruct((2, t), I32),
                   jax.ShapeDtypeStruct((2, t), I32),
                   jax.ShapeDtypeStruct((2, t), F32),
                   jax.ShapeDtypeStruct((N_EXPERTS, 128), I32)),
        grid=(t // tt,),
        in_specs=[pl.BlockSpec((rr, tt), lambda i: (0, i))],
        out_specs=(pl.BlockSpec((2, tt), lambda i: (0, i)),
                   pl.BlockSpec((2, tt), lambda i: (0, i)),
                   pl.BlockSpec((2, tt), lambda i: (0, i)),
                   pl.BlockSpec((N_EXPERTS, 128), lambda i: (0, 0))),
        scratch_shapes=[pltpu.VMEM((N_EXPERTS, 128), F32)],
        compiler_params=_cparams(("arbitrary",)),
        name="router",
    )(lt)


def _invert_kernel(dest_ref, zeros_hbm, out_ref):
    pltpu.sync_copy(zeros_hbm, out_ref)
    t = dest_ref.shape[0] // 2

    def place(i, carry):
        out_ref[dest_ref[i]] = i
        out_ref[dest_ref[t + i]] = i
        return carry
    lax.fori_loop(0, t, place, 0, unroll=8)


def _invert(dest_flat, n_slots):
    return pl.pallas_call(
        _invert_kernel,
        out_shape=jax.ShapeDtypeStruct((n_slots,), I32),
        grid_spec=pltpu.PrefetchScalarGridSpec(
            num_scalar_prefetch=1, grid=(1,),
            in_specs=[pl.BlockSpec(memory_space=pl.ANY)],
            out_specs=pl.BlockSpec(memory_space=pltpu.SMEM)),
        compiler_params=_cparams(("arbitrary",)),
        name="invert",
    )(dest_flat, jnp.zeros((n_slots,), I32))


def _moe_kernel(be_ref, nu_ref, pc_ref, st_ref, hp_hbm, wg_hbm, wu_hbm, wd_hbm, y_ref,
                gbuf, wgf, wuf, wdf, wgb, wub, wdb, par_ref, gsem, wsem):
    b = pl.program_id(0)
    nu = nu_ref[0]
    pr = gbuf.shape[1] // MOE_BLK
    blk = MOE_BLK
    nbuf = gbuf.shape[0]
    look = nbuf - 1

    def row_copy(tok, slot, r):
        src = hp_hbm.at[pl.ds(pl.multiple_of(tok * pr, pr), pr)]
        off = r * pr if isinstance(r, int) else pl.multiple_of(r * pr, pr)
        return pltpu.make_async_copy(src, gbuf.at[slot, pl.ds(off, pr)], gsem.at[slot])

    def issue(block, slot):
        base = block * blk

        def body(r, carry):
            row_copy(st_ref[base + r], slot, r).start()
            return carry
        lax.fori_loop(0, blk, body, 0, unroll=8)

    def weight_copies(e, p):
        return (pltpu.make_async_copy(wg_hbm.at[e], wgf.at[p], wsem.at[p]),
                pltpu.make_async_copy(wu_hbm.at[e], wuf.at[p], wsem.at[p]),
                pltpu.make_async_copy(wd_hbm.at[e], wdf.at[p], wsem.at[p]))

    @pl.when(b == 0)
    def _():
        par_ref[0] = 0
        for cp in weight_copies(be_ref[0], 0):
            cp.start(priority=1)
        issue(0, 0)
        for j in range(1, look):
            @pl.when(j < nu)
            def _():
                issue(j, j)

    def issue_part(k):
        base = (b + look) * blk
        for r in range(k * blk // 4, (k + 1) * blk // 4):
            row_copy(st_ref[base + r], (b + look) % nbuf, r).start()

    def expert_mlp(slot, with_issue):
        los, his = _load_row_packed(gbuf.at[slot], blk, pr)
        x = jnp.concatenate([v.astype(BF16) for v in los + his], axis=1)
        if with_issue:
            issue_part(0)
        hg = jnp.dot(x, wgb[...], preferred_element_type=F32)
        if with_issue:
            issue_part(1)
        hu = jnp.dot(x, wub[...], preferred_element_type=F32)
        if with_issue:
            issue_part(2)
        hh = hg * (1.0 / (1.0 + jnp.exp(-hg))) * hu
        y = jnp.dot(hh.astype(BF16), wdb[...], preferred_element_type=F32)
        if with_issue:
            issue_part(3)
        _store_row_packed(y_ref, y)

    @pl.when(b < nu)
    def _():
        slot = b % nbuf
        e = be_ref[b]
        first = (b == 0) | (e != be_ref[jnp.maximum(b - 1, 0)])

        @pl.when(first)
        def _():
            p = par_ref[0]
            for cp in weight_copies(e, p):
                cp.wait()
            nxt = lax.while_loop(
                lambda c: (c < N_EXPERTS) & (pc_ref[jnp.minimum(c, N_EXPERTS - 1)] == 0),
                lambda c: c + 1, e + 1)

            @pl.when(nxt < N_EXPERTS)
            def _():
                for cp in weight_copies(nxt, 1 - p):
                    cp.start(priority=1)
            wgb[...] = wgf[p].astype(BF16)
            wub[...] = wuf[p].astype(BF16)
            wdb[...] = wdf[p].astype(BF16)
            par_ref[0] = 1 - p

        pltpu.make_async_copy(hp_hbm.at[pl.ds(0, blk * pr)], gbuf.at[slot], gsem.at[slot]).wait()

        @pl.when(b + look < nu)
        def _():
            expert_mlp(slot, True)

        @pl.when(b + look >= nu)
        def _():
            expert_mlp(slot, False)

    @pl.when(b >= nu)
    def _():
        y_ref[...] = jnp.zeros_like(y_ref)


def _moe(block_e, n_used, pcounts, slot_tok, hp, w_gate, w_up, w_down, n_blocks):
    ne, d, de = w_gate.shape
    blk = MOE_BLK
    pr = d // 2 // LANES
    grid_spec = pltpu.PrefetchScalarGridSpec(
        num_scalar_prefetch=4,
        grid=(n_blocks,),
        in_specs=[pl.BlockSpec(memory_space=pl.ANY)] * 4,
        out_specs=pl.BlockSpec((blk * pr, LANES), lambda b, *_: (b, 0)),
        scratch_shapes=[pltpu.VMEM((GATHER_BUFS, blk * pr, LANES), U32),
                        pltpu.VMEM((2, d, de), F32),
                        pltpu.VMEM((2, d, de), F32),
                        pltpu.VMEM((2, de, d), F32),
                        pltpu.VMEM((d, de), BF16),
                        pltpu.VMEM((d, de), BF16),
                        pltpu.VMEM((de, d), BF16),
                        pltpu.SMEM((1,), I32),
                        pltpu.SemaphoreType.DMA((GATHER_BUFS,)),
                        pltpu.SemaphoreType.DMA((2,))],
    )
    return pl.pallas_call(
        _moe_kernel,
        out_shape=jax.ShapeDtypeStruct((n_blocks * blk * pr, LANES), U32),
        grid_spec=grid_spec,
        compiler_params=_cparams(("arbitrary",)),
        name="moe",
    )(block_e, n_used, pcounts, slot_tok, hp, w_gate, w_up, w_down)


def _combine_kernel(d0_ref, d1_ref, h_ref, gate_ref, yb_hbm, g_ref, b_ref, out_ref, buf, sem, *, alpha):
    i = pl.program_id(0)
    n = pl.num_programs(0)
    tm = h_ref.shape[0]
    pr = buf.shape[2] // tm

    def issue(tile, slot):
        base = tile * tm

        def body(r, carry):
            dst = pl.ds(pl.multiple_of(r * pr, pr), pr)
            for k, dref in enumerate((d0_ref, d1_ref)):
                src = yb_hbm.at[pl.ds(pl.multiple_of(dref[base + r] * pr, pr), pr)]
                pltpu.make_async_copy(src, buf.at[slot, k, dst], sem.at[slot]).start()
            return carry
        lax.fori_loop(0, tm, body, 0, unroll=8)

    nbuf = buf.shape[0]
    look = nbuf - 1

    @pl.when(i == 0)
    def _():
        issue(0, 0)
        for j in range(1, look):
            @pl.when(j < n)
            def _():
                issue(j, j)

    slot = i % nbuf
    for k in range(2):
        pltpu.make_async_copy(yb_hbm.at[pl.ds(0, tm * pr)], buf.at[slot, k], sem.at[slot]).wait()

    nq = 4
    sub = tm // nq

    def issue_part(q):
        base = (i + look) * tm
        nslot = (i + look) % nbuf
        for r in range(q * sub, (q + 1) * sub):
            for k, dref in enumerate((d0_ref, d1_ref)):
                src = yb_hbm.at[pl.ds(pl.multiple_of(dref[base + r] * pr, pr), pr)]
                pltpu.make_async_copy(src, buf.at[nslot, k, pl.ds(r * pr, pr)], sem.at[nslot]).start()

    def finish(with_issue):
        for q in range(nq):
            if with_issue:
                issue_part(q)
            rows = pl.ds(q * sub, sub)
            lo0, hi0 = _load_row_packed(buf.at[slot, 0], sub, pr, row0=q * sub)
            lo1, hi1 = _load_row_packed(buf.at[slot, 1], sub, pr, row0=q * sub)
            g0 = gate_ref[rows, 0:1]
            g1 = gate_ref[rows, 1:2]
            ffn = jnp.concatenate([g0 * a + g1 * c for a, c in zip(lo0 + hi0, lo1 + hi1)], axis=1)
            out_ref[rows, :] = _layer_norm(alpha * h_ref[rows, :] + ffn, g_ref[...], b_ref[...])

    @pl.when(i + look < n)
    def _():
        finish(True)

    @pl.when(i + look >= n)
    def _():
        finish(False)


def _combine(dest0, dest1, h1, gate_t, yb, ln_g, ln_b, alpha):
    t, d = h1.shape
    tm = min(256, t)
    grid_spec = pltpu.PrefetchScalarGridSpec(
        num_scalar_prefetch=2,
        grid=(t // tm,),
        in_specs=[pl.BlockSpec((tm, d), lambda i, d0, d1: (i, 0)),
                  pl.BlockSpec((tm, 2), lambda i, d0, d1: (i, 0)),
                  pl.BlockSpec(memory_space=pl.ANY),
                  pl.BlockSpec((1, d), lambda i, d0, d1: (0, 0)),
                  pl.BlockSpec((1, d), lambda i, d0, d1: (0, 0))],
        out_specs=pl.BlockSpec((tm, d), lambda i, d0, d1: (i, 0)),
        scratch_shapes=[pltpu.VMEM((GATHER_BUFS, 2, tm * (d // 2 // LANES), LANES), U32),
                        pltpu.SemaphoreType.DMA((GATHER_BUFS,))],
    )
    return pl.pallas_call(
        functools.partial(_combine_kernel, alpha=alpha),
        out_shape=jax.ShapeDtypeStruct((t, d), F32),
        grid_spec=grid_spec,
        compiler_params=_cparams(("arbitrary",)),
        name="combine",
    )(dest0, dest1, h1, gate_t, yb, ln_g, ln_b)


def _layer(h, w_in, pool_w, pool_scale, a_re, a_im, log_dt, b_re, b_im, c_re, c_im, d_skip,
           glu_w, glu_b, w_out, ln1_g, ln1_b, rg_w, rg_b, re_w, re_b, w_gate, w_up, w_down,
           ln2_g, ln2_b, alpha):
    bsz, seq, d = h.shape
    t = bsz * seq
    L = CHUNK
    dp = pool_w.shape[0] * pool_w.shape[1]
    ds = w_in.shape[1] - dp
    g = ds // SSM_GROUP
    n = t // L

    kk, w1t, w2t, al = _ssm_tables(a_re, a_im, log_dt, b_re, b_im, c_re, c_im, d_skip, L)
    half = ROUTER_LANES // 2
    zpad = lambda k: jnp.zeros((d, k), F32)
    rw = jnp.concatenate([rg_w.astype(F32), zpad(8 - N_EXPERT_GROUPS), re_w.astype(F32),
                          zpad(half - ROUTER_ROWS)], axis=1)
    rw_hi = rw.astype(BF16)
    rw_lo = (rw - rw_hi.astype(F32)).astype(BF16)
    rw2 = jnp.concatenate([rw_hi, rw_lo], axis=1)
    rb = jnp.concatenate([rg_b.astype(F32), jnp.zeros((8 - N_EXPERT_GROUPS,), F32), re_b.astype(F32),
                          jnp.zeros((ROUTER_LANES - ROUTER_ROWS,), F32)]).reshape(1, ROUTER_LANES)

    xt = h.reshape(t, d)
    pool_p, ut = _proj(xt, w_in.astype(BF16), dp, L)
    y_pool = _pool(pool_p, pool_w.astype(BF16), pool_scale.reshape(1, dp).astype(F32), seq)
    yt = _ssm(ut.reshape(g, L * SSM_GROUP, n), kk, w1t, w2t, al, seq // L, L)
    y_ssm = _glu(yt.reshape(g, L, SSM_GROUP, n), glu_w.astype(BF16), glu_b.reshape(1, ds).astype(F32))
    h1, hp, lt = _outproj(y_pool, y_ssm, xt, w_out.astype(BF16), ln1_g.reshape(1, d), ln1_b.reshape(1, d),
                          rw2, rb, alpha)
    eid, rank, gate, cnt = _router(lt)

    blk = MOE_BLK
    m = 2 * t
    n_blocks = -(-m // blk) + N_EXPERTS
    counts = cnt[:, 0]
    pcounts = (counts + blk - 1) // blk * blk
    pends = jnp.cumsum(pcounts)
    pstarts = pends - pcounts
    e_ids = jnp.arange(N_EXPERTS, dtype=I32)
    dest = jnp.sum(jnp.where(eid[..., None] == e_ids, pstarts, 0), axis=-1) + rank
    slot_tok = _invert(dest.reshape(-1), n_blocks * blk)
    n_used = (pends[-1] // blk).astype(I32)
    bidx = jnp.minimum(jnp.arange(n_blocks, dtype=I32), n_used - 1)
    block_e = jnp.minimum(jnp.sum((pends[None, :] <= (bidx * blk)[:, None]).astype(I32), axis=1),
                          N_EXPERTS - 1)

    yb = _moe(block_e, n_used.reshape(1), pcounts, slot_tok, hp, w_gate, w_up, w_down, n_blocks)
    out = _combine(dest[0], dest[1], h1, gate.T, yb, ln2_g.reshape(1, d), ln2_b.reshape(1, d), alpha)
    return out.reshape(bsz, seq, d)


def kernel(x, w_in, pool_w, pool_scale, ssm_a_re, ssm_a_im, ssm_log_dt, ssm_b_re, ssm_b_im, ssm_c_re, ssm_c_im, ssm_d, glu_w, glu_b, w_out, ln1_g, ln1_b, router_g_w, router_g_b, router_e_w, router_e_b, w_gate, w_up, w_down, ln2_g, ln2_b):
    depth = w_in.shape[0]
    alpha = (2.0 * depth) ** 0.25
    h = x
    for l in range(depth):
        h = _layer(h, w_in[l], pool_w[l], pool_scale[l], ssm_a_re[l], ssm_a_im[l], ssm_log_dt[l],
                   ssm_b_re[l], ssm_b_im[l], ssm_c_re[l], ssm_c_im[l], ssm_d[l], glu_w[l], glu_b[l],
                   w_out[l], ln1_g[l], ln1_b[l], router_g_w[l], router_g_b[l], router_e_w[l],
                   router_e_b[l], w_gate[l], w_up[l], w_down[l], ln2_g[l], ln2_b[l], alpha)
    return h
```

```python
import functools
import math

import numpy as np
import jax
import jax.numpy as jnp
from jax import lax
from jax.experimental import pallas as pl
from jax.experimental.pallas import tpu as pltpu

F32 = jnp.float32
BF16 = jnp.bfloat16
I32 = jnp.int32
U32 = jnp.uint32

POOL_WINDOWS = (2, 4, 8, 16)
POOL_GROUP = 256
SSM_GROUP = 16
SSM_STATE = 64
N_EXPERT_GROUPS = 4
EXPERTS_PER_GROUP = 8
N_EXPERTS = N_EXPERT_GROUPS * EXPERTS_PER_GROUP
LN_EPS = 1e-5

CHUNK = 16
MOE_BLK = 256
GATHER_BUFS = 4
ROUTER_ROWS = 8 + N_EXPERTS
ROUTER_LANES = 128
LANES = 128
MXU_N = 256
HALO = 16
VMEM_LIMIT = 56 * 1024 * 1024


def _cparams(sem, vmem=VMEM_LIMIT):
    return pltpu.CompilerParams(dimension_semantics=sem, vmem_limit_bytes=vmem)


def _proj_kernel(x_ref, w_ref, pool_ref, ut_ref, accp_ref, accs_ref):
    kk = pl.program_id(1)
    rows, dp = accp_ref.shape
    nj = accs_ref.shape[0]

    @pl.when(kk == 0)
    def _():
        accp_ref[...] = jnp.zeros_like(accp_ref)
        accs_ref[...] = jnp.zeros_like(accs_ref)

    xb = x_ref[...].astype(BF16)
    for n in range(dp // MXU_N):
        cs = slice(n * MXU_N, (n + 1) * MXU_N)
        accp_ref[:, cs] += jnp.dot(xb, w_ref[:, cs], preferred_element_type=F32)
    per = MXU_N // LANES
    for n in range(nj // per):
        part = jnp.dot(xb, w_ref[:, dp + n * MXU_N: dp + (n + 1) * MXU_N], preferred_element_type=F32)
        for q in range(per):
            accs_ref[n * per + q] += part[:, q * LANES:(q + 1) * LANES]

    @pl.when(kk == pl.num_programs(1) - 1)
    def _():
        pool_ref[...] = accp_ref[...].astype(BF16)
        g, L, c, r = ut_ref.shape
        gj = LANES // c
        for s in range(L):
            for j in range(nj):
                piece = accs_ref[j, pl.ds(s, r, stride=L), :]
                ut_ref[j * gj:(j + 1) * gj, s, :, :] = piece.astype(BF16).T.reshape(gj, c, r)


def _proj(xt, w_in, dp, L):
    t, d = xt.shape
    dm = w_in.shape[1]
    g = (dm - dp) // SSM_GROUP
    tm = LANES * L
    kb = 512
    return pl.pallas_call(
        _proj_kernel,
        out_shape=(jax.ShapeDtypeStruct((t, dp), BF16),
                   jax.ShapeDtypeStruct((g, L, SSM_GROUP, t // L), BF16)),
        grid=(t // tm, d // kb),
        in_specs=[pl.BlockSpec((tm, kb), lambda i, k: (i, k)),
                  pl.BlockSpec((kb, dm), lambda i, k: (k, 0))],
        out_specs=(pl.BlockSpec((tm, dp), lambda i, k: (i, 0)),
                   pl.BlockSpec((g, L, SSM_GROUP, LANES), lambda i, k: (0, 0, 0, i))),
        scratch_shapes=[pltpu.VMEM((tm, dp), F32),
                        pltpu.VMEM(((dm - dp) // LANES, tm, LANES), F32)],
        compiler_params=_cparams(("arbitrary", "arbitrary")),
        name="proj",
    )(xt, w_in)


def _pool_kernel(prev_ref, main_ref, next_ref, pw_ref, sc_ref, out_ref, *, seq, sub):
    i = pl.program_id(0)
    ts = main_ref.shape[0]
    ext = jnp.concatenate([prev_ref[...], main_ref[...], next_ref[...]], axis=0)
    k = sub + 2 * HALO
    row = lax.broadcasted_iota(I32, (sub, k), 0)
    col = lax.broadcasted_iota(I32, (sub, k), 1)
    off0 = col - HALO - row
    for sb in range(ts // sub):
        rows = slice(sb * sub, (sb + 1) * sub)
        base = (i * ts + sb * sub) % seq
        src = base + row + off0
        off = jnp.where((src >= 0) & (src < seq), off0, 2 * HALO)
        pos = base + lax.broadcasted_iota(I32, (sub, 1), 0)
        for g, w in enumerate(POOL_WINDOWS):
            sl = slice(g * POOL_GROUP, (g + 1) * POOL_GROUP)
            band = (off + w // 2).astype(U32) < w
            bm = jnp.where(band, 1.0, 0.0).astype(BF16)
            sums = jnp.dot(bm, ext[sb * sub:sb * sub + k, sl], preferred_element_type=F32)
            lo = jnp.maximum(pos - w // 2, 0)
            hi = jnp.minimum(pos - w // 2 + w, seq)
            inv = 1.0 / (hi - lo).astype(F32)
            dlt = sums * inv - main_ref[rows, sl].astype(F32)
            y = jnp.dot(dlt.astype(BF16), pw_ref[g], preferred_element_type=F32) * sc_ref[:, sl]
            out_ref[rows, sl] = y.astype(BF16)


def _pool(pp, pool_w, pool_scale, seq):
    t, dp = pp.shape
    ts = min(512, seq)
    nh = t // HALO
    per = ts // HALO
    return pl.pallas_call(
        functools.partial(_pool_kernel, seq=seq, sub=256),
        out_shape=jax.ShapeDtypeStruct((t, dp), BF16),
        grid=(t // ts,),
        in_specs=[pl.BlockSpec((HALO, dp), lambda i: (jnp.maximum(i * per - 1, 0), 0)),
                  pl.BlockSpec((ts, dp), lambda i: (i, 0)),
                  pl.BlockSpec((HALO, dp), lambda i: (jnp.minimum((i + 1) * per, nh - 1), 0)),
                  pl.BlockSpec(pool_w.shape, lambda i: (0, 0, 0)),
                  pl.BlockSpec((1, dp), lambda i: (0, 0))],
        out_specs=pl.BlockSpec((ts, dp), lambda i: (i, 0)),
        compiler_params=_cparams(("arbitrary",)),
        name="pool",
    )(pp, pp, pp, pool_w, pool_scale)


def _ssm_tables(a_re, a_im, log_dt, b_re, b_im, c_re, c_im, d_skip, L):
    g = a_re.shape[1]
    p = a_re.shape[2]
    c = b_re.shape[3]
    fl = L * c
    lr = a_re.astype(F32)
    li = a_im.astype(F32)
    dt = jnp.exp(log_dt.astype(F32))[..., None]
    mag = jnp.exp(lr * dt)
    abr = mag * jnp.cos(li * dt)
    abi = mag * jnp.sin(li * dt)
    den = lr * lr + li * li
    zr = ((abr - 1.0) * lr + abi * li) / den
    zi = (abi * lr - (abr - 1.0) * li) / den
    br = b_re.astype(F32)
    bi = b_im.astype(F32)
    bbr = zr[..., None] * br - zi[..., None] * bi
    bbi = zr[..., None] * bi + zi[..., None] * br
    cr = c_re.astype(F32)
    ci = c_im.astype(F32)
    kk = jnp.arange(L + 1, dtype=F32)[None, None, :, None]
    pm = jnp.exp(kk * (lr * dt)[:, :, None, :])
    ang = kk * (li * dt)[:, :, None, :]
    pr = pm * jnp.cos(ang)
    pi = pm * jnp.sin(ang)

    cpr = cr[:, :, None] * pr[:, :, :L, None, :] - ci[:, :, None] * pi[:, :, :L, None, :]
    cpi = cr[:, :, None] * pi[:, :, :L, None, :] + ci[:, :, None] * pr[:, :, :L, None, :]
    kern = jnp.einsum('dgkop,dgpc->dgkoc', cpr, bbr, precision=lax.Precision.HIGHEST) \
        - jnp.einsum('dgkop,dgpc->dgkoc', cpi, bbi, precision=lax.Precision.HIGHEST)
    skip = jnp.eye(c, dtype=F32)[None] * d_skip.astype(F32).reshape(g, c, 1)
    lags = jnp.concatenate([kern[0][:, :0:-1], (kern[0][:, 0] + kern[1][:, 0] + skip)[:, None],
                            kern[1][:, 1:]], axis=1)
    kk = lags.transpose(0, 2, 1, 3).reshape(g, c, (2 * L - 1) * c)
    kk = jnp.pad(kk, ((0, 0), (0, 0), (0, 2 * fl - (2 * L - 1) * c)))

    def w1(d, qr, qi):
        re = qr[..., None] * bbr[d][:, None] - qi[..., None] * bbi[d][:, None]
        im = qr[..., None] * bbi[d][:, None] + qi[..., None] * bbr[d][:, None]
        to = lambda v: v.transpose(0, 2, 1, 3).reshape(g, p, fl)
        return to(re), to(im)
    f_re, f_im = w1(0, pr[0][:, L - 1::-1], pi[0][:, L - 1::-1])
    b_re_, b_im_ = w1(1, pr[1][:, :L], pi[1][:, :L])
    w1t = jnp.concatenate([f_re, f_im, b_re_, b_im_], axis=1)

    def w2(d, qr, qi):
        re = cr[d][:, None] * qr[:, :, None] - ci[d][:, None] * qi[:, :, None]
        im = -(cr[d][:, None] * qi[:, :, None] + ci[d][:, None] * qr[:, :, None])
        to = lambda v: v.reshape(g, fl, p)
        return to(re), to(im)
    rf_re, rf_im = w2(0, pr[0][:, 1:], pi[0][:, 1:])
    rb_re, rb_im = w2(1, pr[1][:, :0:-1], pi[1][:, :0:-1])
    w2t = jnp.concatenate([rf_re, rf_im, rb_re, rb_im], axis=2)

    a4 = jnp.stack([pr[0][:, L], pi[0][:, L], pr[1][:, L], pi[1][:, L]], axis=1)
    ap = a4.reshape(g // 2, 2, 4, p).transpose(0, 2, 1, 3).reshape(g // 2, 4, 2 * p)
    ap = jnp.pad(ap, ((0, 0), (0, 4), (0, 0)))
    return kk.astype(BF16), w1t.astype(BF16), w2t.astype(BF16), ap


def _cmul(ar, ai, br, bi):
    return ar * br - ai * bi, ar * bi + ai * br


def _ssm_kernel(ut_ref, w1_ref, kk_ref, w2_ref, a_ref, yt_ref, mt_s, zt_s, ht_s, *, nseg, L):
    p = SSM_STATE
    c = SSM_GROUP
    fl, n = ut_ref.shape[1:]
    nb = n // nseg
    nrow = nseg // 8
    st = [jnp.dot(w1_ref[q], ut_ref[q], preferred_element_type=F32) for q in range(2)]
    z = jnp.concatenate([st[q][k * p:(k + 1) * p] for k in range(4) for q in range(2)], axis=0)
    zt_s[...] = z.T
    for q in range(2):
        for t in range(L):
            mt_s[q, t * c:(t + 1) * c, :] = kk_ref[q, :, (L - 1 - t) * c:(L - 1 - t) * c + fl]

    sub = lax.broadcasted_iota(I32, (8, LANES), 0)
    full = lambda v: jnp.broadcast_to(v, (8, LANES))
    one, zero = jnp.ones((8, LANES), F32), jnp.zeros((8, LANES), F32)

    def powers(row):
        out = [(full(a_ref[0, row:row + 1, :]), full(a_ref[0, row + 1:row + 2, :]))]
        for _ in range(3):
            out.append(_cmul(*out[-1], *out[-1]))
        return out

    def by_bits(pw, idx):
        qr, qi = one, zero
        for bit in range(3):
            on = (idx >> bit) & 1 == 1
            qr, qi = _cmul(qr, qi, jnp.where(on, pw[bit][0], one), jnp.where(on, pw[bit][1], zero))
        return qr, qi

    pw_f, pw_b = powers(0), powers(2)
    qf = by_bits(pw_f, sub)
    qb = by_bits(pw_b, 7 - sub)

    def local(xr, xi, pw, forward):
        for lvl, d in enumerate((1, 2, 4)):
            keep = (sub >= d) if forward else (sub < 8 - d)
            sh = d if forward else 8 - d
            tr = jnp.where(keep, pltpu.roll(xr, sh, 0), 0.0)
            ti = jnp.where(keep, pltpu.roll(xi, sh, 0), 0.0)
            mr, mi = _cmul(pw[lvl][0], pw[lvl][1], tr, ti)
            xr, xi = xr + mr, xi + mi
        return xr, xi

    def step(i, carry):
        new = []
        for b in range(nb):
            for fwd in (True, False):
                cr, ci = carry[2 * (2 * b + (0 if fwd else 1)):][:2]
                row = b * nseg + (i if fwd else nrow - 1 - i) * 8
                rows = pl.ds(pl.multiple_of(row, 8), 8)
                l0 = 0 if fwd else 2 * LANES
                pw, (qr, qi) = (pw_f, qf) if fwd else (pw_b, qb)
                lr, li = local(zt_s[rows, l0:l0 + LANES], zt_s[rows, l0 + LANES:l0 + 2 * LANES], pw, fwd)
                keep = (sub >= 1) if fwd else (sub < 7)
                er = jnp.where(keep, pltpu.roll(lr, 1 if fwd else 7, 0), 0.0)
                ei = jnp.where(keep, pltpu.roll(li, 1 if fwd else 7, 0), 0.0)
                mr, mi = _cmul(qr, qi, cr, ci)
                ht_s[rows, l0:l0 + LANES] = er + mr
                ht_s[rows, l0 + LANES:l0 + 2 * LANES] = ei + mi
                edge = 7 if fwd else 0
                mr, mi = _cmul(pw[3][0], pw[3][1], cr, ci)
                new += [full(lr[edge:edge + 1, :]) + mr, full(li[edge:edge + 1, :]) + mi]
        return tuple(new)

    lax.fori_loop(0, nrow, step, (zero,) * (4 * nb))

    h = ht_s[...].T
    for q in range(2):
        hq = jnp.concatenate([h[(2 * k + q) * p:(2 * k + q + 1) * p] for k in range(4)], axis=0)
        y = jnp.dot(mt_s[q], ut_ref[q], preferred_element_type=F32) \
            + jnp.dot(w2_ref[q], hq.astype(BF16), preferred_element_type=F32)
        yt_ref[q] = y.astype(BF16)


def _ssm(ut3, kk, w1t, w2t, ap, nseg, L):
    g, fl, n = ut3.shape
    sp = w1t.shape[1]
    return pl.pallas_call(
        functools.partial(_ssm_kernel, nseg=nseg, L=L),
        out_shape=jax.ShapeDtypeStruct((g, fl, n), BF16),
        grid=(g // 2,),
        in_specs=[pl.BlockSpec((2, fl, n), lambda i: (i, 0, 0)),
                  pl.BlockSpec((2, sp, fl), lambda i: (i, 0, 0)),
                  pl.BlockSpec((2,) + kk.shape[1:], lambda i: (i, 0, 0)),
                  pl.BlockSpec((2, fl, sp), lambda i: (i, 0, 0)),
                  pl.BlockSpec((1,) + ap.shape[1:], lambda i: (i, 0, 0))],
        out_specs=pl.BlockSpec((2, fl, n), lambda i: (i, 0, 0)),
        scratch_shapes=[pltpu.VMEM((2, fl, fl), BF16),
                        pltpu.VMEM((n, 2 * sp), F32),
                        pltpu.VMEM((n, 2 * sp), F32)],
        compiler_params=_cparams(("arbitrary",)),
        name="ssm",
    )(ut3, w1t, kk, w2t, ap)


def _glu_kernel(yt_ref, w_ref, b_ref, out_ref, nat_ref, *, sub):
    g, L, c, r = yt_ref.shape
    nj = nat_ref.shape[0]
    gj = LANES // c
    for t in range(L):
        for j in range(nj):
            piece = yt_ref[j * gj:(j + 1) * gj, t, :, :].reshape(LANES, r).astype(F32)
            nat_ref[j, pl.ds(t, r, stride=L), :] = piece.T

    def body(k, carry):
        sl = pl.ds(pl.multiple_of(k * sub, sub), sub)
        y = jnp.concatenate([nat_ref[j, sl, :] for j in range(nj)], axis=1)
        ya = jax.nn.gelu(y, approximate=True)
        z = jnp.dot(ya.astype(BF16), w_ref[...], preferred_element_type=F32) + b_ref[...]
        out_ref[sl, :] = (ya * (1.0 / (1.0 + jnp.exp(-z)))).astype(BF16)
        return carry
    lax.fori_loop(0, nat_ref.shape[1] // sub, body, 0)


def _glu(yt4, glu_w, glu_b):
    g, L, c, n = yt4.shape
    ds = g * c
    r = LANES
    return pl.pallas_call(
        functools.partial(_glu_kernel, sub=512),
        out_shape=jax.ShapeDtypeStruct((n * L, ds), BF16),
        grid=(n // r,),
        in_specs=[pl.BlockSpec((g, L, c, r), lambda j: (0, 0, 0, j)),
                  pl.BlockSpec((ds, ds), lambda j: (0, 0)),
                  pl.BlockSpec((1, ds), lambda j: (0, 0))],
        out_specs=pl.BlockSpec((r * L, ds), lambda j: (j, 0)),
        scratch_shapes=[pltpu.VMEM((ds // LANES, r * L, LANES), F32)],
        compiler_params=_cparams(("arbitrary",)),
        name="glu",
    )(yt4, glu_w, glu_b)


def _layer_norm(r, g, b):
    mu = jnp.mean(r, axis=-1, keepdims=True)
    cen = r - mu
    var = jnp.mean(cen * cen, axis=-1, keepdims=True)
    return cen * lax.rsqrt(var + LN_EPS) * g + b


def _store_row_packed(ref, m, row0=0):
    rows, width = m.shape
    half = width // 2
    nc = half // LANES
    lo = lax.bitcast_convert_type(m[:, :half].astype(BF16).astype(F32), U32) >> 16
    hi = lax.bitcast_convert_type(m[:, half:].astype(BF16).astype(F32), U32) & jnp.uint32(0xFFFF0000)
    pk = lo | hi
    for c in range(nc):
        ref[pl.ds(row0 * nc + c, rows, stride=nc), :] = pk[:, c * LANES:(c + 1) * LANES]


def _load_row_packed(ref, rows, nc, row0=0):
    los, his = [], []
    for c in range(nc):
        u = ref[pl.ds(row0 * nc + c, rows, stride=nc), :]
        los.append(lax.bitcast_convert_type(u << 16, F32))
        his.append(lax.bitcast_convert_type(u & jnp.uint32(0xFFFF0000), F32))
    return los, his


def _outproj_kernel(yp_ref, ys_ref, x_ref, wo_ref, g_ref, b_ref, rw_ref, rb_ref,
                    h_ref, hp_ref, lt_ref, *, alpha, dp, sub):
    for sb in range(h_ref.shape[0] // sub):
        sl = pl.ds(sb * sub, sub)
        mix = jnp.dot(yp_ref[sl, :], wo_ref[0:dp, :], preferred_element_type=F32) \
            + jnp.dot(ys_ref[sl, :], wo_ref[dp:, :], preferred_element_type=F32)
        h = _layer_norm(alpha * x_ref[sl, :] + mix, g_ref[...], b_ref[...])
        h_ref[sl, :] = h
        _store_row_packed(hp_ref, h, row0=sb * sub)
        hh = h.astype(BF16)
        hl = (h - hh.astype(F32)).astype(BF16)
        part = jnp.dot(hh, rw_ref[...], preferred_element_type=F32) \
            + jnp.dot(hl, rw_ref[...], preferred_element_type=F32)
        logits = part + pltpu.roll(part, ROUTER_LANES // 2, 1) + rb_ref[...]
        lt_ref[:, sb * sub:(sb + 1) * sub] = logits.T[0:ROUTER_ROWS, :]


def _outproj(y_pool, y_ssm, xt, w_out, ln_g, ln_b, rw, rb, alpha):
    t, d = xt.shape
    dp = y_pool.shape[1]
    tm = min(512, t)
    pr = d // 2 // LANES
    return pl.pallas_call(
        functools.partial(_outproj_kernel, alpha=alpha, dp=dp, sub=256),
        out_shape=(jax.ShapeDtypeStruct((t, d), F32),
                   jax.ShapeDtypeStruct((t * pr, LANES), U32),
                   jax.ShapeDtypeStruct((ROUTER_ROWS, t), F32)),
        grid=(t // tm,),
        in_specs=[pl.BlockSpec((tm, dp), lambda i: (i, 0)),
                  pl.BlockSpec((tm, y_ssm.shape[1]), lambda i: (i, 0)),
                  pl.BlockSpec((tm, d), lambda i: (i, 0)),
                  pl.BlockSpec(w_out.shape, lambda i: (0, 0)),
                  pl.BlockSpec((1, d), lambda i: (0, 0)),
                  pl.BlockSpec((1, d), lambda i: (0, 0)),
                  pl.BlockSpec(rw.shape, lambda i: (0, 0)),
                  pl.BlockSpec((1, ROUTER_LANES), lambda i: (0, 0))],
        out_specs=(pl.BlockSpec((tm, d), lambda i: (i, 0)),
                   pl.BlockSpec((tm * pr, LANES), lambda i: (i, 0)),
                   pl.BlockSpec((ROUTER_ROWS, tm), lambda i: (0, i))),
        compiler_params=_cparams(("arbitrary",)),
        name="outproj",
    )(y_pool, y_ssm, xt, w_out, ln_g, ln_b, rw, rb)


def _router_kernel(lt_ref, eid_ref, rank_ref, gate_ref, cnt_ref, carry_ref):
    i = pl.program_id(0)
    ne, epg = N_EXPERTS, EXPERTS_PER_GROUP

    @pl.when(i == 0)
    def _():
        carry_ref[...] = jnp.zeros_like(carry_ref)

    lt = lt_ref[...]
    tt = lt.shape[1]
    gl = [lt[j:j + 1, :] for j in range(N_EXPERT_GROUPS)]
    gmax = jnp.maximum(jnp.maximum(gl[0], gl[1]), jnp.maximum(gl[2], gl[3]))
    grp = jnp.where(gl[0] == gmax, 0, jnp.where(gl[1] == gmax, 1, jnp.where(gl[2] == gmax, 2, 3)))
    p_grp = 1.0 / (jnp.exp(gl[0] - gmax) + jnp.exp(gl[1] - gmax)
                   + jnp.exp(gl[2] - gmax) + jnp.exp(gl[3] - gmax))
    eg = [lt[8 + epg * j: 8 + epg * (j + 1), :] for j in range(N_EXPERT_GROUPS)]
    el = jnp.where(grp == 0, eg[0], jnp.where(grp == 1, eg[1], jnp.where(grp == 2, eg[2], eg[3])))
    sub = lax.broadcasted_iota(I32, (epg, tt), 0)
    m1 = jnp.max(el, axis=0, keepdims=True)
    i1 = jnp.min(jnp.where(el == m1, sub, epg), axis=0, keepdims=True)
    rest = jnp.where(sub == i1, -jnp.inf, el)
    m2 = jnp.max(rest, axis=0, keepdims=True)
    i2 = jnp.min(jnp.where(rest == m2, sub, epg), axis=0, keepdims=True)
    r21 = jnp.exp(m2 - m1)
    g1 = p_grp / (1.0 + r21)
    g2 = g1 * r21
    e1 = grp * epg + i1
    e2 = grp * epg + i2

    rows = lax.broadcasted_iota(I32, (ne, tt), 0)
    oh1 = rows == e1
    oh2 = rows == e2
    oh = jnp.where(oh1 | oh2, 1.0, 0.0)
    tri = jnp.where(lax.broadcasted_iota(I32, (tt, tt), 0) < lax.broadcasted_iota(I32, (tt, tt), 1),
                    1.0, 0.0).astype(BF16)
    before = jnp.dot(oh.astype(BF16), tri, preferred_element_type=F32) + carry_ref[:, 0:1]
    r1 = jnp.sum(jnp.where(oh1, before, 0.0), axis=0, keepdims=True)
    r2 = jnp.sum(jnp.where(oh2, before, 0.0), axis=0, keepdims=True)
    carry_ref[...] = carry_ref[...] + jnp.sum(oh, axis=1, keepdims=True)

    eid_ref[...] = jnp.concatenate([e1, e2], axis=0)
    rank_ref[...] = jnp.concatenate([r1, r2], axis=0).astype(I32)
    gate_ref[...] = jnp.concatenate([g1, g2], axis=0)
    cnt_ref[...] = carry_ref[...].astype(I32)


def _router(lt):
    rr, t = lt.shape
    tt = min(512, t)
    return pl.pallas_call(
        _router_kernel,
        out_shape=(jax.ShapeDtypeStruct((2, t), I32),
                   jax.ShapeDtypeStruct((2, t), I32),
                   jax.ShapeDtypeStruct((2, t), F32),
                   jax.ShapeDtypeStruct((N_EXPERTS, 128), I32)),
        grid=(t // tt,),
        in_specs=[pl.BlockSpec((rr, tt), lambda i: (0, i))],
        out_specs=(pl.BlockSpec((2, tt), lambda i: (0, i)),
                   pl.BlockSpec((2, tt), lambda i: (0, i)),
                   pl.BlockSpec((2, tt), lambda i: (0, i)),
                   pl.BlockSpec((N_EXPERTS, 128), lambda i: (0, 0))),
        scratch_shapes=[pltpu.VMEM((N_EXPERTS, 128), F32)],
        compiler_params=_cparams(("arbitrary",)),
        name="router",
    )(lt)


def _invert_kernel(dest_ref, zeros_hbm, out_ref):
    pltpu.sync_copy(zeros_hbm, out_ref)
    t = dest_ref.shape[0] // 2

    def place(i, carry):
        out_ref[dest_ref[i]] = i
        out_ref[dest_ref[t + i]] = i
        return carry
    lax.fori_loop(0, t, place, 0, unroll=8)


def _invert(dest_flat, n_slots):
    return pl.pallas_call(
        _invert_kernel,
        out_shape=jax.ShapeDtypeStruct((n_slots,), I32),
        grid_spec=pltpu.PrefetchScalarGridSpec(
            num_scalar_prefetch=1, grid=(1,),
            in_specs=[pl.BlockSpec(memory_space=pl.ANY)],
            out_specs=pl.BlockSpec(memory_space=pltpu.SMEM)),
        compiler_params=_cparams(("arbitrary",)),
        name="invert",
    )(dest_flat, jnp.zeros((n_slots,), I32))


def _moe_kernel(be_ref, nu_ref, pc_ref, st_ref, hp_hbm, wg_hbm, wu_hbm, wd_hbm, y_ref,
                gbuf, wgf, wuf, wdf, wgb, wub, wdb, par_ref, gsem, wsem):
    b = pl.program_id(0)
    nu = nu_ref[0]
    pr = gbuf.shape[1] // MOE_BLK
    blk = MOE_BLK
    nbuf = gbuf.shape[0]
    look = nbuf - 1

    def row_copy(tok, slot, r):
        src = hp_hbm.at[pl.ds(pl.multiple_of(tok * pr, pr), pr)]
        off = r * pr if isinstance(r, int) else pl.multiple_of(r * pr, pr)
        return pltpu.make_async_copy(src, gbuf.at[slot, pl.ds(off, pr)], gsem.at[slot])

    def issue(block, slot):
        base = block * blk

        def body(r, carry):
            row_copy(st_ref[base + r], slot, r).start()
            return carry
        lax.fori_loop(0, blk, body, 0, unroll=8)

    def weight_copies(e, p):
        return (pltpu.make_async_copy(wg_hbm.at[e], wgf.at[p], wsem.at[p]),
                pltpu.make_async_copy(wu_hbm.at[e], wuf.at[p], wsem.at[p]),
                pltpu.make_async_copy(wd_hbm.at[e], wdf.at[p], wsem.at[p]))

    @pl.when(b == 0)
    def _():
        par_ref[0] = 0
        for cp in weight_copies(be_ref[0], 0):
            cp.start(priority=1)
        issue(0, 0)
        for j in range(1, look):
            @pl.when(j < nu)
            def _():
                issue(j, j)

    def issue_part(k):
        base = (b + look) * blk
        for r in range(k * blk // 4, (k + 1) * blk // 4):
            row_copy(st_ref[base + r], (b + look) % nbuf, r).start()

    def expert_mlp(slot, with_issue):
        los, his = _load_row_packed(gbuf.at[slot], blk, pr)
        x = jnp.concatenate([v.astype(BF16) for v in los + his], axis=1)
        if with_issue:
            issue_part(0)
        hg = jnp.dot(x, wgb[...], preferred_element_type=F32)
        if with_issue:
            issue_part(1)
        hu = jnp.dot(x, wub[...], preferred_element_type=F32)
        if with_issue:
            issue_part(2)
        hh = hg * (1.0 / (1.0 + jnp.exp(-hg))) * hu
        y = jnp.dot(hh.astype(BF16), wdb[...], preferred_element_type=F32)
        if with_issue:
            issue_part(3)
        _store_row_packed(y_ref, y)

    @pl.when(b < nu)
    def _():
        slot = b % nbuf
        e = be_ref[b]
        first = (b == 0) | (e != be_ref[jnp.maximum(b - 1, 0)])

        @pl.when(first)
        def _():
            p = par_ref[0]
            for cp in weight_copies(e, p):
                cp.wait()
            nxt = lax.while_loop(
                lambda c: (c < N_EXPERTS) & (pc_ref[jnp.minimum(c, N_EXPERTS - 1)] == 0),
                lambda c: c + 1, e + 1)

            @pl.when(nxt < N_EXPERTS)
            def _():
                for cp in weight_copies(nxt, 1 - p):
                    cp.start(priority=1)
            wgb[...] = wgf[p].astype(BF16)
            wub[...] = wuf[p].astype(BF16)
            wdb[...] = wdf[p].astype(BF16)
            par_ref[0] = 1 - p

        pltpu.make_async_copy(hp_hbm.at[pl.ds(0, blk * pr)], gbuf.at[slot], gsem.at[slot]).wait()

        @pl.when(b + look < nu)
        def _():
            expert_mlp(slot, True)

        @pl.when(b + look >= nu)
        def _():
            expert_mlp(slot, False)

    @pl.when(b >= nu)
    def _():
        y_ref[...] = jnp.zeros_like(y_ref)


def _moe(block_e, n_used, pcounts, slot_tok, hp, w_gate, w_up, w_down, n_blocks):
    ne, d, de = w_gate.shape
    blk = MOE_BLK
    pr = d // 2 // LANES
    grid_spec = pltpu.PrefetchScalarGridSpec(
        num_scalar_prefetch=4,
        grid=(n_blocks,),
        in_specs=[pl.BlockSpec(memory_space=pl.ANY)] * 4,
        out_specs=pl.BlockSpec((blk * pr, LANES), lambda b, *_: (b, 0)),
        scratch_shapes=[pltpu.VMEM((GATHER_BUFS, blk * pr, LANES), U32),
                        pltpu.VMEM((2, d, de), F32),
                        pltpu.VMEM((2, d, de), F32),
                        pltpu.VMEM((2, de, d), F32),
                        pltpu.VMEM((d, de), BF16),
                        pltpu.VMEM((d, de), BF16),
                        pltpu.VMEM((de, d), BF16),
                        pltpu.SMEM((1,), I32),
                        pltpu.SemaphoreType.DMA((GATHER_BUFS,)),
                        pltpu.SemaphoreType.DMA((2,))],
    )
    return pl.pallas_call(
        _moe_kernel,
        out_shape=jax.ShapeDtypeStruct((n_blocks * blk * pr, LANES), U32),
        grid_spec=grid_spec,
        compiler_params=_cparams(("arbitrary",)),
        name="moe",
    )(block_e, n_used, pcounts, slot_tok, hp, w_gate, w_up, w_down)


def _combine_kernel(d0_ref, d1_ref, h_ref, gate_ref, yb_hbm, g_ref, b_ref, out_ref, buf, sem, *, alpha):
    i = pl.program_id(0)
    n = pl.num_programs(0)
    tm = h_ref.shape[0]
    pr = buf.shape[2] // tm

    def issue(tile, slot):
        base = tile * tm

        def body(r, carry):
            dst = pl.ds(pl.multiple_of(r * pr, pr), pr)
            for k, dref in enumerate((d0_ref, d1_ref)):
                src = yb_hbm.at[pl.ds(pl.multiple_of(dref[base + r] * pr, pr), pr)]
                pltpu.make_async_copy(src, buf.at[slot, k, dst], sem.at[slot]).start()
            return carry
        lax.fori_loop(0, tm, body, 0, unroll=8)

    nbuf = buf.shape[0]
    look = nbuf - 1

    @pl.when(i == 0)
    def _():
        issue(0, 0)
        for j in range(1, look):
            @pl.when(j < n)
            def _():
                issue(j, j)

    slot = i % nbuf
    for k in range(2):
        pltpu.make_async_copy(yb_hbm.at[pl.ds(0, tm * pr)], buf.at[slot, k], sem.at[slot]).wait()

    nq = 4
    sub = tm // nq

    def issue_part(q):
        base = (i + look) * tm
        nslot = (i + look) % nbuf
        for r in range(q * sub, (q + 1) * sub):
            for k, dref in enumerate((d0_ref, d1_ref)):
                src = yb_hbm.at[pl.ds(pl.multiple_of(dref[base + r] * pr, pr), pr)]
                pltpu.make_async_copy(src, buf.at[nslot, k, pl.ds(r * pr, pr)], sem.at[nslot]).start()

    def finish(with_issue):
        for q in range(nq):
            if with_issue:
                issue_part(q)
            rows = pl.ds(q * sub, sub)
            lo0, hi0 = _load_row_packed(buf.at[slot, 0], sub, pr, row0=q * sub)
            lo1, hi1 = _load_row_packed(buf.at[slot, 1], sub, pr, row0=q * sub)
            g0 = gate_ref[rows, 0:1]
            g1 = gate_ref[rows, 1:2]
            ffn = jnp.concatenate([g0 * a + g1 * c for a, c in zip(lo0 + hi0, lo1 + hi1)], axis=1)
            out_ref[rows, :] = _layer_norm(alpha * h_ref[rows, :] + ffn, g_ref[...], b_ref[...])

    @pl.when(i + look < n)
    def _():
        finish(True)

    @pl.when(i + look >= n)
    def _():
        finish(False)


def _combine(dest0, dest1, h1, gate_t, yb, ln_g, ln_b, alpha):
    t, d = h1.shape
    tm = min(256, t)
    grid_spec = pltpu.PrefetchScalarGridSpec(
        num_scalar_prefetch=2,
        grid=(t // tm,),
        in_specs=[pl.BlockSpec((tm, d), lambda i, d0, d1: (i, 0)),
                  pl.BlockSpec((tm, 2), lambda i, d0, d1: (i, 0)),
                  pl.BlockSpec(memory_space=pl.ANY),
                  pl.BlockSpec((1, d), lambda i, d0, d1: (0, 0)),
                  pl.BlockSpec((1, d), lambda i, d0, d1: (0, 0))],
        out_specs=pl.BlockSpec((tm, d), lambda i, d0, d1: (i, 0)),
        scratch_shapes=[pltpu.VMEM((GATHER_BUFS, 2, tm * (d // 2 // LANES), LANES), U32),
                        pltpu.SemaphoreType.DMA((GATHER_BUFS,))],
    )
    return pl.pallas_call(
        functools.partial(_combine_kernel, alpha=alpha),
        out_shape=jax.ShapeDtypeStruct((t, d), F32),
        grid_spec=grid_spec,
        compiler_params=_cparams(("arbitrary",)),
        name="combine",
    )(dest0, dest1, h1, gate_t, yb, ln_g, ln_b)


def _layer(h, w_in, pool_w, pool_scale, a_re, a_im, log_dt, b_re, b_im, c_re, c_im, d_skip,
           glu_w, glu_b, w_out, ln1_g, ln1_b, rg_w, rg_b, re_w, re_b, w_gate, w_up, w_down,
           ln2_g, ln2_b, alpha):
    bsz, seq, d = h.shape
    t = bsz * seq
    L = CHUNK
    dp = pool_w.shape[0] * pool_w.shape[1]
    ds = w_in.shape[1] - dp
    g = ds // SSM_GROUP
    n = t // L

    kk, w1t, w2t, al = _ssm_tables(a_re, a_im, log_dt, b_re, b_im, c_re, c_im, d_skip, L)
    half = ROUTER_LANES // 2
    zpad = lambda k: jnp.zeros((d, k), F32)
    rw = jnp.concatenate([rg_w.astype(F32), zpad(8 - N_EXPERT_GROUPS), re_w.astype(F32),
                          zpad(half - ROUTER_ROWS)], axis=1)
    rw_hi = rw.astype(BF16)
    rw_lo = (rw - rw_hi.astype(F32)).astype(BF16)
    rw2 = jnp.concatenate([rw_hi, rw_lo], axis=1)
    rb = jnp.concatenate([rg_b.astype(F32), jnp.zeros((8 - N_EXPERT_GROUPS,), F32), re_b.astype(F32),
                          jnp.zeros((ROUTER_LANES - ROUTER_ROWS,), F32)]).reshape(1, ROUTER_LANES)

    xt = h.reshape(t, d)
    pool_p, ut = _proj(xt, w_in.astype(BF16), dp, L)
    y_pool = _pool(pool_p, pool_w.astype(BF16), pool_scale.reshape(1, dp).astype(F32), seq)
    yt = _ssm(ut.reshape(g, L * SSM_GROUP, n), kk, w1t, w2t, al, seq // L, L)
    y_ssm = _glu(yt.reshape(g, L, SSM_GROUP, n), glu_w.astype(BF16), glu_b.reshape(1, ds).astype(F32))
    h1, hp, lt = _outproj(y_pool, y_ssm, xt, w_out.astype(BF16), ln1_g.reshape(1, d), ln1_b.reshape(1, d),
                          rw2, rb, alpha)
    eid, rank, gate, cnt = _router(lt)

    blk = MOE_BLK
    m = 2 * t
    n_blocks = -(-m // blk) + N_EXPERTS
    counts = cnt[:, 0]
    pcounts = (counts + blk - 1) // blk * blk
    pends = jnp.cumsum(pcounts)
    pstarts = pends - pcounts
    e_ids = jnp.arange(N_EXPERTS, dtype=I32)
    dest = jnp.sum(jnp.where(eid[..., None] == e_ids, pstarts, 0), axis=-1) + rank
    slot_tok = _invert(dest.reshape(-1), n_blocks * blk)
    n_used = (pends[-1] // blk).astype(I32)
    bidx = jnp.minimum(jnp.arange(n_blocks, dtype=I32), n_used - 1)
    block_e = jnp.minimum(jnp.sum((pends[None, :] <= (bidx * blk)[:, None]).astype(I32), axis=1),
                          N_EXPERTS - 1)

    yb = _moe(block_e, n_used.reshape(1), pcounts, slot_tok, hp, w_gate, w_up, w_down, n_blocks)
    out = _combine(dest[0], dest[1], h1, gate.T, yb, ln2_g.reshape(1, d), ln2_b.reshape(1, d), alpha)
    return out.reshape(bsz, seq, d)


def kernel(x, w_in, pool_w, pool_scale, ssm_a_re, ssm_a_im, ssm_log_dt, ssm_b_re, ssm_b_im, ssm_c_re, ssm_c_im, ssm_d, glu_w, glu_b, w_out, ln1_g, ln1_b, router_g_w, router_g_b, router_e_w, router_e_b, w_gate, w_up, w_down, ln2_g, ln2_b):
    depth = w_in.shape[0]
    alpha = (2.0 * depth) ** 0.25
    h = x
    for l in range(depth):
        h = _layer(h, w_in[l], pool_w[l], pool_scale[l], ssm_a_re[l], ssm_a_im[l], ssm_log_dt[l],
                   ssm_b_re[l], ssm_b_im[l], ssm_c_re[l], ssm_c_im[l], ssm_d[l], glu_w[l], glu_b[l],
                   w_out[l], ln1_g[l], ln1_b[l], router_g_w[l], router_g_b[l], router_e_w[l],
                   router_e_b[l], w_gate[l], w_up[l], w_down[l], ln2_g[l], ln2_b[l], alpha)
    return h
```

```python
import functools
import math

import numpy as np
import jax
import jax.numpy as jnp
from jax import lax
from jax.experimental import pallas as pl
from jax.experimental.pallas import tpu as pltpu

F32 = jnp.float32
BF16 = jnp.bfloat16
I32 = jnp.int32
U32 = jnp.uint32

POOL_WINDOWS = (2, 4, 8, 16)
POOL_GROUP = 256
SSM_GROUP = 16
SSM_STATE = 64
N_EXPERT_GROUPS = 4
EXPERTS_PER_GROUP = 8
N_EXPERTS = N_EXPERT_GROUPS * EXPERTS_PER_GROUP
LN_EPS = 1e-5

CHUNK = 16
MOE_BLK = 256
GATHER_BUFS = 4
ROUTER_ROWS = 8 + N_EXPERTS
ROUTER_LANES = 128
LANES = 128
MXU_N = 256
HALO = 16
VMEM_LIMIT = 56 * 1024 * 1024


def _cparams(sem, vmem=VMEM_LIMIT):
    return pltpu.CompilerParams(dimension_semantics=sem, vmem_limit_bytes=vmem)


def _proj_kernel(x_ref, w_ref, pool_ref, ut_ref, accp_ref, accs_ref):
    kk = pl.program_id(1)
    rows, dp = accp_ref.shape
    nj = accs_ref.shape[0]

    @pl.when(kk == 0)
    def _():
        accp_ref[...] = jnp.zeros_like(accp_ref)
        accs_ref[...] = jnp.zeros_like(accs_ref)

    xb = x_ref[...].astype(BF16)
    for n in range(dp // MXU_N):
        cs = slice(n * MXU_N, (n + 1) * MXU_N)
        accp_ref[:, cs] += jnp.dot(xb, w_ref[:, cs], preferred_element_type=F32)
    per = MXU_N // LANES
    for n in range(nj // per):
        part = jnp.dot(xb, w_ref[:, dp + n * MXU_N: dp + (n + 1) * MXU_N], preferred_element_type=F32)
        for q in range(per):
            accs_ref[n * per + q] += part[:, q * LANES:(q + 1) * LANES]

    @pl.when(kk == pl.num_programs(1) - 1)
    def _():
        pool_ref[...] = accp_ref[...].astype(BF16)
        g, L, c, r = ut_ref.shape
        gj = LANES // c
        for s in range(L):
            for j in range(nj):
                piece = accs_ref[j, pl.ds(s, r, stride=L), :]
                ut_ref[j * gj:(j + 1) * gj, s, :, :] = piece.astype(BF16).T.reshape(gj, c, r)


def _proj(xt, w_in, dp, L):
    t, d = xt.shape
    dm = w_in.shape[1]
    g = (dm - dp) // SSM_GROUP
    tm = LANES * L
    kb = 512
    return pl.pallas_call(
        _proj_kernel,
        out_shape=(jax.ShapeDtypeStruct((t, dp), BF16),
                   jax.ShapeDtypeStruct((g, L, SSM_GROUP, t // L), BF16)),
        grid=(t // tm, d // kb),
        in_specs=[pl.BlockSpec((tm, kb), lambda i, k: (i, k)),
                  pl.BlockSpec((kb, dm), lambda i, k: (k, 0))],
        out_specs=(pl.BlockSpec((tm, dp), lambda i, k: (i, 0)),
                   pl.BlockSpec((g, L, SSM_GROUP, LANES), lambda i, k: (0, 0, 0, i))),
        scratch_shapes=[pltpu.VMEM((tm, dp), F32),
                        pltpu.VMEM(((dm - dp) // LANES, tm, LANES), F32)],
        compiler_params=_cparams(("arbitrary", "arbitrary")),
        name="proj",
    )(xt, w_in)


def _pool_kernel(prev_ref, main_ref, next_ref, pw_ref, sc_ref, out_ref, *, seq, sub):
    i = pl.program_id(0)
    ts = main_ref.shape[0]
    ext = jnp.concatenate([prev_ref[...], main_ref[...], next_ref[...]], axis=0)
    k = sub + 2 * HALO
    row = lax.broadcasted_iota(I32, (sub, k), 0)
    col = lax.broadcasted_iota(I32, (sub, k), 1)
    off0 = col - HALO - row
    for sb in range(ts // sub):
        rows = slice(sb * sub, (sb + 1) * sub)
        base = (i * ts + sb * sub) % seq
        src = base + row + off0
        off = jnp.where((src >= 0) & (src < seq), off0, 2 * HALO)
        pos = base + lax.broadcasted_iota(I32, (sub, 1), 0)
        for g, w in enumerate(POOL_WINDOWS):
            sl = slice(g * POOL_GROUP, (g + 1) * POOL_GROUP)
            band = (off + w // 2).astype(U32) < w
            bm = jnp.where(band, 1.0, 0.0).astype(BF16)
            sums = jnp.dot(bm, ext[sb * sub:sb * sub + k, sl], preferred_element_type=F32)
            lo = jnp.maximum(pos - w // 2, 0)
            hi = jnp.minimum(pos - w // 2 + w, seq)
            inv = 1.0 / (hi - lo).astype(F32)
            dlt = sums * inv - main_ref[rows, sl].astype(F32)
            y = jnp.dot(dlt.astype(BF16), pw_ref[g], preferred_element_type=F32) * sc_ref[:, sl]
            out_ref[rows, sl] = y.astype(BF16)


def _pool(pp, pool_w, pool_scale, seq):
    t, dp = pp.shape
    ts = min(512, seq)
    nh = t // HALO
    per = ts // HALO
    return pl.pallas_call(
        functools.partial(_pool_kernel, seq=seq, sub=256),
        out_shape=jax.ShapeDtypeStruct((t, dp), BF16),
        grid=(t // ts,),
        in_specs=[pl.BlockSpec((HALO, dp), lambda i: (jnp.maximum(i * per - 1, 0), 0)),
                  pl.BlockSpec((ts, dp), lambda i: (i, 0)),
                  pl.BlockSpec((HALO, dp), lambda i: (jnp.minimum((i + 1) * per, nh - 1), 0)),
                  pl.BlockSpec(pool_w.shape, lambda i: (0, 0, 0)),
                  pl.BlockSpec((1, dp), lambda i: (0, 0))],
        out_specs=pl.BlockSpec((ts, dp), lambda i: (i, 0)),
        compiler_params=_cparams(("arbitrary",)),
        name="pool",
    )(pp, pp, pp, pool_w, pool_scale)


def _ssm_tables(a_re, a_im, log_dt, b_re, b_im, c_re, c_im, d_skip, L):
    g = a_re.shape[1]
    p = a_re.shape[2]
    c = b_re.shape[3]
    fl = L * c
    lr = a_re.astype(F32)
    li = a_im.astype(F32)
    dt = jnp.exp(log_dt.astype(F32))[..., None]
    mag = jnp.exp(lr * dt)
    abr = mag * jnp.cos(li * dt)
    abi = mag * jnp.sin(li * dt)
    den = lr * lr + li * li
    zr = ((abr - 1.0) * lr + abi * li) / den
    zi = (abi * lr - (abr - 1.0) * li) / den
    br = b_re.astype(F32)
    bi = b_im.astype(F32)
    bbr = zr[..., None] * br - zi[..., None] * bi
    bbi = zr[..., None] * bi + zi[..., None] * br
    cr = c_re.astype(F32)
    ci = c_im.astype(F32)
    kk = jnp.arange(L + 1, dtype=F32)[None, None, :, None]
    pm = jnp.exp(kk * (lr * dt)[:, :, None, :])
    ang = kk * (li * dt)[:, :, None, :]
    pr = pm * jnp.cos(ang)
    pi = pm * jnp.sin(ang)

    skip = jnp.eye(c, dtype=F32)[None] * d_skip.astype(F32).reshape(g, c, 1)
    lag0 = jnp.einsum('dgop,dgpc->goc', cr, bbr, precision=lax.Precision.HIGHEST) \
        - jnp.einsum('dgop,dgpc->goc', ci, bbi, precision=lax.Precision.HIGHEST) + skip
    lag0 = jnp.pad(lag0, ((0, 0), (0, 0), (0, LANES - c)))
    bb = jnp.concatenate([bbr[0], bbi[0], bbr[1], bbi[1]], axis=1)
    bb = jnp.pad(bb, ((0, 0), (0, 0), (0, LANES - c)))

    def w1(d, qr, qi):
        qr = qr.transpose(0, 2, 1)[..., None]
        qi = qi.transpose(0, 2, 1)[..., None]
        re = qr * bbr[d][:, :, None] - qi * bbi[d][:, :, None]
        im = qr * bbi[d][:, :, None] + qi * bbr[d][:, :, None]
        to = lambda v: v.reshape(g, p, fl)
        return to(re), to(im)
    f_re, f_im = w1(0, pr[0][:, L - 1::-1], pi[0][:, L - 1::-1])
    b_re_, b_im_ = w1(1, pr[1][:, :L], pi[1][:, :L])
    w1t = jnp.concatenate([f_re, f_im, b_re_, b_im_], axis=1)

    def w2(d, qr, qi):
        re = cr[d][:, None] * qr[:, :, None] - ci[d][:, None] * qi[:, :, None]
        im = -(cr[d][:, None] * qi[:, :, None] + ci[d][:, None] * qr[:, :, None])
        to = lambda v: v.reshape(g, fl, p)
        return to(re), to(im)
    rf_re, rf_im = w2(0, pr[0][:, 1:], pi[0][:, 1:])
    rb_re, rb_im = w2(1, pr[1][:, :0:-1], pi[1][:, :0:-1])
    w2t = jnp.concatenate([rf_re, rf_im, rb_re, rb_im], axis=2)

    a4 = jnp.stack([pr[0][:, L], pi[0][:, L], pr[1][:, L], pi[1][:, L]], axis=1)
    ap = a4.reshape(g // 2, 2, 4, p).transpose(0, 2, 1, 3).reshape(g // 2, 4, 2 * p)
    ap = jnp.pad(ap, ((0, 0), (0, 4), (0, 0)))
    return lag0, bb.astype(BF16), w1t.astype(BF16), w2t.astype(BF16), ap


def _cmul(ar, ai, br, bi):
    return ar * br - ai * bi, ar * bi + ai * br


def _ssm_kernel(ut_ref, w1_ref, lag0_ref, bb_ref, w2_ref, a_ref, yt_ref, mt_s, kk_s, zt_s, ht_s, *, nseg, L):
    p = SSM_STATE
    c = SSM_GROUP
    fl, n = ut_ref.shape[1:]
    nb = n // nseg
    nrow = nseg // 8
    st = [jnp.dot(w1_ref[q], ut_ref[q], preferred_element_type=F32) for q in range(2)]
    z = jnp.concatenate([st[q][k * p:(k + 1) * p] for k in range(4) for q in range(2)], axis=0)
    zt_s[...] = z.T
    per = LANES // c
    for q in range(2):
        kf = jnp.dot(w2_ref[q, :, 0:2 * p], bb_ref[q, 0:2 * p, :], preferred_element_type=F32)
        kb = jnp.dot(w2_ref[q, :, 2 * p:4 * p], bb_ref[q, 2 * p:4 * p, :], preferred_element_type=F32)
        tiles = [None] * (2 * L // per)
        for j in range(2 * L - 1):
            lag = L - 1 - j
            if lag > 0:
                blk = kf[(lag - 1) * c:lag * c, :]
            elif lag == 0:
                blk = lag0_ref[q]
            else:
                blk = kb[(L + lag) * c:(L + lag + 1) * c, :]
            sh = (j % per) * c
            blk = blk if sh == 0 else pltpu.roll(blk, sh, 1)
            tiles[j // per] = blk if tiles[j // per] is None else tiles[j // per] + blk
        kk_s[...] = jnp.concatenate(tiles, axis=1).astype(BF16)
        for t in range(L):
            mt_s[q, t * c:(t + 1) * c, :] = kk_s[:, (L - 1 - t) * c:(L - 1 - t) * c + fl]

    sub = lax.broadcasted_iota(I32, (8, LANES), 0)
    full = lambda v: jnp.broadcast_to(v, (8, LANES))
    one, zero = jnp.ones((8, LANES), F32), jnp.zeros((8, LANES), F32)

    def powers(row):
        out = [(full(a_ref[0, row:row + 1, :]), full(a_ref[0, row + 1:row + 2, :]))]
        for _ in range(3):
            out.append(_cmul(*out[-1], *out[-1]))
        return out

    def by_bits(pw, idx):
        qr, qi = one, zero
        for bit in range(3):
            on = (idx >> bit) & 1 == 1
            qr, qi = _cmul(qr, qi, jnp.where(on, pw[bit][0], one), jnp.where(on, pw[bit][1], zero))
        return qr, qi

    pw_f, pw_b = powers(0), powers(2)
    qf = by_bits(pw_f, sub)
    qb = by_bits(pw_b, 7 - sub)

    def local(xr, xi, pw, forward):
        for lvl, d in enumerate((1, 2, 4)):
            keep = (sub >= d) if forward else (sub < 8 - d)
            sh = d if forward else 8 - d
            tr = jnp.where(keep, pltpu.roll(xr, sh, 0), 0.0)
            ti = jnp.where(keep, pltpu.roll(xi, sh, 0), 0.0)
            mr, mi = _cmul(pw[lvl][0], pw[lvl][1], tr, ti)
            xr, xi = xr + mr, xi + mi
        return xr, xi

    def step(i, carry):
        new = []
        for b in range(nb):
            for fwd in (True, False):
                cr, ci = carry[2 * (2 * b + (0 if fwd else 1)):][:2]
                row = b * nseg + (i if fwd else nrow - 1 - i) * 8
                rows = pl.ds(pl.multiple_of(row, 8), 8)
                l0 = 0 if fwd else 2 * LANES
                pw, (qr, qi) = (pw_f, qf) if fwd else (pw_b, qb)
                lr, li = local(zt_s[rows, l0:l0 + LANES], zt_s[rows, l0 + LANES:l0 + 2 * LANES], pw, fwd)
                keep = (sub >= 1) if fwd else (sub < 7)
                er = jnp.where(keep, pltpu.roll(lr, 1 if fwd else 7, 0), 0.0)
                ei = jnp.where(keep, pltpu.roll(li, 1 if fwd else 7, 0), 0.0)
                mr, mi = _cmul(qr, qi, cr, ci)
                ht_s[rows, l0:l0 + LANES] = er + mr
                ht_s[rows, l0 + LANES:l0 + 2 * LANES] = ei + mi
                edge = 7 if fwd else 0
                mr, mi = _cmul(pw[3][0], pw[3][1], cr, ci)
                new += [full(lr[edge:edge + 1, :]) + mr, full(li[edge:edge + 1, :]) + mi]
        return tuple(new)

    lax.fori_loop(0, nrow, step, (zero,) * (4 * nb))

    h = ht_s[...].astype(BF16).T
    for q in range(2):
        hq = jnp.concatenate([h[(2 * k + q) * p:(2 * k + q + 1) * p] for k in range(4)], axis=0)
        y = jnp.dot(mt_s[q], ut_ref[q], preferred_element_type=F32) \
            + jnp.dot(w2_ref[q], hq, preferred_element_type=F32)
        yt_ref[q] = y.astype(BF16)


def _ssm(ut3, lag0, bb, w1t, w2t, ap, nseg, L):
    g, fl, n = ut3.shape
    sp = w1t.shape[1]
    return pl.pallas_call(
        functools.partial(_ssm_kernel, nseg=nseg, L=L),
        out_shape=jax.ShapeDtypeStruct((g, fl, n), BF16),
        grid=(g // 2,),
        in_specs=[pl.BlockSpec((2, fl, n), lambda i: (i, 0, 0)),
                  pl.BlockSpec((2, sp, fl), lambda i: (i, 0, 0)),
                  pl.BlockSpec((2,) + lag0.shape[1:], lambda i: (i, 0, 0)),
                  pl.BlockSpec((2,) + bb.shape[1:], lambda i: (i, 0, 0)),
                  pl.BlockSpec((2, fl, sp), lambda i: (i, 0, 0)),
                  pl.BlockSpec((1,) + ap.shape[1:], lambda i: (i, 0, 0))],
        out_specs=pl.BlockSpec((2, fl, n), lambda i: (i, 0, 0)),
        scratch_shapes=[pltpu.VMEM((2, fl, fl), BF16),
                        pltpu.VMEM((SSM_GROUP, 2 * fl), BF16),
                        pltpu.VMEM((n, 2 * sp), F32),
                        pltpu.VMEM((n, 2 * sp), F32)],
        compiler_params=_cparams(("arbitrary",)),
        name="ssm",
    )(ut3, w1t, lag0, bb, w2t, ap)


def _glu_kernel(yt_ref, w_ref, b_ref, out_ref, nat_ref, *, sub):
    g, L, c, r = yt_ref.shape
    nj = nat_ref.shape[0]
    gj = LANES // c
    for t in range(L):
        for j in range(nj):
            piece = yt_ref[j * gj:(j + 1) * gj, t, :, :].reshape(LANES, r).astype(F32)
            nat_ref[j, pl.ds(t, r, stride=L), :] = piece.T

    def body(k, carry):
        sl = pl.ds(pl.multiple_of(k * sub, sub), sub)
        y = jnp.concatenate([nat_ref[j, sl, :] for j in range(nj)], axis=1)
        ya = jax.nn.gelu(y, approximate=True)
        z = jnp.dot(ya.astype(BF16), w_ref[...], preferred_element_type=F32) + b_ref[...]
        out_ref[sl, :] = (ya * (1.0 / (1.0 + jnp.exp(-z)))).astype(BF16)
        return carry
    lax.fori_loop(0, nat_ref.shape[1] // sub, body, 0)


def _glu(yt4, glu_w, glu_b):
    g, L, c, n = yt4.shape
    ds = g * c
    r = LANES
    return pl.pallas_call(
        functools.partial(_glu_kernel, sub=512),
        out_shape=jax.ShapeDtypeStruct((n * L, ds), BF16),
        grid=(n // r,),
        in_specs=[pl.BlockSpec((g, L, c, r), lambda j: (0, 0, 0, j)),
                  pl.BlockSpec((ds, ds), lambda j: (0, 0)),
                  pl.BlockSpec((1, ds), lambda j: (0, 0))],
        out_specs=pl.BlockSpec((r * L, ds), lambda j: (j, 0)),
        scratch_shapes=[pltpu.VMEM((ds // LANES, r * L, LANES), F32)],
        compiler_params=_cparams(("arbitrary",)),
        name="glu",
    )(yt4, glu_w, glu_b)


def _layer_norm(r, g, b):
    mu = jnp.mean(r, axis=-1, keepdims=True)
    cen = r - mu
    var = jnp.mean(cen * cen, axis=-1, keepdims=True)
    return cen * lax.rsqrt(var + LN_EPS) * g + b


def _store_row_packed(ref, m, row0=0):
    rows, width = m.shape
    half = width // 2
    nc = half // LANES
    lo = lax.bitcast_convert_type(m[:, :half].astype(BF16).astype(F32), U32) >> 16
    hi = lax.bitcast_convert_type(m[:, half:].astype(BF16).astype(F32), U32) & jnp.uint32(0xFFFF0000)
    pk = lo | hi
    for c in range(nc):
        ref[pl.ds(row0 * nc + c, rows, stride=nc), :] = pk[:, c * LANES:(c + 1) * LANES]


def _load_row_packed(ref, rows, nc, row0=0):
    los, his = [], []
    for c in range(nc):
        u = ref[pl.ds(row0 * nc + c, rows, stride=nc), :]
        los.append(lax.bitcast_convert_type(u << 16, F32))
        his.append(lax.bitcast_convert_type(u & jnp.uint32(0xFFFF0000), F32))
    return los, his


def _outproj_kernel(yp_ref, ys_ref, x_ref, wo_ref, g_ref, b_ref, rw_ref, rb_ref,
                    h_ref, hp_ref, lt_ref, *, alpha, dp, sub):
    for sb in range(h_ref.shape[0] // sub):
        sl = pl.ds(sb * sub, sub)
        mix = jnp.dot(yp_ref[sl, :], wo_ref[0:dp, :], preferred_element_type=F32) \
            + jnp.dot(ys_ref[sl, :], wo_ref[dp:, :], preferred_element_type=F32)
        h = _layer_norm(alpha * x_ref[sl, :] + mix, g_ref[...], b_ref[...])
        h_ref[sl, :] = h
        _store_row_packed(hp_ref, h, row0=sb * sub)
        hh = h.astype(BF16)
        hl = (h - hh.astype(F32)).astype(BF16)
        part = jnp.dot(hh, rw_ref[...], preferred_element_type=F32) \
            + jnp.dot(hl, rw_ref[...], preferred_element_type=F32)
        logits = part + pltpu.roll(part, ROUTER_LANES // 2, 1) + rb_ref[...]
        lt_ref[:, sb * sub:(sb + 1) * sub] = logits.T[0:ROUTER_ROWS, :]


def _outproj(y_pool, y_ssm, xt, w_out, ln_g, ln_b, rw, rb, alpha):
    t, d = xt.shape
    dp = y_pool.shape[1]
    tm = min(512, t)
    pr = d // 2 // LANES
    return pl.pallas_call(
        functools.partial(_outproj_kernel, alpha=alpha, dp=dp, sub=256),
        out_shape=(jax.ShapeDtypeStruct((t, d), F32),
                   jax.ShapeDtypeStruct((t * pr, LANES), U32),
                   jax.ShapeDtypeStruct((ROUTER_ROWS, t), F32)),
        grid=(t // tm,),
        in_specs=[pl.BlockSpec((tm, dp), lambda i: (i, 0)),
                  pl.BlockSpec((tm, y_ssm.shape[1]), lambda i: (i, 0)),
                  pl.BlockSpec((tm, d), lambda i: (i, 0)),
                  pl.BlockSpec(w_out.shape, lambda i: (0, 0)),
                  pl.BlockSpec((1, d), lambda i: (0, 0)),
                  pl.BlockSpec((1, d), lambda i: (0, 0)),
                  pl.BlockSpec(rw.shape, lambda i: (0, 0)),
                  pl.BlockSpec((1, ROUTER_LANES), lambda i: (0, 0))],
        out_specs=(pl.BlockSpec((tm, d), lambda i: (i, 0)),
                   pl.BlockSpec((tm * pr, LANES), lambda i: (i, 0)),
                   pl.BlockSpec((ROUTER_ROWS, tm), lambda i: (0, i))),
        compiler_params=_cparams(("arbitrary",)),
        name="outproj",
    )(y_pool, y_ssm, xt, w_out, ln_g, ln_b, rw, rb)


def _router_kernel(lt_ref, eid_ref, rank_ref, gate_ref, cnt_ref, carry_ref):
    i = pl.program_id(0)
    ne, epg = N_EXPERTS, EXPERTS_PER_GROUP

    @pl.when(i == 0)
    def _():
        carry_ref[...] = jnp.zeros_like(carry_ref)

    lt = lt_ref[...]
    tt = lt.shape[1]
    gl = [lt[j:j + 1, :] for j in range(N_EXPERT_GROUPS)]
    gmax = jnp.maximum(jnp.maximum(gl[0], gl[1]), jnp.maximum(gl[2], gl[3]))
    grp = jnp.where(gl[0] == gmax, 0, jnp.where(gl[1] == gmax, 1, jnp.where(gl[2] == gmax, 2, 3)))
    p_grp = 1.0 / (jnp.exp(gl[0] - gmax) + jnp.exp(gl[1] - gmax)
                   + jnp.exp(gl[2] - gmax) + jnp.exp(gl[3] - gmax))
    eg = [lt[8 + epg * j: 8 + epg * (j + 1), :] for j in range(N_EXPERT_GROUPS)]
    el = jnp.where(grp == 0, eg[0], jnp.where(grp == 1, eg[1], jnp.where(grp == 2, eg[2], eg[3])))
    sub = lax.broadcasted_iota(I32, (epg, tt), 0)
    m1 = jnp.max(el, axis=0, keepdims=True)
    i1 = jnp.min(jnp.where(el == m1, sub, epg), axis=0, keepdims=True)
    rest = jnp.where(sub == i1, -jnp.inf, el)
    m2 = jnp.max(rest, axis=0, keepdims=True)
    i2 = jnp.min(jnp.where(rest == m2, sub, epg), axis=0, keepdims=True)
    r21 = jnp.exp(m2 - m1)
    g1 = p_grp / (1.0 + r21)
    g2 = g1 * r21
    e1 = grp * epg + i1
    e2 = grp * epg + i2

    rows = lax.broadcasted_iota(I32, (ne, tt), 0)
    oh1 = rows == e1
    oh2 = rows == e2
    oh = jnp.where(oh1 | oh2, 1.0, 0.0)
    tri = jnp.where(lax.broadcasted_iota(I32, (tt, tt), 0) < lax.broadcasted_iota(I32, (tt, tt), 1),
                    1.0, 0.0).astype(BF16)
    before = jnp.dot(oh.astype(BF16), tri, preferred_element_type=F32) + carry_ref[:, 0:1]
    r1 = jnp.sum(jnp.where(oh1, before, 0.0), axis=0, keepdims=True)
    r2 = jnp.sum(jnp.where(oh2, before, 0.0), axis=0, keepdims=True)
    carry_ref[...] = carry_ref[...] + jnp.sum(oh, axis=1, keepdims=True)

    eid_ref[...] = jnp.concatenate([e1, e2], axis=0)
    rank_ref[...] = jnp.concatenate([r1, r2], axis=0).astype(I32)
    gate_ref[...] = jnp.concatenate([g1, g2], axis=0)
    cnt_ref[...] = carry_ref[...].astype(I32)


def _router(lt):
    rr, t = lt.shape
    tt = min(512, t)
    return pl.pallas_call(
        _router_kernel,
        out_shape=(jax.ShapeDtypeStruct((2, t), I32),
                   jax.ShapeDtypeStruct((2, t), I32),
                   jax.ShapeDtypeStruct((2, t), F32),
                   jax.ShapeDtypeStruct((N_EXPERTS, 128), I32)),
        grid=(t // tt,),
        in_specs=[pl.BlockSpec((rr, tt), lambda i: (0, i))],
        out_specs=(pl.BlockSpec((2, tt), lambda i: (0, i)),
                   pl.BlockSpec((2, tt), lambda i: (0, i)),
                   pl.BlockSpec((2, tt), lambda i: (0, i)),
                   pl.BlockSpec((N_EXPERTS, 128), lambda i: (0, 0))),
        scratch_shapes=[pltpu.VMEM((N_EXPERTS, 128), F32)],
        compiler_params=_cparams(("arbitrary",)),
        name="router",
    )(lt)


def _invert_kernel(dest_ref, zeros_hbm, out_ref):
    pltpu.sync_copy(zeros_hbm, out_ref)
    t = dest_ref.shape[0] // 2

    def place(i, carry):
        out_ref[dest_ref[i]] = i
        out_ref[dest_ref[t + i]] = i
        return carry
    lax.fori_loop(0, t, place, 0, unroll=8)


def _invert(dest_flat, n_slots):
    return pl.pallas_call(
        _invert_kernel,
        out_shape=jax.ShapeDtypeStruct((n_slots,), I32),
        grid_spec=pltpu.PrefetchScalarGridSpec(
            num_scalar_prefetch=1, grid=(1,),
            in_specs=[pl.BlockSpec(memory_space=pl.ANY)],
            out_specs=pl.BlockSpec(memory_space=pltpu.SMEM)),
        compiler_params=_cparams(("arbitrary",)),
        name="invert",
    )(dest_flat, jnp.zeros((n_slots,), I32))


def _moe_kernel(be_ref, nu_ref, pc_ref, st_ref, hp_hbm, wg_hbm, wu_hbm, wd_hbm, y_ref,
                gbuf, wgf, wuf, wdf, wgb, wub, wdb, par_ref, gsem, wsem):
    b = pl.program_id(0)
    nu = nu_ref[0]
    pr = gbuf.shape[1] // MOE_BLK
    blk = MOE_BLK
    nbuf = gbuf.shape[0]
    look = nbuf - 1

    def row_copy(tok, slot, r):
        src = hp_hbm.at[pl.ds(pl.multiple_of(tok * pr, pr), pr)]
        off = r * pr if isinstance(r, int) else pl.multiple_of(r * pr, pr)
        return pltpu.make_async_copy(src, gbuf.at[slot, pl.ds(off, pr)], gsem.at[slot])

    def issue(block, slot):
        base = block * blk

        def body(r, carry):
            row_copy(st_ref[base + r], slot, r).start()
            return carry
        lax.fori_loop(0, blk, body, 0, unroll=8)

    def weight_copies(e, p):
        return (pltpu.make_async_copy(wg_hbm.at[e], wgf.at[p], wsem.at[p]),
                pltpu.make_async_copy(wu_hbm.at[e], wuf.at[p], wsem.at[p]),
                pltpu.make_async_copy(wd_hbm.at[e], wdf.at[p], wsem.at[p]))

    @pl.when(b == 0)
    def _():
        par_ref[0] = 0
        for cp in weight_copies(be_ref[0], 0):
            cp.start(priority=1)
        issue(0, 0)
        for j in range(1, look):
            @pl.when(j < nu)
            def _():
                issue(j, j)

    def issue_part(k):
        base = (b + look) * blk
        for r in range(k * blk // 4, (k + 1) * blk // 4):
            row_copy(st_ref[base + r], (b + look) % nbuf, r).start()

    def expert_mlp(slot, with_issue):
        los, his = _load_row_packed(gbuf.at[slot], blk, pr)
        x = jnp.concatenate([v.astype(BF16) for v in los + his], axis=1)
        if with_issue:
            issue_part(0)
        hg = jnp.dot(x, wgb[...], preferred_element_type=F32)
        if with_issue:
            issue_part(1)
        hu = jnp.dot(x, wub[...], preferred_element_type=F32)
        if with_issue:
            issue_part(2)
        hh = hg * (1.0 / (1.0 + jnp.exp(-hg))) * hu
        y = jnp.dot(hh.astype(BF16), wdb[...], preferred_element_type=F32)
        if with_issue:
            issue_part(3)
        _store_row_packed(y_ref, y)

    @pl.when(b < nu)
    def _():
        slot = b % nbuf
        e = be_ref[b]
        first = (b == 0) | (e != be_ref[jnp.maximum(b - 1, 0)])

        @pl.when(first)
        def _():
            p = par_ref[0]
            for cp in weight_copies(e, p):
                cp.wait()
            nxt = lax.while_loop(
                lambda c: (c < N_EXPERTS) & (pc_ref[jnp.minimum(c, N_EXPERTS - 1)] == 0),
                lambda c: c + 1, e + 1)

            @pl.when(nxt < N_EXPERTS)
            def _():
                for cp in weight_copies(nxt, 1 - p):
                    cp.start(priority=1)
            wgb[...] = wgf[p].astype(BF16)
            wub[...] = wuf[p].astype(BF16)
            wdb[...] = wdf[p].astype(BF16)
            par_ref[0] = 1 - p

        pltpu.make_async_copy(hp_hbm.at[pl.ds(0, blk * pr)], gbuf.at[slot], gsem.at[slot]).wait()

        @pl.when(b + look < nu)
        def _():
            expert_mlp(slot, True)

        @pl.when(b + look >= nu)
        def _():
            expert_mlp(slot, False)

    @pl.when(b >= nu)
    def _():
        y_ref[...] = jnp.zeros_like(y_ref)


def _moe(block_e, n_used, pcounts, slot_tok, hp, w_gate, w_up, w_down, n_blocks):
    ne, d, de = w_gate.shape
    blk = MOE_BLK
    pr = d // 2 // LANES
    grid_spec = pltpu.PrefetchScalarGridSpec(
        num_scalar_prefetch=4,
        grid=(n_blocks,),
        in_specs=[pl.BlockSpec(memory_space=pl.ANY)] * 4,
        out_specs=pl.BlockSpec((blk * pr, LANES), lambda b, *_: (b, 0)),
        scratch_shapes=[pltpu.VMEM((GATHER_BUFS, blk * pr, LANES), U32),
                        pltpu.VMEM((2, d, de), F32),
                        pltpu.VMEM((2, d, de), F32),
                        pltpu.VMEM((2, de, d), F32),
                        pltpu.VMEM((d, de), BF16),
                        pltpu.VMEM((d, de), BF16),
                        pltpu.VMEM((de, d), BF16),
                        pltpu.SMEM((1,), I32),
                        pltpu.SemaphoreType.DMA((GATHER_BUFS,)),
                        pltpu.SemaphoreType.DMA((2,))],
    )
    return pl.pallas_call(
        _moe_kernel,
        out_shape=jax.ShapeDtypeStruct((n_blocks * blk * pr, LANES), U32),
        grid_spec=grid_spec,
        compiler_params=_cparams(("arbitrary",)),
        name="moe",
    )(block_e, n_used, pcounts, slot_tok, hp, w_gate, w_up, w_down)


def _combine_kernel(d0_ref, d1_ref, h_ref, gate_ref, yb_hbm, g_ref, b_ref, out_ref, buf, sem, *, alpha):
    i = pl.program_id(0)
    n = pl.num_programs(0)
    tm = h_ref.shape[0]
    pr = buf.shape[2] // tm

    def issue(tile, slot):
        base = tile * tm

        def body(r, carry):
            dst = pl.ds(pl.multiple_of(r * pr, pr), pr)
            for k, dref in enumerate((d0_ref, d1_ref)):
                src = yb_hbm.at[pl.ds(pl.multiple_of(dref[base + r] * pr, pr), pr)]
                pltpu.make_async_copy(src, buf.at[slot, k, dst], sem.at[slot]).start()
            return carry
        lax.fori_loop(0, tm, body, 0, unroll=8)

    nbuf = buf.shape[0]
    look = nbuf - 1

    @pl.when(i == 0)
    def _():
        issue(0, 0)
        for j in range(1, look):
            @pl.when(j < n)
            def _():
                issue(j, j)

    slot = i % nbuf
    for k in range(2):
        pltpu.make_async_copy(yb_hbm.at[pl.ds(0, tm * pr)], buf.at[slot, k], sem.at[slot]).wait()

    nq = 4
    sub = tm // nq

    def issue_part(q):
        base = (i + look) * tm
        nslot = (i + look) % nbuf
        for r in range(q * sub, (q + 1) * sub):
            for k, dref in enumerate((d0_ref, d1_ref)):
                src = yb_hbm.at[pl.ds(pl.multiple_of(dref[base + r] * pr, pr), pr)]
                pltpu.make_async_copy(src, buf.at[nslot, k, pl.ds(r * pr, pr)], sem.at[nslot]).start()

    def finish(with_issue):
        for q in range(nq):
            if with_issue:
                issue_part(q)
            rows = pl.ds(q * sub, sub)
            lo0, hi0 = _load_row_packed(buf.at[slot, 0], sub, pr, row0=q * sub)
            lo1, hi1 = _load_row_packed(buf.at[slot, 1], sub, pr, row0=q * sub)
            g0 = gate_ref[rows, 0:1]
            g1 = gate_ref[rows, 1:2]
            ffn = jnp.concatenate([g0 * a + g1 * c for a, c in zip(lo0 + hi0, lo1 + hi1)], axis=1)
            out_ref[rows, :] = _layer_norm(alpha * h_ref[rows, :] + ffn, g_ref[...], b_ref[...])

    @pl.when(i + look < n)
    def _():
        finish(True)

    @pl.when(i + look >= n)
    def _():
        finish(False)


def _combine(dest0, dest1, h1, gate_t, yb, ln_g, ln_b, alpha):
    t, d = h1.shape
    tm = min(256, t)
    grid_spec = pltpu.PrefetchScalarGridSpec(
        num_scalar_prefetch=2,
        grid=(t // tm,),
        in_specs=[pl.BlockSpec((tm, d), lambda i, d0, d1: (i, 0)),
                  pl.BlockSpec((tm, 2), lambda i, d0, d1: (i, 0)),
                  pl.BlockSpec(memory_space=pl.ANY),
                  pl.BlockSpec((1, d), lambda i, d0, d1: (0, 0)),
                  pl.BlockSpec((1, d), lambda i, d0, d1: (0, 0))],
        out_specs=pl.BlockSpec((tm, d), lambda i, d0, d1: (i, 0)),
        scratch_shapes=[pltpu.VMEM((GATHER_BUFS, 2, tm * (d // 2 // LANES), LANES), U32),
                        pltpu.SemaphoreType.DMA((GATHER_BUFS,))],
    )
    return pl.pallas_call(
        functools.partial(_combine_kernel, alpha=alpha),
        out_shape=jax.ShapeDtypeStruct((t, d), F32),
        grid_spec=grid_spec,
        compiler_params=_cparams(("arbitrary",)),
        name="combine",
    )(dest0, dest1, h1, gate_t, yb, ln_g, ln_b)


def _layer(h, w_in, pool_w, pool_scale, a_re, a_im, log_dt, b_re, b_im, c_re, c_im, d_skip,
           glu_w, glu_b, w_out, ln1_g, ln1_b, rg_w, rg_b, re_w, re_b, w_gate, w_up, w_down,
           ln2_g, ln2_b, alpha):
    bsz, seq, d = h.shape
    t = bsz * seq
    L = CHUNK
    dp = pool_w.shape[0] * pool_w.shape[1]
    ds = w_in.shape[1] - dp
    g = ds // SSM_GROUP
    n = t // L

    lag0, bb, w1t, w2t, al = _ssm_tables(a_re, a_im, log_dt, b_re, b_im, c_re, c_im, d_skip, L)
    half = ROUTER_LANES // 2
    zpad = lambda k: jnp.zeros((d, k), F32)
    rw = jnp.concatenate([rg_w.astype(F32), zpad(8 - N_EXPERT_GROUPS), re_w.astype(F32),
                          zpad(half - ROUTER_ROWS)], axis=1)
    rw_hi = rw.astype(BF16)
    rw_lo = (rw - rw_hi.astype(F32)).astype(BF16)
    rw2 = jnp.concatenate([rw_hi, rw_lo], axis=1)
    rb = jnp.concatenate([rg_b.astype(F32), jnp.zeros((8 - N_EXPERT_GROUPS,), F32), re_b.astype(F32),
                          jnp.zeros((ROUTER_LANES - ROUTER_ROWS,), F32)]).reshape(1, ROUTER_LANES)

    xt = h.reshape(t, d)
    pool_p, ut = _proj(xt, w_in.astype(BF16), dp, L)
    y_pool = _pool(pool_p, pool_w.astype(BF16), pool_scale.reshape(1, dp).astype(F32), seq)
    yt = _ssm(ut.reshape(g, L * SSM_GROUP, n), lag0, bb, w1t, w2t, al, seq // L, L)
    y_ssm = _glu(yt.reshape(g, L, SSM_GROUP, n), glu_w.astype(BF16), glu_b.reshape(1, ds).astype(F32))
    h1, hp, lt = _outproj(y_pool, y_ssm, xt, w_out.astype(BF16), ln1_g.reshape(1, d), ln1_b.reshape(1, d),
                          rw2, rb, alpha)
    eid, rank, gate, cnt = _router(lt)

    blk = MOE_BLK
    m = 2 * t
    n_blocks = -(-m // blk) + N_EXPERTS
    counts = cnt[:, 0]
    pcounts = (counts + blk - 1) // blk * blk
    pends = jnp.cumsum(pcounts)
    pstarts = pends - pcounts
    e_ids = jnp.arange(N_EXPERTS, dtype=I32)
    dest = jnp.sum(jnp.where(eid[..., None] == e_ids, pstarts, 0), axis=-1) + rank
    slot_tok = _invert(dest.reshape(-1), n_blocks * blk)
    n_used = (pends[-1] // blk).astype(I32)
    bidx = jnp.minimum(jnp.arange(n_blocks, dtype=I32), n_used - 1)
    block_e = jnp.minimum(jnp.sum((pends[None, :] <= (bidx * blk)[:, None]).astype(I32), axis=1),
                          N_EXPERTS - 1)

    yb = _moe(block_e, n_used.reshape(1), pcounts, slot_tok, hp, w_gate, w_up, w_down, n_blocks)
    out = _combine(dest[0], dest[1], h1, gate.T, yb, ln2_g.reshape(1, d), ln2_b.reshape(1, d), alpha)
    return out.reshape(bsz, seq, d)


def kernel(x, w_in, pool_w, pool_scale, ssm_a_re, ssm_a_im, ssm_log_dt, ssm_b_re, ssm_b_im, ssm_c_re, ssm_c_im, ssm_d, glu_w, glu_b, w_out, ln1_g, ln1_b, router_g_w, router_g_b, router_e_w, router_e_b, w_gate, w_up, w_down, ln2_g, ln2_b):
    depth = w_in.shape[0]
    alpha = (2.0 * depth) ** 0.25
    h = x
    for l in range(depth):
        h = _layer(h, w_in[l], pool_w[l], pool_scale[l], ssm_a_re[l], ssm_a_im[l], ssm_log_dt[l],
                   ssm_b_re[l], ssm_b_im[l], ssm_c_re[l], ssm_c_im[l], ssm_d[l], glu_w[l], glu_b[l],
                   w_out[l], ln1_g[l], ln1_b[l], router_g_w[l], router_g_b[l], router_e_w[l],
                   router_e_b[l], w_gate[l], w_up[l], w_down[l], ln2_g[l], ln2_b[l], alpha)
    return h
```

```python
import functools
import math

import numpy as np
import jax
import jax.numpy as jnp
from jax import lax
from jax.experimental import pallas as pl
from jax.experimental.pallas import tpu as pltpu

F32 = jnp.float32
BF16 = jnp.bfloat16
I32 = jnp.int32
U32 = jnp.uint32

POOL_WINDOWS = (2, 4, 8, 16)
POOL_GROUP = 256
SSM_GROUP = 16
SSM_STATE = 64
N_EXPERT_GROUPS = 4
EXPERTS_PER_GROUP = 8
N_EXPERTS = N_EXPERT_GROUPS * EXPERTS_PER_GROUP
LN_EPS = 1e-5

CHUNK = 16
MOE_BLK = 256
GATHER_BUFS = 4
ROUTER_ROWS = 8 + N_EXPERTS
ROUTER_LANES = 128
LANES = 128
MXU_N = 256
HALO = 16
VMEM_LIMIT = 56 * 1024 * 1024


def _cparams(sem, vmem=VMEM_LIMIT):
    return pltpu.CompilerParams(dimension_semantics=sem, vmem_limit_bytes=vmem)


def _proj_kernel(x_ref, w_ref, pool_ref, ut_ref, accp_ref, accs_ref):
    kk = pl.program_id(1)
    rows, dp = accp_ref.shape
    nj = accs_ref.shape[0]

    xb = x_ref[...].astype(BF16)
    per = MXU_N // LANES

    def accumulate(first):
        for n in range(dp // MXU_N):
            cs = slice(n * MXU_N, (n + 1) * MXU_N)
            part = jnp.dot(xb, w_ref[:, cs], preferred_element_type=F32)
            accp_ref[:, cs] = part if first else accp_ref[:, cs] + part
        for n in range(nj // per):
            part = jnp.dot(xb, w_ref[:, dp + n * MXU_N: dp + (n + 1) * MXU_N], preferred_element_type=F32)
            for q in range(per):
                piece = part[:, q * LANES:(q + 1) * LANES]
                accs_ref[n * per + q] = piece if first else accs_ref[n * per + q] + piece

    @pl.when(kk == 0)
    def _():
        accumulate(True)

    @pl.when(kk > 0)
    def _():
        accumulate(False)

    @pl.when(kk == pl.num_programs(1) - 1)
    def _():
        pool_ref[...] = accp_ref[...].astype(BF16)
        g, L, c, r = ut_ref.shape
        gj = LANES // c
        for s in range(L):
            for j in range(nj):
                piece = accs_ref[j, pl.ds(s, r, stride=L), :]
                ut_ref[j * gj:(j + 1) * gj, s, :, :] = piece.astype(BF16).T.reshape(gj, c, r)


def _proj(xt, w_in, dp, L):
    t, d = xt.shape
    dm = w_in.shape[1]
    g = (dm - dp) // SSM_GROUP
    tm = LANES * L
    kb = 512
    return pl.pallas_call(
        _proj_kernel,
        out_shape=(jax.ShapeDtypeStruct((t, dp), BF16),
                   jax.ShapeDtypeStruct((g, L, SSM_GROUP, t // L), BF16)),
        grid=(t // tm, d // kb),
        in_specs=[pl.BlockSpec((tm, kb), lambda i, k: (i, k)),
                  pl.BlockSpec((kb, dm), lambda i, k: (k, 0))],
        out_specs=(pl.BlockSpec((tm, dp), lambda i, k: (i, 0)),
                   pl.BlockSpec((g, L, SSM_GROUP, LANES), lambda i, k: (0, 0, 0, i))),
        scratch_shapes=[pltpu.VMEM((tm, dp), F32),
                        pltpu.VMEM(((dm - dp) // LANES, tm, LANES), F32)],
        compiler_params=_cparams(("arbitrary", "arbitrary")),
        name="proj",
    )(xt, w_in)


def _pool_kernel(prev_ref, main_ref, next_ref, pw_ref, sc_ref, out_ref, *, seq, sub):
    i = pl.program_id(0)
    ts = main_ref.shape[0]
    ext = jnp.concatenate([prev_ref[...], main_ref[...], next_ref[...]], axis=0)
    k = sub + 2 * HALO
    row = lax.broadcasted_iota(I32, (sub, k), 0)
    col = lax.broadcasted_iota(I32, (sub, k), 1)
    off0 = col - HALO - row
    for sb in range(ts // sub):
        rows = slice(sb * sub, (sb + 1) * sub)
        base = (i * ts + sb * sub) % seq
        src = base + row + off0
        off = jnp.where((src >= 0) & (src < seq), off0, 2 * HALO)
        pos = base + lax.broadcasted_iota(I32, (sub, 1), 0)
        for g, w in enumerate(POOL_WINDOWS):
            sl = slice(g * POOL_GROUP, (g + 1) * POOL_GROUP)
            band = (off + w // 2).astype(U32) < w
            bm = jnp.where(band, 1.0, 0.0).astype(BF16)
            sums = jnp.dot(bm, ext[sb * sub:sb * sub + k, sl], preferred_element_type=F32)
            lo = jnp.maximum(pos - w // 2, 0)
            hi = jnp.minimum(pos - w // 2 + w, seq)
            inv = 1.0 / (hi - lo).astype(F32)
            dlt = sums * inv - main_ref[rows, sl].astype(F32)
            y = jnp.dot(dlt.astype(BF16), pw_ref[g], preferred_element_type=F32) * sc_ref[:, sl]
            out_ref[rows, sl] = y.astype(BF16)


def _pool(pp, pool_w, pool_scale, seq):
    t, dp = pp.shape
    ts = min(512, seq)
    nh = t // HALO
    per = ts // HALO
    return pl.pallas_call(
        functools.partial(_pool_kernel, seq=seq, sub=256),
        out_shape=jax.ShapeDtypeStruct((t, dp), BF16),
        grid=(t // ts,),
        in_specs=[pl.BlockSpec((HALO, dp), lambda i: (jnp.maximum(i * per - 1, 0), 0)),
                  pl.BlockSpec((ts, dp), lambda i: (i, 0)),
                  pl.BlockSpec((HALO, dp), lambda i: (jnp.minimum((i + 1) * per, nh - 1), 0)),
                  pl.BlockSpec(pool_w.shape, lambda i: (0, 0, 0)),
                  pl.BlockSpec((1, dp), lambda i: (0, 0))],
        out_specs=pl.BlockSpec((ts, dp), lambda i: (i, 0)),
        compiler_params=_cparams(("arbitrary",)),
        name="pool",
    )(pp, pp, pp, pool_w, pool_scale)


def _ssm_tables(a_re, a_im, log_dt, b_re, b_im, c_re, c_im, d_skip, L):
    g = a_re.shape[1]
    p = a_re.shape[2]
    c = b_re.shape[3]
    fl = L * c
    lr = a_re.astype(F32)
    li = a_im.astype(F32)
    dt = jnp.exp(log_dt.astype(F32))[..., None]
    mag = jnp.exp(lr * dt)
    abr = mag * jnp.cos(li * dt)
    abi = mag * jnp.sin(li * dt)
    den = lr * lr + li * li
    zr = ((abr - 1.0) * lr + abi * li) / den
    zi = (abi * lr - (abr - 1.0) * li) / den
    br = b_re.astype(F32)
    bi = b_im.astype(F32)
    bbr = zr[..., None] * br - zi[..., None] * bi
    bbi = zr[..., None] * bi + zi[..., None] * br
    cr = c_re.astype(F32)
    ci = c_im.astype(F32)
    kk = jnp.arange(L + 1, dtype=F32)[None, None, :, None]
    pm = jnp.exp(kk * (lr * dt)[:, :, None, :])
    ang = kk * (li * dt)[:, :, None, :]
    pr = pm * jnp.cos(ang)
    pi = pm * jnp.sin(ang)

    skip = jnp.eye(c, dtype=F32)[None] * d_skip.astype(F32).reshape(g, c, 1)
    lag0 = jnp.einsum('dgop,dgpc->goc', cr, bbr, precision=lax.Precision.HIGHEST) \
        - jnp.einsum('dgop,dgpc->goc', ci, bbi, precision=lax.Precision.HIGHEST) + skip
    lag0 = jnp.pad(lag0, ((0, 0), (0, 0), (0, LANES - c)))
    bb = jnp.concatenate([bbr[0], bbi[0], bbr[1], bbi[1]], axis=1)
    bb = jnp.pad(bb, ((0, 0), (0, 0), (0, LANES - c)))

    def w1(d, qr, qi):
        qr = qr.transpose(0, 2, 1)[..., None]
        qi = qi.transpose(0, 2, 1)[..., None]
        re = qr * bbr[d][:, :, None] - qi * bbi[d][:, :, None]
        im = qr * bbi[d][:, :, None] + qi * bbr[d][:, :, None]
        to = lambda v: v.reshape(g, p, fl)
        return to(re), to(im)
    f_re, f_im = w1(0, pr[0][:, L - 1::-1], pi[0][:, L - 1::-1])
    b_re_, b_im_ = w1(1, pr[1][:, :L], pi[1][:, :L])
    w1t = jnp.concatenate([f_re, f_im, b_re_, b_im_], axis=1)

    def w2(d, qr, qi):
        re = cr[d][:, None] * qr[:, :, None] - ci[d][:, None] * qi[:, :, None]
        im = -(cr[d][:, None] * qi[:, :, None] + ci[d][:, None] * qr[:, :, None])
        to = lambda v: v.reshape(g, fl, p)
        return to(re), to(im)
    rf_re, rf_im = w2(0, pr[0][:, 1:], pi[0][:, 1:])
    rb_re, rb_im = w2(1, pr[1][:, :0:-1], pi[1][:, :0:-1])
    w2t = jnp.concatenate([rf_re, rf_im, rb_re, rb_im], axis=2)

    a4 = jnp.stack([pr[0][:, L], pi[0][:, L], pr[1][:, L], pi[1][:, L]], axis=1)
    ap = a4.reshape(g // 2, 2, 4, p).transpose(0, 2, 1, 3).reshape(g // 2, 4, 2 * p)
    ap = jnp.pad(ap, ((0, 0), (0, 4), (0, 0)))
    return lag0, bb.astype(BF16), w1t.astype(BF16), w2t.astype(BF16), ap


def _cmul(ar, ai, br, bi):
    return ar * br - ai * bi, ar * bi + ai * br


def _ssm_kernel(ut_ref, w1_ref, lag0_ref, bb_ref, w2_ref, a_ref, yt_ref, mt_s, kk_s, zt_s, ht_s, *, nseg, L):
    p = SSM_STATE
    c = SSM_GROUP
    fl, n = ut_ref.shape[1:]
    nb = n // nseg
    nrow = nseg // 8
    st = [jnp.dot(w1_ref[q], ut_ref[q], preferred_element_type=F32) for q in range(2)]
    z = jnp.concatenate([st[q][k * p:(k + 1) * p] for k in range(4) for q in range(2)], axis=0)
    zt_s[...] = z.T
    per = LANES // c
    for q in range(2):
        kf = jnp.dot(w2_ref[q, :, 0:2 * p], bb_ref[q, 0:2 * p, :], preferred_element_type=F32)
        kb = jnp.dot(w2_ref[q, :, 2 * p:4 * p], bb_ref[q, 2 * p:4 * p, :], preferred_element_type=F32)
        tiles = [None] * (2 * L // per)
        for j in range(2 * L - 1):
            lag = L - 1 - j
            if lag > 0:
                blk = kf[(lag - 1) * c:lag * c, :]
            elif lag == 0:
                blk = lag0_ref[q]
            else:
                blk = kb[(L + lag) * c:(L + lag + 1) * c, :]
            sh = (j % per) * c
            blk = blk if sh == 0 else pltpu.roll(blk, sh, 1)
            tiles[j // per] = blk if tiles[j // per] is None else tiles[j // per] + blk
        kk_s[...] = jnp.concatenate(tiles, axis=1).astype(BF16)
        for t in range(L):
            mt_s[q, t * c:(t + 1) * c, :] = kk_s[:, (L - 1 - t) * c:(L - 1 - t) * c + fl]

    sub = lax.broadcasted_iota(I32, (8, LANES), 0)
    full = lambda v: jnp.broadcast_to(v, (8, LANES))
    one, zero = jnp.ones((8, LANES), F32), jnp.zeros((8, LANES), F32)

    def powers(row):
        out = [(full(a_ref[0, row:row + 1, :]), full(a_ref[0, row + 1:row + 2, :]))]
        for _ in range(3):
            out.append(_cmul(*out[-1], *out[-1]))
        return out

    def by_bits(pw, idx):
        qr, qi = one, zero
        for bit in range(3):
            on = (idx >> bit) & 1 == 1
            qr, qi = _cmul(qr, qi, jnp.where(on, pw[bit][0], one), jnp.where(on, pw[bit][1], zero))
        return qr, qi

    pw_f, pw_b = powers(0), powers(2)
    qf = by_bits(pw_f, sub)
    qb = by_bits(pw_b, 7 - sub)

    def local(xr, xi, pw, forward):
        for lvl, d in enumerate((1, 2, 4)):
            keep = (sub >= d) if forward else (sub < 8 - d)
            sh = d if forward else 8 - d
            tr = jnp.where(keep, pltpu.roll(xr, sh, 0), 0.0)
            ti = jnp.where(keep, pltpu.roll(xi, sh, 0), 0.0)
            mr, mi = _cmul(pw[lvl][0], pw[lvl][1], tr, ti)
            xr, xi = xr + mr, xi + mi
        return xr, xi

    def step(i, carry):
        new = []
        for b in range(nb):
            for fwd in (True, False):
                cr, ci = carry[2 * (2 * b + (0 if fwd else 1)):][:2]
                row = b * nseg + (i if fwd else nrow - 1 - i) * 8
                rows = pl.ds(pl.multiple_of(row, 8), 8)
                l0 = 0 if fwd else 2 * LANES
                pw, (qr, qi) = (pw_f, qf) if fwd else (pw_b, qb)
                lr, li = local(zt_s[rows, l0:l0 + LANES], zt_s[rows, l0 + LANES:l0 + 2 * LANES], pw, fwd)
                keep = (sub >= 1) if fwd else (sub < 7)
                er = jnp.where(keep, pltpu.roll(lr, 1 if fwd else 7, 0), 0.0)
                ei = jnp.where(keep, pltpu.roll(li, 1 if fwd else 7, 0), 0.0)
                mr, mi = _cmul(qr, qi, cr, ci)
                ht_s[rows, l0:l0 + LANES] = er + mr
                ht_s[rows, l0 + LANES:l0 + 2 * LANES] = ei + mi
                edge = 7 if fwd else 0
                mr, mi = _cmul(pw[3][0], pw[3][1], cr, ci)
                new += [full(lr[edge:edge + 1, :]) + mr, full(li[edge:edge + 1, :]) + mi]
        return tuple(new)

    lax.fori_loop(0, nrow, step, (zero,) * (4 * nb))

    h = ht_s[...].astype(BF16).T
    for q in range(2):
        hq = jnp.concatenate([h[(2 * k + q) * p:(2 * k + q + 1) * p] for k in range(4)], axis=0)
        y = jnp.dot(mt_s[q], ut_ref[q], preferred_element_type=F32) \
            + jnp.dot(w2_ref[q], hq, preferred_element_type=F32)
        yt_ref[q] = y.astype(BF16)


def _ssm(ut3, lag0, bb, w1t, w2t, ap, nseg, L):
    g, fl, n = ut3.shape
    sp = w1t.shape[1]
    return pl.pallas_call(
        functools.partial(_ssm_kernel, nseg=nseg, L=L),
        out_shape=jax.ShapeDtypeStruct((g, fl, n), BF16),
        grid=(g // 2,),
        in_specs=[pl.BlockSpec((2, fl, n), lambda i: (i, 0, 0)),
                  pl.BlockSpec((2, sp, fl), lambda i: (i, 0, 0)),
                  pl.BlockSpec((2,) + lag0.shape[1:], lambda i: (i, 0, 0)),
                  pl.BlockSpec((2,) + bb.shape[1:], lambda i: (i, 0, 0)),
                  pl.BlockSpec((2, fl, sp), lambda i: (i, 0, 0)),
                  pl.BlockSpec((1,) + ap.shape[1:], lambda i: (i, 0, 0))],
        out_specs=pl.BlockSpec((2, fl, n), lambda i: (i, 0, 0)),
        scratch_shapes=[pltpu.VMEM((2, fl, fl), BF16),
                        pltpu.VMEM((SSM_GROUP, 2 * fl), BF16),
                        pltpu.VMEM((n, 2 * sp), F32),
                        pltpu.VMEM((n, 2 * sp), F32)],
        compiler_params=_cparams(("arbitrary",)),
        name="ssm",
    )(ut3, w1t, lag0, bb, w2t, ap)


def _glu_kernel(yt_ref, w_ref, b_ref, out_ref, nat_ref, *, sub):
    g, L, c, r = yt_ref.shape
    nj = nat_ref.shape[0]
    gj = LANES // c
    for t in range(L):
        for j in range(nj):
            piece = yt_ref[j * gj:(j + 1) * gj, t, :, :].reshape(LANES, r).astype(F32)
            nat_ref[j, pl.ds(t, r, stride=L), :] = piece.T

    def body(k, carry):
        sl = pl.ds(pl.multiple_of(k * sub, sub), sub)
        y = jnp.concatenate([nat_ref[j, sl, :] for j in range(nj)], axis=1)
        ya = jax.nn.gelu(y, approximate=True)
        z = jnp.dot(ya.astype(BF16), w_ref[...], preferred_element_type=F32) + b_ref[...]
        out_ref[sl, :] = (ya * (1.0 / (1.0 + jnp.exp(-z)))).astype(BF16)
        return carry
    lax.fori_loop(0, nat_ref.shape[1] // sub, body, 0)


def _glu(yt4, glu_w, glu_b):
    g, L, c, n = yt4.shape
    ds = g * c
    r = LANES
    return pl.pallas_call(
        functools.partial(_glu_kernel, sub=512),
        out_shape=jax.ShapeDtypeStruct((n * L, ds), BF16),
        grid=(n // r,),
        in_specs=[pl.BlockSpec((g, L, c, r), lambda j: (0, 0, 0, j)),
                  pl.BlockSpec((ds, ds), lambda j: (0, 0)),
                  pl.BlockSpec((1, ds), lambda j: (0, 0))],
        out_specs=pl.BlockSpec((r * L, ds), lambda j: (j, 0)),
        scratch_shapes=[pltpu.VMEM((ds // LANES, r * L, LANES), F32)],
        compiler_params=_cparams(("arbitrary",)),
        name="glu",
    )(yt4, glu_w, glu_b)


def _layer_norm(r, g, b):
    mu = jnp.mean(r, axis=-1, keepdims=True)
    cen = r - mu
    var = jnp.mean(cen * cen, axis=-1, keepdims=True)
    return cen * lax.rsqrt(var + LN_EPS) * g + b


def _store_row_packed(ref, m, row0=0):
    rows, width = m.shape
    half = width // 2
    nc = half // LANES
    lo = lax.bitcast_convert_type(m[:, :half].astype(BF16).astype(F32), U32) >> 16
    hi = lax.bitcast_convert_type(m[:, half:].astype(BF16).astype(F32), U32) & jnp.uint32(0xFFFF0000)
    pk = lo | hi
    for c in range(nc):
        ref[pl.ds(row0 * nc + c, rows, stride=nc), :] = pk[:, c * LANES:(c + 1) * LANES]


def _load_row_packed(ref, rows, nc, row0=0):
    los, his = [], []
    for c in range(nc):
        u = ref[pl.ds(row0 * nc + c, rows, stride=nc), :]
        los.append(lax.bitcast_convert_type(u << 16, F32))
        his.append(lax.bitcast_convert_type(u & jnp.uint32(0xFFFF0000), F32))
    return los, his


def _outproj_kernel(yp_ref, ys_ref, x_ref, wo_ref, g_ref, b_ref, rw_ref, rb_ref,
                    h_ref, hp_ref, lt_ref, *, alpha, dp, sub):
    for sb in range(h_ref.shape[0] // sub):
        sl = pl.ds(sb * sub, sub)
        mix = jnp.dot(yp_ref[sl, :], wo_ref[0:dp, :], preferred_element_type=F32) \
            + jnp.dot(ys_ref[sl, :], wo_ref[dp:, :], preferred_element_type=F32)
        h = _layer_norm(alpha * x_ref[sl, :] + mix, g_ref[...], b_ref[...])
        h_ref[sl, :] = h
        _store_row_packed(hp_ref, h, row0=sb * sub)
        hh = h.astype(BF16)
        hl = (h - hh.astype(F32)).astype(BF16)
        part = jnp.dot(hh, rw_ref[...], preferred_element_type=F32) \
            + jnp.dot(hl, rw_ref[...], preferred_element_type=F32)
        logits = part + pltpu.roll(part, ROUTER_LANES // 2, 1) + rb_ref[...]
        lt_ref[:, sb * sub:(sb + 1) * sub] = logits.T[0:ROUTER_ROWS, :]


def _outproj(y_pool, y_ssm, xt, w_out, ln_g, ln_b, rw, rb, alpha):
    t, d = xt.shape
    dp = y_pool.shape[1]
    tm = min(512, t)
    pr = d // 2 // LANES
    return pl.pallas_call(
        functools.partial(_outproj_kernel, alpha=alpha, dp=dp, sub=256),
        out_shape=(jax.ShapeDtypeStruct((t, d), F32),
                   jax.ShapeDtypeStruct((t * pr, LANES), U32),
                   jax.ShapeDtypeStruct((ROUTER_ROWS, t), F32)),
        grid=(t // tm,),
        in_specs=[pl.BlockSpec((tm, dp), lambda i: (i, 0)),
                  pl.BlockSpec((tm, y_ssm.shape[1]), lambda i: (i, 0)),
                  pl.BlockSpec((tm, d), lambda i: (i, 0)),
                  pl.BlockSpec(w_out.shape, lambda i: (0, 0)),
                  pl.BlockSpec((1, d), lambda i: (0, 0)),
                  pl.BlockSpec((1, d), lambda i: (0, 0)),
                  pl.BlockSpec(rw.shape, lambda i: (0, 0)),
                  pl.BlockSpec((1, ROUTER_LANES), lambda i: (0, 0))],
        out_specs=(pl.BlockSpec((tm, d), lambda i: (i, 0)),
                   pl.BlockSpec((tm * pr, LANES), lambda i: (i, 0)),
                   pl.BlockSpec((ROUTER_ROWS, tm), lambda i: (0, i))),
        compiler_params=_cparams(("arbitrary",)),
        name="outproj",
    )(y_pool, y_ssm, xt, w_out, ln_g, ln_b, rw, rb)


def _router_kernel(lt_ref, eid_ref, rank_ref, gate_ref, cnt_ref, carry_ref):
    i = pl.program_id(0)
    ne, epg = N_EXPERTS, EXPERTS_PER_GROUP

    @pl.when(i == 0)
    def _():
        carry_ref[...] = jnp.zeros_like(carry_ref)

    lt = lt_ref[...]
    tt = lt.shape[1]
    gl = [lt[j:j + 1, :] for j in range(N_EXPERT_GROUPS)]
    gmax = jnp.maximum(jnp.maximum(gl[0], gl[1]), jnp.maximum(gl[2], gl[3]))
    grp = jnp.where(gl[0] == gmax, 0, jnp.where(gl[1] == gmax, 1, jnp.where(gl[2] == gmax, 2, 3)))
    p_grp = 1.0 / (jnp.exp(gl[0] - gmax) + jnp.exp(gl[1] - gmax)
                   + jnp.exp(gl[2] - gmax) + jnp.exp(gl[3] - gmax))
    eg = [lt[8 + epg * j: 8 + epg * (j + 1), :] for j in range(N_EXPERT_GROUPS)]
    el = jnp.where(grp == 0, eg[0], jnp.where(grp == 1, eg[1], jnp.where(grp == 2, eg[2], eg[3])))
    sub = lax.broadcasted_iota(I32, (epg, tt), 0)
    m1 = jnp.max(el, axis=0, keepdims=True)
    i1 = jnp.min(jnp.where(el == m1, sub, epg), axis=0, keepdims=True)
    rest = jnp.where(sub == i1, -jnp.inf, el)
    m2 = jnp.max(rest, axis=0, keepdims=True)
    i2 = jnp.min(jnp.where(rest == m2, sub, epg), axis=0, keepdims=True)
    r21 = jnp.exp(m2 - m1)
    g1 = p_grp / (1.0 + r21)
    g2 = g1 * r21
    e1 = grp * epg + i1
    e2 = grp * epg + i2

    rows = lax.broadcasted_iota(I32, (ne, tt), 0)
    oh1 = rows == e1
    oh2 = rows == e2
    oh = jnp.where(oh1 | oh2, 1.0, 0.0)
    tri = jnp.where(lax.broadcasted_iota(I32, (tt, tt), 0) < lax.broadcasted_iota(I32, (tt, tt), 1),
                    1.0, 0.0).astype(BF16)
    before = jnp.dot(oh.astype(BF16), tri, preferred_element_type=F32) + carry_ref[:, 0:1]
    r1 = jnp.sum(jnp.where(oh1, before, 0.0), axis=0, keepdims=True)
    r2 = jnp.sum(jnp.where(oh2, before, 0.0), axis=0, keepdims=True)
    carry_ref[...] = carry_ref[...] + jnp.sum(oh, axis=1, keepdims=True)

    eid_ref[...] = jnp.concatenate([e1, e2], axis=0)
    rank_ref[...] = jnp.concatenate([r1, r2], axis=0).astype(I32)
    gate_ref[...] = jnp.concatenate([g1, g2], axis=0)
    cnt_ref[...] = carry_ref[...].astype(I32)


def _router(lt):
    rr, t = lt.shape
    tt = min(512, t)
    return pl.pallas_call(
        _router_kernel,
        out_shape=(jax.ShapeDtypeStruct((2, t), I32),
                   jax.ShapeDtypeStruct((2, t), I32),
                   jax.ShapeDtypeStruct((2, t), F32),
                   jax.ShapeDtypeStruct((N_EXPERTS, 128), I32)),
        grid=(t // tt,),
        in_specs=[pl.BlockSpec((rr, tt), lambda i: (0, i))],
        out_specs=(pl.BlockSpec((2, tt), lambda i: (0, i)),
                   pl.BlockSpec((2, tt), lambda i: (0, i)),
                   pl.BlockSpec((2, tt), lambda i: (0, i)),
                   pl.BlockSpec((N_EXPERTS, 128), lambda i: (0, 0))),
        scratch_shapes=[pltpu.VMEM((N_EXPERTS, 128), F32)],
        compiler_params=_cparams(("arbitrary",)),
        name="router",
    )(lt)


def _invert_kernel(dest_ref, zeros_hbm, out_ref):
    pltpu.sync_copy(zeros_hbm, out_ref)
    t = dest_ref.shape[0] // 2

    def place(i, carry):
        out_ref[dest_ref[i]] = i
        out_ref[dest_ref[t + i]] = i
        return carry
    lax.fori_loop(0, t, place, 0, unroll=8)


def _invert(dest_flat, n_slots):
    return pl.pallas_call(
        _invert_kernel,
        out_shape=jax.ShapeDtypeStruct((n_slots,), I32),
        grid_spec=pltpu.PrefetchScalarGridSpec(
            num_scalar_prefetch=1, grid=(1,),
            in_specs=[pl.BlockSpec(memory_space=pl.ANY)],
            out_specs=pl.BlockSpec(memory_space=pltpu.SMEM)),
        compiler_params=_cparams(("arbitrary",)),
        name="invert",
    )(dest_flat, jnp.zeros((n_slots,), I32))


def _moe_kernel(be_ref, nu_ref, pc_ref, st_ref, hp_hbm, wg_hbm, wu_hbm, wd_hbm, y_ref,
                gbuf, wgf, wuf, wdf, wgb, wub, wdb, par_ref, gsem, wsem):
    b = pl.program_id(0)
    nu = nu_ref[0]
    pr = gbuf.shape[1] // MOE_BLK
    blk = MOE_BLK
    nbuf = gbuf.shape[0]
    look = nbuf - 1

    def row_copy(tok, slot, r):
        src = hp_hbm.at[pl.ds(pl.multiple_of(tok * pr, pr), pr)]
        off = r * pr if isinstance(r, int) else pl.multiple_of(r * pr, pr)
        return pltpu.make_async_copy(src, gbuf.at[slot, pl.ds(off, pr)], gsem.at[slot])

    def issue(block, slot):
        base = block * blk

        def body(r, carry):
            row_copy(st_ref[base + r], slot, r).start()
            return carry
        lax.fori_loop(0, blk, body, 0, unroll=8)

    def weight_copies(e, p):
        return (pltpu.make_async_copy(wg_hbm.at[e], wgf.at[p], wsem.at[p]),
                pltpu.make_async_copy(wu_hbm.at[e], wuf.at[p], wsem.at[p]),
                pltpu.make_async_copy(wd_hbm.at[e], wdf.at[p], wsem.at[p]))

    @pl.when(b == 0)
    def _():
        par_ref[0] = 0
        for cp in weight_copies(be_ref[0], 0):
            cp.start(priority=1)
        issue(0, 0)
        for j in range(1, look):
            @pl.when(j < nu)
            def _():
                issue(j, j)

    def issue_part(k):
        base = (b + look) * blk
        for r in range(k * blk // 4, (k + 1) * blk // 4):
            row_copy(st_ref[base + r], (b + look) % nbuf, r).start(priority=1 if r % 3 == 2 else 0)

    def expert_mlp(slot, with_issue):
        los, his = _load_row_packed(gbuf.at[slot], blk, pr)
        x = jnp.concatenate([v.astype(BF16) for v in los + his], axis=1)
        if with_issue:
            issue_part(0)
        hg = jnp.dot(x, wgb[...], preferred_element_type=F32)
        if with_issue:
            issue_part(1)
        hu = jnp.dot(x, wub[...], preferred_element_type=F32)
        if with_issue:
            issue_part(2)
        hh = hg * (1.0 / (1.0 + jnp.exp(-hg))) * hu
        y = jnp.dot(hh.astype(BF16), wdb[...], preferred_element_type=F32)
        if with_issue:
            issue_part(3)
        _store_row_packed(y_ref, y)

    @pl.when(b < nu)
    def _():
        slot = b % nbuf
        e = be_ref[b]
        first = (b == 0) | (e != be_ref[jnp.maximum(b - 1, 0)])

        @pl.when(first)
        def _():
            p = par_ref[0]
            for cp in weight_copies(e, p):
                cp.wait()
            nxt = lax.while_loop(
                lambda c: (c < N_EXPERTS) & (pc_ref[jnp.minimum(c, N_EXPERTS - 1)] == 0),
                lambda c: c + 1, e + 1)

            @pl.when(nxt < N_EXPERTS)
            def _():
                for cp in weight_copies(nxt, 1 - p):
                    cp.start(priority=1)
            wgb[...] = wgf[p].astype(BF16)
            wub[...] = wuf[p].astype(BF16)
            wdb[...] = wdf[p].astype(BF16)
            par_ref[0] = 1 - p

        pltpu.make_async_copy(hp_hbm.at[pl.ds(0, blk * pr)], gbuf.at[slot], gsem.at[slot]).wait()

        @pl.when(b + look < nu)
        def _():
            expert_mlp(slot, True)

        @pl.when(b + look >= nu)
        def _():
            expert_mlp(slot, False)

    @pl.when(b >= nu)
    def _():
        y_ref[...] = jnp.zeros_like(y_ref)


def _moe(block_e, n_used, pcounts, slot_tok, hp, w_gate, w_up, w_down, n_blocks):
    ne, d, de = w_gate.shape
    blk = MOE_BLK
    pr = d // 2 // LANES
    grid_spec = pltpu.PrefetchScalarGridSpec(
        num_scalar_prefetch=4,
        grid=(n_blocks,),
        in_specs=[pl.BlockSpec(memory_space=pl.ANY)] * 4,
        out_specs=pl.BlockSpec((blk * pr, LANES), lambda b, *_: (b, 0)),
        scratch_shapes=[pltpu.VMEM((GATHER_BUFS, blk * pr, LANES), U32),
                        pltpu.VMEM((2, d, de), F32),
                        pltpu.VMEM((2, d, de), F32),
                        pltpu.VMEM((2, de, d), F32),
                        pltpu.VMEM((d, de), BF16),
                        pltpu.VMEM((d, de), BF16),
                        pltpu.VMEM((de, d), BF16),
                        pltpu.SMEM((1,), I32),
                        pltpu.SemaphoreType.DMA((GATHER_BUFS,)),
                        pltpu.SemaphoreType.DMA((2,))],
    )
    return pl.pallas_call(
        _moe_kernel,
        out_shape=jax.ShapeDtypeStruct((n_blocks * blk * pr, LANES), U32),
        grid_spec=grid_spec,
        compiler_params=_cparams(("arbitrary",)),
        name="moe",
    )(block_e, n_used, pcounts, slot_tok, hp, w_gate, w_up, w_down)


def _combine_kernel(d0_ref, d1_ref, h_ref, gate_ref, yb_hbm, g_ref, b_ref, out_ref, buf, sem, *, alpha):
    i = pl.program_id(0)
    n = pl.num_programs(0)
    tm = h_ref.shape[0]
    pr = buf.shape[2] // tm

    def issue(tile, slot):
        base = tile * tm

        def body(r, carry):
            dst = pl.ds(pl.multiple_of(r * pr, pr), pr)
            for k, dref in enumerate((d0_ref, d1_ref)):
                src = yb_hbm.at[pl.ds(pl.multiple_of(dref[base + r] * pr, pr), pr)]
                pltpu.make_async_copy(src, buf.at[slot, k, dst], sem.at[slot]).start()
            return carry
        lax.fori_loop(0, tm, body, 0, unroll=8)

    nbuf = buf.shape[0]
    look = nbuf - 1

    @pl.when(i == 0)
    def _():
        issue(0, 0)
        for j in range(1, look):
            @pl.when(j < n)
            def _():
                issue(j, j)

    slot = i % nbuf
    for k in range(2):
        pltpu.make_async_copy(yb_hbm.at[pl.ds(0, tm * pr)], buf.at[slot, k], sem.at[slot]).wait()

    nq = 4
    sub = tm // nq

    def issue_part(q):
        base = (i + look) * tm
        nslot = (i + look) % nbuf
        for r in range(q * sub, (q + 1) * sub):
            for k, dref in enumerate((d0_ref, d1_ref)):
                src = yb_hbm.at[pl.ds(pl.multiple_of(dref[base + r] * pr, pr), pr)]
                pltpu.make_async_copy(src, buf.at[nslot, k, pl.ds(r * pr, pr)], sem.at[nslot]).start(priority=k)

    def finish(with_issue):
        for q in range(nq):
            if with_issue:
                issue_part(q)
            rows = pl.ds(q * sub, sub)
            lo0, hi0 = _load_row_packed(buf.at[slot, 0], sub, pr, row0=q * sub)
            lo1, hi1 = _load_row_packed(buf.at[slot, 1], sub, pr, row0=q * sub)
            g0 = gate_ref[rows, 0:1]
            g1 = gate_ref[rows, 1:2]
            ffn = jnp.concatenate([g0 * a + g1 * c for a, c in zip(lo0 + hi0, lo1 + hi1)], axis=1)
            out_ref[rows, :] = _layer_norm(alpha * h_ref[rows, :] + ffn, g_ref[...], b_ref[...])

    @pl.when(i + look < n)
    def _():
        finish(True)

    @pl.when(i + look >= n)
    def _():
        finish(False)


def _combine(dest0, dest1, h1, gate_t, yb, ln_g, ln_b, alpha):
    t, d = h1.shape
    tm = min(256, t)
    grid_spec = pltpu.PrefetchScalarGridSpec(
        num_scalar_prefetch=2,
        grid=(t // tm,),
        in_specs=[pl.BlockSpec((tm, d), lambda i, d0, d1: (i, 0)),
                  pl.BlockSpec((tm, 2), lambda i, d0, d1: (i, 0)),
                  pl.BlockSpec(memory_space=pl.ANY),
                  pl.BlockSpec((1, d), lambda i, d0, d1: (0, 0)),
                  pl.BlockSpec((1, d), lambda i, d0, d1: (0, 0))],
        out_specs=pl.BlockSpec((tm, d), lambda i, d0, d1: (i, 0)),
        scratch_shapes=[pltpu.VMEM((GATHER_BUFS, 2, tm * (d // 2 // LANES), LANES), U32),
                        pltpu.SemaphoreType.DMA((GATHER_BUFS,))],
    )
    return pl.pallas_call(
        functools.partial(_combine_kernel, alpha=alpha),
        out_shape=jax.ShapeDtypeStruct((t, d), F32),
        grid_spec=grid_spec,
        compiler_params=_cparams(("arbitrary",)),
        name="combine",
    )(dest0, dest1, h1, gate_t, yb, ln_g, ln_b)


def _layer(h, w_in, pool_w, pool_scale, a_re, a_im, log_dt, b_re, b_im, c_re, c_im, d_skip,
           glu_w, glu_b, w_out, ln1_g, ln1_b, rg_w, rg_b, re_w, re_b, w_gate, w_up, w_down,
           ln2_g, ln2_b, alpha):
    bsz, seq, d = h.shape
    t = bsz * seq
    L = CHUNK
    dp = pool_w.shape[0] * pool_w.shape[1]
    ds = w_in.shape[1] - dp
    g = ds // SSM_GROUP
    n = t // L

    lag0, bb, w1t, w2t, al = _ssm_tables(a_re, a_im, log_dt, b_re, b_im, c_re, c_im, d_skip, L)
    half = ROUTER_LANES // 2
    zpad = lambda k: jnp.zeros((d, k), F32)
    rw = jnp.concatenate([rg_w.astype(F32), zpad(8 - N_EXPERT_GROUPS), re_w.astype(F32),
                          zpad(half - ROUTER_ROWS)], axis=1)
    rw_hi = rw.astype(BF16)
    rw_lo = (rw - rw_hi.astype(F32)).astype(BF16)
    rw2 = jnp.concatenate([rw_hi, rw_lo], axis=1)
    rb = jnp.concatenate([rg_b.astype(F32), jnp.zeros((8 - N_EXPERT_GROUPS,), F32), re_b.astype(F32),
                          jnp.zeros((ROUTER_LANES - ROUTER_ROWS,), F32)]).reshape(1, ROUTER_LANES)

    xt = h.reshape(t, d)
    pool_p, ut = _proj(xt, w_in.astype(BF16), dp, L)
    y_pool = _pool(pool_p, pool_w.astype(BF16), pool_scale.reshape(1, dp).astype(F32), seq)
    yt = _ssm(ut.reshape(g, L * SSM_GROUP, n), lag0, bb, w1t, w2t, al, seq // L, L)
    y_ssm = _glu(yt.reshape(g, L, SSM_GROUP, n), glu_w.astype(BF16), glu_b.reshape(1, ds).astype(F32))
    h1, hp, lt = _outproj(y_pool, y_ssm, xt, w_out.astype(BF16), ln1_g.reshape(1, d), ln1_b.reshape(1, d),
                          rw2, rb, alpha)
    eid, rank, gate, cnt = _router(lt)

    blk = MOE_BLK
    m = 2 * t
    n_blocks = -(-m // blk) + N_EXPERTS
    counts = cnt[:, 0]
    pcounts = (counts + blk - 1) // blk * blk
    pends = jnp.cumsum(pcounts)
    pstarts = pends - pcounts
    e_ids = jnp.arange(N_EXPERTS, dtype=I32)
    dest = jnp.sum(jnp.where(eid[..., None] == e_ids, pstarts, 0), axis=-1) + rank
    slot_tok = _invert(dest.reshape(-1), n_blocks * blk)
    n_used = (pends[-1] // blk).astype(I32)
    bidx = jnp.minimum(jnp.arange(n_blocks, dtype=I32), n_used - 1)
    block_e = jnp.minimum(jnp.sum((pends[None, :] <= (bidx * blk)[:, None]).astype(I32), axis=1),
                          N_EXPERTS - 1)

    yb = _moe(block_e, n_used.reshape(1), pcounts, slot_tok, hp, w_gate, w_up, w_down, n_blocks)
    out = _combine(dest[0], dest[1], h1, gate.T, yb, ln2_g.reshape(1, d), ln2_b.reshape(1, d), alpha)
    return out.reshape(bsz, seq, d)


def kernel(x, w_in, pool_w, pool_scale, ssm_a_re, ssm_a_im, ssm_log_dt, ssm_b_re, ssm_b_im, ssm_c_re, ssm_c_im, ssm_d, glu_w, glu_b, w_out, ln1_g, ln1_b, router_g_w, router_g_b, router_e_w, router_e_b, w_gate, w_up, w_down, ln2_g, ln2_b):
    depth = w_in.shape[0]
    alpha = (2.0 * depth) ** 0.25
    h = x
    for l in range(depth):
        h = _layer(h, w_in[l], pool_w[l], pool_scale[l], ssm_a_re[l], ssm_a_im[l], ssm_log_dt[l],
                   ssm_b_re[l], ssm_b_im[l], ssm_c_re[l], ssm_c_im[l], ssm_d[l], glu_w[l], glu_b[l],
                   w_out[l], ln1_g[l], ln1_b[l], router_g_w[l], router_g_b[l], router_e_w[l],
                   router_e_b[l], w_gate[l], w_up[l], w_down[l], ln2_g[l], ln2_b[l], alpha)
    return h
```

```python
import functools
import math

import numpy as np
import jax
import jax.numpy as jnp
from jax import lax
from jax.experimental import pallas as pl
from jax.experimental.pallas import tpu as pltpu

F32 = jnp.float32
BF16 = jnp.bfloat16
I32 = jnp.int32
U32 = jnp.uint32

POOL_WINDOWS = (2, 4, 8, 16)
POOL_GROUP = 256
SSM_GROUP = 16
SSM_STATE = 64
N_EXPERT_GROUPS = 4
EXPERTS_PER_GROUP = 8
N_EXPERTS = N_EXPERT_GROUPS * EXPERTS_PER_GROUP
LN_EPS = 1e-5

CHUNK = 16
MOE_BLK = 256
GATHER_BUFS = 4
ROUTER_ROWS = 8 + N_EXPERTS
ROUTER_LANES = 128
LANES = 128
MXU_N = 256
HALO = 16
VMEM_LIMIT = 56 * 1024 * 1024


def _cparams(sem, vmem=VMEM_LIMIT):
    return pltpu.CompilerParams(dimension_semantics=sem, vmem_limit_bytes=vmem)


def _proj_kernel(x_ref, w_ref, pool_ref, ut_ref, accp_ref, accs_ref):
    kk = pl.program_id(1)
    rows, dp = accp_ref.shape
    nj = accs_ref.shape[0]

    xb = x_ref[...].astype(BF16)
    per = MXU_N // LANES

    def accumulate(first):
        for n in range(dp // MXU_N):
            cs = slice(n * MXU_N, (n + 1) * MXU_N)
            part = jnp.dot(xb, w_ref[:, cs], preferred_element_type=F32)
            accp_ref[:, cs] = part if first else accp_ref[:, cs] + part
        for n in range(nj // per):
            part = jnp.dot(xb, w_ref[:, dp + n * MXU_N: dp + (n + 1) * MXU_N], preferred_element_type=F32)
            for q in range(per):
                piece = part[:, q * LANES:(q + 1) * LANES]
                accs_ref[n * per + q] = piece if first else accs_ref[n * per + q] + piece

    @pl.when(kk == 0)
    def _():
        accumulate(True)

    @pl.when(kk > 0)
    def _():
        accumulate(False)

    @pl.when(kk == pl.num_programs(1) - 1)
    def _():
        pool_ref[...] = accp_ref[...].astype(BF16)
        g, L, c, r = ut_ref.shape
        gj = LANES // c
        for s in range(L):
            for j in range(nj):
                piece = accs_ref[j, pl.ds(s, r, stride=L), :]
                ut_ref[j * gj:(j + 1) * gj, s, :, :] = piece.astype(BF16).T.reshape(gj, c, r)


def _proj(xt, w_in, dp, L):
    t, d = xt.shape
    dm = w_in.shape[1]
    g = (dm - dp) // SSM_GROUP
    tm = LANES * L
    kb = 512
    return pl.pallas_call(
        _proj_kernel,
        out_shape=(jax.ShapeDtypeStruct((t, dp), BF16),
                   jax.ShapeDtypeStruct((g, L, SSM_GROUP, t // L), BF16)),
        grid=(t // tm, d // kb),
        in_specs=[pl.BlockSpec((tm, kb), lambda i, k: (i, k)),
                  pl.BlockSpec((kb, dm), lambda i, k: (k, 0))],
        out_specs=(pl.BlockSpec((tm, dp), lambda i, k: (i, 0)),
                   pl.BlockSpec((g, L, SSM_GROUP, LANES), lambda i, k: (0, 0, 0, i))),
        scratch_shapes=[pltpu.VMEM((tm, dp), F32),
                        pltpu.VMEM(((dm - dp) // LANES, tm, LANES), F32)],
        compiler_params=_cparams(("arbitrary", "arbitrary")),
        name="proj",
    )(xt, w_in)


def _pool_kernel(prev_ref, main_ref, next_ref, pw_ref, sc_ref, out_ref, *, seq, sub):
    i = pl.program_id(0)
    ts = main_ref.shape[0]
    ext = jnp.concatenate([prev_ref[...], main_ref[...], next_ref[...]], axis=0)
    k = sub + 2 * HALO
    row = lax.broadcasted_iota(I32, (sub, k), 0)
    col = lax.broadcasted_iota(I32, (sub, k), 1)
    off0 = col - HALO - row
    for sb in range(ts // sub):
        rows = slice(sb * sub, (sb + 1) * sub)
        base = (i * ts + sb * sub) % seq
        src = base + row + off0
        off = jnp.where((src >= 0) & (src < seq), off0, 2 * HALO)
        pos = base + lax.broadcasted_iota(I32, (sub, 1), 0)
        for g, w in enumerate(POOL_WINDOWS):
            sl = slice(g * POOL_GROUP, (g + 1) * POOL_GROUP)
            band = (off + w // 2).astype(U32) < w
            bm = jnp.where(band, 1.0, 0.0).astype(BF16)
            sums = jnp.dot(bm, ext[sb * sub:sb * sub + k, sl], preferred_element_type=F32)
            lo = jnp.maximum(pos - w // 2, 0)
            hi = jnp.minimum(pos - w // 2 + w, seq)
            inv = 1.0 / (hi - lo).astype(F32)
            dlt = sums * inv - main_ref[rows, sl].astype(F32)
            y = jnp.dot(dlt.astype(BF16), pw_ref[g], preferred_element_type=F32) * sc_ref[:, sl]
            out_ref[rows, sl] = y.astype(BF16)


def _pool(pp, pool_w, pool_scale, seq):
    t, dp = pp.shape
    ts = min(512, seq)
    nh = t // HALO
    per = ts // HALO
    return pl.pallas_call(
        functools.partial(_pool_kernel, seq=seq, sub=256),
        out_shape=jax.ShapeDtypeStruct((t, dp), BF16),
        grid=(t // ts,),
        in_specs=[pl.BlockSpec((HALO, dp), lambda i: (jnp.maximum(i * per - 1, 0), 0)),
                  pl.BlockSpec((ts, dp), lambda i: (i, 0)),
                  pl.BlockSpec((HALO, dp), lambda i: (jnp.minimum((i + 1) * per, nh - 1), 0)),
                  pl.BlockSpec(pool_w.shape, lambda i: (0, 0, 0)),
                  pl.BlockSpec((1, dp), lambda i: (0, 0))],
        out_specs=pl.BlockSpec((ts, dp), lambda i: (i, 0)),
        compiler_params=_cparams(("arbitrary",)),
        name="pool",
    )(pp, pp, pp, pool_w, pool_scale)


def _ssm_tables(a_re, a_im, log_dt, b_re, b_im, c_re, c_im, d_skip, L):
    g = a_re.shape[1]
    p = a_re.shape[2]
    c = b_re.shape[3]
    fl = L * c
    lr = a_re.astype(F32)
    li = a_im.astype(F32)
    dt = jnp.exp(log_dt.astype(F32))[..., None]
    mag = jnp.exp(lr * dt)
    abr = mag * jnp.cos(li * dt)
    abi = mag * jnp.sin(li * dt)
    den = lr * lr + li * li
    zr = ((abr - 1.0) * lr + abi * li) / den
    zi = (abi * lr - (abr - 1.0) * li) / den
    br = b_re.astype(F32)
    bi = b_im.astype(F32)
    bbr = zr[..., None] * br - zi[..., None] * bi
    bbi = zr[..., None] * bi + zi[..., None] * br
    cr = c_re.astype(F32)
    ci = c_im.astype(F32)
    kk = jnp.arange(L + 1, dtype=F32)[None, None, :, None]
    pm = jnp.exp(kk * (lr * dt)[:, :, None, :])
    ang = kk * (li * dt)[:, :, None, :]
    pr = pm * jnp.cos(ang)
    pi = pm * jnp.sin(ang)

    skip = jnp.eye(c, dtype=F32)[None] * d_skip.astype(F32).reshape(g, c, 1)
    lag0 = jnp.einsum('dgop,dgpc->goc', cr, bbr, precision=lax.Precision.HIGHEST) \
        - jnp.einsum('dgop,dgpc->goc', ci, bbi, precision=lax.Precision.HIGHEST) + skip
    lag0 = jnp.pad(lag0, ((0, 0), (0, 0), (0, LANES - c)))
    bb = jnp.concatenate([bbr[0], bbi[0], bbr[1], bbi[1]], axis=1)
    bb = jnp.pad(bb, ((0, 0), (0, 0), (0, LANES - c)))

    def w1(d, qr, qi):
        qr = qr.transpose(0, 2, 1)[..., None]
        qi = qi.transpose(0, 2, 1)[..., None]
        re = qr * bbr[d][:, :, None] - qi * bbi[d][:, :, None]
        im = qr * bbi[d][:, :, None] + qi * bbr[d][:, :, None]
        to = lambda v: v.reshape(g, p, fl)
        return to(re), to(im)
    f_re, f_im = w1(0, pr[0][:, L - 1::-1], pi[0][:, L - 1::-1])
    b_re_, b_im_ = w1(1, pr[1][:, :L], pi[1][:, :L])
    w1t = jnp.concatenate([f_re, f_im, b_re_, b_im_], axis=1)

    def w2(d, qr, qi):
        re = cr[d][:, None] * qr[:, :, None] - ci[d][:, None] * qi[:, :, None]
        im = -(cr[d][:, None] * qi[:, :, None] + ci[d][:, None] * qr[:, :, None])
        to = lambda v: v.reshape(g, fl, p)
        return to(re), to(im)
    rf_re, rf_im = w2(0, pr[0][:, 1:], pi[0][:, 1:])
    rb_re, rb_im = w2(1, pr[1][:, :0:-1], pi[1][:, :0:-1])
    w2t = jnp.concatenate([rf_re, rf_im, rb_re, rb_im], axis=2)

    a4 = jnp.stack([pr[0][:, L], pi[0][:, L], pr[1][:, L], pi[1][:, L]], axis=1)
    ap = a4.reshape(g // 2, 2, 4, p).transpose(0, 2, 1, 3).reshape(g // 2, 4, 2 * p)
    ap = jnp.pad(ap, ((0, 0), (0, 4), (0, 0)))
    return lag0, bb.astype(BF16), w1t.astype(BF16), w2t.astype(BF16), ap


def _cmul(ar, ai, br, bi):
    return ar * br - ai * bi, ar * bi + ai * br


def _ssm_kernel(ut_ref, w1_ref, lag0_ref, bb_ref, w2_ref, a_ref, yt_ref, mt_s, kk_s, zt_s, ht_s, *, nseg, L):
    p = SSM_STATE
    c = SSM_GROUP
    fl, n = ut_ref.shape[1:]
    nb = n // nseg
    nrow = nseg // 8
    st = [jnp.dot(w1_ref[q], ut_ref[q], preferred_element_type=F32) for q in range(2)]
    z = jnp.concatenate([st[q][k * p:(k + 1) * p] for k in range(4) for q in range(2)], axis=0)
    zt_s[...] = z.T
    per = LANES // c
    for q in range(2):
        kf = jnp.dot(w2_ref[q, :, 0:2 * p], bb_ref[q, 0:2 * p, :], preferred_element_type=F32)
        kb = jnp.dot(w2_ref[q, :, 2 * p:4 * p], bb_ref[q, 2 * p:4 * p, :], preferred_element_type=F32)
        tiles = [None] * (2 * L // per)
        for j in range(2 * L - 1):
            lag = L - 1 - j
            if lag > 0:
                blk = kf[(lag - 1) * c:lag * c, :]
            elif lag == 0:
                blk = lag0_ref[q]
            else:
                blk = kb[(L + lag) * c:(L + lag + 1) * c, :]
            sh = (j % per) * c
            blk = blk if sh == 0 else pltpu.roll(blk, sh, 1)
            tiles[j // per] = blk if tiles[j // per] is None else tiles[j // per] + blk
        kk_s[...] = jnp.concatenate(tiles, axis=1).astype(BF16)
        for t in range(L):
            mt_s[q, t * c:(t + 1) * c, :] = kk_s[:, (L - 1 - t) * c:(L - 1 - t) * c + fl]

    sub = lax.broadcasted_iota(I32, (8, LANES), 0)
    full = lambda v: jnp.broadcast_to(v, (8, LANES))
    one, zero = jnp.ones((8, LANES), F32), jnp.zeros((8, LANES), F32)

    def powers(row):
        out = [(full(a_ref[0, row:row + 1, :]), full(a_ref[0, row + 1:row + 2, :]))]
        for _ in range(3):
            out.append(_cmul(*out[-1], *out[-1]))
        return out

    def by_bits(pw, idx):
        qr, qi = one, zero
        for bit in range(3):
            on = (idx >> bit) & 1 == 1
            qr, qi = _cmul(qr, qi, jnp.where(on, pw[bit][0], one), jnp.where(on, pw[bit][1], zero))
        return qr, qi

    pw_f, pw_b = powers(0), powers(2)
    qf = by_bits(pw_f, sub)
    qb = by_bits(pw_b, 7 - sub)

    def local(xr, xi, pw, forward):
        for lvl, d in enumerate((1, 2, 4)):
            keep = (sub >= d) if forward else (sub < 8 - d)
            sh = d if forward else 8 - d
            tr = jnp.where(keep, pltpu.roll(xr, sh, 0), 0.0)
            ti = jnp.where(keep, pltpu.roll(xi, sh, 0), 0.0)
            mr, mi = _cmul(pw[lvl][0], pw[lvl][1], tr, ti)
            xr, xi = xr + mr, xi + mi
        return xr, xi

    def step(i, carry):
        new = []
        for b in range(nb):
            for fwd in (True, False):
                cr, ci = carry[2 * (2 * b + (0 if fwd else 1)):][:2]
                row = b * nseg + (i if fwd else nrow - 1 - i) * 8
                rows = pl.ds(pl.multiple_of(row, 8), 8)
                l0 = 0 if fwd else 2 * LANES
                pw, (qr, qi) = (pw_f, qf) if fwd else (pw_b, qb)
                lr, li = local(zt_s[rows, l0:l0 + LANES], zt_s[rows, l0 + LANES:l0 + 2 * LANES], pw, fwd)
                keep = (sub >= 1) if fwd else (sub < 7)
                er = jnp.where(keep, pltpu.roll(lr, 1 if fwd else 7, 0), 0.0)
                ei = jnp.where(keep, pltpu.roll(li, 1 if fwd else 7, 0), 0.0)
                mr, mi = _cmul(qr, qi, cr, ci)
                ht_s[rows, l0:l0 + LANES] = er + mr
                ht_s[rows, l0 + LANES:l0 + 2 * LANES] = ei + mi
                edge = 7 if fwd else 0
                mr, mi = _cmul(pw[3][0], pw[3][1], cr, ci)
                new += [full(lr[edge:edge + 1, :]) + mr, full(li[edge:edge + 1, :]) + mi]
        return tuple(new)

    lax.fori_loop(0, nrow, step, (zero,) * (4 * nb))

    h = ht_s[...].astype(BF16).T
    for q in range(2):
        hq = jnp.concatenate([h[(2 * k + q) * p:(2 * k + q + 1) * p] for k in range(4)], axis=0)
        y = jnp.dot(mt_s[q], ut_ref[q], preferred_element_type=F32) \
            + jnp.dot(w2_ref[q], hq, preferred_element_type=F32)
        yt_ref[q] = y.astype(BF16)


def _ssm(ut3, lag0, bb, w1t, w2t, ap, nseg, L):
    g, fl, n = ut3.shape
    sp = w1t.shape[1]
    return pl.pallas_call(
        functools.partial(_ssm_kernel, nseg=nseg, L=L),
        out_shape=jax.ShapeDtypeStruct((g, fl, n), BF16),
        grid=(g // 2,),
        in_specs=[pl.BlockSpec((2, fl, n), lambda i: (i, 0, 0)),
                  pl.BlockSpec((2, sp, fl), lambda i: (i, 0, 0)),
                  pl.BlockSpec((2,) + lag0.shape[1:], lambda i: (i, 0, 0)),
                  pl.BlockSpec((2,) + bb.shape[1:], lambda i: (i, 0, 0)),
                  pl.BlockSpec((2, fl, sp), lambda i: (i, 0, 0)),
                  pl.BlockSpec((1,) + ap.shape[1:], lambda i: (i, 0, 0))],
        out_specs=pl.BlockSpec((2, fl, n), lambda i: (i, 0, 0)),
        scratch_shapes=[pltpu.VMEM((2, fl, fl), BF16),
                        pltpu.VMEM((SSM_GROUP, 2 * fl), BF16),
                        pltpu.VMEM((n, 2 * sp), F32),
                        pltpu.VMEM((n, 2 * sp), F32)],
        compiler_params=_cparams(("arbitrary",)),
        name="ssm",
    )(ut3, w1t, lag0, bb, w2t, ap)


def _glu_kernel(yt_ref, w_ref, b_ref, out_ref, nat_ref, *, sub):
    g, L, c, r = yt_ref.shape
    nj = nat_ref.shape[0]
    gj = LANES // c
    for t in range(L):
        for j in range(nj):
            piece = yt_ref[j * gj:(j + 1) * gj, t, :, :].reshape(LANES, r).astype(F32)
            nat_ref[j, pl.ds(t, r, stride=L), :] = piece.T

    def body(k, carry):
        sl = pl.ds(pl.multiple_of(k * sub, sub), sub)
        y = jnp.concatenate([nat_ref[j, sl, :] for j in range(nj)], axis=1)
        ya = jax.nn.gelu(y, approximate=True)
        z = jnp.dot(ya.astype(BF16), w_ref[...], preferred_element_type=F32) + b_ref[...]
        out_ref[sl, :] = (ya * (1.0 / (1.0 + jnp.exp(-z)))).astype(BF16)
        return carry
    lax.fori_loop(0, nat_ref.shape[1] // sub, body, 0)


def _glu(yt4, glu_w, glu_b):
    g, L, c, n = yt4.shape
    ds = g * c
    r = LANES
    return pl.pallas_call(
        functools.partial(_glu_kernel, sub=512),
        out_shape=jax.ShapeDtypeStruct((n * L, ds), BF16),
        grid=(n // r,),
        in_specs=[pl.BlockSpec((g, L, c, r), lambda j: (0, 0, 0, j)),
                  pl.BlockSpec((ds, ds), lambda j: (0, 0)),
                  pl.BlockSpec((1, ds), lambda j: (0, 0))],
        out_specs=pl.BlockSpec((r * L, ds), lambda j: (j, 0)),
        scratch_shapes=[pltpu.VMEM((ds // LANES, r * L, LANES), F32)],
        compiler_params=_cparams(("arbitrary",)),
        name="glu",
    )(yt4, glu_w, glu_b)


def _layer_norm(r, g, b):
    mu = jnp.mean(r, axis=-1, keepdims=True)
    cen = r - mu
    var = jnp.mean(cen * cen, axis=-1, keepdims=True)
    return cen * lax.rsqrt(var + LN_EPS) * g + b


def _store_row_packed(ref, m, row0=0):
    rows, width = m.shape
    half = width // 2
    nc = half // LANES
    lo = lax.bitcast_convert_type(m[:, :half].astype(BF16).astype(F32), U32) >> 16
    hi = lax.bitcast_convert_type(m[:, half:].astype(BF16).astype(F32), U32) & jnp.uint32(0xFFFF0000)
    pk = lo | hi
    for c in range(nc):
        ref[pl.ds(row0 * nc + c, rows, stride=nc), :] = pk[:, c * LANES:(c + 1) * LANES]


def _load_row_packed(ref, rows, nc, row0=0):
    los, his = [], []
    for c in range(nc):
        u = ref[pl.ds(row0 * nc + c, rows, stride=nc), :]
        los.append(lax.bitcast_convert_type(u << 16, F32))
        his.append(lax.bitcast_convert_type(u & jnp.uint32(0xFFFF0000), F32))
    return los, his


def _outproj_kernel(yp_ref, ys_ref, x_ref, wo_ref, g_ref, b_ref, rw_ref, rb_ref,
                    h_ref, hp_ref, lt_ref, mix_a, mix_b, *, alpha, dp, sub):
    i = pl.program_id(0)
    n = pl.num_programs(0) - 1
    nsub = h_ref.shape[0] // sub
    even = i % 2 == 0

    def matmuls(sb, dst):
        sl = pl.ds(sb * sub, sub)
        dst[sl, :] = jnp.dot(yp_ref[sl, :], wo_ref[0:dp, :], preferred_element_type=F32) \
            + jnp.dot(ys_ref[sl, :], wo_ref[dp:, :], preferred_element_type=F32)

    def finish(sb, src):
        sl = pl.ds(sb * sub, sub)
        h = _layer_norm(alpha * x_ref[sl, :] + src[sl, :], g_ref[...], b_ref[...])
        h_ref[sl, :] = h
        _store_row_packed(hp_ref, h, row0=sb * sub)
        hh = h.astype(BF16)
        hl = (h - hh.astype(F32)).astype(BF16)
        part = jnp.dot(hh, rw_ref[...], preferred_element_type=F32) \
            + jnp.dot(hl, rw_ref[...], preferred_element_type=F32)
        out = part + pltpu.roll(part, ROUTER_LANES // 2, 1) + rb_ref[...]
        lt_ref[:, sb * sub:(sb + 1) * sub] = out.T[0:ROUTER_ROWS, :]

    def body(dst, src):
        for sb in range(nsub):
            if dst is not None:
                matmuls(sb, dst)
            if src is not None:
                finish(sb, src)

    @pl.when(i == 0)
    def _():
        body(mix_a, None)

    for cond, dst, src in ((even, mix_a, mix_b), (jnp.logical_not(even), mix_b, mix_a)):
        @pl.when((i > 0) & (i < n) & cond)
        def _():
            body(dst, src)

        @pl.when((i == n) & cond)
        def _():
            body(None, src)


def _outproj(y_pool, y_ssm, xt, w_out, ln_g, ln_b, rw, rb, alpha):
    t, d = xt.shape
    dp = y_pool.shape[1]
    tm = min(512, t)
    n = t // tm
    pr = d // 2 // LANES
    cur = lambda i: jnp.minimum(i, n - 1)
    prev = lambda i: jnp.maximum(i - 1, 0)
    return pl.pallas_call(
        functools.partial(_outproj_kernel, alpha=alpha, dp=dp, sub=256),
        out_shape=(jax.ShapeDtypeStruct((t, d), F32),
                   jax.ShapeDtypeStruct((t * pr, LANES), U32),
                   jax.ShapeDtypeStruct((ROUTER_ROWS, t), F32)),
        grid=(n + 1,),
        in_specs=[pl.BlockSpec((tm, dp), lambda i: (cur(i), 0)),
                  pl.BlockSpec((tm, y_ssm.shape[1]), lambda i: (cur(i), 0)),
                  pl.BlockSpec((tm, d), lambda i: (prev(i), 0)),
                  pl.BlockSpec(w_out.shape, lambda i: (0, 0)),
                  pl.BlockSpec((1, d), lambda i: (0, 0)),
                  pl.BlockSpec((1, d), lambda i: (0, 0)),
                  pl.BlockSpec(rw.shape, lambda i: (0, 0)),
                  pl.BlockSpec((1, ROUTER_LANES), lambda i: (0, 0))],
        out_specs=(pl.BlockSpec((tm, d), lambda i: (prev(i), 0)),
                   pl.BlockSpec((tm * pr, LANES), lambda i: (prev(i), 0)),
                   pl.BlockSpec((ROUTER_ROWS, tm), lambda i: (0, prev(i)))),
        scratch_shapes=[pltpu.VMEM((tm, d), F32), pltpu.VMEM((tm, d), F32)],
        compiler_params=_cparams(("arbitrary",)),
        name="outproj",
    )(y_pool, y_ssm, xt, w_out, ln_g, ln_b, rw, rb)


def _router_kernel(lt_ref, eid_ref, rank_ref, gate_ref, cnt_ref, carry_ref):
    i = pl.program_id(0)
    ne, epg = N_EXPERTS, EXPERTS_PER_GROUP

    @pl.when(i == 0)
    def _():
        carry_ref[...] = jnp.zeros_like(carry_ref)

    lt = lt_ref[...]
    tt = lt.shape[1]
    gl = [lt[j:j + 1, :] for j in range(N_EXPERT_GROUPS)]
    gmax = jnp.maximum(jnp.maximum(gl[0], gl[1]), jnp.maximum(gl[2], gl[3]))
    grp = jnp.where(gl[0] == gmax, 0, jnp.where(gl[1] == gmax, 1, jnp.where(gl[2] == gmax, 2, 3)))
    p_grp = 1.0 / (jnp.exp(gl[0] - gmax) + jnp.exp(gl[1] - gmax)
                   + jnp.exp(gl[2] - gmax) + jnp.exp(gl[3] - gmax))
    eg = [lt[8 + epg * j: 8 + epg * (j + 1), :] for j in range(N_EXPERT_GROUPS)]
    el = jnp.where(grp == 0, eg[0], jnp.where(grp == 1, eg[1], jnp.where(grp == 2, eg[2], eg[3])))
    sub = lax.broadcasted_iota(I32, (epg, tt), 0)
    m1 = jnp.max(el, axis=0, keepdims=True)
    i1 = jnp.min(jnp.where(el == m1, sub, epg), axis=0, keepdims=True)
    rest = jnp.where(sub == i1, -jnp.inf, el)
    m2 = jnp.max(rest, axis=0, keepdims=True)
    i2 = jnp.min(jnp.where(rest == m2, sub, epg), axis=0, keepdims=True)
    r21 = jnp.exp(m2 - m1)
    g1 = p_grp / (1.0 + r21)
    g2 = g1 * r21
    e1 = grp * epg + i1
    e2 = grp * epg + i2

    rows = lax.broadcasted_iota(I32, (ne, tt), 0)
    oh1 = rows == e1
    oh2 = rows == e2
    oh = jnp.where(oh1 | oh2, 1.0, 0.0)
    tri = jnp.where(lax.broadcasted_iota(I32, (tt, tt), 0) < lax.broadcasted_iota(I32, (tt, tt), 1),
                    1.0, 0.0).astype(BF16)
    before = jnp.dot(oh.astype(BF16), tri, preferred_element_type=F32) + carry_ref[:, 0:1]
    r1 = jnp.sum(jnp.where(oh1, before, 0.0), axis=0, keepdims=True)
    r2 = jnp.sum(jnp.where(oh2, before, 0.0), axis=0, keepdims=True)
    carry_ref[...] = carry_ref[...] + jnp.sum(oh, axis=1, keepdims=True)

    eid_ref[...] = jnp.concatenate([e1, e2], axis=0)
    rank_ref[...] = jnp.concatenate([r1, r2], axis=0).astype(I32)
    gate_ref[...] = jnp.concatenate([g1, g2], axis=0)
    cnt_ref[...] = carry_ref[...].astype(I32)


def _router(lt):
    rr, t = lt.shape
    tt = min(512, t)
    return pl.pallas_call(
        _router_kernel,
        out_shape=(jax.ShapeDtypeStruct((2, t), I32),
                   jax.ShapeDtypeStruct((2, t), I32),
                   jax.ShapeDtypeStruct((2, t), F32),
                   jax.ShapeDtypeStruct((N_EXPERTS, 128), I32)),
        grid=(t // tt,),
        in_specs=[pl.BlockSpec((rr, tt), lambda i: (0, i))],
        out_specs=(pl.BlockSpec((2, tt), lambda i: (0, i)),
                   pl.BlockSpec((2, tt), lambda i: (0, i)),
                   pl.BlockSpec((2, tt), lambda i: (0, i)),
                   pl.BlockSpec((N_EXPERTS, 128), lambda i: (0, 0))),
        scratch_shapes=[pltpu.VMEM((N_EXPERTS, 128), F32)],
        compiler_params=_cparams(("arbitrary",)),
        name="router",
    )(lt)


def _invert_kernel(dest_ref, zeros_hbm, out_ref):
    pltpu.sync_copy(zeros_hbm, out_ref)
    t = dest_ref.shape[0] // 2

    def place(i, carry):
        out_ref[dest_ref[i]] = i
        out_ref[dest_ref[t + i]] = i
        return carry
    lax.fori_loop(0, t, place, 0, unroll=8)


def _invert(dest_flat, n_slots):
    return pl.pallas_call(
        _invert_kernel,
        out_shape=jax.ShapeDtypeStruct((n_slots,), I32),
        grid_spec=pltpu.PrefetchScalarGridSpec(
            num_scalar_prefetch=1, grid=(1,),
            in_specs=[pl.BlockSpec(memory_space=pl.ANY)],
            out_specs=pl.BlockSpec(memory_space=pltpu.SMEM)),
        compiler_params=_cparams(("arbitrary",)),
        name="invert",
    )(dest_flat, jnp.zeros((n_slots,), I32))


def _moe_kernel(be_ref, nu_ref, pc_ref, st_ref, hp_hbm, wg_hbm, wu_hbm, wd_hbm, y_ref,
                gbuf, wgf, wuf, wdf, wgb, wub, wdb, par_ref, gsem, wsem):
    b = pl.program_id(0)
    nu = nu_ref[0]
    pr = gbuf.shape[1] // MOE_BLK
    blk = MOE_BLK
    nbuf = gbuf.shape[0]
    look = nbuf - 1

    def row_copy(tok, slot, r):
        src = hp_hbm.at[pl.ds(pl.multiple_of(tok * pr, pr), pr)]
        off = r * pr if isinstance(r, int) else pl.multiple_of(r * pr, pr)
        return pltpu.make_async_copy(src, gbuf.at[slot, pl.ds(off, pr)], gsem.at[slot])

    def issue(block, slot):
        base = block * blk

        def body(r, carry):
            row_copy(st_ref[base + r], slot, r).start()
            return carry
        lax.fori_loop(0, blk, body, 0, unroll=8)

    def weight_copies(e, p):
        return (pltpu.make_async_copy(wg_hbm.at[e], wgf.at[p], wsem.at[p]),
                pltpu.make_async_copy(wu_hbm.at[e], wuf.at[p], wsem.at[p]),
                pltpu.make_async_copy(wd_hbm.at[e], wdf.at[p], wsem.at[p]))

    @pl.when(b == 0)
    def _():
        par_ref[0] = 0
        for cp in weight_copies(be_ref[0], 0):
            cp.start(priority=1)
        issue(0, 0)
        for j in range(1, look):
            @pl.when(j < nu)
            def _():
                issue(j, j)

    def issue_part(k):
        base = (b + look) * blk
        for r in range(k * blk // 4, (k + 1) * blk // 4):
            row_copy(st_ref[base + r], (b + look) % nbuf, r).start(priority=1 if r % 3 == 2 else 0)

    def expert_mlp(slot, with_issue):
        los, his = _load_row_packed(gbuf.at[slot], blk, pr)
        x = jnp.concatenate([v.astype(BF16) for v in los + his], axis=1)
        if with_issue:
            issue_part(0)
        hg = jnp.dot(x, wgb[...], preferred_element_type=F32)
        if with_issue:
            issue_part(1)
        hu = jnp.dot(x, wub[...], preferred_element_type=F32)
        if with_issue:
            issue_part(2)
        hh = hg * (1.0 / (1.0 + jnp.exp(-hg))) * hu
        y = jnp.dot(hh.astype(BF16), wdb[...], preferred_element_type=F32)
        if with_issue:
            issue_part(3)
        _store_row_packed(y_ref, y)

    @pl.when(b < nu)
    def _():
        slot = b % nbuf
        e = be_ref[b]
        first = (b == 0) | (e != be_ref[jnp.maximum(b - 1, 0)])

        @pl.when(first)
        def _():
            p = par_ref[0]
            for cp in weight_copies(e, p):
                cp.wait()
            nxt = lax.while_loop(
                lambda c: (c < N_EXPERTS) & (pc_ref[jnp.minimum(c, N_EXPERTS - 1)] == 0),
                lambda c: c + 1, e + 1)

            @pl.when(nxt < N_EXPERTS)
            def _():
                for cp in weight_copies(nxt, 1 - p):
                    cp.start(priority=1)
            wgb[...] = wgf[p].astype(BF16)
            wub[...] = wuf[p].astype(BF16)
            wdb[...] = wdf[p].astype(BF16)
            par_ref[0] = 1 - p

        pltpu.make_async_copy(hp_hbm.at[pl.ds(0, blk * pr)], gbuf.at[slot], gsem.at[slot]).wait()

        @pl.when(b + look < nu)
        def _():
            expert_mlp(slot, True)

        @pl.when(b + look >= nu)
        def _():
            expert_mlp(slot, False)

    @pl.when(b >= nu)
    def _():
        y_ref[...] = jnp.zeros_like(y_ref)


def _moe(block_e, n_used, pcounts, slot_tok, hp, w_gate, w_up, w_down, n_blocks):
    ne, d, de = w_gate.shape
    blk = MOE_BLK
    pr = d // 2 // LANES
    grid_spec = pltpu.PrefetchScalarGridSpec(
        num_scalar_prefetch=4,
        grid=(n_blocks,),
        in_specs=[pl.BlockSpec(memory_space=pl.ANY)] * 4,
        out_specs=pl.BlockSpec((blk * pr, LANES), lambda b, *_: (b, 0)),
        scratch_shapes=[pltpu.VMEM((GATHER_BUFS, blk * pr, LANES), U32),
                        pltpu.VMEM((2, d, de), F32),
                        pltpu.VMEM((2, d, de), F32),
                        pltpu.VMEM((2, de, d), F32),
                        pltpu.VMEM((d, de), BF16),
                        pltpu.VMEM((d, de), BF16),
                        pltpu.VMEM((de, d), BF16),
                        pltpu.SMEM((1,), I32),
                        pltpu.SemaphoreType.DMA((GATHER_BUFS,)),
                        pltpu.SemaphoreType.DMA((2,))],
    )
    return pl.pallas_call(
        _moe_kernel,
        out_shape=jax.ShapeDtypeStruct((n_blocks * blk * pr, LANES), U32),
        grid_spec=grid_spec,
        compiler_params=_cparams(("arbitrary",)),
        name="moe",
    )(block_e, n_used, pcounts, slot_tok, hp, w_gate, w_up, w_down)


def _combine_kernel(d0_ref, d1_ref, h_ref, gate_ref, yb_hbm, g_ref, b_ref, out_ref, buf, sem, *, alpha):
    i = pl.program_id(0)
    n = pl.num_programs(0)
    tm = h_ref.shape[0]
    pr = buf.shape[2] // tm

    def issue(tile, slot):
        base = tile * tm

        def body(r, carry):
            dst = pl.ds(pl.multiple_of(r * pr, pr), pr)
            for k, dref in enumerate((d0_ref, d1_ref)):
                src = yb_hbm.at[pl.ds(pl.multiple_of(dref[base + r] * pr, pr), pr)]
                pltpu.make_async_copy(src, buf.at[slot, k, dst], sem.at[slot]).start()
            return carry
        lax.fori_loop(0, tm, body, 0, unroll=8)

    nbuf = buf.shape[0]
    look = nbuf - 1

    @pl.when(i == 0)
    def _():
        issue(0, 0)
        for j in range(1, look):
            @pl.when(j < n)
            def _():
                issue(j, j)

    slot = i % nbuf
    for k in range(2):
        pltpu.make_async_copy(yb_hbm.at[pl.ds(0, tm * pr)], buf.at[slot, k], sem.at[slot]).wait()

    nq = 4
    sub = tm // nq

    def issue_part(q):
        base = (i + look) * tm
        nslot = (i + look) % nbuf
        for r in range(q * sub, (q + 1) * sub):
            for k, dref in enumerate((d0_ref, d1_ref)):
                src = yb_hbm.at[pl.ds(pl.multiple_of(dref[base + r] * pr, pr), pr)]
                pltpu.make_async_copy(src, buf.at[nslot, k, pl.ds(r * pr, pr)], sem.at[nslot]).start(priority=k)

    def finish(with_issue):
        for q in range(nq):
            if with_issue:
                issue_part(q)
            rows = pl.ds(q * sub, sub)
            lo0, hi0 = _load_row_packed(buf.at[slot, 0], sub, pr, row0=q * sub)
            lo1, hi1 = _load_row_packed(buf.at[slot, 1], sub, pr, row0=q * sub)
            g0 = gate_ref[rows, 0:1]
            g1 = gate_ref[rows, 1:2]
            ffn = jnp.concatenate([g0 * a + g1 * c for a, c in zip(lo0 + hi0, lo1 + hi1)], axis=1)
            out_ref[rows, :] = _layer_norm(alpha * h_ref[rows, :] + ffn, g_ref[...], b_ref[...])

    @pl.when(i + look < n)
    def _():
        finish(True)

    @pl.when(i + look >= n)
    def _():
        finish(False)


def _combine(dest0, dest1, h1, gate_t, yb, ln_g, ln_b, alpha):
    t, d = h1.shape
    tm = min(256, t)
    grid_spec = pltpu.PrefetchScalarGridSpec(
        num_scalar_prefetch=2,
        grid=(t // tm,),
        in_specs=[pl.BlockSpec((tm, d), lambda i, d0, d1: (i, 0)),
                  pl.BlockSpec((tm, 2), lambda i, d0, d1: (i, 0)),
                  pl.BlockSpec(memory_space=pl.ANY),
                  pl.BlockSpec((1, d), lambda i, d0, d1: (0, 0)),
                  pl.BlockSpec((1, d), lambda i, d0, d1: (0, 0))],
        out_specs=pl.BlockSpec((tm, d), lambda i, d0, d1: (i, 0)),
        scratch_shapes=[pltpu.VMEM((GATHER_BUFS, 2, tm * (d // 2 // LANES), LANES), U32),
                        pltpu.SemaphoreType.DMA((GATHER_BUFS,))],
    )
    return pl.pallas_call(
        functools.partial(_combine_kernel, alpha=alpha),
        out_shape=jax.ShapeDtypeStruct((t, d), F32),
        grid_spec=grid_spec,
        compiler_params=_cparams(("arbitrary",)),
        name="combine",
    )(dest0, dest1, h1, gate_t, yb, ln_g, ln_b)


def _layer(h, w_in, pool_w, pool_scale, a_re, a_im, log_dt, b_re, b_im, c_re, c_im, d_skip,
           glu_w, glu_b, w_out, ln1_g, ln1_b, rg_w, rg_b, re_w, re_b, w_gate, w_up, w_down,
           ln2_g, ln2_b, alpha):
    bsz, seq, d = h.shape
    t = bsz * seq
    L = CHUNK
    dp = pool_w.shape[0] * pool_w.shape[1]
    ds = w_in.shape[1] - dp
    g = ds // SSM_GROUP
    n = t // L

    lag0, bb, w1t, w2t, al = _ssm_tables(a_re, a_im, log_dt, b_re, b_im, c_re, c_im, d_skip, L)
    half = ROUTER_LANES // 2
    zpad = lambda k: jnp.zeros((d, k), F32)
    rw = jnp.concatenate([rg_w.astype(F32), zpad(8 - N_EXPERT_GROUPS), re_w.astype(F32),
                          zpad(half - ROUTER_ROWS)], axis=1)
    rw_hi = rw.astype(BF16)
    rw_lo = (rw - rw_hi.astype(F32)).astype(BF16)
    rw2 = jnp.concatenate([rw_hi, rw_lo], axis=1)
    rb = jnp.concatenate([rg_b.astype(F32), jnp.zeros((8 - N_EXPERT_GROUPS,), F32), re_b.astype(F32),
                          jnp.zeros((ROUTER_LANES - ROUTER_ROWS,), F32)]).reshape(1, ROUTER_LANES)

    xt = h.reshape(t, d)
    pool_p, ut = _proj(xt, w_in.astype(BF16), dp, L)
    y_pool = _pool(pool_p, pool_w.astype(BF16), pool_scale.reshape(1, dp).astype(F32), seq)
    yt = _ssm(ut.reshape(g, L * SSM_GROUP, n), lag0, bb, w1t, w2t, al, seq // L, L)
    y_ssm = _glu(yt.reshape(g, L, SSM_GROUP, n), glu_w.astype(BF16), glu_b.reshape(1, ds).astype(F32))
    h1, hp, lt = _outproj(y_pool, y_ssm, xt, w_out.astype(BF16), ln1_g.reshape(1, d), ln1_b.reshape(1, d),
                          rw2, rb, alpha)
    eid, rank, gate, cnt = _router(lt)

    blk = MOE_BLK
    m = 2 * t
    n_blocks = -(-m // blk) + N_EXPERTS
    counts = cnt[:, 0]
    pcounts = (counts + blk - 1) // blk * blk
    pends = jnp.cumsum(pcounts)
    pstarts = pends - pcounts
    e_ids = jnp.arange(N_EXPERTS, dtype=I32)
    dest = jnp.sum(jnp.where(eid[..., None] == e_ids, pstarts, 0), axis=-1) + rank
    slot_tok = _invert(dest.reshape(-1), n_blocks * blk)
    n_used = (pends[-1] // blk).astype(I32)
    bidx = jnp.minimum(jnp.arange(n_blocks, dtype=I32), n_used - 1)
    block_e = jnp.minimum(jnp.sum((pends[None, :] <= (bidx * blk)[:, None]).astype(I32), axis=1),
                          N_EXPERTS - 1)

    yb = _moe(block_e, n_used.reshape(1), pcounts, slot_tok, hp, w_gate, w_up, w_down, n_blocks)
    out = _combine(dest[0], dest[1], h1, gate.T, yb, ln2_g.reshape(1, d), ln2_b.reshape(1, d), alpha)
    return out.reshape(bsz, seq, d)


def kernel(x, w_in, pool_w, pool_scale, ssm_a_re, ssm_a_im, ssm_log_dt, ssm_b_re, ssm_b_im, ssm_c_re, ssm_c_im, ssm_d, glu_w, glu_b, w_out, ln1_g, ln1_b, router_g_w, router_g_b, router_e_w, router_e_b, w_gate, w_up, w_down, ln2_g, ln2_b):
    depth = w_in.shape[0]
    alpha = (2.0 * depth) ** 0.25
    h = x
    for l in range(depth):
        h = _layer(h, w_in[l], pool_w[l], pool_scale[l], ssm_a_re[l], ssm_a_im[l], ssm_log_dt[l],
                   ssm_b_re[l], ssm_b_im[l], ssm_c_re[l], ssm_c_im[l], ssm_d[l], glu_w[l], glu_b[l],
                   w_out[l], ln1_g[l], ln1_b[l], router_g_w[l], router_g_b[l], router_e_w[l],
                   router_e_b[l], w_gate[l], w_up[l], w_down[l], ln2_g[l], ln2_b[l], alpha)
    return h
```

```python
import functools
import math

import numpy as np
import jax
import jax.numpy as jnp
from jax import lax
from jax.experimental import pallas as pl
from jax.experimental.pallas import tpu as pltpu

F32 = jnp.float32
BF16 = jnp.bfloat16
I32 = jnp.int32
U32 = jnp.uint32

POOL_WINDOWS = (2, 4, 8, 16)
POOL_GROUP = 256
SSM_GROUP = 16
SSM_STATE = 64
N_EXPERT_GROUPS = 4
EXPERTS_PER_GROUP = 8
N_EXPERTS = N_EXPERT_GROUPS * EXPERTS_PER_GROUP
LN_EPS = 1e-5

CHUNK = 16
MOE_BLK = 256
GATHER_BUFS = 4
ROUTER_ROWS = 8 + N_EXPERTS
ROUTER_LANES = 128
LANES = 128
MXU_N = 256
HALO = 16
VMEM_LIMIT = 56 * 1024 * 1024


def _cparams(sem, vmem=VMEM_LIMIT):
    return pltpu.CompilerParams(dimension_semantics=sem, vmem_limit_bytes=vmem)


def _proj_kernel(x_ref, w_ref, pool_ref, ut_ref, accp_ref, accs_ref):
    kk = pl.program_id(1)
    rows, dp = accp_ref.shape
    nj = accs_ref.shape[0]

    xb = x_ref[...].astype(BF16)
    per = MXU_N // LANES

    def accumulate(first):
        for n in range(dp // MXU_N):
            cs = slice(n * MXU_N, (n + 1) * MXU_N)
            part = jnp.dot(xb, w_ref[:, cs], preferred_element_type=F32)
            accp_ref[:, cs] = part if first else accp_ref[:, cs] + part
        for n in range(nj // per):
            part = jnp.dot(xb, w_ref[:, dp + n * MXU_N: dp + (n + 1) * MXU_N], preferred_element_type=F32)
            for q in range(per):
                piece = part[:, q * LANES:(q + 1) * LANES]
                accs_ref[n * per + q] = piece if first else accs_ref[n * per + q] + piece

    @pl.when(kk == 0)
    def _():
        accumulate(True)

    @pl.when(kk > 0)
    def _():
        accumulate(False)

    @pl.when(kk == pl.num_programs(1) - 1)
    def _():
        pool_ref[...] = accp_ref[...].astype(BF16)
        g, L, c, r = ut_ref.shape
        gj = LANES // c
        for s in range(L):
            for j in range(nj):
                piece = accs_ref[j, pl.ds(s, r, stride=L), :]
                ut_ref[j * gj:(j + 1) * gj, s, :, :] = piece.astype(BF16).T.reshape(gj, c, r)


def _proj(xt, w_in, dp, L):
    t, d = xt.shape
    dm = w_in.shape[1]
    g = (dm - dp) // SSM_GROUP
    tm = LANES * L
    kb = 512
    return pl.pallas_call(
        _proj_kernel,
        out_shape=(jax.ShapeDtypeStruct((t, dp), BF16),
                   jax.ShapeDtypeStruct((g, L, SSM_GROUP, t // L), BF16)),
        grid=(t // tm, d // kb),
        in_specs=[pl.BlockSpec((tm, kb), lambda i, k: (i, k)),
                  pl.BlockSpec((kb, dm), lambda i, k: (k, 0))],
        out_specs=(pl.BlockSpec((tm, dp), lambda i, k: (i, 0)),
                   pl.BlockSpec((g, L, SSM_GROUP, LANES), lambda i, k: (0, 0, 0, i))),
        scratch_shapes=[pltpu.VMEM((tm, dp), F32),
                        pltpu.VMEM(((dm - dp) // LANES, tm, LANES), F32)],
        compiler_params=_cparams(("arbitrary", "arbitrary")),
        name="proj",
    )(xt, w_in)


def _pool_kernel(prev_ref, main_ref, next_ref, pw_ref, sc_ref, out_ref, *, seq, sub):
    i = pl.program_id(0)
    ts = main_ref.shape[0]
    ext = jnp.concatenate([prev_ref[...], main_ref[...], next_ref[...]], axis=0)
    k = sub + 2 * HALO
    row = lax.broadcasted_iota(I32, (sub, k), 0)
    col = lax.broadcasted_iota(I32, (sub, k), 1)
    off0 = col - HALO - row
    for sb in range(ts // sub):
        rows = slice(sb * sub, (sb + 1) * sub)
        base = (i * ts + sb * sub) % seq
        src = base + row + off0
        off = jnp.where((src >= 0) & (src < seq), off0, 2 * HALO)
        pos = base + lax.broadcasted_iota(I32, (sub, 1), 0)
        for g, w in enumerate(POOL_WINDOWS):
            sl = slice(g * POOL_GROUP, (g + 1) * POOL_GROUP)
            band = (off + w // 2).astype(U32) < w
            bm = jnp.where(band, 1.0, 0.0).astype(BF16)
            sums = jnp.dot(bm, ext[sb * sub:sb * sub + k, sl], preferred_element_type=F32)
            lo = jnp.maximum(pos - w // 2, 0)
            hi = jnp.minimum(pos - w // 2 + w, seq)
            inv = 1.0 / (hi - lo).astype(F32)
            dlt = sums * inv - main_ref[rows, sl].astype(F32)
            y = jnp.dot(dlt.astype(BF16), pw_ref[g], preferred_element_type=F32) * sc_ref[:, sl]
            out_ref[rows, sl] = y.astype(BF16)


def _pool(pp, pool_w, pool_scale, seq):
    t, dp = pp.shape
    ts = min(512, seq)
    nh = t // HALO
    per = ts // HALO
    return pl.pallas_call(
        functools.partial(_pool_kernel, seq=seq, sub=256),
        out_shape=jax.ShapeDtypeStruct((t, dp), BF16),
        grid=(t // ts,),
        in_specs=[pl.BlockSpec((HALO, dp), lambda i: (jnp.maximum(i * per - 1, 0), 0)),
                  pl.BlockSpec((ts, dp), lambda i: (i, 0)),
                  pl.BlockSpec((HALO, dp), lambda i: (jnp.minimum((i + 1) * per, nh - 1), 0)),
                  pl.BlockSpec(pool_w.shape, lambda i: (0, 0, 0)),
                  pl.BlockSpec((1, dp), lambda i: (0, 0))],
        out_specs=pl.BlockSpec((ts, dp), lambda i: (i, 0)),
        compiler_params=_cparams(("arbitrary",)),
        name="pool",
    )(pp, pp, pp, pool_w, pool_scale)


def _ssm_tables(a_re, a_im, log_dt, b_re, b_im, c_re, c_im, d_skip, L):
    g = a_re.shape[1]
    p = a_re.shape[2]
    c = b_re.shape[3]
    fl = L * c
    lr = a_re.astype(F32)
    li = a_im.astype(F32)
    dt = jnp.exp(log_dt.astype(F32))[..., None]
    mag = jnp.exp(lr * dt)
    abr = mag * jnp.cos(li * dt)
    abi = mag * jnp.sin(li * dt)
    den = lr * lr + li * li
    zr = ((abr - 1.0) * lr + abi * li) / den
    zi = (abi * lr - (abr - 1.0) * li) / den
    br = b_re.astype(F32)
    bi = b_im.astype(F32)
    bbr = zr[..., None] * br - zi[..., None] * bi
    bbi = zr[..., None] * bi + zi[..., None] * br
    cr = c_re.astype(F32)
    ci = c_im.astype(F32)
    kk = jnp.arange(L + 1, dtype=F32)[None, None, :, None]
    pm = jnp.exp(kk * (lr * dt)[:, :, None, :])
    ang = kk * (li * dt)[:, :, None, :]
    pr = pm * jnp.cos(ang)
    pi = pm * jnp.sin(ang)

    skip = jnp.eye(c, dtype=F32)[None] * d_skip.astype(F32).reshape(g, c, 1)
    lag0 = jnp.einsum('dgop,dgpc->goc', cr, bbr, precision=lax.Precision.HIGHEST) \
        - jnp.einsum('dgop,dgpc->goc', ci, bbi, precision=lax.Precision.HIGHEST) + skip
    lag0 = jnp.pad(lag0, ((0, 0), (0, 0), (0, LANES - c)))
    bb = jnp.concatenate([bbr[0], bbi[0], bbr[1], bbi[1]], axis=1)
    bb = jnp.pad(bb, ((0, 0), (0, 0), (0, LANES - c)))

    def w1(d, qr, qi):
        qr = qr.transpose(0, 2, 1)[..., None]
        qi = qi.transpose(0, 2, 1)[..., None]
        re = qr * bbr[d][:, :, None] - qi * bbi[d][:, :, None]
        im = qr * bbi[d][:, :, None] + qi * bbr[d][:, :, None]
        to = lambda v: v.reshape(g, p, fl)
        return to(re), to(im)
    f_re, f_im = w1(0, pr[0][:, L - 1::-1], pi[0][:, L - 1::-1])
    b_re_, b_im_ = w1(1, pr[1][:, :L], pi[1][:, :L])
    w1t = jnp.concatenate([f_re, f_im, b_re_, b_im_], axis=1)

    def w2(d, qr, qi):
        re = cr[d][:, None] * qr[:, :, None] - ci[d][:, None] * qi[:, :, None]
        im = -(cr[d][:, None] * qi[:, :, None] + ci[d][:, None] * qr[:, :, None])
        to = lambda v: v.reshape(g, fl, p)
        return to(re), to(im)
    rf_re, rf_im = w2(0, pr[0][:, 1:], pi[0][:, 1:])
    rb_re, rb_im = w2(1, pr[1][:, :0:-1], pi[1][:, :0:-1])
    w2t = jnp.concatenate([rf_re, rf_im, rb_re, rb_im], axis=2)

    a4 = jnp.stack([pr[0][:, L], pi[0][:, L], pr[1][:, L], pi[1][:, L]], axis=1)
    ap = a4.reshape(g // 2, 2, 4, p).transpose(0, 2, 1, 3).reshape(g // 2, 4, 2 * p)
    ap = jnp.pad(ap, ((0, 0), (0, 4), (0, 0)))
    return lag0, bb.astype(BF16), w1t.astype(BF16), w2t.astype(BF16), ap


def _cmul(ar, ai, br, bi):
    return ar * br - ai * bi, ar * bi + ai * br


def _ssm_kernel(ut_ref, w1_ref, lag0_ref, bb_ref, w2_ref, a_ref, yt_ref, mt_s, kk_s, zt_s, ht_s, *, nseg, L):
    p = SSM_STATE
    c = SSM_GROUP
    fl, n = ut_ref.shape[1:]
    nb = n // nseg
    nrow = nseg // 8
    st = [jnp.dot(w1_ref[q], ut_ref[q], preferred_element_type=F32) for q in range(2)]
    z = jnp.concatenate([st[q][k * p:(k + 1) * p] for k in range(4) for q in range(2)], axis=0)
    zt_s[...] = z.T
    per = LANES // c
    for q in range(2):
        kf = jnp.dot(w2_ref[q, :, 0:2 * p], bb_ref[q, 0:2 * p, :], preferred_element_type=F32)
        kb = jnp.dot(w2_ref[q, :, 2 * p:4 * p], bb_ref[q, 2 * p:4 * p, :], preferred_element_type=F32)
        tiles = [None] * (2 * L // per)
        for j in range(2 * L - 1):
            lag = L - 1 - j
            if lag > 0:
                blk = kf[(lag - 1) * c:lag * c, :]
            elif lag == 0:
                blk = lag0_ref[q]
            else:
                blk = kb[(L + lag) * c:(L + lag + 1) * c, :]
            sh = (j % per) * c
            blk = blk if sh == 0 else pltpu.roll(blk, sh, 1)
            tiles[j // per] = blk if tiles[j // per] is None else tiles[j // per] + blk
        kk_s[...] = jnp.concatenate(tiles, axis=1).astype(BF16)
        for t in range(L):
            mt_s[q, t * c:(t + 1) * c, :] = kk_s[:, (L - 1 - t) * c:(L - 1 - t) * c + fl]

    sub = lax.broadcasted_iota(I32, (8, LANES), 0)
    full = lambda v: jnp.broadcast_to(v, (8, LANES))
    one, zero = jnp.ones((8, LANES), F32), jnp.zeros((8, LANES), F32)

    def powers(row):
        out = [(full(a_ref[0, row:row + 1, :]), full(a_ref[0, row + 1:row + 2, :]))]
        for _ in range(3):
            out.append(_cmul(*out[-1], *out[-1]))
        return out

    def by_bits(pw, idx):
        qr, qi = one, zero
        for bit in range(3):
            on = (idx >> bit) & 1 == 1
            qr, qi = _cmul(qr, qi, jnp.where(on, pw[bit][0], one), jnp.where(on, pw[bit][1], zero))
        return qr, qi

    pw_f, pw_b = powers(0), powers(2)
    qf = by_bits(pw_f, sub)
    qb = by_bits(pw_b, 7 - sub)

    def local(xr, xi, pw, forward):
        for lvl, d in enumerate((1, 2, 4)):
            keep = (sub >= d) if forward else (sub < 8 - d)
            sh = d if forward else 8 - d
            tr = jnp.where(keep, pltpu.roll(xr, sh, 0), 0.0)
            ti = jnp.where(keep, pltpu.roll(xi, sh, 0), 0.0)
            mr, mi = _cmul(pw[lvl][0], pw[lvl][1], tr, ti)
            xr, xi = xr + mr, xi + mi
        return xr, xi

    def step(i, carry):
        new = []
        for b in range(nb):
            for fwd in (True, False):
                cr, ci = carry[2 * (2 * b + (0 if fwd else 1)):][:2]
                row = b * nseg + (i if fwd else nrow - 1 - i) * 8
                rows = pl.ds(pl.multiple_of(row, 8), 8)
                l0 = 0 if fwd else 2 * LANES
                pw, (qr, qi) = (pw_f, qf) if fwd else (pw_b, qb)
                lr, li = local(zt_s[rows, l0:l0 + LANES], zt_s[rows, l0 + LANES:l0 + 2 * LANES], pw, fwd)
                keep = (sub >= 1) if fwd else (sub < 7)
                er = jnp.where(keep, pltpu.roll(lr, 1 if fwd else 7, 0), 0.0)
                ei = jnp.where(keep, pltpu.roll(li, 1 if fwd else 7, 0), 0.0)
                mr, mi = _cmul(qr, qi, cr, ci)
                ht_s[rows, l0:l0 + LANES] = er + mr
                ht_s[rows, l0 + LANES:l0 + 2 * LANES] = ei + mi
                edge = 7 if fwd else 0
                mr, mi = _cmul(pw[3][0], pw[3][1], cr, ci)
                new += [full(lr[edge:edge + 1, :]) + mr, full(li[edge:edge + 1, :]) + mi]
        return tuple(new)

    lax.fori_loop(0, nrow, step, (zero,) * (4 * nb))

    h = ht_s[...].astype(BF16).T
    for q in range(2):
        hq = jnp.concatenate([h[(2 * k + q) * p:(2 * k + q + 1) * p] for k in range(4)], axis=0)
        y = jnp.dot(mt_s[q], ut_ref[q], preferred_element_type=F32) \
            + jnp.dot(w2_ref[q], hq, preferred_element_type=F32)
        yt_ref[q] = y.astype(BF16)


def _ssm(ut3, lag0, bb, w1t, w2t, ap, nseg, L):
    g, fl, n = ut3.shape
    sp = w1t.shape[1]
    return pl.pallas_call(
        functools.partial(_ssm_kernel, nseg=nseg, L=L),
        out_shape=jax.ShapeDtypeStruct((g, fl, n), BF16),
        grid=(g // 2,),
        in_specs=[pl.BlockSpec((2, fl, n), lambda i: (i, 0, 0)),
                  pl.BlockSpec((2, sp, fl), lambda i: (i, 0, 0)),
                  pl.BlockSpec((2,) + lag0.shape[1:], lambda i: (i, 0, 0)),
                  pl.BlockSpec((2,) + bb.shape[1:], lambda i: (i, 0, 0)),
                  pl.BlockSpec((2, fl, sp), lambda i: (i, 0, 0)),
                  pl.BlockSpec((1,) + ap.shape[1:], lambda i: (i, 0, 0))],
        out_specs=pl.BlockSpec((2, fl, n), lambda i: (i, 0, 0)),
        scratch_shapes=[pltpu.VMEM((2, fl, fl), BF16),
                        pltpu.VMEM((SSM_GROUP, 2 * fl), BF16),
                        pltpu.VMEM((n, 2 * sp), F32),
                        pltpu.VMEM((n, 2 * sp), F32)],
        compiler_params=_cparams(("arbitrary",)),
        name="ssm",
    )(ut3, w1t, lag0, bb, w2t, ap)


def _glu_kernel(yt_ref, w_ref, b_ref, out_ref, nat_a, nat_b, *, sub):
    i = pl.program_id(0)
    n = pl.num_programs(0) - 1
    g, L, c, r = yt_ref.shape
    nj = nat_a.shape[0]
    gj = LANES // c

    def stage(dst):
        for t in range(L):
            for j in range(nj):
                piece = yt_ref[j * gj:(j + 1) * gj, t, :, :].reshape(LANES, r).astype(F32)
                dst[j, pl.ds(t, r, stride=L), :] = piece.T

    def compute(src):
        for k in range(src.shape[1] // sub):
            sl = pl.ds(k * sub, sub)
            y = jnp.concatenate([src[j, sl, :] for j in range(nj)], axis=1)
            ya = jax.nn.gelu(y, approximate=True)
            z = jnp.dot(ya.astype(BF16), w_ref[...], preferred_element_type=F32) + b_ref[...]
            out_ref[sl, :] = (ya * (1.0 / (1.0 + jnp.exp(-z)))).astype(BF16)

    even = i % 2 == 0

    @pl.when(i == 0)
    def _():
        stage(nat_a)

    for cond, dst, src in ((even, nat_a, nat_b), (jnp.logical_not(even), nat_b, nat_a)):
        @pl.when((i > 0) & (i < n) & cond)
        def _():
            stage(dst)
            compute(src)

        @pl.when((i == n) & cond)
        def _():
            compute(src)


def _glu(yt4, glu_w, glu_b):
    g, L, c, n = yt4.shape
    ds = g * c
    r = LANES
    nt = n // r
    nat = pltpu.VMEM((ds // LANES, r * L, LANES), F32)
    return pl.pallas_call(
        functools.partial(_glu_kernel, sub=512),
        out_shape=jax.ShapeDtypeStruct((n * L, ds), BF16),
        grid=(nt + 1,),
        in_specs=[pl.BlockSpec((g, L, c, r), lambda j: (0, 0, 0, jnp.minimum(j, nt - 1))),
                  pl.BlockSpec((ds, ds), lambda j: (0, 0)),
                  pl.BlockSpec((1, ds), lambda j: (0, 0))],
        out_specs=pl.BlockSpec((r * L, ds), lambda j: (jnp.maximum(j - 1, 0), 0)),
        scratch_shapes=[nat, nat],
        compiler_params=_cparams(("arbitrary",)),
        name="glu",
    )(yt4, glu_w, glu_b)


def _layer_norm(r, g, b):
    mu = jnp.mean(r, axis=-1, keepdims=True)
    cen = r - mu
    var = jnp.mean(cen * cen, axis=-1, keepdims=True)
    return cen * lax.rsqrt(var + LN_EPS) * g + b


def _store_row_packed(ref, m, row0=0):
    rows, width = m.shape
    half = width // 2
    nc = half // LANES
    lo = lax.bitcast_convert_type(m[:, :half].astype(BF16).astype(F32), U32) >> 16
    hi = lax.bitcast_convert_type(m[:, half:].astype(BF16).astype(F32), U32) & jnp.uint32(0xFFFF0000)
    pk = lo | hi
    for c in range(nc):
        ref[pl.ds(row0 * nc + c, rows, stride=nc), :] = pk[:, c * LANES:(c + 1) * LANES]


def _load_row_packed(ref, rows, nc, row0=0):
    los, his = [], []
    for c in range(nc):
        u = ref[pl.ds(row0 * nc + c, rows, stride=nc), :]
        los.append(lax.bitcast_convert_type(u << 16, F32))
        his.append(lax.bitcast_convert_type(u & jnp.uint32(0xFFFF0000), F32))
    return los, his


def _outproj_kernel(yp_ref, ys_ref, x_ref, wo_ref, g_ref, b_ref, rw_ref, rb_ref,
                    h_ref, hp_ref, lt_ref, *, alpha, dp, sub):
    for sb in range(h_ref.shape[0] // sub):
        sl = pl.ds(sb * sub, sub)
        mix = jnp.dot(yp_ref[sl, :], wo_ref[0:dp, :], preferred_element_type=F32) \
            + jnp.dot(ys_ref[sl, :], wo_ref[dp:, :], preferred_element_type=F32)
        h = _layer_norm(alpha * x_ref[sl, :] + mix, g_ref[...], b_ref[...])
        h_ref[sl, :] = h
        _store_row_packed(hp_ref, h, row0=sb * sub)
        hh = h.astype(BF16)
        hl = (h - hh.astype(F32)).astype(BF16)
        part = jnp.dot(hh, rw_ref[...], preferred_element_type=F32) \
            + jnp.dot(hl, rw_ref[...], preferred_element_type=F32)
        logits = part + pltpu.roll(part, ROUTER_LANES // 2, 1) + rb_ref[...]
        lt_ref[:, sb * sub:(sb + 1) * sub] = logits.T[0:ROUTER_ROWS, :]


def _outproj(y_pool, y_ssm, xt, w_out, ln_g, ln_b, rw, rb, alpha):
    t, d = xt.shape
    dp = y_pool.shape[1]
    tm = min(512, t)
    pr = d // 2 // LANES
    return pl.pallas_call(
        functools.partial(_outproj_kernel, alpha=alpha, dp=dp, sub=256),
        out_shape=(jax.ShapeDtypeStruct((t, d), F32),
                   jax.ShapeDtypeStruct((t * pr, LANES), U32),
                   jax.ShapeDtypeStruct((ROUTER_ROWS, t), F32)),
        grid=(t // tm,),
        in_specs=[pl.BlockSpec((tm, dp), lambda i: (i, 0)),
                  pl.BlockSpec((tm, y_ssm.shape[1]), lambda i: (i, 0)),
                  pl.BlockSpec((tm, d), lambda i: (i, 0)),
                  pl.BlockSpec(w_out.shape, lambda i: (0, 0)),
                  pl.BlockSpec((1, d), lambda i: (0, 0)),
                  pl.BlockSpec((1, d), lambda i: (0, 0)),
                  pl.BlockSpec(rw.shape, lambda i: (0, 0)),
                  pl.BlockSpec((1, ROUTER_LANES), lambda i: (0, 0))],
        out_specs=(pl.BlockSpec((tm, d), lambda i: (i, 0)),
                   pl.BlockSpec((tm * pr, LANES), lambda i: (i, 0)),
                   pl.BlockSpec((ROUTER_ROWS, tm), lambda i: (0, i))),
        compiler_params=_cparams(("arbitrary",)),
        name="outproj",
    )(y_pool, y_ssm, xt, w_out, ln_g, ln_b, rw, rb)


def _router_kernel(lt_ref, eid_ref, rank_ref, gate_ref, cnt_ref, carry_ref):
    i = pl.program_id(0)
    ne, epg = N_EXPERTS, EXPERTS_PER_GROUP

    @pl.when(i == 0)
    def _():
        carry_ref[...] = jnp.zeros_like(carry_ref)

    lt = lt_ref[...]
    tt = lt.shape[1]
    gl = [lt[j:j + 1, :] for j in range(N_EXPERT_GROUPS)]
    gmax = jnp.maximum(jnp.maximum(gl[0], gl[1]), jnp.maximum(gl[2], gl[3]))
    grp = jnp.where(gl[0] == gmax, 0, jnp.where(gl[1] == gmax, 1, jnp.where(gl[2] == gmax, 2, 3)))
    p_grp = 1.0 / (jnp.exp(gl[0] - gmax) + jnp.exp(gl[1] - gmax)
                   + jnp.exp(gl[2] - gmax) + jnp.exp(gl[3] - gmax))
    eg = [lt[8 + epg * j: 8 + epg * (j + 1), :] for j in range(N_EXPERT_GROUPS)]
    el = jnp.where(grp == 0, eg[0], jnp.where(grp == 1, eg[1], jnp.where(grp == 2, eg[2], eg[3])))
    sub = lax.broadcasted_iota(I32, (epg, tt), 0)
    m1 = jnp.max(el, axis=0, keepdims=True)
    i1 = jnp.min(jnp.where(el == m1, sub, epg), axis=0, keepdims=True)
    rest = jnp.where(sub == i1, -jnp.inf, el)
    m2 = jnp.max(rest, axis=0, keepdims=True)
    i2 = jnp.min(jnp.where(rest == m2, sub, epg), axis=0, keepdims=True)
    r21 = jnp.exp(m2 - m1)
    g1 = p_grp / (1.0 + r21)
    g2 = g1 * r21
    e1 = grp * epg + i1
    e2 = grp * epg + i2

    rows = lax.broadcasted_iota(I32, (ne, tt), 0)
    oh1 = rows == e1
    oh2 = rows == e2
    oh = jnp.where(oh1 | oh2, 1.0, 0.0)
    tri = jnp.where(lax.broadcasted_iota(I32, (tt, tt), 0) < lax.broadcasted_iota(I32, (tt, tt), 1),
                    1.0, 0.0).astype(BF16)
    before = jnp.dot(oh.astype(BF16), tri, preferred_element_type=F32) + carry_ref[:, 0:1]
    r1 = jnp.sum(jnp.where(oh1, before, 0.0), axis=0, keepdims=True)
    r2 = jnp.sum(jnp.where(oh2, before, 0.0), axis=0, keepdims=True)
    carry_ref[...] = carry_ref[...] + jnp.sum(oh, axis=1, keepdims=True)

    eid_ref[...] = jnp.concatenate([e1, e2], axis=0)
    rank_ref[...] = jnp.concatenate([r1, r2], axis=0).astype(I32)
    gate_ref[...] = jnp.concatenate([g1, g2], axis=0)
    cnt_ref[...] = carry_ref[...].astype(I32)


def _router(lt):
    rr, t = lt.shape
    tt = min(512, t)
    return pl.pallas_call(
        _router_kernel,
        out_shape=(jax.ShapeDtypeStruct((2, t), I32),
                   jax.ShapeDtypeStruct((2, t), I32),
                   jax.ShapeDtypeStruct((2, t), F32),
                   jax.ShapeDtypeStruct((N_EXPERTS, 128), I32)),
        grid=(t // tt,),
        in_specs=[pl.BlockSpec((rr, tt), lambda i: (0, i))],
        out_specs=(pl.BlockSpec((2, tt), lambda i: (0, i)),
                   pl.BlockSpec((2, tt), lambda i: (0, i)),
                   pl.BlockSpec((2, tt), lambda i: (0, i)),
                   pl.BlockSpec((N_EXPERTS, 128), lambda i: (0, 0))),
        scratch_shapes=[pltpu.VMEM((N_EXPERTS, 128), F32)],
        compiler_params=_cparams(("arbitrary",)),
        name="router",
    )(lt)


def _invert_kernel(dest_ref, zeros_hbm, out_ref):
    pltpu.sync_copy(zeros_hbm, out_ref)
    t = dest_ref.shape[0] // 2

    def place(i, carry):
        out_ref[dest_ref[i]] = i
        out_ref[dest_ref[t + i]] = i
        return carry
    lax.fori_loop(0, t, place, 0, unroll=8)


def _invert(dest_flat, n_slots):
    return pl.pallas_call(
        _invert_kernel,
        out_shape=jax.ShapeDtypeStruct((n_slots,), I32),
        grid_spec=pltpu.PrefetchScalarGridSpec(
            num_scalar_prefetch=1, grid=(1,),
            in_specs=[pl.BlockSpec(memory_space=pl.ANY)],
            out_specs=pl.BlockSpec(memory_space=pltpu.SMEM)),
        compiler_params=_cparams(("arbitrary",)),
        name="invert",
    )(dest_flat, jnp.zeros((n_slots,), I32))


def _moe_kernel(be_ref, nu_ref, pc_ref, st_ref, hp_hbm, wg_hbm, wu_hbm, wd_hbm, y_ref,
                gbuf, wgf, wuf, wdf, wgb, wub, wdb, par_ref, gsem, wsem):
    b = pl.program_id(0)
    nu = nu_ref[0]
    pr = gbuf.shape[1] // MOE_BLK
    blk = MOE_BLK
    nbuf = gbuf.shape[0]
    look = nbuf - 1

    def row_copy(tok, slot, r):
        src = hp_hbm.at[pl.ds(pl.multiple_of(tok * pr, pr), pr)]
        off = r * pr if isinstance(r, int) else pl.multiple_of(r * pr, pr)
        return pltpu.make_async_copy(src, gbuf.at[slot, pl.ds(off, pr)], gsem.at[slot])

    def issue(block, slot):
        base = block * blk

        def body(r, carry):
            row_copy(st_ref[base + r], slot, r).start()
            return carry
        lax.fori_loop(0, blk, body, 0, unroll=8)

    def weight_copies(e, p):
        return (pltpu.make_async_copy(wg_hbm.at[e], wgf.at[p], wsem.at[p]),
                pltpu.make_async_copy(wu_hbm.at[e], wuf.at[p], wsem.at[p]),
                pltpu.make_async_copy(wd_hbm.at[e], wdf.at[p], wsem.at[p]))

    @pl.when(b == 0)
    def _():
        par_ref[0] = 0
        for cp in weight_copies(be_ref[0], 0):
            cp.start(priority=1)
        issue(0, 0)
        for j in range(1, look):
            @pl.when(j < nu)
            def _():
                issue(j, j)

    def issue_part(k):
        base = (b + look) * blk
        for r in range(k * blk // 4, (k + 1) * blk // 4):
            row_copy(st_ref[base + r], (b + look) % nbuf, r).start(priority=1 if r % 3 == 2 else 0)

    def expert_mlp(slot, with_issue):
        los, his = _load_row_packed(gbuf.at[slot], blk, pr)
        x = jnp.concatenate([v.astype(BF16) for v in los + his], axis=1)
        if with_issue:
            issue_part(0)
        hg = jnp.dot(x, wgb[...], preferred_element_type=F32)
        if with_issue:
            issue_part(1)
        hu = jnp.dot(x, wub[...], preferred_element_type=F32)
        if with_issue:
            issue_part(2)
        hh = hg * (1.0 / (1.0 + jnp.exp(-hg))) * hu
        y = jnp.dot(hh.astype(BF16), wdb[...], preferred_element_type=F32)
        if with_issue:
            issue_part(3)
        _store_row_packed(y_ref, y)

    @pl.when(b < nu)
    def _():
        slot = b % nbuf
        e = be_ref[b]
        first = (b == 0) | (e != be_ref[jnp.maximum(b - 1, 0)])

        @pl.when(first)
        def _():
            p = par_ref[0]
            for cp in weight_copies(e, p):
                cp.wait()
            nxt = lax.while_loop(
                lambda c: (c < N_EXPERTS) & (pc_ref[jnp.minimum(c, N_EXPERTS - 1)] == 0),
                lambda c: c + 1, e + 1)

            @pl.when(nxt < N_EXPERTS)
            def _():
                for cp in weight_copies(nxt, 1 - p):
                    cp.start(priority=1)
            wgb[...] = wgf[p].astype(BF16)
            wub[...] = wuf[p].astype(BF16)
            wdb[...] = wdf[p].astype(BF16)
            par_ref[0] = 1 - p

        pltpu.make_async_copy(hp_hbm.at[pl.ds(0, blk * pr)], gbuf.at[slot], gsem.at[slot]).wait()

        @pl.when(b + look < nu)
        def _():
            expert_mlp(slot, True)

        @pl.when(b + look >= nu)
        def _():
            expert_mlp(slot, False)

    @pl.when(b >= nu)
    def _():
        y_ref[...] = jnp.zeros_like(y_ref)


def _moe(block_e, n_used, pcounts, slot_tok, hp, w_gate, w_up, w_down, n_blocks):
    ne, d, de = w_gate.shape
    blk = MOE_BLK
    pr = d // 2 // LANES
    grid_spec = pltpu.PrefetchScalarGridSpec(
        num_scalar_prefetch=4,
        grid=(n_blocks,),
        in_specs=[pl.BlockSpec(memory_space=pl.ANY)] * 4,
        out_specs=pl.BlockSpec((blk * pr, LANES), lambda b, *_: (b, 0)),
        scratch_shapes=[pltpu.VMEM((GATHER_BUFS, blk * pr, LANES), U32),
                        pltpu.VMEM((2, d, de), F32),
                        pltpu.VMEM((2, d, de), F32),
                        pltpu.VMEM((2, de, d), F32),
                        pltpu.VMEM((d, de), BF16),
                        pltpu.VMEM((d, de), BF16),
                        pltpu.VMEM((de, d), BF16),
                        pltpu.SMEM((1,), I32),
                        pltpu.SemaphoreType.DMA((GATHER_BUFS,)),
                        pltpu.SemaphoreType.DMA((2,))],
    )
    return pl.pallas_call(
        _moe_kernel,
        out_shape=jax.ShapeDtypeStruct((n_blocks * blk * pr, LANES), U32),
        grid_spec=grid_spec,
        compiler_params=_cparams(("arbitrary",)),
        name="moe",
    )(block_e, n_used, pcounts, slot_tok, hp, w_gate, w_up, w_down)


def _combine_kernel(d0_ref, d1_ref, h_ref, gate_ref, yb_hbm, g_ref, b_ref, out_ref, buf, sem, *, alpha):
    i = pl.program_id(0)
    n = pl.num_programs(0)
    tm = h_ref.shape[0]
    pr = buf.shape[2] // tm

    def issue(tile, slot):
        base = tile * tm

        def body(r, carry):
            dst = pl.ds(pl.multiple_of(r * pr, pr), pr)
            for k, dref in enumerate((d0_ref, d1_ref)):
                src = yb_hbm.at[pl.ds(pl.multiple_of(dref[base + r] * pr, pr), pr)]
                pltpu.make_async_copy(src, buf.at[slot, k, dst], sem.at[slot]).start()
            return carry
        lax.fori_loop(0, tm, body, 0, unroll=8)

    nbuf = buf.shape[0]
    look = nbuf - 1

    @pl.when(i == 0)
    def _():
        issue(0, 0)
        for j in range(1, look):
            @pl.when(j < n)
            def _():
                issue(j, j)

    slot = i % nbuf
    for k in range(2):
        pltpu.make_async_copy(yb_hbm.at[pl.ds(0, tm * pr)], buf.at[slot, k], sem.at[slot]).wait()

    sub = 64
    nq = tm // sub

    def issue_part(q):
        base = (i + look) * tm
        nslot = (i + look) % nbuf
        for r in range(q * sub, (q + 1) * sub):
            for k, dref in enumerate((d0_ref, d1_ref)):
                src = yb_hbm.at[pl.ds(pl.multiple_of(dref[base + r] * pr, pr), pr)]
                pltpu.make_async_copy(src, buf.at[nslot, k, pl.ds(r * pr, pr)], sem.at[nslot]).start(priority=k)

    def finish(with_issue):
        for q in range(nq):
            if with_issue:
                issue_part(q)
            rows = pl.ds(q * sub, sub)
            lo0, hi0 = _load_row_packed(buf.at[slot, 0], sub, pr, row0=q * sub)
            lo1, hi1 = _load_row_packed(buf.at[slot, 1], sub, pr, row0=q * sub)
            g0 = gate_ref[rows, 0:1]
            g1 = gate_ref[rows, 1:2]
            ffn = jnp.concatenate([g0 * a + g1 * c for a, c in zip(lo0 + hi0, lo1 + hi1)], axis=1)
            out_ref[rows, :] = _layer_norm(alpha * h_ref[rows, :] + ffn, g_ref[...], b_ref[...])

    @pl.when(i + look < n)
    def _():
        finish(True)

    @pl.when(i + look >= n)
    def _():
        finish(False)


def _combine(dest0, dest1, h1, gate_t, yb, ln_g, ln_b, alpha):
    t, d = h1.shape
    tm = min(512, t)
    grid_spec = pltpu.PrefetchScalarGridSpec(
        num_scalar_prefetch=2,
        grid=(t // tm,),
        in_specs=[pl.BlockSpec((tm, d), lambda i, d0, d1: (i, 0)),
                  pl.BlockSpec((tm, 2), lambda i, d0, d1: (i, 0)),
                  pl.BlockSpec(memory_space=pl.ANY),
                  pl.BlockSpec((1, d), lambda i, d0, d1: (0, 0)),
                  pl.BlockSpec((1, d), lambda i, d0, d1: (0, 0))],
        out_specs=pl.BlockSpec((tm, d), lambda i, d0, d1: (i, 0)),
        scratch_shapes=[pltpu.VMEM((GATHER_BUFS, 2, tm * (d // 2 // LANES), LANES), U32),
                        pltpu.SemaphoreType.DMA((GATHER_BUFS,))],
    )
    return pl.pallas_call(
        functools.partial(_combine_kernel, alpha=alpha),
        out_shape=jax.ShapeDtypeStruct((t, d), F32),
        grid_spec=grid_spec,
        compiler_params=_cparams(("arbitrary",)),
        name="combine",
    )(dest0, dest1, h1, gate_t, yb, ln_g, ln_b)


def _layer(h, w_in, pool_w, pool_scale, a_re, a_im, log_dt, b_re, b_im, c_re, c_im, d_skip,
           glu_w, glu_b, w_out, ln1_g, ln1_b, rg_w, rg_b, re_w, re_b, w_gate, w_up, w_down,
           ln2_g, ln2_b, alpha):
    bsz, seq, d = h.shape
    t = bsz * seq
    L = CHUNK
    dp = pool_w.shape[0] * pool_w.shape[1]
    ds = w_in.shape[1] - dp
    g = ds // SSM_GROUP
    n = t // L

    lag0, bb, w1t, w2t, al = _ssm_tables(a_re, a_im, log_dt, b_re, b_im, c_re, c_im, d_skip, L)
    half = ROUTER_LANES // 2
    zpad = lambda k: jnp.zeros((d, k), F32)
    rw = jnp.concatenate([rg_w.astype(F32), zpad(8 - N_EXPERT_GROUPS), re_w.astype(F32),
                          zpad(half - ROUTER_ROWS)], axis=1)
    rw_hi = rw.astype(BF16)
    rw_lo = (rw - rw_hi.astype(F32)).astype(BF16)
    rw2 = jnp.concatenate([rw_hi, rw_lo], axis=1)
    rb = jnp.concatenate([rg_b.astype(F32), jnp.zeros((8 - N_EXPERT_GROUPS,), F32), re_b.astype(F32),
                          jnp.zeros((ROUTER_LANES - ROUTER_ROWS,), F32)]).reshape(1, ROUTER_LANES)

    xt = h.reshape(t, d)
    pool_p, ut = _proj(xt, w_in.astype(BF16), dp, L)
    y_pool = _pool(pool_p, pool_w.astype(BF16), pool_scale.reshape(1, dp).astype(F32), seq)
    yt = _ssm(ut.reshape(g, L * SSM_GROUP, n), lag0, bb, w1t, w2t, al, seq // L, L)
    y_ssm = _glu(yt.reshape(g, L, SSM_GROUP, n), glu_w.astype(BF16), glu_b.reshape(1, ds).astype(F32))
    h1, hp, lt = _outproj(y_pool, y_ssm, xt, w_out.astype(BF16), ln1_g.reshape(1, d), ln1_b.reshape(1, d),
                          rw2, rb, alpha)
    eid, rank, gate, cnt = _router(lt)

    blk = MOE_BLK
    m = 2 * t
    n_blocks = -(-m // blk) + N_EXPERTS
    counts = cnt[:, 0]
    pcounts = (counts + blk - 1) // blk * blk
    pends = jnp.cumsum(pcounts)
    pstarts = pends - pcounts
    e_ids = jnp.arange(N_EXPERTS, dtype=I32)
    dest = jnp.sum(jnp.where(eid[..., None] == e_ids, pstarts, 0), axis=-1) + rank
    slot_tok = _invert(dest.reshape(-1), n_blocks * blk)
    n_used = (pends[-1] // blk).astype(I32)
    bidx = jnp.minimum(jnp.arange(n_blocks, dtype=I32), n_used - 1)
    block_e = jnp.minimum(jnp.sum((pends[None, :] <= (bidx * blk)[:, None]).astype(I32), axis=1),
                          N_EXPERTS - 1)

    yb = _moe(block_e, n_used.reshape(1), pcounts, slot_tok, hp, w_gate, w_up, w_down, n_blocks)
    out = _combine(dest[0], dest[1], h1, gate.T, yb, ln2_g.reshape(1, d), ln2_b.reshape(1, d), alpha)
    return out.reshape(bsz, seq, d)


def kernel(x, w_in, pool_w, pool_scale, ssm_a_re, ssm_a_im, ssm_log_dt, ssm_b_re, ssm_b_im, ssm_c_re, ssm_c_im, ssm_d, glu_w, glu_b, w_out, ln1_g, ln1_b, router_g_w, router_g_b, router_e_w, router_e_b, w_gate, w_up, w_down, ln2_g, ln2_b):
    depth = w_in.shape[0]
    alpha = (2.0 * depth) ** 0.25
    h = x
    for l in range(depth):
        h = _layer(h, w_in[l], pool_w[l], pool_scale[l], ssm_a_re[l], ssm_a_im[l], ssm_log_dt[l],
                   ssm_b_re[l], ssm_b_im[l], ssm_c_re[l], ssm_c_im[l], ssm_d[l], glu_w[l], glu_b[l],
                   w_out[l], ln1_g[l], ln1_b[l], router_g_w[l], router_g_b[l], router_e_w[l],
                   router_e_b[l], w_gate[l], w_up[l], w_down[l], ln2_g[l], ln2_b[l], alpha)
    return h
```

```python
import functools
import math

import numpy as np
import jax
import jax.numpy as jnp
from jax import lax
from jax.experimental import pallas as pl
from jax.experimental.pallas import tpu as pltpu

F32 = jnp.float32
BF16 = jnp.bfloat16
I32 = jnp.int32
U32 = jnp.uint32

POOL_WINDOWS = (2, 4, 8, 16)
POOL_GROUP = 256
SSM_GROUP = 16
SSM_STATE = 64
N_EXPERT_GROUPS = 4
EXPERTS_PER_GROUP = 8
N_EXPERTS = N_EXPERT_GROUPS * EXPERTS_PER_GROUP
LN_EPS = 1e-5

CHUNK = 16
MOE_BLK = 256
MOE_BLOCKS_PER_STEP = 2
GATHER_BUFS = 4
ROUTER_ROWS = 8 + N_EXPERTS
ROUTER_LANES = 128
LANES = 128
MXU_N = 256
HALO = 16
VMEM_LIMIT = 56 * 1024 * 1024


def _cparams(sem, vmem=VMEM_LIMIT):
    return pltpu.CompilerParams(dimension_semantics=sem, vmem_limit_bytes=vmem)


def _proj_kernel(x_ref, w_ref, pool_ref, ut_ref, accp_ref, accs_ref):
    kk = pl.program_id(1)
    rows, dp = accp_ref.shape
    nj = accs_ref.shape[0]

    xb = x_ref[...].astype(BF16)
    per = MXU_N // LANES

    def accumulate(first):
        for n in range(dp // MXU_N):
            cs = slice(n * MXU_N, (n + 1) * MXU_N)
            part = jnp.dot(xb, w_ref[:, cs], preferred_element_type=F32)
            accp_ref[:, cs] = part if first else accp_ref[:, cs] + part
        for n in range(nj // per):
            part = jnp.dot(xb, w_ref[:, dp + n * MXU_N: dp + (n + 1) * MXU_N], preferred_element_type=F32)
            for q in range(per):
                piece = part[:, q * LANES:(q + 1) * LANES]
                accs_ref[n * per + q] = piece if first else accs_ref[n * per + q] + piece

    @pl.when(kk == 0)
    def _():
        accumulate(True)

    @pl.when(kk > 0)
    def _():
        accumulate(False)

    @pl.when(kk == pl.num_programs(1) - 1)
    def _():
        pool_ref[...] = accp_ref[...].astype(BF16)
        g, L, c, r = ut_ref.shape
        gj = LANES // c
        for s in range(L):
            for j in range(nj):
                piece = accs_ref[j, pl.ds(s, r, stride=L), :]
                ut_ref[j * gj:(j + 1) * gj, s, :, :] = piece.astype(BF16).T.reshape(gj, c, r)


def _proj(xt, w_in, dp, L):
    t, d = xt.shape
    dm = w_in.shape[1]
    g = (dm - dp) // SSM_GROUP
    tm = LANES * L
    kb = 512
    return pl.pallas_call(
        _proj_kernel,
        out_shape=(jax.ShapeDtypeStruct((t, dp), BF16),
                   jax.ShapeDtypeStruct((g, L, SSM_GROUP, t // L), BF16)),
        grid=(t // tm, d // kb),
        in_specs=[pl.BlockSpec((tm, kb), lambda i, k: (i, k)),
                  pl.BlockSpec((kb, dm), lambda i, k: (k, 0))],
        out_specs=(pl.BlockSpec((tm, dp), lambda i, k: (i, 0)),
                   pl.BlockSpec((g, L, SSM_GROUP, LANES), lambda i, k: (0, 0, 0, i))),
        scratch_shapes=[pltpu.VMEM((tm, dp), F32),
                        pltpu.VMEM(((dm - dp) // LANES, tm, LANES), F32)],
        compiler_params=_cparams(("arbitrary", "arbitrary")),
        name="proj",
    )(xt, w_in)


def _pool_kernel(prev_ref, main_ref, next_ref, pw_ref, sc_ref, out_ref, *, seq, sub):
    i = pl.program_id(0)
    ts = main_ref.shape[0]
    ext = jnp.concatenate([prev_ref[...], main_ref[...], next_ref[...]], axis=0)
    k = sub + 2 * HALO
    row = lax.broadcasted_iota(I32, (sub, k), 0)
    col = lax.broadcasted_iota(I32, (sub, k), 1)
    off0 = col - HALO - row
    for sb in range(ts // sub):
        rows = slice(sb * sub, (sb + 1) * sub)
        base = (i * ts + sb * sub) % seq
        src = base + row + off0
        off = jnp.where((src >= 0) & (src < seq), off0, 2 * HALO)
        pos = base + lax.broadcasted_iota(I32, (sub, 1), 0)
        for g, w in enumerate(POOL_WINDOWS):
            sl = slice(g * POOL_GROUP, (g + 1) * POOL_GROUP)
            band = (off + w // 2).astype(U32) < w
            bm = jnp.where(band, 1.0, 0.0).astype(BF16)
            sums = jnp.dot(bm, ext[sb * sub:sb * sub + k, sl], preferred_element_type=F32)
            lo = jnp.maximum(pos - w // 2, 0)
            hi = jnp.minimum(pos - w // 2 + w, seq)
            inv = 1.0 / (hi - lo).astype(F32)
            dlt = sums * inv - main_ref[rows, sl].astype(F32)
            y = jnp.dot(dlt.astype(BF16), pw_ref[g], preferred_element_type=F32) * sc_ref[:, sl]
            out_ref[rows, sl] = y.astype(BF16)


def _pool(pp, pool_w, pool_scale, seq):
    t, dp = pp.shape
    ts = min(512, seq)
    nh = t // HALO
    per = ts // HALO
    return pl.pallas_call(
        functools.partial(_pool_kernel, seq=seq, sub=256),
        out_shape=jax.ShapeDtypeStruct((t, dp), BF16),
        grid=(t // ts,),
        in_specs=[pl.BlockSpec((HALO, dp), lambda i: (jnp.maximum(i * per - 1, 0), 0)),
                  pl.BlockSpec((ts, dp), lambda i: (i, 0)),
                  pl.BlockSpec((HALO, dp), lambda i: (jnp.minimum((i + 1) * per, nh - 1), 0)),
                  pl.BlockSpec(pool_w.shape, lambda i: (0, 0, 0)),
                  pl.BlockSpec((1, dp), lambda i: (0, 0))],
        out_specs=pl.BlockSpec((ts, dp), lambda i: (i, 0)),
        compiler_params=_cparams(("arbitrary",)),
        name="pool",
    )(pp, pp, pp, pool_w, pool_scale)


def _ssm_tables(a_re, a_im, log_dt, b_re, b_im, c_re, c_im, d_skip, L):
    g = a_re.shape[1]
    p = a_re.shape[2]
    c = b_re.shape[3]
    fl = L * c
    lr = a_re.astype(F32)
    li = a_im.astype(F32)
    dt = jnp.exp(log_dt.astype(F32))[..., None]
    mag = jnp.exp(lr * dt)
    abr = mag * jnp.cos(li * dt)
    abi = mag * jnp.sin(li * dt)
    den = lr * lr + li * li
    zr = ((abr - 1.0) * lr + abi * li) / den
    zi = (abi * lr - (abr - 1.0) * li) / den
    br = b_re.astype(F32)
    bi = b_im.astype(F32)
    bbr = zr[..., None] * br - zi[..., None] * bi
    bbi = zr[..., None] * bi + zi[..., None] * br
    cr = c_re.astype(F32)
    ci = c_im.astype(F32)
    kk = jnp.arange(L + 1, dtype=F32)[None, None, :, None]
    pm = jnp.exp(kk * (lr * dt)[:, :, None, :])
    ang = kk * (li * dt)[:, :, None, :]
    pr = pm * jnp.cos(ang)
    pi = pm * jnp.sin(ang)

    skip = jnp.eye(c, dtype=F32)[None] * d_skip.astype(F32).reshape(g, c, 1)
    lag0 = jnp.einsum('dgop,dgpc->goc', cr, bbr, precision=lax.Precision.HIGHEST) \
        - jnp.einsum('dgop,dgpc->goc', ci, bbi, precision=lax.Precision.HIGHEST) + skip
    lag0 = jnp.pad(lag0, ((0, 0), (0, 0), (0, LANES - c)))
    bb = jnp.concatenate([bbr[0], bbi[0], bbr[1], bbi[1]], axis=1)
    bb = jnp.pad(bb, ((0, 0), (0, 0), (0, LANES - c)))

    def w1(d, qr, qi):
        qr = qr.transpose(0, 2, 1)[..., None]
        qi = qi.transpose(0, 2, 1)[..., None]
        re = qr * bbr[d][:, :, None] - qi * bbi[d][:, :, None]
        im = qr * bbi[d][:, :, None] + qi * bbr[d][:, :, None]
        to = lambda v: v.reshape(g, p, fl)
        return to(re), to(im)
    f_re, f_im = w1(0, pr[0][:, L - 1::-1], pi[0][:, L - 1::-1])
    b_re_, b_im_ = w1(1, pr[1][:, :L], pi[1][:, :L])
    w1t = jnp.concatenate([f_re, f_im, b_re_, b_im_], axis=1)

    def w2(d, qr, qi):
        re = cr[d][:, None] * qr[:, :, None] - ci[d][:, None] * qi[:, :, None]
        im = -(cr[d][:, None] * qi[:, :, None] + ci[d][:, None] * qr[:, :, None])
        to = lambda v: v.reshape(g, fl, p)
        return to(re), to(im)
    rf_re, rf_im = w2(0, pr[0][:, 1:], pi[0][:, 1:])
    rb_re, rb_im = w2(1, pr[1][:, :0:-1], pi[1][:, :0:-1])
    w2t = jnp.concatenate([rf_re, rf_im, rb_re, rb_im], axis=2)

    a4 = jnp.stack([pr[0][:, L], pi[0][:, L], pr[1][:, L], pi[1][:, L]], axis=1)
    ap = a4.reshape(g // 2, 2, 4, p).transpose(0, 2, 1, 3).reshape(g // 2, 4, 2 * p)
    ap = jnp.pad(ap, ((0, 0), (0, 4), (0, 0)))
    return lag0, bb.astype(BF16), w1t.astype(BF16), w2t.astype(BF16), ap


def _cmul(ar, ai, br, bi):
    return ar * br - ai * bi, ar * bi + ai * br


def _ssm_kernel(ut_ref, w1_ref, lag0_ref, bb_ref, w2_ref, a_ref, yt_ref, mt_s, kk_s, zt_s, ht_s, *, nseg, L):
    p = SSM_STATE
    c = SSM_GROUP
    fl, n = ut_ref.shape[1:]
    nb = n // nseg
    nrow = nseg // 8
    st = [jnp.dot(w1_ref[q], ut_ref[q], preferred_element_type=F32) for q in range(2)]
    z = jnp.concatenate([st[q][k * p:(k + 1) * p] for k in range(4) for q in range(2)], axis=0)
    zt_s[...] = z.T
    per = LANES // c
    for q in range(2):
        kf = jnp.dot(w2_ref[q, :, 0:2 * p], bb_ref[q, 0:2 * p, :], preferred_element_type=F32)
        kb = jnp.dot(w2_ref[q, :, 2 * p:4 * p], bb_ref[q, 2 * p:4 * p, :], preferred_element_type=F32)
        tiles = [None] * (2 * L // per)
        for j in range(2 * L - 1):
            lag = L - 1 - j
            if lag > 0:
                blk = kf[(lag - 1) * c:lag * c, :]
            elif lag == 0:
                blk = lag0_ref[q]
            else:
                blk = kb[(L + lag) * c:(L + lag + 1) * c, :]
            sh = (j % per) * c
            blk = blk if sh == 0 else pltpu.roll(blk, sh, 1)
            tiles[j // per] = blk if tiles[j // per] is None else tiles[j // per] + blk
        kk_s[...] = jnp.concatenate(tiles, axis=1).astype(BF16)
        for t in range(L):
            mt_s[q, t * c:(t + 1) * c, :] = kk_s[:, (L - 1 - t) * c:(L - 1 - t) * c + fl]

    sub = lax.broadcasted_iota(I32, (8, LANES), 0)
    full = lambda v: jnp.broadcast_to(v, (8, LANES))
    one, zero = jnp.ones((8, LANES), F32), jnp.zeros((8, LANES), F32)

    def powers(row):
        out = [(full(a_ref[0, row:row + 1, :]), full(a_ref[0, row + 1:row + 2, :]))]
        for _ in range(3):
            out.append(_cmul(*out[-1], *out[-1]))
        return out

    def by_bits(pw, idx):
        qr, qi = one, zero
        for bit in range(3):
            on = (idx >> bit) & 1 == 1
            qr, qi = _cmul(qr, qi, jnp.where(on, pw[bit][0], one), jnp.where(on, pw[bit][1], zero))
        return qr, qi

    pw_f, pw_b = powers(0), powers(2)
    qf = by_bits(pw_f, sub)
    qb = by_bits(pw_b, 7 - sub)

    def local(xr, xi, pw, forward):
        for lvl, d in enumerate((1, 2, 4)):
            keep = (sub >= d) if forward else (sub < 8 - d)
            sh = d if forward else 8 - d
            tr = jnp.where(keep, pltpu.roll(xr, sh, 0), 0.0)
            ti = jnp.where(keep, pltpu.roll(xi, sh, 0), 0.0)
            mr, mi = _cmul(pw[lvl][0], pw[lvl][1], tr, ti)
            xr, xi = xr + mr, xi + mi
        return xr, xi

    def step(i, carry):
        new = []
        for b in range(nb):
            for fwd in (True, False):
                cr, ci = carry[2 * (2 * b + (0 if fwd else 1)):][:2]
                row = b * nseg + (i if fwd else nrow - 1 - i) * 8
                rows = pl.ds(pl.multiple_of(row, 8), 8)
                l0 = 0 if fwd else 2 * LANES
                pw, (qr, qi) = (pw_f, qf) if fwd else (pw_b, qb)
                lr, li = local(zt_s[rows, l0:l0 + LANES], zt_s[rows, l0 + LANES:l0 + 2 * LANES], pw, fwd)
                keep = (sub >= 1) if fwd else (sub < 7)
                er = jnp.where(keep, pltpu.roll(lr, 1 if fwd else 7, 0), 0.0)
                ei = jnp.where(keep, pltpu.roll(li, 1 if fwd else 7, 0), 0.0)
                mr, mi = _cmul(qr, qi, cr, ci)
                ht_s[rows, l0:l0 + LANES] = er + mr
                ht_s[rows, l0 + LANES:l0 + 2 * LANES] = ei + mi
                edge = 7 if fwd else 0
                mr, mi = _cmul(pw[3][0], pw[3][1], cr, ci)
                new += [full(lr[edge:edge + 1, :]) + mr, full(li[edge:edge + 1, :]) + mi]
        return tuple(new)

    lax.fori_loop(0, nrow, step, (zero,) * (4 * nb))

    h = ht_s[...].astype(BF16).T
    for q in range(2):
        hq = jnp.concatenate([h[(2 * k + q) * p:(2 * k + q + 1) * p] for k in range(4)], axis=0)
        y = jnp.dot(mt_s[q], ut_ref[q], preferred_element_type=F32) \
            + jnp.dot(w2_ref[q], hq, preferred_element_type=F32)
        yt_ref[q] = y.astype(BF16)


def _ssm(ut3, lag0, bb, w1t, w2t, ap, nseg, L):
    g, fl, n = ut3.shape
    sp = w1t.shape[1]
    return pl.pallas_call(
        functools.partial(_ssm_kernel, nseg=nseg, L=L),
        out_shape=jax.ShapeDtypeStruct((g, fl, n), BF16),
        grid=(g // 2,),
        in_specs=[pl.BlockSpec((2, fl, n), lambda i: (i, 0, 0)),
                  pl.BlockSpec((2, sp, fl), lambda i: (i, 0, 0)),
                  pl.BlockSpec((2,) + lag0.shape[1:], lambda i: (i, 0, 0)),
                  pl.BlockSpec((2,) + bb.shape[1:], lambda i: (i, 0, 0)),
                  pl.BlockSpec((2, fl, sp), lambda i: (i, 0, 0)),
                  pl.BlockSpec((1,) + ap.shape[1:], lambda i: (i, 0, 0))],
        out_specs=pl.BlockSpec((2, fl, n), lambda i: (i, 0, 0)),
        scratch_shapes=[pltpu.VMEM((2, fl, fl), BF16),
                        pltpu.VMEM((SSM_GROUP, 2 * fl), BF16),
                        pltpu.VMEM((n, 2 * sp), F32),
                        pltpu.VMEM((n, 2 * sp), F32)],
        compiler_params=_cparams(("arbitrary",)),
        name="ssm",
    )(ut3, w1t, lag0, bb, w2t, ap)


def _glu_kernel(yt_ref, w_ref, b_ref, out_ref, nat_ref, *, sub):
    g, L, c, r = yt_ref.shape
    nj = nat_ref.shape[0]
    gj = LANES // c
    for t in range(L):
        for j in range(nj):
            piece = yt_ref[j * gj:(j + 1) * gj, t, :, :].reshape(LANES, r).astype(F32)
            nat_ref[j, pl.ds(t, r, stride=L), :] = piece.T

    def body(k, carry):
        sl = pl.ds(pl.multiple_of(k * sub, sub), sub)
        y = jnp.concatenate([nat_ref[j, sl, :] for j in range(nj)], axis=1)
        ya = jax.nn.gelu(y, approximate=True)
        z = jnp.dot(ya.astype(BF16), w_ref[...], preferred_element_type=F32) + b_ref[...]
        out_ref[sl, :] = (ya * (1.0 / (1.0 + jnp.exp(-z)))).astype(BF16)
        return carry
    lax.fori_loop(0, nat_ref.shape[1] // sub, body, 0)


def _glu(yt4, glu_w, glu_b):
    g, L, c, n = yt4.shape
    ds = g * c
    r = LANES
    return pl.pallas_call(
        functools.partial(_glu_kernel, sub=512),
        out_shape=jax.ShapeDtypeStruct((n * L, ds), BF16),
        grid=(n // r,),
        in_specs=[pl.BlockSpec((g, L, c, r), lambda j: (0, 0, 0, j)),
                  pl.BlockSpec((ds, ds), lambda j: (0, 0)),
                  pl.BlockSpec((1, ds), lambda j: (0, 0))],
        out_specs=pl.BlockSpec((r * L, ds), lambda j: (j, 0)),
        scratch_shapes=[pltpu.VMEM((ds // LANES, r * L, LANES), F32)],
        compiler_params=_cparams(("arbitrary",)),
        name="glu",
    )(yt4, glu_w, glu_b)


def _layer_norm(r, g, b):
    mu = jnp.mean(r, axis=-1, keepdims=True)
    cen = r - mu
    var = jnp.mean(cen * cen, axis=-1, keepdims=True)
    return cen * lax.rsqrt(var + LN_EPS) * g + b


def _store_row_packed(ref, m, row0=0):
    rows, width = m.shape
    half = width // 2
    nc = half // LANES
    lo = lax.bitcast_convert_type(m[:, :half].astype(BF16).astype(F32), U32) >> 16
    hi = lax.bitcast_convert_type(m[:, half:].astype(BF16).astype(F32), U32) & jnp.uint32(0xFFFF0000)
    pk = lo | hi
    for c in range(nc):
        ref[pl.ds(row0 * nc + c, rows, stride=nc), :] = pk[:, c * LANES:(c + 1) * LANES]


def _load_row_packed(ref, rows, nc, row0=0):
    los, his = [], []
    for c in range(nc):
        u = ref[pl.ds(row0 * nc + c, rows, stride=nc), :]
        los.append(lax.bitcast_convert_type(u << 16, F32))
        his.append(lax.bitcast_convert_type(u & jnp.uint32(0xFFFF0000), F32))
    return los, his


def _outproj_kernel(yp_ref, ys_ref, x_ref, wo_ref, g_ref, b_ref, rw_ref, rb_ref,
                    h_ref, hp_ref, lt_ref, *, alpha, dp, sub):
    for sb in range(h_ref.shape[0] // sub):
        sl = pl.ds(sb * sub, sub)
        mix = jnp.dot(yp_ref[sl, :], wo_ref[0:dp, :], preferred_element_type=F32) \
            + jnp.dot(ys_ref[sl, :], wo_ref[dp:, :], preferred_element_type=F32)
        h = _layer_norm(alpha * x_ref[sl, :] + mix, g_ref[...], b_ref[...])
        h_ref[sl, :] = h
        _store_row_packed(hp_ref, h, row0=sb * sub)
        hh = h.astype(BF16)
        hl = (h - hh.astype(F32)).astype(BF16)
        part = jnp.dot(hh, rw_ref[...], preferred_element_type=F32) \
            + jnp.dot(hl, rw_ref[...], preferred_element_type=F32)
        logits = part + pltpu.roll(part, ROUTER_LANES // 2, 1) + rb_ref[...]
        lt_ref[:, sb * sub:(sb + 1) * sub] = logits.T[0:ROUTER_ROWS, :]


def _outproj(y_pool, y_ssm, xt, w_out, ln_g, ln_b, rw, rb, alpha):
    t, d = xt.shape
    dp = y_pool.shape[1]
    tm = min(512, t)
    pr = d // 2 // LANES
    return pl.pallas_call(
        functools.partial(_outproj_kernel, alpha=alpha, dp=dp, sub=256),
        out_shape=(jax.ShapeDtypeStruct((t, d), F32),
                   jax.ShapeDtypeStruct((t * pr, LANES), U32),
                   jax.ShapeDtypeStruct((ROUTER_ROWS, t), F32)),
        grid=(t // tm,),
        in_specs=[pl.BlockSpec((tm, dp), lambda i: (i, 0)),
                  pl.BlockSpec((tm, y_ssm.shape[1]), lambda i: (i, 0)),
                  pl.BlockSpec((tm, d), lambda i: (i, 0)),
                  pl.BlockSpec(w_out.shape, lambda i: (0, 0)),
                  pl.BlockSpec((1, d), lambda i: (0, 0)),
                  pl.BlockSpec((1, d), lambda i: (0, 0)),
                  pl.BlockSpec(rw.shape, lambda i: (0, 0)),
                  pl.BlockSpec((1, ROUTER_LANES), lambda i: (0, 0))],
        out_specs=(pl.BlockSpec((tm, d), lambda i: (i, 0)),
                   pl.BlockSpec((tm * pr, LANES), lambda i: (i, 0)),
                   pl.BlockSpec((ROUTER_ROWS, tm), lambda i: (0, i))),
        compiler_params=_cparams(("arbitrary",)),
        name="outproj",
    )(y_pool, y_ssm, xt, w_out, ln_g, ln_b, rw, rb)


def _router_kernel(lt_ref, eid_ref, rank_ref, gate_ref, cnt_ref, carry_ref):
    i = pl.program_id(0)
    ne, epg = N_EXPERTS, EXPERTS_PER_GROUP

    @pl.when(i == 0)
    def _():
        carry_ref[...] = jnp.zeros_like(carry_ref)

    lt = lt_ref[...]
    tt = lt.shape[1]
    gl = [lt[j:j + 1, :] for j in range(N_EXPERT_GROUPS)]
    gmax = jnp.maximum(jnp.maximum(gl[0], gl[1]), jnp.maximum(gl[2], gl[3]))
    grp = jnp.where(gl[0] == gmax, 0, jnp.where(gl[1] == gmax, 1, jnp.where(gl[2] == gmax, 2, 3)))
    p_grp = 1.0 / (jnp.exp(gl[0] - gmax) + jnp.exp(gl[1] - gmax)
                   + jnp.exp(gl[2] - gmax) + jnp.exp(gl[3] - gmax))
    eg = [lt[8 + epg * j: 8 + epg * (j + 1), :] for j in range(N_EXPERT_GROUPS)]
    el = jnp.where(grp == 0, eg[0], jnp.where(grp == 1, eg[1], jnp.where(grp == 2, eg[2], eg[3])))
    sub = lax.broadcasted_iota(I32, (epg, tt), 0)
    m1 = jnp.max(el, axis=0, keepdims=True)
    i1 = jnp.min(jnp.where(el == m1, sub, epg), axis=0, keepdims=True)
    rest = jnp.where(sub == i1, -jnp.inf, el)
    m2 = jnp.max(rest, axis=0, keepdims=True)
    i2 = jnp.min(jnp.where(rest == m2, sub, epg), axis=0, keepdims=True)
    r21 = jnp.exp(m2 - m1)
    g1 = p_grp / (1.0 + r21)
    g2 = g1 * r21
    e1 = grp * epg + i1
    e2 = grp * epg + i2

    rows = lax.broadcasted_iota(I32, (ne, tt), 0)
    oh1 = rows == e1
    oh2 = rows == e2
    oh = jnp.where(oh1 | oh2, 1.0, 0.0)
    tri = jnp.where(lax.broadcasted_iota(I32, (tt, tt), 0) < lax.broadcasted_iota(I32, (tt, tt), 1),
                    1.0, 0.0).astype(BF16)
    before = jnp.dot(oh.astype(BF16), tri, preferred_element_type=F32) + carry_ref[:, 0:1]
    r1 = jnp.sum(jnp.where(oh1, before, 0.0), axis=0, keepdims=True)
    r2 = jnp.sum(jnp.where(oh2, before, 0.0), axis=0, keepdims=True)
    carry_ref[...] = carry_ref[...] + jnp.sum(oh, axis=1, keepdims=True)

    eid_ref[...] = jnp.concatenate([e1, e2], axis=0)
    rank_ref[...] = jnp.concatenate([r1, r2], axis=0).astype(I32)
    gate_ref[...] = jnp.concatenate([g1, g2], axis=0)
    cnt_ref[...] = carry_ref[...].astype(I32)


def _router(lt):
    rr, t = lt.shape
    tt = min(512, t)
    return pl.pallas_call(
        _router_kernel,
        out_shape=(jax.ShapeDtypeStruct((2, t), I32),
                   jax.ShapeDtypeStruct((2, t), I32),
                   jax.ShapeDtypeStruct((2, t), F32),
                   jax.ShapeDtypeStruct((N_EXPERTS, 128), I32)),
        grid=(t // tt,),
        in_specs=[pl.BlockSpec((rr, tt), lambda i: (0, i))],
        out_specs=(pl.BlockSpec((2, tt), lambda i: (0, i)),
                   pl.BlockSpec((2, tt), lambda i: (0, i)),
                   pl.BlockSpec((2, tt), lambda i: (0, i)),
                   pl.BlockSpec((N_EXPERTS, 128), lambda i: (0, 0))),
        scratch_shapes=[pltpu.VMEM((N_EXPERTS, 128), F32)],
        compiler_params=_cparams(("arbitrary",)),
        name="router",
    )(lt)


def _invert_kernel(dest_ref, zeros_hbm, out_ref):
    pltpu.sync_copy(zeros_hbm, out_ref)
    t = dest_ref.shape[0] // 2

    def place(i, carry):
        out_ref[dest_ref[i]] = i
        out_ref[dest_ref[t + i]] = i
        return carry
    lax.fori_loop(0, t, place, 0, unroll=8)


def _invert(dest_flat, n_slots):
    return pl.pallas_call(
        _invert_kernel,
        out_shape=jax.ShapeDtypeStruct((n_slots,), I32),
        grid_spec=pltpu.PrefetchScalarGridSpec(
            num_scalar_prefetch=1, grid=(1,),
            in_specs=[pl.BlockSpec(memory_space=pl.ANY)],
            out_specs=pl.BlockSpec(memory_space=pltpu.SMEM)),
        compiler_params=_cparams(("arbitrary",)),
        name="invert",
    )(dest_flat, jnp.zeros((n_slots,), I32))


def _moe_kernel(be_ref, nu_ref, pc_ref, st_ref, hp_hbm, wg_hbm, wu_hbm, wd_hbm, y_ref,
                gbuf, wgf, wuf, wdf, wgb, wub, wdb, par_ref, gsem, wsem):
    for half in range(MOE_BLOCKS_PER_STEP):
        _moe_block(pl.program_id(0) * MOE_BLOCKS_PER_STEP + half, half * MOE_BLK,
                   be_ref, nu_ref, pc_ref, st_ref, hp_hbm, wg_hbm, wu_hbm, wd_hbm, y_ref,
                   gbuf, wgf, wuf, wdf, wgb, wub, wdb, par_ref, gsem, wsem)


def _moe_block(b, row0, be_ref, nu_ref, pc_ref, st_ref, hp_hbm, wg_hbm, wu_hbm, wd_hbm, y_ref,
               gbuf, wgf, wuf, wdf, wgb, wub, wdb, par_ref, gsem, wsem):
    nu = nu_ref[0]
    pr = gbuf.shape[1] // MOE_BLK
    blk = MOE_BLK
    nbuf = gbuf.shape[0]
    look = nbuf - 1

    def row_copy(tok, slot, r):
        src = hp_hbm.at[pl.ds(pl.multiple_of(tok * pr, pr), pr)]
        off = r * pr if isinstance(r, int) else pl.multiple_of(r * pr, pr)
        return pltpu.make_async_copy(src, gbuf.at[slot, pl.ds(off, pr)], gsem.at[slot])

    def issue(block, slot):
        base = block * blk

        def body(r, carry):
            row_copy(st_ref[base + r], slot, r).start()
            return carry
        lax.fori_loop(0, blk, body, 0, unroll=8)

    def weight_copies(e, p):
        return (pltpu.make_async_copy(wg_hbm.at[e], wgf.at[p], wsem.at[p]),
                pltpu.make_async_copy(wu_hbm.at[e], wuf.at[p], wsem.at[p]),
                pltpu.make_async_copy(wd_hbm.at[e], wdf.at[p], wsem.at[p]))

    @pl.when(b == 0)
    def _():
        par_ref[0] = 0
        for cp in weight_copies(be_ref[0], 0):
            cp.start(priority=1)
        issue(0, 0)
        for j in range(1, look):
            @pl.when(j < nu)
            def _():
                issue(j, j)

    def issue_part(k):
        base = (b + look) * blk
        for r in range(k * blk // 4, (k + 1) * blk // 4):
            row_copy(st_ref[base + r], (b + look) % nbuf, r).start(priority=1 if r % 3 == 2 else 0)

    def expert_mlp(slot, with_issue):
        los, his = _load_row_packed(gbuf.at[slot], blk, pr)
        x = jnp.concatenate([v.astype(BF16) for v in los + his], axis=1)
        if with_issue:
            issue_part(0)
        hg = jnp.dot(x, wgb[...], preferred_element_type=F32)
        if with_issue:
            issue_part(1)
        hu = jnp.dot(x, wub[...], preferred_element_type=F32)
        if with_issue:
            issue_part(2)
        hh = hg * (1.0 / (1.0 + jnp.exp(-hg))) * hu
        y = jnp.dot(hh.astype(BF16), wdb[...], preferred_element_type=F32)
        if with_issue:
            issue_part(3)
        _store_row_packed(y_ref, y, row0=row0)

    @pl.when(b < nu)
    def _():
        slot = b % nbuf
        e = be_ref[b]
        first = (b == 0) | (e != be_ref[jnp.maximum(b - 1, 0)])

        @pl.when(first)
        def _():
            p = par_ref[0]
            for cp in weight_copies(e, p):
                cp.wait()
            nxt = lax.while_loop(
                lambda c: (c < N_EXPERTS) & (pc_ref[jnp.minimum(c, N_EXPERTS - 1)] == 0),
                lambda c: c + 1, e + 1)

            @pl.when(nxt < N_EXPERTS)
            def _():
                for cp in weight_copies(nxt, 1 - p):
                    cp.start(priority=1)
            wgb[...] = wgf[p].astype(BF16)
            wub[...] = wuf[p].astype(BF16)
            wdb[...] = wdf[p].astype(BF16)
            par_ref[0] = 1 - p

        pltpu.make_async_copy(hp_hbm.at[pl.ds(0, blk * pr)], gbuf.at[slot], gsem.at[slot]).wait()

        @pl.when(b + look < nu)
        def _():
            expert_mlp(slot, True)

        @pl.when(b + look >= nu)
        def _():
            expert_mlp(slot, False)

    @pl.when(b >= nu)
    def _():
        y_ref[pl.ds(row0 * pr, blk * pr), :] = jnp.zeros((blk * pr, LANES), U32)


def _moe(block_e, n_used, pcounts, slot_tok, hp, w_gate, w_up, w_down, n_blocks):
    ne, d, de = w_gate.shape
    blk = MOE_BLK
    pr = d // 2 // LANES
    bps = MOE_BLOCKS_PER_STEP
    assert n_blocks % bps == 0
    grid_spec = pltpu.PrefetchScalarGridSpec(
        num_scalar_prefetch=4,
        grid=(n_blocks // bps,),
        in_specs=[pl.BlockSpec(memory_space=pl.ANY)] * 4,
        out_specs=pl.BlockSpec((bps * blk * pr, LANES), lambda b, *_: (b, 0)),
        scratch_shapes=[pltpu.VMEM((GATHER_BUFS, blk * pr, LANES), U32),
                        pltpu.VMEM((2, d, de), F32),
                        pltpu.VMEM((2, d, de), F32),
                        pltpu.VMEM((2, de, d), F32),
                        pltpu.VMEM((d, de), BF16),
                        pltpu.VMEM((d, de), BF16),
                        pltpu.VMEM((de, d), BF16),
                        pltpu.SMEM((1,), I32),
                        pltpu.SemaphoreType.DMA((GATHER_BUFS,)),
                        pltpu.SemaphoreType.DMA((2,))],
    )
    return pl.pallas_call(
        _moe_kernel,
        out_shape=jax.ShapeDtypeStruct((n_blocks * blk * pr, LANES), U32),
        grid_spec=grid_spec,
        compiler_params=_cparams(("arbitrary",)),
        name="moe",
    )(block_e, n_used, pcounts, slot_tok, hp, w_gate, w_up, w_down)


def _combine_kernel(d0_ref, d1_ref, h_ref, gate_ref, yb_hbm, g_ref, b_ref, out_ref, buf, sem, *, alpha):
    i = pl.program_id(0)
    n = pl.num_programs(0)
    tm = h_ref.shape[0]
    pr = buf.shape[2] // tm

    def issue(tile, slot):
        base = tile * tm

        def body(r, carry):
            dst = pl.ds(pl.multiple_of(r * pr, pr), pr)
            for k, dref in enumerate((d0_ref, d1_ref)):
                src = yb_hbm.at[pl.ds(pl.multiple_of(dref[base + r] * pr, pr), pr)]
                pltpu.make_async_copy(src, buf.at[slot, k, dst], sem.at[slot]).start()
            return carry
        lax.fori_loop(0, tm, body, 0, unroll=8)

    nbuf = buf.shape[0]
    look = nbuf - 1

    @pl.when(i == 0)
    def _():
        issue(0, 0)
        for j in range(1, look):
            @pl.when(j < n)
            def _():
                issue(j, j)

    slot = i % nbuf
    for k in range(2):
        pltpu.make_async_copy(yb_hbm.at[pl.ds(0, tm * pr)], buf.at[slot, k], sem.at[slot]).wait()

    nq = 4
    sub = tm // nq

    def issue_part(q):
        base = (i + look) * tm
        nslot = (i + look) % nbuf
        for r in range(q * sub, (q + 1) * sub):
            for k, dref in enumerate((d0_ref, d1_ref)):
                src = yb_hbm.at[pl.ds(pl.multiple_of(dref[base + r] * pr, pr), pr)]
                pltpu.make_async_copy(src, buf.at[nslot, k, pl.ds(r * pr, pr)], sem.at[nslot]).start(priority=k)

    def finish(with_issue):
        for q in range(nq):
            if with_issue:
                issue_part(q)
            rows = pl.ds(q * sub, sub)
            lo0, hi0 = _load_row_packed(buf.at[slot, 0], sub, pr, row0=q * sub)
            lo1, hi1 = _load_row_packed(buf.at[slot, 1], sub, pr, row0=q * sub)
            g0 = gate_ref[rows, 0:1]
            g1 = gate_ref[rows, 1:2]
            ffn = jnp.concatenate([g0 * a + g1 * c for a, c in zip(lo0 + hi0, lo1 + hi1)], axis=1)
            out_ref[rows, :] = _layer_norm(alpha * h_ref[rows, :] + ffn, g_ref[...], b_ref[...])

    @pl.when(i + look < n)
    def _():
        finish(True)

    @pl.when(i + look >= n)
    def _():
        finish(False)


def _combine(dest0, dest1, h1, gate_t, yb, ln_g, ln_b, alpha):
    t, d = h1.shape
    tm = min(256, t)
    grid_spec = pltpu.PrefetchScalarGridSpec(
        num_scalar_prefetch=2,
        grid=(t // tm,),
        in_specs=[pl.BlockSpec((tm, d), lambda i, d0, d1: (i, 0)),
                  pl.BlockSpec((tm, 2), lambda i, d0, d1: (i, 0)),
                  pl.BlockSpec(memory_space=pl.ANY),
                  pl.BlockSpec((1, d), lambda i, d0, d1: (0, 0)),
                  pl.BlockSpec((1, d), lambda i, d0, d1: (0, 0))],
        out_specs=pl.BlockSpec((tm, d), lambda i, d0, d1: (i, 0)),
        scratch_shapes=[pltpu.VMEM((GATHER_BUFS, 2, tm * (d // 2 // LANES), LANES), U32),
                        pltpu.SemaphoreType.DMA((GATHER_BUFS,))],
    )
    return pl.pallas_call(
        functools.partial(_combine_kernel, alpha=alpha),
        out_shape=jax.ShapeDtypeStruct((t, d), F32),
        grid_spec=grid_spec,
        compiler_params=_cparams(("arbitrary",)),
        name="combine",
    )(dest0, dest1, h1, gate_t, yb, ln_g, ln_b)


def _layer(h, w_in, pool_w, pool_scale, a_re, a_im, log_dt, b_re, b_im, c_re, c_im, d_skip,
           glu_w, glu_b, w_out, ln1_g, ln1_b, rg_w, rg_b, re_w, re_b, w_gate, w_up, w_down,
           ln2_g, ln2_b, alpha):
    bsz, seq, d = h.shape
    t = bsz * seq
    L = CHUNK
    dp = pool_w.shape[0] * pool_w.shape[1]
    ds = w_in.shape[1] - dp
    g = ds // SSM_GROUP
    n = t // L

    lag0, bb, w1t, w2t, al = _ssm_tables(a_re, a_im, log_dt, b_re, b_im, c_re, c_im, d_skip, L)
    half = ROUTER_LANES // 2
    zpad = lambda k: jnp.zeros((d, k), F32)
    rw = jnp.concatenate([rg_w.astype(F32), zpad(8 - N_EXPERT_GROUPS), re_w.astype(F32),
                          zpad(half - ROUTER_ROWS)], axis=1)
    rw_hi = rw.astype(BF16)
    rw_lo = (rw - rw_hi.astype(F32)).astype(BF16)
    rw2 = jnp.concatenate([rw_hi, rw_lo], axis=1)
    rb = jnp.concatenate([rg_b.astype(F32), jnp.zeros((8 - N_EXPERT_GROUPS,), F32), re_b.astype(F32),
                          jnp.zeros((ROUTER_LANES - ROUTER_ROWS,), F32)]).reshape(1, ROUTER_LANES)

    xt = h.reshape(t, d)
    pool_p, ut = _proj(xt, w_in.astype(BF16), dp, L)
    y_pool = _pool(pool_p, pool_w.astype(BF16), pool_scale.reshape(1, dp).astype(F32), seq)
    yt = _ssm(ut.reshape(g, L * SSM_GROUP, n), lag0, bb, w1t, w2t, al, seq // L, L)
    y_ssm = _glu(yt.reshape(g, L, SSM_GROUP, n), glu_w.astype(BF16), glu_b.reshape(1, ds).astype(F32))
    h1, hp, lt = _outproj(y_pool, y_ssm, xt, w_out.astype(BF16), ln1_g.reshape(1, d), ln1_b.reshape(1, d),
                          rw2, rb, alpha)
    eid, rank, gate, cnt = _router(lt)

    blk = MOE_BLK
    m = 2 * t
    n_blocks = -(-m // blk) + N_EXPERTS
    counts = cnt[:, 0]
    pcounts = (counts + blk - 1) // blk * blk
    pends = jnp.cumsum(pcounts)
    pstarts = pends - pcounts
    e_ids = jnp.arange(N_EXPERTS, dtype=I32)
    dest = jnp.sum(jnp.where(eid[..., None] == e_ids, pstarts, 0), axis=-1) + rank
    slot_tok = _invert(dest.reshape(-1), n_blocks * blk)
    n_used = (pends[-1] // blk).astype(I32)
    bidx = jnp.minimum(jnp.arange(n_blocks, dtype=I32), n_used - 1)
    block_e = jnp.minimum(jnp.sum((pends[None, :] <= (bidx * blk)[:, None]).astype(I32), axis=1),
                          N_EXPERTS - 1)

    yb = _moe(block_e, n_used.reshape(1), pcounts, slot_tok, hp, w_gate, w_up, w_down, n_blocks)
    out = _combine(dest[0], dest[1], h1, gate.T, yb, ln2_g.reshape(1, d), ln2_b.reshape(1, d), alpha)
    return out.reshape(bsz, seq, d)


def kernel(x, w_in, pool_w, pool_scale, ssm_a_re, ssm_a_im, ssm_log_dt, ssm_b_re, ssm_b_im, ssm_c_re, ssm_c_im, ssm_d, glu_w, glu_b, w_out, ln1_g, ln1_b, router_g_w, router_g_b, router_e_w, router_e_b, w_gate, w_up, w_down, ln2_g, ln2_b):
    depth = w_in.shape[0]
    alpha = (2.0 * depth) ** 0.25
    h = x
    for l in range(depth):
        h = _layer(h, w_in[l], pool_w[l], pool_scale[l], ssm_a_re[l], ssm_a_im[l], ssm_log_dt[l],
                   ssm_b_re[l], ssm_b_im[l], ssm_c_re[l], ssm_c_im[l], ssm_d[l], glu_w[l], glu_b[l],
                   w_out[l], ln1_g[l], ln1_b[l], router_g_w[l], router_g_b[l], router_e_w[l],
                   router_e_b[l], w_gate[l], w_up[l], w_down[l], ln2_g[l], ln2_b[l], alpha)
    return h
```

```python
import functools
import math

import numpy as np
import jax
import jax.numpy as jnp
from jax import lax
from jax.experimental import pallas as pl
from jax.experimental.pallas import tpu as pltpu

F32 = jnp.float32
BF16 = jnp.bfloat16
I32 = jnp.int32
U32 = jnp.uint32

POOL_WINDOWS = (2, 4, 8, 16)
POOL_GROUP = 256
SSM_GROUP = 16
SSM_STATE = 64
N_EXPERT_GROUPS = 4
EXPERTS_PER_GROUP = 8
N_EXPERTS = N_EXPERT_GROUPS * EXPERTS_PER_GROUP
LN_EPS = 1e-5

CHUNK = 16
MOE_BLK = 256
MOE_BLOCKS_PER_STEP = 2
GATHER_BUFS = 4
ROUTER_ROWS = 8 + N_EXPERTS
ROUTER_LANES = 128
LANES = 128
MXU_N = 256
HALO = 16
VMEM_LIMIT = 56 * 1024 * 1024


def _cparams(sem, vmem=VMEM_LIMIT):
    return pltpu.CompilerParams(dimension_semantics=sem, vmem_limit_bytes=vmem)


def _proj_kernel(x_ref, w_ref, pool_ref, ut_ref, accp_ref, accs_ref):
    kk = pl.program_id(1)
    rows, dp = accp_ref.shape
    nj = accs_ref.shape[0]

    xb = x_ref[...].astype(BF16)
    per = MXU_N // LANES

    def accumulate(first):
        for n in range(dp // MXU_N):
            cs = slice(n * MXU_N, (n + 1) * MXU_N)
            part = jnp.dot(xb, w_ref[:, cs], preferred_element_type=F32)
            accp_ref[:, cs] = part if first else accp_ref[:, cs] + part
        for n in range(nj // per):
            part = jnp.dot(xb, w_ref[:, dp + n * MXU_N: dp + (n + 1) * MXU_N], preferred_element_type=F32)
            for q in range(per):
                piece = part[:, q * LANES:(q + 1) * LANES]
                accs_ref[n * per + q] = piece if first else accs_ref[n * per + q] + piece

    @pl.when(kk == 0)
    def _():
        accumulate(True)

    @pl.when(kk > 0)
    def _():
        accumulate(False)

    @pl.when(kk == pl.num_programs(1) - 1)
    def _():
        pool_ref[...] = accp_ref[...].astype(BF16)
        g, L, c, r = ut_ref.shape
        gj = LANES // c
        for s in range(L):
            for j in range(nj):
                piece = accs_ref[j, pl.ds(s, r, stride=L), :]
                ut_ref[j * gj:(j + 1) * gj, s, :, :] = piece.astype(BF16).T.reshape(gj, c, r)


def _proj(xt, w_in, dp, L):
    t, d = xt.shape
    dm = w_in.shape[1]
    g = (dm - dp) // SSM_GROUP
    tm = LANES * L
    kb = 512
    return pl.pallas_call(
        _proj_kernel,
        out_shape=(jax.ShapeDtypeStruct((t, dp), BF16),
                   jax.ShapeDtypeStruct((g, L, SSM_GROUP, t // L), BF16)),
        grid=(t // tm, d // kb),
        in_specs=[pl.BlockSpec((tm, kb), lambda i, k: (i, k)),
                  pl.BlockSpec((kb, dm), lambda i, k: (k, 0))],
        out_specs=(pl.BlockSpec((tm, dp), lambda i, k: (i, 0)),
                   pl.BlockSpec((g, L, SSM_GROUP, LANES), lambda i, k: (0, 0, 0, i))),
        scratch_shapes=[pltpu.VMEM((tm, dp), F32),
                        pltpu.VMEM(((dm - dp) // LANES, tm, LANES), F32)],
        compiler_params=_cparams(("arbitrary", "arbitrary")),
        name="proj",
    )(xt, w_in)


def _pool_kernel(prev_ref, main_ref, next_ref, pw_ref, sc_ref, out_ref, *, seq, sub):
    i = pl.program_id(0)
    ts = main_ref.shape[0]
    ext = jnp.concatenate([prev_ref[...], main_ref[...], next_ref[...]], axis=0)
    k = sub + 2 * HALO
    row = lax.broadcasted_iota(I32, (sub, k), 0)
    col = lax.broadcasted_iota(I32, (sub, k), 1)
    off0 = col - HALO - row
    for sb in range(ts // sub):
        rows = slice(sb * sub, (sb + 1) * sub)
        base = (i * ts + sb * sub) % seq
        src = base + row + off0
        off = jnp.where((src >= 0) & (src < seq), off0, 2 * HALO)
        pos = base + lax.broadcasted_iota(I32, (sub, 1), 0)
        for g, w in enumerate(POOL_WINDOWS):
            sl = slice(g * POOL_GROUP, (g + 1) * POOL_GROUP)
            band = (off + w // 2).astype(U32) < w
            bm = jnp.where(band, 1.0, 0.0).astype(BF16)
            sums = jnp.dot(bm, ext[sb * sub:sb * sub + k, sl], preferred_element_type=F32)
            lo = jnp.maximum(pos - w // 2, 0)
            hi = jnp.minimum(pos - w // 2 + w, seq)
            inv = 1.0 / (hi - lo).astype(F32)
            dlt = sums * inv - main_ref[rows, sl].astype(F32)
            y = jnp.dot(dlt.astype(BF16), pw_ref[g], preferred_element_type=F32) * sc_ref[:, sl]
            out_ref[rows, sl] = y.astype(BF16)


def _pool(pp, pool_w, pool_scale, seq):
    t, dp = pp.shape
    ts = min(512, seq)
    nh = t // HALO
    per = ts // HALO
    return pl.pallas_call(
        functools.partial(_pool_kernel, seq=seq, sub=256),
        out_shape=jax.ShapeDtypeStruct((t, dp), BF16),
        grid=(t // ts,),
        in_specs=[pl.BlockSpec((HALO, dp), lambda i: (jnp.maximum(i * per - 1, 0), 0)),
                  pl.BlockSpec((ts, dp), lambda i: (i, 0)),
                  pl.BlockSpec((HALO, dp), lambda i: (jnp.minimum((i + 1) * per, nh - 1), 0)),
                  pl.BlockSpec(pool_w.shape, lambda i: (0, 0, 0)),
                  pl.BlockSpec((1, dp), lambda i: (0, 0))],
        out_specs=pl.BlockSpec((ts, dp), lambda i: (i, 0)),
        compiler_params=_cparams(("arbitrary",)),
        name="pool",
    )(pp, pp, pp, pool_w, pool_scale)


def _ssm_tables(a_re, a_im, log_dt, b_re, b_im, c_re, c_im, d_skip, L):
    g = a_re.shape[1]
    p = a_re.shape[2]
    c = b_re.shape[3]
    fl = L * c
    lr = a_re.astype(F32)
    li = a_im.astype(F32)
    dt = jnp.exp(log_dt.astype(F32))[..., None]
    mag = jnp.exp(lr * dt)
    abr = mag * jnp.cos(li * dt)
    abi = mag * jnp.sin(li * dt)
    den = lr * lr + li * li
    zr = ((abr - 1.0) * lr + abi * li) / den
    zi = (abi * lr - (abr - 1.0) * li) / den
    br = b_re.astype(F32)
    bi = b_im.astype(F32)
    bbr = zr[..., None] * br - zi[..., None] * bi
    bbi = zr[..., None] * bi + zi[..., None] * br
    cr = c_re.astype(F32)
    ci = c_im.astype(F32)
    kk = jnp.arange(L + 1, dtype=F32)[None, None, :, None]
    pm = jnp.exp(kk * (lr * dt)[:, :, None, :])
    ang = kk * (li * dt)[:, :, None, :]
    pr = pm * jnp.cos(ang)
    pi = pm * jnp.sin(ang)

    skip = jnp.eye(c, dtype=F32)[None] * d_skip.astype(F32).reshape(g, c, 1)
    lag0 = jnp.einsum('dgop,dgpc->goc', cr, bbr, precision=lax.Precision.HIGHEST) \
        - jnp.einsum('dgop,dgpc->goc', ci, bbi, precision=lax.Precision.HIGHEST) + skip
    lag0 = jnp.pad(lag0, ((0, 0), (0, 0), (0, LANES - c)))
    bb = jnp.concatenate([bbr[0], bbi[0], bbr[1], bbi[1]], axis=1)
    bb = jnp.pad(bb, ((0, 0), (0, 0), (0, LANES - c)))

    def w1(d, qr, qi):
        qr = qr.transpose(0, 2, 1)[..., None]
        qi = qi.transpose(0, 2, 1)[..., None]
        re = qr * bbr[d][:, :, None] - qi * bbi[d][:, :, None]
        im = qr * bbi[d][:, :, None] + qi * bbr[d][:, :, None]
        to = lambda v: v.reshape(g, p, fl)
        return to(re), to(im)
    f_re, f_im = w1(0, pr[0][:, L - 1::-1], pi[0][:, L - 1::-1])
    b_re_, b_im_ = w1(1, pr[1][:, :L], pi[1][:, :L])
    w1t = jnp.concatenate([f_re, f_im, b_re_, b_im_], axis=1)

    def w2(d, qr, qi):
        re = cr[d][:, None] * qr[:, :, None] - ci[d][:, None] * qi[:, :, None]
        im = -(cr[d][:, None] * qi[:, :, None] + ci[d][:, None] * qr[:, :, None])
        to = lambda v: v.reshape(g, fl, p)
        return to(re), to(im)
    rf_re, rf_im = w2(0, pr[0][:, 1:], pi[0][:, 1:])
    rb_re, rb_im = w2(1, pr[1][:, :0:-1], pi[1][:, :0:-1])
    w2t = jnp.concatenate([rf_re, rf_im, rb_re, rb_im], axis=2)

    a4 = jnp.stack([pr[0][:, L], pi[0][:, L], pr[1][:, L], pi[1][:, L]], axis=1)
    ap = a4.reshape(g // 2, 2, 4, p).transpose(0, 2, 1, 3).reshape(g // 2, 4, 2 * p)
    ap = jnp.pad(ap, ((0, 0), (0, 4), (0, 0)))
    return lag0, bb.astype(BF16), w1t.astype(BF16), w2t.astype(BF16), ap


def _cmul(ar, ai, br, bi):
    return ar * br - ai * bi, ar * bi + ai * br


def _ssm_kernel(ut_ref, w1_ref, lag0_ref, bb_ref, w2_ref, a_ref, yt_ref, mt_s, kk_s, zt_s, ht_s, *, nseg, L):
    p = SSM_STATE
    c = SSM_GROUP
    fl, n = ut_ref.shape[1:]
    nb = n // nseg
    nrow = nseg // 8
    st = [jnp.dot(w1_ref[q], ut_ref[q], preferred_element_type=F32) for q in range(2)]
    z = jnp.concatenate([st[q][k * p:(k + 1) * p] for k in range(4) for q in range(2)], axis=0)
    zt_s[...] = z.T
    per = LANES // c
    for q in range(2):
        kf = jnp.dot(w2_ref[q, :, 0:2 * p], bb_ref[q, 0:2 * p, :], preferred_element_type=F32)
        kb = jnp.dot(w2_ref[q, :, 2 * p:4 * p], bb_ref[q, 2 * p:4 * p, :], preferred_element_type=F32)
        tiles = [None] * (2 * L // per)
        for j in range(2 * L - 1):
            lag = L - 1 - j
            if lag > 0:
                blk = kf[(lag - 1) * c:lag * c, :]
            elif lag == 0:
                blk = lag0_ref[q]
            else:
                blk = kb[(L + lag) * c:(L + lag + 1) * c, :]
            sh = (j % per) * c
            blk = blk if sh == 0 else pltpu.roll(blk, sh, 1)
            tiles[j // per] = blk if tiles[j // per] is None else tiles[j // per] + blk
        kk_s[...] = jnp.concatenate(tiles, axis=1).astype(BF16)
        for t in range(L):
            mt_s[q, t * c:(t + 1) * c, :] = kk_s[:, (L - 1 - t) * c:(L - 1 - t) * c + fl]

    sub = lax.broadcasted_iota(I32, (8, LANES), 0)
    full = lambda v: jnp.broadcast_to(v, (8, LANES))
    one, zero = jnp.ones((8, LANES), F32), jnp.zeros((8, LANES), F32)

    def powers(row):
        out = [(full(a_ref[0, row:row + 1, :]), full(a_ref[0, row + 1:row + 2, :]))]
        for _ in range(3):
            out.append(_cmul(*out[-1], *out[-1]))
        return out

    def by_bits(pw, idx):
        qr, qi = one, zero
        for bit in range(3):
            on = (idx >> bit) & 1 == 1
            qr, qi = _cmul(qr, qi, jnp.where(on, pw[bit][0], one), jnp.where(on, pw[bit][1], zero))
        return qr, qi

    pw_f, pw_b = powers(0), powers(2)
    qf = by_bits(pw_f, sub)
    qb = by_bits(pw_b, 7 - sub)

    def local(xr, xi, pw, forward):
        for lvl, d in enumerate((1, 2, 4)):
            keep = (sub >= d) if forward else (sub < 8 - d)
            sh = d if forward else 8 - d
            tr = jnp.where(keep, pltpu.roll(xr, sh, 0), 0.0)
            ti = jnp.where(keep, pltpu.roll(xi, sh, 0), 0.0)
            mr, mi = _cmul(pw[lvl][0], pw[lvl][1], tr, ti)
            xr, xi = xr + mr, xi + mi
        return xr, xi

    def step(i, carry):
        new = []
        for b in range(nb):
            for fwd in (True, False):
                cr, ci = carry[2 * (2 * b + (0 if fwd else 1)):][:2]
                row = b * nseg + (i if fwd else nrow - 1 - i) * 8
                rows = pl.ds(pl.multiple_of(row, 8), 8)
                l0 = 0 if fwd else 2 * LANES
                pw, (qr, qi) = (pw_f, qf) if fwd else (pw_b, qb)
                lr, li = local(zt_s[rows, l0:l0 + LANES], zt_s[rows, l0 + LANES:l0 + 2 * LANES], pw, fwd)
                keep = (sub >= 1) if fwd else (sub < 7)
                er = jnp.where(keep, pltpu.roll(lr, 1 if fwd else 7, 0), 0.0)
                ei = jnp.where(keep, pltpu.roll(li, 1 if fwd else 7, 0), 0.0)
                mr, mi = _cmul(qr, qi, cr, ci)
                ht_s[rows, l0:l0 + LANES] = er + mr
                ht_s[rows, l0 + LANES:l0 + 2 * LANES] = ei + mi
                edge = 7 if fwd else 0
                mr, mi = _cmul(pw[3][0], pw[3][1], cr, ci)
                new += [full(lr[edge:edge + 1, :]) + mr, full(li[edge:edge + 1, :]) + mi]
        return tuple(new)

    lax.fori_loop(0, nrow, step, (zero,) * (4 * nb))

    h = ht_s[...].astype(BF16).T
    for q in range(2):
        hq = jnp.concatenate([h[(2 * k + q) * p:(2 * k + q + 1) * p] for k in range(4)], axis=0)
        y = jnp.dot(mt_s[q], ut_ref[q], preferred_element_type=F32) \
            + jnp.dot(w2_ref[q], hq, preferred_element_type=F32)
        yt_ref[q] = y.astype(BF16)


def _ssm(ut3, lag0, bb, w1t, w2t, ap, nseg, L):
    g, fl, n = ut3.shape
    sp = w1t.shape[1]
    return pl.pallas_call(
        functools.partial(_ssm_kernel, nseg=nseg, L=L),
        out_shape=jax.ShapeDtypeStruct((g, fl, n), BF16),
        grid=(g // 2,),
        in_specs=[pl.BlockSpec((2, fl, n), lambda i: (i, 0, 0)),
                  pl.BlockSpec((2, sp, fl), lambda i: (i, 0, 0)),
                  pl.BlockSpec((2,) + lag0.shape[1:], lambda i: (i, 0, 0)),
                  pl.BlockSpec((2,) + bb.shape[1:], lambda i: (i, 0, 0)),
                  pl.BlockSpec((2, fl, sp), lambda i: (i, 0, 0)),
                  pl.BlockSpec((1,) + ap.shape[1:], lambda i: (i, 0, 0))],
        out_specs=pl.BlockSpec((2, fl, n), lambda i: (i, 0, 0)),
        scratch_shapes=[pltpu.VMEM((2, fl, fl), BF16),
                        pltpu.VMEM((SSM_GROUP, 2 * fl), BF16),
                        pltpu.VMEM((n, 2 * sp), F32),
                        pltpu.VMEM((n, 2 * sp), F32)],
        compiler_params=_cparams(("arbitrary",)),
        name="ssm",
    )(ut3, w1t, lag0, bb, w2t, ap)


def _glu_kernel(yt_ref, w_ref, b_ref, out_ref, nat_ref, *, sub):
    g, L, c, r = yt_ref.shape
    nj = nat_ref.shape[0]
    gj = LANES // c
    for t in range(L):
        for j in range(nj):
            piece = yt_ref[j * gj:(j + 1) * gj, t, :, :].reshape(LANES, r).astype(F32)
            nat_ref[j, pl.ds(t, r, stride=L), :] = piece.T

    def body(k, carry):
        sl = pl.ds(pl.multiple_of(k * sub, sub), sub)
        y = jnp.concatenate([nat_ref[j, sl, :] for j in range(nj)], axis=1)
        ya = jax.nn.gelu(y, approximate=True)
        z = jnp.dot(ya.astype(BF16), w_ref[...], preferred_element_type=F32) + b_ref[...]
        out_ref[sl, :] = (ya * (1.0 / (1.0 + jnp.exp(-z)))).astype(BF16)
        return carry
    lax.fori_loop(0, nat_ref.shape[1] // sub, body, 0)


def _glu(yt4, glu_w, glu_b):
    g, L, c, n = yt4.shape
    ds = g * c
    r = LANES
    return pl.pallas_call(
        functools.partial(_glu_kernel, sub=512),
        out_shape=jax.ShapeDtypeStruct((n * L, ds), BF16),
        grid=(n // r,),
        in_specs=[pl.BlockSpec((g, L, c, r), lambda j: (0, 0, 0, j)),
                  pl.BlockSpec((ds, ds), lambda j: (0, 0)),
                  pl.BlockSpec((1, ds), lambda j: (0, 0))],
        out_specs=pl.BlockSpec((r * L, ds), lambda j: (j, 0)),
        scratch_shapes=[pltpu.VMEM((ds // LANES, r * L, LANES), F32)],
        compiler_params=_cparams(("arbitrary",)),
        name="glu",
    )(yt4, glu_w, glu_b)


def _layer_norm(r, g, b):
    mu = jnp.mean(r, axis=-1, keepdims=True)
    cen = r - mu
    var = jnp.mean(cen * cen, axis=-1, keepdims=True)
    return cen * lax.rsqrt(var + LN_EPS) * g + b


def _store_row_packed(ref, m, row0=0):
    rows, width = m.shape
    half = width // 2
    nc = half // LANES
    lo = lax.bitcast_convert_type(m[:, :half].astype(BF16).astype(F32), U32) >> 16
    hi = lax.bitcast_convert_type(m[:, half:].astype(BF16).astype(F32), U32) & jnp.uint32(0xFFFF0000)
    pk = lo | hi
    for c in range(nc):
        ref[pl.ds(row0 * nc + c, rows, stride=nc), :] = pk[:, c * LANES:(c + 1) * LANES]


def _load_row_packed(ref, rows, nc, row0=0):
    los, his = [], []
    for c in range(nc):
        u = ref[pl.ds(row0 * nc + c, rows, stride=nc), :]
        los.append(lax.bitcast_convert_type(u << 16, F32))
        his.append(lax.bitcast_convert_type(u & jnp.uint32(0xFFFF0000), F32))
    return los, his


def _outproj_kernel(yp_ref, ys_ref, x_ref, wo_ref, g_ref, b_ref, rw_ref, rb_ref,
                    h_ref, hp_ref, lt_ref, *, alpha, dp, sub):
    for sb in range(h_ref.shape[0] // sub):
        sl = pl.ds(sb * sub, sub)
        mix = jnp.dot(yp_ref[sl, :], wo_ref[0:dp, :], preferred_element_type=F32) \
            + jnp.dot(ys_ref[sl, :], wo_ref[dp:, :], preferred_element_type=F32)
        h = _layer_norm(alpha * x_ref[sl, :] + mix, g_ref[...], b_ref[...])
        h_ref[sl, :] = h
        _store_row_packed(hp_ref, h, row0=sb * sub)
        hh = h.astype(BF16)
        hl = (h - hh.astype(F32)).astype(BF16)
        part = jnp.dot(hh, rw_ref[...], preferred_element_type=F32) \
            + jnp.dot(hl, rw_ref[...], preferred_element_type=F32)
        logits = part + pltpu.roll(part, ROUTER_LANES // 2, 1) + rb_ref[...]
        lt_ref[:, sb * sub:(sb + 1) * sub] = logits.T[0:ROUTER_ROWS, :]


def _outproj(y_pool, y_ssm, xt, w_out, ln_g, ln_b, rw, rb, alpha):
    t, d = xt.shape
    dp = y_pool.shape[1]
    tm = min(512, t)
    pr = d // 2 // LANES
    return pl.pallas_call(
        functools.partial(_outproj_kernel, alpha=alpha, dp=dp, sub=256),
        out_shape=(jax.ShapeDtypeStruct((t, d), F32),
                   jax.ShapeDtypeStruct((t * pr, LANES), U32),
                   jax.ShapeDtypeStruct((ROUTER_ROWS, t), F32)),
        grid=(t // tm,),
        in_specs=[pl.BlockSpec((tm, dp), lambda i: (i, 0)),
                  pl.BlockSpec((tm, y_ssm.shape[1]), lambda i: (i, 0)),
                  pl.BlockSpec((tm, d), lambda i: (i, 0)),
                  pl.BlockSpec(w_out.shape, lambda i: (0, 0)),
                  pl.BlockSpec((1, d), lambda i: (0, 0)),
                  pl.BlockSpec((1, d), lambda i: (0, 0)),
                  pl.BlockSpec(rw.shape, lambda i: (0, 0)),
                  pl.BlockSpec((1, ROUTER_LANES), lambda i: (0, 0))],
        out_specs=(pl.BlockSpec((tm, d), lambda i: (i, 0)),
                   pl.BlockSpec((tm * pr, LANES), lambda i: (i, 0)),
                   pl.BlockSpec((ROUTER_ROWS, tm), lambda i: (0, i))),
        compiler_params=_cparams(("arbitrary",)),
        name="outproj",
    )(y_pool, y_ssm, xt, w_out, ln_g, ln_b, rw, rb)


def _router_kernel(lt_ref, eid_ref, rank_ref, gate_ref, cnt_ref, carry_ref):
    i = pl.program_id(0)
    ne, epg = N_EXPERTS, EXPERTS_PER_GROUP

    @pl.when(i == 0)
    def _():
        carry_ref[...] = jnp.zeros_like(carry_ref)

    lt = lt_ref[...]
    tt = lt.shape[1]
    gl = [lt[j:j + 1, :] for j in range(N_EXPERT_GROUPS)]
    gmax = jnp.maximum(jnp.maximum(gl[0], gl[1]), jnp.maximum(gl[2], gl[3]))
    grp = jnp.where(gl[0] == gmax, 0, jnp.where(gl[1] == gmax, 1, jnp.where(gl[2] == gmax, 2, 3)))
    p_grp = 1.0 / (jnp.exp(gl[0] - gmax) + jnp.exp(gl[1] - gmax)
                   + jnp.exp(gl[2] - gmax) + jnp.exp(gl[3] - gmax))
    eg = [lt[8 + epg * j: 8 + epg * (j + 1), :] for j in range(N_EXPERT_GROUPS)]
    el = jnp.where(grp == 0, eg[0], jnp.where(grp == 1, eg[1], jnp.where(grp == 2, eg[2], eg[3])))
    sub = lax.broadcasted_iota(I32, (epg, tt), 0)
    m1 = jnp.max(el, axis=0, keepdims=True)
    i1 = jnp.min(jnp.where(el == m1, sub, epg), axis=0, keepdims=True)
    rest = jnp.where(sub == i1, -jnp.inf, el)
    m2 = jnp.max(rest, axis=0, keepdims=True)
    i2 = jnp.min(jnp.where(rest == m2, sub, epg), axis=0, keepdims=True)
    r21 = jnp.exp(m2 - m1)
    g1 = p_grp / (1.0 + r21)
    g2 = g1 * r21
    e1 = grp * epg + i1
    e2 = grp * epg + i2

    rows = lax.broadcasted_iota(I32, (ne, tt), 0)
    oh1 = rows == e1
    oh2 = rows == e2
    oh = jnp.where(oh1 | oh2, 1.0, 0.0)
    tri = jnp.where(lax.broadcasted_iota(I32, (tt, tt), 0) < lax.broadcasted_iota(I32, (tt, tt), 1),
                    1.0, 0.0).astype(BF16)
    before = jnp.dot(oh.astype(BF16), tri, preferred_element_type=F32) + carry_ref[:, 0:1]
    r1 = jnp.sum(jnp.where(oh1, before, 0.0), axis=0, keepdims=True)
    r2 = jnp.sum(jnp.where(oh2, before, 0.0), axis=0, keepdims=True)
    carry_ref[...] = carry_ref[...] + jnp.sum(oh, axis=1, keepdims=True)

    eid_ref[...] = jnp.concatenate([e1, e2], axis=0)
    rank_ref[...] = jnp.concatenate([r1, r2], axis=0).astype(I32)
    gate_ref[...] = jnp.concatenate([g1, g2], axis=0)
    cnt_ref[...] = carry_ref[...].astype(I32)


def _router(lt):
    rr, t = lt.shape
    tt = min(512, t)
    return pl.pallas_call(
        _router_kernel,
        out_shape=(jax.ShapeDtypeStruct((2, t), I32),
                   jax.ShapeDtypeStruct((2, t), I32),
                   jax.ShapeDtypeStruct((2, t), F32),
                   jax.ShapeDtypeStruct((N_EXPERTS, 128), I32)),
        grid=(t // tt,),
        in_specs=[pl.BlockSpec((rr, tt), lambda i: (0, i))],
        out_specs=(pl.BlockSpec((2, tt), lambda i: (0, i)),
                   pl.BlockSpec((2, tt), lambda i: (0, i)),
                   pl.BlockSpec((2, tt), lambda i: (0, i)),
                   pl.BlockSpec((N_EXPERTS, 128), lambda i: (0, 0))),
        scratch_shapes=[pltpu.VMEM((N_EXPERTS, 128), F32)],
        compiler_params=_cparams(("arbitrary",)),
        name="router",
    )(lt)


def _invert_kernel(dest_ref, zeros_hbm, out_ref):
    pltpu.sync_copy(zeros_hbm, out_ref)

    def place(i, carry):
        d = dest_ref[i]
        out_ref[d & 0xFFFF] = i
        out_ref[lax.shift_right_logical(d, 16)] = i
        return carry
    lax.fori_loop(0, dest_ref.shape[0], place, 0, unroll=8)


def _invert(dest, n_slots):
    assert n_slots <= 1 << 16
    dest_flat = dest[0] | (dest[1] << 16)
    return pl.pallas_call(
        _invert_kernel,
        out_shape=jax.ShapeDtypeStruct((n_slots,), I32),
        grid_spec=pltpu.PrefetchScalarGridSpec(
            num_scalar_prefetch=1, grid=(1,),
            in_specs=[pl.BlockSpec(memory_space=pl.ANY)],
            out_specs=pl.BlockSpec(memory_space=pltpu.SMEM)),
        compiler_params=_cparams(("arbitrary",)),
        name="invert",
    )(dest_flat, jnp.zeros((n_slots,), I32))


def _moe_kernel(be_ref, nu_ref, pc_ref, st_ref, hp_hbm, wg_hbm, wu_hbm, wd_hbm, y_ref,
                gbuf, wgf, wuf, wdf, wgb, wub, wdb, par_ref, gsem, wsem):
    for half in range(MOE_BLOCKS_PER_STEP):
        _moe_block(pl.program_id(0) * MOE_BLOCKS_PER_STEP + half, half * MOE_BLK,
                   be_ref, nu_ref, pc_ref, st_ref, hp_hbm, wg_hbm, wu_hbm, wd_hbm, y_ref,
                   gbuf, wgf, wuf, wdf, wgb, wub, wdb, par_ref, gsem, wsem)


def _moe_block(b, row0, be_ref, nu_ref, pc_ref, st_ref, hp_hbm, wg_hbm, wu_hbm, wd_hbm, y_ref,
               gbuf, wgf, wuf, wdf, wgb, wub, wdb, par_ref, gsem, wsem):
    nu = nu_ref[0]
    pr = gbuf.shape[1] // MOE_BLK
    blk = MOE_BLK
    nbuf = gbuf.shape[0]
    look = nbuf - 1

    def row_copy(tok, slot, r):
        src = hp_hbm.at[pl.ds(pl.multiple_of(tok * pr, pr), pr)]
        off = r * pr if isinstance(r, int) else pl.multiple_of(r * pr, pr)
        return pltpu.make_async_copy(src, gbuf.at[slot, pl.ds(off, pr)], gsem.at[slot])

    def issue(block, slot):
        base = block * blk

        def body(r, carry):
            row_copy(st_ref[base + r], slot, r).start()
            return carry
        lax.fori_loop(0, blk, body, 0, unroll=8)

    def weight_copies(e, p):
        return (pltpu.make_async_copy(wg_hbm.at[e], wgf.at[p], wsem.at[p]),
                pltpu.make_async_copy(wu_hbm.at[e], wuf.at[p], wsem.at[p]),
                pltpu.make_async_copy(wd_hbm.at[e], wdf.at[p], wsem.at[p]))

    @pl.when(b == 0)
    def _():
        par_ref[0] = 0
        for cp in weight_copies(be_ref[0], 0):
            cp.start()
        issue(0, 0)
        for j in range(1, look):
            @pl.when(j < nu)
            def _():
                issue(j, j)

    def issue_part(k):
        base = (b + look) * blk
        for r in range(k * blk // 4, (k + 1) * blk // 4):
            row_copy(st_ref[base + r], (b + look) % nbuf, r).start(priority=1)

    def expert_mlp(slot, with_issue):
        los, his = _load_row_packed(gbuf.at[slot], blk, pr)
        x = jnp.concatenate([v.astype(BF16) for v in los + his], axis=1)
        if with_issue:
            issue_part(0)
        hg = jnp.dot(x, wgb[...], preferred_element_type=F32)
        if with_issue:
            issue_part(1)
        hu = jnp.dot(x, wub[...], preferred_element_type=F32)
        if with_issue:
            issue_part(2)
        hh = hg * (1.0 / (1.0 + jnp.exp(-hg))) * hu
        y = jnp.dot(hh.astype(BF16), wdb[...], preferred_element_type=F32)
        if with_issue:
            issue_part(3)
        _store_row_packed(y_ref, y, row0=row0)

    @pl.when(b < nu)
    def _():
        slot = b % nbuf
        e = be_ref[b]
        first = (b == 0) | (e != be_ref[jnp.maximum(b - 1, 0)])

        @pl.when(first)
        def _():
            p = par_ref[0]
            for cp in weight_copies(e, p):
                cp.wait()
            nxt = lax.while_loop(
                lambda c: (c < N_EXPERTS) & (pc_ref[jnp.minimum(c, N_EXPERTS - 1)] == 0),
                lambda c: c + 1, e + 1)

            @pl.when(nxt < N_EXPERTS)
            def _():
                for cp in weight_copies(nxt, 1 - p):
                    cp.start()
            wgb[...] = wgf[p].astype(BF16)
            wub[...] = wuf[p].astype(BF16)
            wdb[...] = wdf[p].astype(BF16)
            par_ref[0] = 1 - p

        pltpu.make_async_copy(hp_hbm.at[pl.ds(0, blk * pr)], gbuf.at[slot], gsem.at[slot]).wait()

        @pl.when(b + look < nu)
        def _():
            expert_mlp(slot, True)

        @pl.when(b + look >= nu)
        def _():
            expert_mlp(slot, False)

    @pl.when(b >= nu)
    def _():
        y_ref[pl.ds(row0 * pr, blk * pr), :] = jnp.zeros((blk * pr, LANES), U32)


def _moe(block_e, n_used, pcounts, slot_tok, hp, w_gate, w_up, w_down, n_blocks):
    ne, d, de = w_gate.shape
    blk = MOE_BLK
    pr = d // 2 // LANES
    bps = MOE_BLOCKS_PER_STEP
    assert n_blocks % bps == 0
    grid_spec = pltpu.PrefetchScalarGridSpec(
        num_scalar_prefetch=4,
        grid=(n_blocks // bps,),
        in_specs=[pl.BlockSpec(memory_space=pl.ANY)] * 4,
        out_specs=pl.BlockSpec((bps * blk * pr, LANES), lambda b, *_: (b, 0)),
        scratch_shapes=[pltpu.VMEM((GATHER_BUFS, blk * pr, LANES), U32),
                        pltpu.VMEM((2, d, de), F32),
                        pltpu.VMEM((2, d, de), F32),
                        pltpu.VMEM((2, de, d), F32),
                        pltpu.VMEM((d, de), BF16),
                        pltpu.VMEM((d, de), BF16),
                        pltpu.VMEM((de, d), BF16),
                        pltpu.SMEM((1,), I32),
                        pltpu.SemaphoreType.DMA((GATHER_BUFS,)),
                        pltpu.SemaphoreType.DMA((2,))],
    )
    return pl.pallas_call(
        _moe_kernel,
        out_shape=jax.ShapeDtypeStruct((n_blocks * blk * pr, LANES), U32),
        grid_spec=grid_spec,
        compiler_params=_cparams(("arbitrary",)),
        name="moe",
    )(block_e, n_used, pcounts, slot_tok, hp, w_gate, w_up, w_down)


def _combine_kernel(d0_ref, d1_ref, h_ref, gate_ref, yb_hbm, g_ref, b_ref, out_ref, buf, sem, *, alpha):
    i = pl.program_id(0)
    n = pl.num_programs(0)
    tm = h_ref.shape[0]
    pr = buf.shape[2] // tm

    def issue(tile, slot):
        base = tile * tm

        def body(r, carry):
            dst = pl.ds(pl.multiple_of(r * pr, pr), pr)
            for k, dref in enumerate((d0_ref, d1_ref)):
                src = yb_hbm.at[pl.ds(pl.multiple_of(dref[base + r] * pr, pr), pr)]
                pltpu.make_async_copy(src, buf.at[slot, k, dst], sem.at[slot]).start()
            return carry
        lax.fori_loop(0, tm, body, 0, unroll=8)

    nbuf = buf.shape[0]
    look = nbuf - 1

    @pl.when(i == 0)
    def _():
        issue(0, 0)
        for j in range(1, look):
            @pl.when(j < n)
            def _():
                issue(j, j)

    slot = i % nbuf
    for k in range(2):
        pltpu.make_async_copy(yb_hbm.at[pl.ds(0, tm * pr)], buf.at[slot, k], sem.at[slot]).wait()

    nq = 4
    sub = tm // nq

    def issue_part(q):
        base = (i + look) * tm
        nslot = (i + look) % nbuf
        for r in range(q * sub, (q + 1) * sub):
            for k, dref in enumerate((d0_ref, d1_ref)):
                src = yb_hbm.at[pl.ds(pl.multiple_of(dref[base + r] * pr, pr), pr)]
                pltpu.make_async_copy(src, buf.at[nslot, k, pl.ds(r * pr, pr)], sem.at[nslot]).start(priority=k)

    def finish(with_issue):
        for q in range(nq):
            if with_issue:
                issue_part(q)
            rows = pl.ds(q * sub, sub)
            lo0, hi0 = _load_row_packed(buf.at[slot, 0], sub, pr, row0=q * sub)
            lo1, hi1 = _load_row_packed(buf.at[slot, 1], sub, pr, row0=q * sub)
            g0 = gate_ref[rows, 0:1]
            g1 = gate_ref[rows, 1:2]
            ffn = jnp.concatenate([g0 * a + g1 * c for a, c in zip(lo0 + hi0, lo1 + hi1)], axis=1)
            out_ref[rows, :] = _layer_norm(alpha * h_ref[rows, :] + ffn, g_ref[...], b_ref[...])

    @pl.when(i + look < n)
    def _():
        finish(True)

    @pl.when(i + look >= n)
    def _():
        finish(False)


def _combine(dest0, dest1, h1, gate_t, yb, ln_g, ln_b, alpha):
    t, d = h1.shape
    tm = min(256, t)
    grid_spec = pltpu.PrefetchScalarGridSpec(
        num_scalar_prefetch=2,
        grid=(t // tm,),
        in_specs=[pl.BlockSpec((tm, d), lambda i, d0, d1: (i, 0)),
                  pl.BlockSpec((tm, 2), lambda i, d0, d1: (i, 0)),
                  pl.BlockSpec(memory_space=pl.ANY),
                  pl.BlockSpec((1, d), lambda i, d0, d1: (0, 0)),
                  pl.BlockSpec((1, d), lambda i, d0, d1: (0, 0))],
        out_specs=pl.BlockSpec((tm, d), lambda i, d0, d1: (i, 0)),
        scratch_shapes=[pltpu.VMEM((GATHER_BUFS, 2, tm * (d // 2 // LANES), LANES), U32),
                        pltpu.SemaphoreType.DMA((GATHER_BUFS,))],
    )
    return pl.pallas_call(
        functools.partial(_combine_kernel, alpha=alpha),
        out_shape=jax.ShapeDtypeStruct((t, d), F32),
        grid_spec=grid_spec,
        compiler_params=_cparams(("arbitrary",)),
        name="combine",
    )(dest0, dest1, h1, gate_t, yb, ln_g, ln_b)


def _layer(h, w_in, pool_w, pool_scale, a_re, a_im, log_dt, b_re, b_im, c_re, c_im, d_skip,
           glu_w, glu_b, w_out, ln1_g, ln1_b, rg_w, rg_b, re_w, re_b, w_gate, w_up, w_down,
           ln2_g, ln2_b, alpha):
    bsz, seq, d = h.shape
    t = bsz * seq
    L = CHUNK
    dp = pool_w.shape[0] * pool_w.shape[1]
    ds = w_in.shape[1] - dp
    g = ds // SSM_GROUP
    n = t // L

    lag0, bb, w1t, w2t, al = _ssm_tables(a_re, a_im, log_dt, b_re, b_im, c_re, c_im, d_skip, L)
    half = ROUTER_LANES // 2
    zpad = lambda k: jnp.zeros((d, k), F32)
    rw = jnp.concatenate([rg_w.astype(F32), zpad(8 - N_EXPERT_GROUPS), re_w.astype(F32),
                          zpad(half - ROUTER_ROWS)], axis=1)
    rw_hi = rw.astype(BF16)
    rw_lo = (rw - rw_hi.astype(F32)).astype(BF16)
    rw2 = jnp.concatenate([rw_hi, rw_lo], axis=1)
    rb = jnp.concatenate([rg_b.astype(F32), jnp.zeros((8 - N_EXPERT_GROUPS,), F32), re_b.astype(F32),
                          jnp.zeros((ROUTER_LANES - ROUTER_ROWS,), F32)]).reshape(1, ROUTER_LANES)

    xt = h.reshape(t, d)
    pool_p, ut = _proj(xt, w_in.astype(BF16), dp, L)
    y_pool = _pool(pool_p, pool_w.astype(BF16), pool_scale.reshape(1, dp).astype(F32), seq)
    yt = _ssm(ut.reshape(g, L * SSM_GROUP, n), lag0, bb, w1t, w2t, al, seq // L, L)
    y_ssm = _glu(yt.reshape(g, L, SSM_GROUP, n), glu_w.astype(BF16), glu_b.reshape(1, ds).astype(F32))
    h1, hp, lt = _outproj(y_pool, y_ssm, xt, w_out.astype(BF16), ln1_g.reshape(1, d), ln1_b.reshape(1, d),
                          rw2, rb, alpha)
    eid, rank, gate, cnt = _router(lt)

    blk = MOE_BLK
    m = 2 * t
    n_blocks = -(-m // blk) + N_EXPERTS
    counts = cnt[:, 0]
    pcounts = (counts + blk - 1) // blk * blk
    pends = jnp.cumsum(pcounts)
    pstarts = pends - pcounts
    e_ids = jnp.arange(N_EXPERTS, dtype=I32)
    dest = jnp.sum(jnp.where(eid[..., None] == e_ids, pstarts, 0), axis=-1) + rank
    slot_tok = _invert(dest, n_blocks * blk)
    n_used = (pends[-1] // blk).astype(I32)
    bidx = jnp.minimum(jnp.arange(n_blocks, dtype=I32), n_used - 1)
    block_e = jnp.minimum(jnp.sum((pends[None, :] <= (bidx * blk)[:, None]).astype(I32), axis=1),
                          N_EXPERTS - 1)

    yb = _moe(block_e, n_used.reshape(1), pcounts, slot_tok, hp, w_gate, w_up, w_down, n_blocks)
    out = _combine(dest[0], dest[1], h1, gate.T, yb, ln2_g.reshape(1, d), ln2_b.reshape(1, d), alpha)
    return out.reshape(bsz, seq, d)


def kernel(x, w_in, pool_w, pool_scale, ssm_a_re, ssm_a_im, ssm_log_dt, ssm_b_re, ssm_b_im, ssm_c_re, ssm_c_im, ssm_d, glu_w, glu_b, w_out, ln1_g, ln1_b, router_g_w, router_g_b, router_e_w, router_e_b, w_gate, w_up, w_down, ln2_g, ln2_b):
    depth = w_in.shape[0]
    alpha = (2.0 * depth) ** 0.25
    h = x
    for l in range(depth):
        h = _layer(h, w_in[l], pool_w[l], pool_scale[l], ssm_a_re[l], ssm_a_im[l], ssm_log_dt[l],
                   ssm_b_re[l], ssm_b_im[l], ssm_c_re[l], ssm_c_im[l], ssm_d[l], glu_w[l], glu_b[l],
                   w_out[l], ln1_g[l], ln1_b[l], router_g_w[l], router_g_b[l], router_e_w[l],
                   router_e_b[l], w_gate[l], w_up[l], w_down[l], ln2_g[l], ln2_b[l], alpha)
    return h
```

```python
import functools
import math

import numpy as np
import jax
import jax.numpy as jnp
from jax import lax
from jax.experimental import pallas as pl
from jax.experimental.pallas import tpu as pltpu

F32 = jnp.float32
BF16 = jnp.bfloat16
I32 = jnp.int32
U32 = jnp.uint32

POOL_WINDOWS = (2, 4, 8, 16)
POOL_GROUP = 256
SSM_GROUP = 16
SSM_STATE = 64
N_EXPERT_GROUPS = 4
EXPERTS_PER_GROUP = 8
N_EXPERTS = N_EXPERT_GROUPS * EXPERTS_PER_GROUP
LN_EPS = 1e-5

CHUNK = 16
MOE_BLK = 256
MOE_BLOCKS_PER_STEP = 2
GATHER_BUFS = 4
ROUTER_ROWS = 8 + N_EXPERTS
ROUTER_LANES = 128
LANES = 128
MXU_N = 256
HALO = 16
VMEM_LIMIT = 56 * 1024 * 1024


def _cparams(sem, vmem=VMEM_LIMIT):
    return pltpu.CompilerParams(dimension_semantics=sem, vmem_limit_bytes=vmem)


def _proj_kernel(x_ref, w_ref, pool_ref, ut_ref, accp_ref, accs_ref):
    kk = pl.program_id(1)
    rows, dp = accp_ref.shape
    nj = accs_ref.shape[0]

    xb = x_ref[...].astype(BF16)
    per = MXU_N // LANES

    def accumulate(first):
        for n in range(dp // MXU_N):
            cs = slice(n * MXU_N, (n + 1) * MXU_N)
            part = jnp.dot(xb, w_ref[:, cs], preferred_element_type=F32)
            accp_ref[:, cs] = part if first else accp_ref[:, cs] + part
        for n in range(nj // per):
            part = jnp.dot(xb, w_ref[:, dp + n * MXU_N: dp + (n + 1) * MXU_N], preferred_element_type=F32)
            for q in range(per):
                piece = part[:, q * LANES:(q + 1) * LANES]
                accs_ref[n * per + q] = piece if first else accs_ref[n * per + q] + piece

    @pl.when(kk == 0)
    def _():
        accumulate(True)

    @pl.when(kk > 0)
    def _():
        accumulate(False)

    @pl.when(kk == pl.num_programs(1) - 1)
    def _():
        pool_ref[...] = accp_ref[...].astype(BF16)
        g, L, c, r = ut_ref.shape
        gj = LANES // c
        for s in range(L):
            for j in range(nj):
                piece = accs_ref[j, pl.ds(s, r, stride=L), :]
                ut_ref[j * gj:(j + 1) * gj, s, :, :] = piece.astype(BF16).T.reshape(gj, c, r)


def _proj(xt, w_in, dp, L):
    t, d = xt.shape
    dm = w_in.shape[1]
    g = (dm - dp) // SSM_GROUP
    tm = LANES * L
    kb = 512
    return pl.pallas_call(
        _proj_kernel,
        out_shape=(jax.ShapeDtypeStruct((t, dp), BF16),
                   jax.ShapeDtypeStruct((g, L, SSM_GROUP, t // L), BF16)),
        grid=(t // tm, d // kb),
        in_specs=[pl.BlockSpec((tm, kb), lambda i, k: (i, k)),
                  pl.BlockSpec((kb, dm), lambda i, k: (k, 0))],
        out_specs=(pl.BlockSpec((tm, dp), lambda i, k: (i, 0)),
                   pl.BlockSpec((g, L, SSM_GROUP, LANES), lambda i, k: (0, 0, 0, i))),
        scratch_shapes=[pltpu.VMEM((tm, dp), F32),
                        pltpu.VMEM(((dm - dp) // LANES, tm, LANES), F32)],
        compiler_params=_cparams(("arbitrary", "arbitrary")),
        name="proj",
    )(xt, w_in)


def _pool_kernel(prev_ref, main_ref, next_ref, pw_ref, sc_ref, out_ref, *, seq, sub):
    i = pl.program_id(0)
    ts = main_ref.shape[0]
    ext = jnp.concatenate([prev_ref[...], main_ref[...], next_ref[...]], axis=0)
    k = sub + 2 * HALO
    row = lax.broadcasted_iota(I32, (sub, k), 0)
    col = lax.broadcasted_iota(I32, (sub, k), 1)
    off0 = col - HALO - row
    for sb in range(ts // sub):
        rows = slice(sb * sub, (sb + 1) * sub)
        base = (i * ts + sb * sub) % seq
        src = base + row + off0
        off = jnp.where((src >= 0) & (src < seq), off0, 2 * HALO)
        pos = base + lax.broadcasted_iota(I32, (sub, 1), 0)
        for g, w in enumerate(POOL_WINDOWS):
            sl = slice(g * POOL_GROUP, (g + 1) * POOL_GROUP)
            band = (off + w // 2).astype(U32) < w
            bm = jnp.where(band, 1.0, 0.0).astype(BF16)
            sums = jnp.dot(bm, ext[sb * sub:sb * sub + k, sl], preferred_element_type=F32)
            lo = jnp.maximum(pos - w // 2, 0)
            hi = jnp.minimum(pos - w // 2 + w, seq)
            inv = 1.0 / (hi - lo).astype(F32)
            dlt = sums * inv - main_ref[rows, sl].astype(F32)
            y = jnp.dot(dlt.astype(BF16), pw_ref[g], preferred_element_type=F32) * sc_ref[:, sl]
            out_ref[rows, sl] = y.astype(BF16)


def _pool(pp, pool_w, pool_scale, seq):
    t, dp = pp.shape
    ts = min(512, seq)
    nh = t // HALO
    per = ts // HALO
    return pl.pallas_call(
        functools.partial(_pool_kernel, seq=seq, sub=256),
        out_shape=jax.ShapeDtypeStruct((t, dp), BF16),
        grid=(t // ts,),
        in_specs=[pl.BlockSpec((HALO, dp), lambda i: (jnp.maximum(i * per - 1, 0), 0)),
                  pl.BlockSpec((ts, dp), lambda i: (i, 0)),
                  pl.BlockSpec((HALO, dp), lambda i: (jnp.minimum((i + 1) * per, nh - 1), 0)),
                  pl.BlockSpec(pool_w.shape, lambda i: (0, 0, 0)),
                  pl.BlockSpec((1, dp), lambda i: (0, 0))],
        out_specs=pl.BlockSpec((ts, dp), lambda i: (i, 0)),
        compiler_params=_cparams(("arbitrary",)),
        name="pool",
    )(pp, pp, pp, pool_w, pool_scale)


def _ssm_tables(a_re, a_im, log_dt, b_re, b_im, c_re, c_im, d_skip, L):
    g = a_re.shape[1]
    p = a_re.shape[2]
    c = b_re.shape[3]
    fl = L * c
    lr = a_re.astype(F32)
    li = a_im.astype(F32)
    dt = jnp.exp(log_dt.astype(F32))[..., None]
    mag = jnp.exp(lr * dt)
    abr = mag * jnp.cos(li * dt)
    abi = mag * jnp.sin(li * dt)
    den = lr * lr + li * li
    zr = ((abr - 1.0) * lr + abi * li) / den
    zi = (abi * lr - (abr - 1.0) * li) / den
    br = b_re.astype(F32)
    bi = b_im.astype(F32)
    bbr = zr[..., None] * br - zi[..., None] * bi
    bbi = zr[..., None] * bi + zi[..., None] * br
    cr = c_re.astype(F32)
    ci = c_im.astype(F32)
    kk = jnp.arange(L + 1, dtype=F32)[None, None, :, None]
    pm = jnp.exp(kk * (lr * dt)[:, :, None, :])
    ang = kk * (li * dt)[:, :, None, :]
    pr = pm * jnp.cos(ang)
    pi = pm * jnp.sin(ang)

    skip = jnp.eye(c, dtype=F32)[None] * d_skip.astype(F32).reshape(g, c, 1)
    lag0 = jnp.einsum('dgop,dgpc->goc', cr, bbr, precision=lax.Precision.HIGHEST) \
        - jnp.einsum('dgop,dgpc->goc', ci, bbi, precision=lax.Precision.HIGHEST) + skip
    lag0 = jnp.pad(lag0, ((0, 0), (0, 0), (0, LANES - c)))
    bb = jnp.concatenate([bbr[0], bbi[0], bbr[1], bbi[1]], axis=1)
    bb = jnp.pad(bb, ((0, 0), (0, 0), (0, LANES - c)))

    def w1(d, qr, qi):
        qr = qr.transpose(0, 2, 1)[..., None]
        qi = qi.transpose(0, 2, 1)[..., None]
        re = qr * bbr[d][:, :, None] - qi * bbi[d][:, :, None]
        im = qr * bbi[d][:, :, None] + qi * bbr[d][:, :, None]
        to = lambda v: v.reshape(g, p, fl)
        return to(re), to(im)
    f_re, f_im = w1(0, pr[0][:, L - 1::-1], pi[0][:, L - 1::-1])
    b_re_, b_im_ = w1(1, pr[1][:, :L], pi[1][:, :L])
    w1t = jnp.concatenate([f_re, f_im, b_re_, b_im_], axis=1)

    def w2(d, qr, qi):
        re = cr[d][:, None] * qr[:, :, None] - ci[d][:, None] * qi[:, :, None]
        im = -(cr[d][:, None] * qi[:, :, None] + ci[d][:, None] * qr[:, :, None])
        to = lambda v: v.reshape(g, fl, p)
        return to(re), to(im)
    rf_re, rf_im = w2(0, pr[0][:, 1:], pi[0][:, 1:])
    rb_re, rb_im = w2(1, pr[1][:, :0:-1], pi[1][:, :0:-1])
    w2t = jnp.concatenate([rf_re, rf_im, rb_re, rb_im], axis=2)

    a4 = jnp.stack([pr[0][:, L], pi[0][:, L], pr[1][:, L], pi[1][:, L]], axis=1)
    ap = a4.reshape(g // 2, 2, 4, p).transpose(0, 2, 1, 3).reshape(g // 2, 4, 2 * p)
    ap = jnp.pad(ap, ((0, 0), (0, 4), (0, 0)))
    return lag0, bb.astype(BF16), w1t.astype(BF16), w2t.astype(BF16), ap


def _cmul(ar, ai, br, bi):
    return ar * br - ai * bi, ar * bi + ai * br


def _ssm_kernel(ut_ref, w1_ref, lag0_ref, bb_ref, w2_ref, a_ref, yt_ref, mt_s, kk_s, zt_s, ht_s, *, nseg, L):
    p = SSM_STATE
    c = SSM_GROUP
    fl, n = ut_ref.shape[1:]
    nb = n // nseg
    nrow = nseg // 8
    st = [jnp.dot(w1_ref[q], ut_ref[q], preferred_element_type=F32) for q in range(2)]
    z = jnp.concatenate([st[q][k * p:(k + 1) * p] for k in range(4) for q in range(2)], axis=0)
    zt_s[...] = z.T
    per = LANES // c
    for q in range(2):
        kf = jnp.dot(w2_ref[q, :, 0:2 * p], bb_ref[q, 0:2 * p, :], preferred_element_type=F32)
        kb = jnp.dot(w2_ref[q, :, 2 * p:4 * p], bb_ref[q, 2 * p:4 * p, :], preferred_element_type=F32)
        tiles = [None] * (2 * L // per)
        for j in range(2 * L - 1):
            lag = L - 1 - j
            if lag > 0:
                blk = kf[(lag - 1) * c:lag * c, :]
            elif lag == 0:
                blk = lag0_ref[q]
            else:
                blk = kb[(L + lag) * c:(L + lag + 1) * c, :]
            sh = (j % per) * c
            blk = blk if sh == 0 else pltpu.roll(blk, sh, 1)
            tiles[j // per] = blk if tiles[j // per] is None else tiles[j // per] + blk
        kk_s[...] = jnp.concatenate(tiles, axis=1).astype(BF16)
        for t in range(L):
            mt_s[q, t * c:(t + 1) * c, :] = kk_s[:, (L - 1 - t) * c:(L - 1 - t) * c + fl]

    sub = lax.broadcasted_iota(I32, (8, LANES), 0)
    full = lambda v: jnp.broadcast_to(v, (8, LANES))
    one, zero = jnp.ones((8, LANES), F32), jnp.zeros((8, LANES), F32)

    def powers(row):
        out = [(full(a_ref[0, row:row + 1, :]), full(a_ref[0, row + 1:row + 2, :]))]
        for _ in range(3):
            out.append(_cmul(*out[-1], *out[-1]))
        return out

    def by_bits(pw, idx):
        qr, qi = one, zero
        for bit in range(3):
            on = (idx >> bit) & 1 == 1
            qr, qi = _cmul(qr, qi, jnp.where(on, pw[bit][0], one), jnp.where(on, pw[bit][1], zero))
        return qr, qi

    pw_f, pw_b = powers(0), powers(2)
    qf = by_bits(pw_f, sub)
    qb = by_bits(pw_b, 7 - sub)

    def local(xr, xi, pw, forward):
        for lvl, d in enumerate((1, 2, 4)):
            keep = (sub >= d) if forward else (sub < 8 - d)
            sh = d if forward else 8 - d
            tr = jnp.where(keep, pltpu.roll(xr, sh, 0), 0.0)
            ti = jnp.where(keep, pltpu.roll(xi, sh, 0), 0.0)
            mr, mi = _cmul(pw[lvl][0], pw[lvl][1], tr, ti)
            xr, xi = xr + mr, xi + mi
        return xr, xi

    def step(i, carry):
        new = []
        for b in range(nb):
            for fwd in (True, False):
                cr, ci = carry[2 * (2 * b + (0 if fwd else 1)):][:2]
                row = b * nseg + (i if fwd else nrow - 1 - i) * 8
                rows = pl.ds(pl.multiple_of(row, 8), 8)
                l0 = 0 if fwd else 2 * LANES
                pw, (qr, qi) = (pw_f, qf) if fwd else (pw_b, qb)
                lr, li = local(zt_s[rows, l0:l0 + LANES], zt_s[rows, l0 + LANES:l0 + 2 * LANES], pw, fwd)
                keep = (sub >= 1) if fwd else (sub < 7)
                er = jnp.where(keep, pltpu.roll(lr, 1 if fwd else 7, 0), 0.0)
                ei = jnp.where(keep, pltpu.roll(li, 1 if fwd else 7, 0), 0.0)
                mr, mi = _cmul(qr, qi, cr, ci)
                ht_s[rows, l0:l0 + LANES] = er + mr
                ht_s[rows, l0 + LANES:l0 + 2 * LANES] = ei + mi
                edge = 7 if fwd else 0
                mr, mi = _cmul(pw[3][0], pw[3][1], cr, ci)
                new += [full(lr[edge:edge + 1, :]) + mr, full(li[edge:edge + 1, :]) + mi]
        return tuple(new)

    lax.fori_loop(0, nrow, step, (zero,) * (4 * nb))

    h = ht_s[...].astype(BF16).T
    for q in range(2):
        hq = jnp.concatenate([h[(2 * k + q) * p:(2 * k + q + 1) * p] for k in range(4)], axis=0)
        y = jnp.dot(mt_s[q], ut_ref[q], preferred_element_type=F32) \
            + jnp.dot(w2_ref[q], hq, preferred_element_type=F32)
        yt_ref[q] = y.astype(BF16)


def _ssm(ut3, lag0, bb, w1t, w2t, ap, nseg, L):
    g, fl, n = ut3.shape
    sp = w1t.shape[1]
    return pl.pallas_call(
        functools.partial(_ssm_kernel, nseg=nseg, L=L),
        out_shape=jax.ShapeDtypeStruct((g, fl, n), BF16),
        grid=(g // 2,),
        in_specs=[pl.BlockSpec((2, fl, n), lambda i: (i, 0, 0)),
                  pl.BlockSpec((2, sp, fl), lambda i: (i, 0, 0)),
                  pl.BlockSpec((2,) + lag0.shape[1:], lambda i: (i, 0, 0)),
                  pl.BlockSpec((2,) + bb.shape[1:], lambda i: (i, 0, 0)),
                  pl.BlockSpec((2, fl, sp), lambda i: (i, 0, 0)),
                  pl.BlockSpec((1,) + ap.shape[1:], lambda i: (i, 0, 0))],
        out_specs=pl.BlockSpec((2, fl, n), lambda i: (i, 0, 0)),
        scratch_shapes=[pltpu.VMEM((2, fl, fl), BF16),
                        pltpu.VMEM((SSM_GROUP, 2 * fl), BF16),
                        pltpu.VMEM((n, 2 * sp), F32),
                        pltpu.VMEM((n, 2 * sp), F32)],
        compiler_params=_cparams(("arbitrary",)),
        name="ssm",
    )(ut3, w1t, lag0, bb, w2t, ap)


def _glu_kernel(yt_ref, w_ref, b_ref, out_ref, nat_ref, *, sub):
    g, L, c, r = yt_ref.shape
    nj = nat_ref.shape[0]
    gj = LANES // c
    for t in range(L):
        for j in range(nj):
            piece = yt_ref[j * gj:(j + 1) * gj, t, :, :].reshape(LANES, r).astype(F32)
            nat_ref[j, pl.ds(t, r, stride=L), :] = piece.T

    def body(k, carry):
        sl = pl.ds(pl.multiple_of(k * sub, sub), sub)
        y = jnp.concatenate([nat_ref[j, sl, :] for j in range(nj)], axis=1)
        ya = jax.nn.gelu(y, approximate=True)
        z = jnp.dot(ya.astype(BF16), w_ref[...], preferred_element_type=F32) + b_ref[...]
        out_ref[sl, :] = (ya * (1.0 / (1.0 + jnp.exp(-z)))).astype(BF16)
        return carry
    lax.fori_loop(0, nat_ref.shape[1] // sub, body, 0)


def _glu(yt4, glu_w, glu_b):
    g, L, c, n = yt4.shape
    ds = g * c
    r = LANES
    return pl.pallas_call(
        functools.partial(_glu_kernel, sub=512),
        out_shape=jax.ShapeDtypeStruct((n * L, ds), BF16),
        grid=(n // r,),
        in_specs=[pl.BlockSpec((g, L, c, r), lambda j: (0, 0, 0, j)),
                  pl.BlockSpec((ds, ds), lambda j: (0, 0)),
                  pl.BlockSpec((1, ds), lambda j: (0, 0))],
        out_specs=pl.BlockSpec((r * L, ds), lambda j: (j, 0)),
        scratch_shapes=[pltpu.VMEM((ds // LANES, r * L, LANES), F32)],
        compiler_params=_cparams(("arbitrary",)),
        name="glu",
    )(yt4, glu_w, glu_b)


def _layer_norm(r, g, b):
    mu = jnp.mean(r, axis=-1, keepdims=True)
    cen = r - mu
    var = jnp.mean(cen * cen, axis=-1, keepdims=True)
    return cen * lax.rsqrt(var + LN_EPS) * g + b


def _store_row_packed(ref, m, row0=0):
    rows, width = m.shape
    half = width // 2
    nc = half // LANES
    lo = lax.bitcast_convert_type(m[:, :half].astype(BF16).astype(F32), U32) >> 16
    hi = lax.bitcast_convert_type(m[:, half:].astype(BF16).astype(F32), U32) & jnp.uint32(0xFFFF0000)
    pk = lo | hi
    for c in range(nc):
        ref[pl.ds(row0 * nc + c, rows, stride=nc), :] = pk[:, c * LANES:(c + 1) * LANES]


def _load_row_packed(ref, rows, nc, row0=0):
    los, his = [], []
    for c in range(nc):
        u = ref[pl.ds(row0 * nc + c, rows, stride=nc), :]
        los.append(lax.bitcast_convert_type(u << 16, F32))
        his.append(lax.bitcast_convert_type(u & jnp.uint32(0xFFFF0000), F32))
    return los, his


def _outproj_kernel(yp_ref, ys_ref, x_ref, wo_ref, g_ref, b_ref, rw_ref, rb_ref,
                    h_ref, hp_ref, lt_ref, *, alpha, dp, sub):
    for sb in range(h_ref.shape[0] // sub):
        sl = pl.ds(sb * sub, sub)
        mix = jnp.dot(yp_ref[sl, :], wo_ref[0:dp, :], preferred_element_type=F32) \
            + jnp.dot(ys_ref[sl, :], wo_ref[dp:, :], preferred_element_type=F32)
        h = _layer_norm(alpha * x_ref[sl, :] + mix, g_ref[...], b_ref[...])
        h_ref[sl, :] = h
        _store_row_packed(hp_ref, h, row0=sb * sub)
        hh = h.astype(BF16)
        hl = (h - hh.astype(F32)).astype(BF16)
        part = jnp.dot(hh, rw_ref[...], preferred_element_type=F32) \
            + jnp.dot(hl, rw_ref[...], preferred_element_type=F32)
        logits = part + pltpu.roll(part, ROUTER_LANES // 2, 1) + rb_ref[...]
        lt_ref[:, sb * sub:(sb + 1) * sub] = logits.T[0:ROUTER_ROWS, :]


def _outproj(y_pool, y_ssm, xt, w_out, ln_g, ln_b, rw, rb, alpha):
    t, d = xt.shape
    dp = y_pool.shape[1]
    tm = min(512, t)
    pr = d // 2 // LANES
    return pl.pallas_call(
        functools.partial(_outproj_kernel, alpha=alpha, dp=dp, sub=256),
        out_shape=(jax.ShapeDtypeStruct((t, d), F32),
                   jax.ShapeDtypeStruct((t * pr, LANES), U32),
                   jax.ShapeDtypeStruct((ROUTER_ROWS, t), F32)),
        grid=(t // tm,),
        in_specs=[pl.BlockSpec((tm, dp), lambda i: (i, 0)),
                  pl.BlockSpec((tm, y_ssm.shape[1]), lambda i: (i, 0)),
                  pl.BlockSpec((tm, d), lambda i: (i, 0)),
                  pl.BlockSpec(w_out.shape, lambda i: (0, 0)),
                  pl.BlockSpec((1, d), lambda i: (0, 0)),
                  pl.BlockSpec((1, d), lambda i: (0, 0)),
                  pl.BlockSpec(rw.shape, lambda i: (0, 0)),
                  pl.BlockSpec((1, ROUTER_LANES), lambda i: (0, 0))],
        out_specs=(pl.BlockSpec((tm, d), lambda i: (i, 0)),
                   pl.BlockSpec((tm * pr, LANES), lambda i: (i, 0)),
                   pl.BlockSpec((ROUTER_ROWS, tm), lambda i: (0, i))),
        compiler_params=_cparams(("arbitrary",)),
        name="outproj",
    )(y_pool, y_ssm, xt, w_out, ln_g, ln_b, rw, rb)


def _router_kernel(lt_ref, eid_ref, rank_ref, gate_ref, cnt_ref, carry_ref):
    i = pl.program_id(0)
    ne, epg = N_EXPERTS, EXPERTS_PER_GROUP

    @pl.when(i == 0)
    def _():
        carry_ref[...] = jnp.zeros_like(carry_ref)

    lt = lt_ref[...]
    tt = lt.shape[1]
    gl = [lt[j:j + 1, :] for j in range(N_EXPERT_GROUPS)]
    gmax = jnp.maximum(jnp.maximum(gl[0], gl[1]), jnp.maximum(gl[2], gl[3]))
    grp = jnp.where(gl[0] == gmax, 0, jnp.where(gl[1] == gmax, 1, jnp.where(gl[2] == gmax, 2, 3)))
    p_grp = 1.0 / (jnp.exp(gl[0] - gmax) + jnp.exp(gl[1] - gmax)
                   + jnp.exp(gl[2] - gmax) + jnp.exp(gl[3] - gmax))
    eg = [lt[8 + epg * j: 8 + epg * (j + 1), :] for j in range(N_EXPERT_GROUPS)]
    el = jnp.where(grp == 0, eg[0], jnp.where(grp == 1, eg[1], jnp.where(grp == 2, eg[2], eg[3])))
    sub = lax.broadcasted_iota(I32, (epg, tt), 0)
    m1 = jnp.max(el, axis=0, keepdims=True)
    i1 = jnp.min(jnp.where(el == m1, sub, epg), axis=0, keepdims=True)
    rest = jnp.where(sub == i1, -jnp.inf, el)
    m2 = jnp.max(rest, axis=0, keepdims=True)
    i2 = jnp.min(jnp.where(rest == m2, sub, epg), axis=0, keepdims=True)
    r21 = jnp.exp(m2 - m1)
    g1 = p_grp / (1.0 + r21)
    g2 = g1 * r21
    e1 = grp * epg + i1
    e2 = grp * epg + i2

    rows = lax.broadcasted_iota(I32, (ne, tt), 0)
    oh1 = rows == e1
    oh2 = rows == e2
    oh = jnp.where(oh1 | oh2, 1.0, 0.0)
    tri = jnp.where(lax.broadcasted_iota(I32, (tt, tt), 0) < lax.broadcasted_iota(I32, (tt, tt), 1),
                    1.0, 0.0).astype(BF16)
    before = jnp.dot(oh.astype(BF16), tri, preferred_element_type=F32) + carry_ref[:, 0:1]
    r1 = jnp.sum(jnp.where(oh1, before, 0.0), axis=0, keepdims=True)
    r2 = jnp.sum(jnp.where(oh2, before, 0.0), axis=0, keepdims=True)
    carry_ref[...] = carry_ref[...] + jnp.sum(oh, axis=1, keepdims=True)

    eid_ref[...] = jnp.concatenate([e1, e2], axis=0)
    rank_ref[...] = jnp.concatenate([r1, r2], axis=0).astype(I32)
    gate_ref[...] = jnp.concatenate([g1, g2], axis=0)
    cnt_ref[...] = carry_ref[...].astype(I32)


def _router(lt):
    rr, t = lt.shape
    tt = min(512, t)
    return pl.pallas_call(
        _router_kernel,
        out_shape=(jax.ShapeDtypeStruct((2, t), I32),
                   jax.ShapeDtypeStruct((2, t), I32),
                   jax.ShapeDtypeStruct((2, t), F32),
                   jax.ShapeDtypeStruct((N_EXPERTS, 128), I32)),
        grid=(t // tt,),
        in_specs=[pl.BlockSpec((rr, tt), lambda i: (0, i))],
        out_specs=(pl.BlockSpec((2, tt), lambda i: (0, i)),
                   pl.BlockSpec((2, tt), lambda i: (0, i)),
                   pl.BlockSpec((2, tt), lambda i: (0, i)),
                   pl.BlockSpec((N_EXPERTS, 128), lambda i: (0, 0))),
        scratch_shapes=[pltpu.VMEM((N_EXPERTS, 128), F32)],
        compiler_params=_cparams(("arbitrary",)),
        name="router",
    )(lt)


def _moe_kernel(be_ref, nu_ref, pc_ref, cnt_ref, dest_ref, hp_hbm, wg_hbm, wu_hbm, wd_hbm, y_ref,
                gbuf, wgf, wuf, wdf, wgb, wub, wdb, par_ref, st_ref, gsem, wsem):
    @pl.when(pl.program_id(0) == 0)
    def _():
        def place(i, carry):
            d = dest_ref[i]
            st_ref[d & 0xFFFF] = i
            st_ref[lax.shift_right_logical(d, 16)] = i
            return carry
        lax.fori_loop(0, dest_ref.shape[0], place, 0, unroll=8)

        def pad(e, start):
            def fill(s, carry):
                st_ref[s] = 0
                return carry
            lax.fori_loop(start + cnt_ref[e], start + pc_ref[e], fill, 0)
            return start + pc_ref[e]
        lax.fori_loop(0, N_EXPERTS, pad, 0)

    for half in range(MOE_BLOCKS_PER_STEP):
        _moe_block(pl.program_id(0) * MOE_BLOCKS_PER_STEP + half, half * MOE_BLK,
                   be_ref, nu_ref, pc_ref, st_ref, hp_hbm, wg_hbm, wu_hbm, wd_hbm, y_ref,
                   gbuf, wgf, wuf, wdf, wgb, wub, wdb, par_ref, gsem, wsem)


def _moe_block(b, row0, be_ref, nu_ref, pc_ref, st_ref, hp_hbm, wg_hbm, wu_hbm, wd_hbm, y_ref,
               gbuf, wgf, wuf, wdf, wgb, wub, wdb, par_ref, gsem, wsem):
    nu = nu_ref[0]
    pr = gbuf.shape[1] // MOE_BLK
    blk = MOE_BLK
    nbuf = gbuf.shape[0]
    look = nbuf - 1

    def row_copy(tok, slot, r):
        src = hp_hbm.at[pl.ds(pl.multiple_of(tok * pr, pr), pr)]
        off = r * pr if isinstance(r, int) else pl.multiple_of(r * pr, pr)
        return pltpu.make_async_copy(src, gbuf.at[slot, pl.ds(off, pr)], gsem.at[slot])

    def issue(block, slot):
        base = block * blk

        def body(r, carry):
            row_copy(st_ref[base + r], slot, r).start()
            return carry
        lax.fori_loop(0, blk, body, 0, unroll=8)

    def weight_copies(e, p):
        return (pltpu.make_async_copy(wg_hbm.at[e], wgf.at[p], wsem.at[p]),
                pltpu.make_async_copy(wu_hbm.at[e], wuf.at[p], wsem.at[p]),
                pltpu.make_async_copy(wd_hbm.at[e], wdf.at[p], wsem.at[p]))

    @pl.when(b == 0)
    def _():
        par_ref[0] = 0
        for cp in weight_copies(be_ref[0], 0):
            cp.start()
        issue(0, 0)
        for j in range(1, look):
            @pl.when(j < nu)
            def _():
                issue(j, j)

    def issue_part(k):
        base = (b + look) * blk
        for r in range(k * blk // 4, (k + 1) * blk // 4):
            row_copy(st_ref[base + r], (b + look) % nbuf, r).start(priority=1)

    def expert_mlp(slot, with_issue):
        los, his = _load_row_packed(gbuf.at[slot], blk, pr)
        x = jnp.concatenate([v.astype(BF16) for v in los + his], axis=1)
        if with_issue:
            issue_part(0)
        hg = jnp.dot(x, wgb[...], preferred_element_type=F32)
        if with_issue:
            issue_part(1)
        hu = jnp.dot(x, wub[...], preferred_element_type=F32)
        if with_issue:
            issue_part(2)
        hh = hg * (1.0 / (1.0 + jnp.exp(-hg))) * hu
        y = jnp.dot(hh.astype(BF16), wdb[...], preferred_element_type=F32)
        if with_issue:
            issue_part(3)
        _store_row_packed(y_ref, y, row0=row0)

    @pl.when(b < nu)
    def _():
        slot = b % nbuf
        e = be_ref[b]
        first = (b == 0) | (e != be_ref[jnp.maximum(b - 1, 0)])

        @pl.when(first)
        def _():
            p = par_ref[0]
            for cp in weight_copies(e, p):
                cp.wait()
            nxt = lax.while_loop(
                lambda c: (c < N_EXPERTS) & (pc_ref[jnp.minimum(c, N_EXPERTS - 1)] == 0),
                lambda c: c + 1, e + 1)

            @pl.when(nxt < N_EXPERTS)
            def _():
                for cp in weight_copies(nxt, 1 - p):
                    cp.start()
            wgb[...] = wgf[p].astype(BF16)
            wub[...] = wuf[p].astype(BF16)
            wdb[...] = wdf[p].astype(BF16)
            par_ref[0] = 1 - p

        pltpu.make_async_copy(hp_hbm.at[pl.ds(0, blk * pr)], gbuf.at[slot], gsem.at[slot]).wait()

        @pl.when(b + look < nu)
        def _():
            expert_mlp(slot, True)

        @pl.when(b + look >= nu)
        def _():
            expert_mlp(slot, False)

    @pl.when(b >= nu)
    def _():
        y_ref[pl.ds(row0 * pr, blk * pr), :] = jnp.zeros((blk * pr, LANES), U32)


def _moe(block_e, n_used, pcounts, counts, dest, hp, w_gate, w_up, w_down, n_blocks):
    ne, d, de = w_gate.shape
    blk = MOE_BLK
    pr = d // 2 // LANES
    bps = MOE_BLOCKS_PER_STEP
    assert n_blocks % bps == 0 and n_blocks * blk <= 1 << 16
    dest_packed = dest[0] | (dest[1] << 16)
    grid_spec = pltpu.PrefetchScalarGridSpec(
        num_scalar_prefetch=5,
        grid=(n_blocks // bps,),
        in_specs=[pl.BlockSpec(memory_space=pl.ANY)] * 4,
        out_specs=pl.BlockSpec((bps * blk * pr, LANES), lambda b, *_: (b, 0)),
        scratch_shapes=[pltpu.VMEM((GATHER_BUFS, blk * pr, LANES), U32),
                        pltpu.VMEM((2, d, de), F32),
                        pltpu.VMEM((2, d, de), F32),
                        pltpu.VMEM((2, de, d), F32),
                        pltpu.VMEM((d, de), BF16),
                        pltpu.VMEM((d, de), BF16),
                        pltpu.VMEM((de, d), BF16),
                        pltpu.SMEM((1,), I32),
                        pltpu.SMEM((n_blocks * blk,), I32),
                        pltpu.SemaphoreType.DMA((GATHER_BUFS,)),
                        pltpu.SemaphoreType.DMA((2,))],
    )
    return pl.pallas_call(
        _moe_kernel,
        out_shape=jax.ShapeDtypeStruct((n_blocks * blk * pr, LANES), U32),
        grid_spec=grid_spec,
        compiler_params=_cparams(("arbitrary",)),
        name="moe",
    )(block_e, n_used, pcounts, counts, dest_packed, hp, w_gate, w_up, w_down)


def _combine_kernel(d0_ref, d1_ref, h_ref, gate_ref, yb_hbm, g_ref, b_ref, out_ref, buf, sem, *, alpha):
    i = pl.program_id(0)
    n = pl.num_programs(0)
    tm = h_ref.shape[0]
    pr = buf.shape[2] // tm

    def issue(tile, slot):
        base = tile * tm

        def body(r, carry):
            dst = pl.ds(pl.multiple_of(r * pr, pr), pr)
            for k, dref in enumerate((d0_ref, d1_ref)):
                src = yb_hbm.at[pl.ds(pl.multiple_of(dref[base + r] * pr, pr), pr)]
                pltpu.make_async_copy(src, buf.at[slot, k, dst], sem.at[slot]).start()
            return carry
        lax.fori_loop(0, tm, body, 0, unroll=8)

    nbuf = buf.shape[0]
    look = nbuf - 1

    @pl.when(i == 0)
    def _():
        issue(0, 0)
        for j in range(1, look):
            @pl.when(j < n)
            def _():
                issue(j, j)

    slot = i % nbuf
    for k in range(2):
        pltpu.make_async_copy(yb_hbm.at[pl.ds(0, tm * pr)], buf.at[slot, k], sem.at[slot]).wait()

    nq = 4
    sub = tm // nq

    def issue_part(q):
        base = (i + look) * tm
        nslot = (i + look) % nbuf
        for r in range(q * sub, (q + 1) * sub):
            for k, dref in enumerate((d0_ref, d1_ref)):
                src = yb_hbm.at[pl.ds(pl.multiple_of(dref[base + r] * pr, pr), pr)]
                pltpu.make_async_copy(src, buf.at[nslot, k, pl.ds(r * pr, pr)], sem.at[nslot]).start(priority=k)

    def finish(with_issue):
        for q in range(nq):
            if with_issue:
                issue_part(q)
            rows = pl.ds(q * sub, sub)
            lo0, hi0 = _load_row_packed(buf.at[slot, 0], sub, pr, row0=q * sub)
            lo1, hi1 = _load_row_packed(buf.at[slot, 1], sub, pr, row0=q * sub)
            g0 = gate_ref[rows, 0:1]
            g1 = gate_ref[rows, 1:2]
            ffn = jnp.concatenate([g0 * a + g1 * c for a, c in zip(lo0 + hi0, lo1 + hi1)], axis=1)
            out_ref[rows, :] = _layer_norm(alpha * h_ref[rows, :] + ffn, g_ref[...], b_ref[...])

    @pl.when(i + look < n)
    def _():
        finish(True)

    @pl.when(i + look >= n)
    def _():
        finish(False)


def _combine(dest0, dest1, h1, gate_t, yb, ln_g, ln_b, alpha):
    t, d = h1.shape
    tm = min(256, t)
    grid_spec = pltpu.PrefetchScalarGridSpec(
        num_scalar_prefetch=2,
        grid=(t // tm,),
        in_specs=[pl.BlockSpec((tm, d), lambda i, d0, d1: (i, 0)),
                  pl.BlockSpec((tm, 2), lambda i, d0, d1: (i, 0)),
                  pl.BlockSpec(memory_space=pl.ANY),
                  pl.BlockSpec((1, d), lambda i, d0, d1: (0, 0)),
                  pl.BlockSpec((1, d), lambda i, d0, d1: (0, 0))],
        out_specs=pl.BlockSpec((tm, d), lambda i, d0, d1: (i, 0)),
        scratch_shapes=[pltpu.VMEM((GATHER_BUFS, 2, tm * (d // 2 // LANES), LANES), U32),
                        pltpu.SemaphoreType.DMA((GATHER_BUFS,))],
    )
    return pl.pallas_call(
        functools.partial(_combine_kernel, alpha=alpha),
        out_shape=jax.ShapeDtypeStruct((t, d), F32),
        grid_spec=grid_spec,
        compiler_params=_cparams(("arbitrary",)),
        name="combine",
    )(dest0, dest1, h1, gate_t, yb, ln_g, ln_b)


def _layer(h, w_in, pool_w, pool_scale, a_re, a_im, log_dt, b_re, b_im, c_re, c_im, d_skip,
           glu_w, glu_b, w_out, ln1_g, ln1_b, rg_w, rg_b, re_w, re_b, w_gate, w_up, w_down,
           ln2_g, ln2_b, alpha):
    bsz, seq, d = h.shape
    t = bsz * seq
    L = CHUNK
    dp = pool_w.shape[0] * pool_w.shape[1]
    ds = w_in.shape[1] - dp
    g = ds // SSM_GROUP
    n = t // L

    lag0, bb, w1t, w2t, al = _ssm_tables(a_re, a_im, log_dt, b_re, b_im, c_re, c_im, d_skip, L)
    half = ROUTER_LANES // 2
    zpad = lambda k: jnp.zeros((d, k), F32)
    rw = jnp.concatenate([rg_w.astype(F32), zpad(8 - N_EXPERT_GROUPS), re_w.astype(F32),
                          zpad(half - ROUTER_ROWS)], axis=1)
    rw_hi = rw.astype(BF16)
    rw_lo = (rw - rw_hi.astype(F32)).astype(BF16)
    rw2 = jnp.concatenate([rw_hi, rw_lo], axis=1)
    rb = jnp.concatenate([rg_b.astype(F32), jnp.zeros((8 - N_EXPERT_GROUPS,), F32), re_b.astype(F32),
                          jnp.zeros((ROUTER_LANES - ROUTER_ROWS,), F32)]).reshape(1, ROUTER_LANES)

    xt = h.reshape(t, d)
    pool_p, ut = _proj(xt, w_in.astype(BF16), dp, L)
    y_pool = _pool(pool_p, pool_w.astype(BF16), pool_scale.reshape(1, dp).astype(F32), seq)
    yt = _ssm(ut.reshape(g, L * SSM_GROUP, n), lag0, bb, w1t, w2t, al, seq // L, L)
    y_ssm = _glu(yt.reshape(g, L, SSM_GROUP, n), glu_w.astype(BF16), glu_b.reshape(1, ds).astype(F32))
    h1, hp, lt = _outproj(y_pool, y_ssm, xt, w_out.astype(BF16), ln1_g.reshape(1, d), ln1_b.reshape(1, d),
                          rw2, rb, alpha)
    eid, rank, gate, cnt = _router(lt)

    blk = MOE_BLK
    m = 2 * t
    n_blocks = -(-m // blk) + N_EXPERTS
    counts = cnt[:, 0]
    pcounts = (counts + blk - 1) // blk * blk
    pends = jnp.cumsum(pcounts)
    pstarts = pends - pcounts
    e_ids = jnp.arange(N_EXPERTS, dtype=I32)
    dest = jnp.sum(jnp.where(eid[..., None] == e_ids, pstarts, 0), axis=-1) + rank
    n_used = (pends[-1] // blk).astype(I32)
    bidx = jnp.minimum(jnp.arange(n_blocks, dtype=I32), n_used - 1)
    block_e = jnp.minimum(jnp.sum((pends[None, :] <= (bidx * blk)[:, None]).astype(I32), axis=1),
                          N_EXPERTS - 1)

    yb = _moe(block_e, n_used.reshape(1), pcounts, counts, dest, hp, w_gate, w_up, w_down, n_blocks)
    out = _combine(dest[0], dest[1], h1, gate.T, yb, ln2_g.reshape(1, d), ln2_b.reshape(1, d), alpha)
    return out.reshape(bsz, seq, d)


def kernel(x, w_in, pool_w, pool_scale, ssm_a_re, ssm_a_im, ssm_log_dt, ssm_b_re, ssm_b_im, ssm_c_re, ssm_c_im, ssm_d, glu_w, glu_b, w_out, ln1_g, ln1_b, router_g_w, router_g_b, router_e_w, router_e_b, w_gate, w_up, w_down, ln2_g, ln2_b):
    depth = w_in.shape[0]
    alpha = (2.0 * depth) ** 0.25
    h = x
    for l in range(depth):
        h = _layer(h, w_in[l], pool_w[l], pool_scale[l], ssm_a_re[l], ssm_a_im[l], ssm_log_dt[l],
                   ssm_b_re[l], ssm_b_im[l], ssm_c_re[l], ssm_c_im[l], ssm_d[l], glu_w[l], glu_b[l],
                   w_out[l], ln1_g[l], ln1_b[l], router_g_w[l], router_g_b[l], router_e_w[l],
                   router_e_b[l], w_gate[l], w_up[l], w_down[l], ln2_g[l], ln2_b[l], alpha)
    return h
```

```python
import functools
import math

import numpy as np
import jax
import jax.numpy as jnp
from jax import lax
from jax.experimental import pallas as pl
from jax.experimental.pallas import tpu as pltpu

F32 = jnp.float32
BF16 = jnp.bfloat16
I32 = jnp.int32
U32 = jnp.uint32

POOL_WINDOWS = (2, 4, 8, 16)
POOL_GROUP = 256
SSM_GROUP = 16
SSM_STATE = 64
N_EXPERT_GROUPS = 4
EXPERTS_PER_GROUP = 8
N_EXPERTS = N_EXPERT_GROUPS * EXPERTS_PER_GROUP
LN_EPS = 1e-5

CHUNK = 16
MOE_BLK = 256
MOE_BLOCKS_PER_STEP = 2
GATHER_BUFS = 4
ROUTER_ROWS = 8 + N_EXPERTS
ROUTER_LANES = 128
LANES = 128
MXU_N = 256
HALO = 16
VMEM_LIMIT = 56 * 1024 * 1024


def _cparams(sem, vmem=VMEM_LIMIT):
    return pltpu.CompilerParams(dimension_semantics=sem, vmem_limit_bytes=vmem)


def _proj_kernel(x_ref, w_ref, pool_ref, ut_ref, accp_ref, accs_ref):
    kk = pl.program_id(1)
    rows, dp = accp_ref.shape
    nj = accs_ref.shape[0]

    xb = x_ref[...].astype(BF16)
    per = MXU_N // LANES

    def accumulate(first):
        for n in range(dp // MXU_N):
            cs = slice(n * MXU_N, (n + 1) * MXU_N)
            part = jnp.dot(xb, w_ref[:, cs], preferred_element_type=F32)
            accp_ref[:, cs] = part if first else accp_ref[:, cs] + part
        for n in range(nj // per):
            part = jnp.dot(xb, w_ref[:, dp + n * MXU_N: dp + (n + 1) * MXU_N], preferred_element_type=F32)
            for q in range(per):
                piece = part[:, q * LANES:(q + 1) * LANES]
                accs_ref[n * per + q] = piece if first else accs_ref[n * per + q] + piece

    @pl.when(kk == 0)
    def _():
        accumulate(True)

    @pl.when(kk > 0)
    def _():
        accumulate(False)

    @pl.when(kk == pl.num_programs(1) - 1)
    def _():
        pool_ref[...] = accp_ref[...].astype(BF16)
        g, L, c, r = ut_ref.shape
        gj = LANES // c
        for s in range(L):
            for j in range(nj):
                piece = accs_ref[j, pl.ds(s, r, stride=L), :]
                ut_ref[j * gj:(j + 1) * gj, s, :, :] = piece.astype(BF16).T.reshape(gj, c, r)


def _proj(xt, w_in, dp, L):
    t, d = xt.shape
    dm = w_in.shape[1]
    g = (dm - dp) // SSM_GROUP
    tm = LANES * L
    kb = 512
    return pl.pallas_call(
        _proj_kernel,
        out_shape=(jax.ShapeDtypeStruct((t, dp), BF16),
                   jax.ShapeDtypeStruct((g, L, SSM_GROUP, t // L), BF16)),
        grid=(t // tm, d // kb),
        in_specs=[pl.BlockSpec((tm, kb), lambda i, k: (i, k)),
                  pl.BlockSpec((kb, dm), lambda i, k: (k, 0))],
        out_specs=(pl.BlockSpec((tm, dp), lambda i, k: (i, 0)),
                   pl.BlockSpec((g, L, SSM_GROUP, LANES), lambda i, k: (0, 0, 0, i))),
        scratch_shapes=[pltpu.VMEM((tm, dp), F32),
                        pltpu.VMEM(((dm - dp) // LANES, tm, LANES), F32)],
        compiler_params=_cparams(("arbitrary", "arbitrary")),
        name="proj",
    )(xt, w_in)


def _pool_kernel(prev_ref, main_ref, next_ref, pw_ref, sc_ref, out_ref, *, seq, sub):
    i = pl.program_id(0)
    ts = main_ref.shape[0]
    ext = jnp.concatenate([prev_ref[...], main_ref[...], next_ref[...]], axis=0)
    k = sub + 2 * HALO
    row = lax.broadcasted_iota(I32, (sub, k), 0)
    col = lax.broadcasted_iota(I32, (sub, k), 1)
    off0 = col - HALO - row
    for sb in range(ts // sub):
        rows = slice(sb * sub, (sb + 1) * sub)
        base = (i * ts + sb * sub) % seq
        src = base + row + off0
        off = jnp.where((src >= 0) & (src < seq), off0, 2 * HALO)
        pos = base + lax.broadcasted_iota(I32, (sub, 1), 0)
        for g, w in enumerate(POOL_WINDOWS):
            sl = slice(g * POOL_GROUP, (g + 1) * POOL_GROUP)
            band = (off + w // 2).astype(U32) < w
            bm = jnp.where(band, 1.0, 0.0).astype(BF16)
            sums = jnp.dot(bm, ext[sb * sub:sb * sub + k, sl], preferred_element_type=F32)
            lo = jnp.maximum(pos - w // 2, 0)
            hi = jnp.minimum(pos - w // 2 + w, seq)
            inv = 1.0 / (hi - lo).astype(F32)
            dlt = sums * inv - main_ref[rows, sl].astype(F32)
            y = jnp.dot(dlt.astype(BF16), pw_ref[g], preferred_element_type=F32) * sc_ref[:, sl]
            out_ref[rows, sl] = y.astype(BF16)


def _pool(pp, pool_w, pool_scale, seq):
    t, dp = pp.shape
    ts = min(1024, seq)
    nh = t // HALO
    per = ts // HALO
    return pl.pallas_call(
        functools.partial(_pool_kernel, seq=seq, sub=256),
        out_shape=jax.ShapeDtypeStruct((t, dp), BF16),
        grid=(t // ts,),
        in_specs=[pl.BlockSpec((HALO, dp), lambda i: (jnp.maximum(i * per - 1, 0), 0)),
                  pl.BlockSpec((ts, dp), lambda i: (i, 0)),
                  pl.BlockSpec((HALO, dp), lambda i: (jnp.minimum((i + 1) * per, nh - 1), 0)),
                  pl.BlockSpec(pool_w.shape, lambda i: (0, 0, 0)),
                  pl.BlockSpec((1, dp), lambda i: (0, 0))],
        out_specs=pl.BlockSpec((ts, dp), lambda i: (i, 0)),
        compiler_params=_cparams(("arbitrary",)),
        name="pool",
    )(pp, pp, pp, pool_w, pool_scale)


def _ssm_tables(a_re, a_im, log_dt, b_re, b_im, c_re, c_im, d_skip, L):
    g = a_re.shape[1]
    p = a_re.shape[2]
    c = b_re.shape[3]
    fl = L * c
    lr = a_re.astype(F32)
    li = a_im.astype(F32)
    dt = jnp.exp(log_dt.astype(F32))[..., None]
    mag = jnp.exp(lr * dt)
    abr = mag * jnp.cos(li * dt)
    abi = mag * jnp.sin(li * dt)
    den = lr * lr + li * li
    zr = ((abr - 1.0) * lr + abi * li) / den
    zi = (abi * lr - (abr - 1.0) * li) / den
    br = b_re.astype(F32)
    bi = b_im.astype(F32)
    bbr = zr[..., None] * br - zi[..., None] * bi
    bbi = zr[..., None] * bi + zi[..., None] * br
    cr = c_re.astype(F32)
    ci = c_im.astype(F32)
    kk = jnp.arange(L + 1, dtype=F32)[None, None, :, None]
    pm = jnp.exp(kk * (lr * dt)[:, :, None, :])
    ang = kk * (li * dt)[:, :, None, :]
    pr = pm * jnp.cos(ang)
    pi = pm * jnp.sin(ang)

    skip = jnp.eye(c, dtype=F32)[None] * d_skip.astype(F32).reshape(g, c, 1)
    lag0 = jnp.einsum('dgop,dgpc->goc', cr, bbr, precision=lax.Precision.HIGHEST) \
        - jnp.einsum('dgop,dgpc->goc', ci, bbi, precision=lax.Precision.HIGHEST) + skip
    lag0 = jnp.pad(lag0, ((0, 0), (0, 0), (0, LANES - c)))
    bb = jnp.concatenate([bbr[0], bbi[0], bbr[1], bbi[1]], axis=1)
    bb = jnp.pad(bb, ((0, 0), (0, 0), (0, LANES - c)))

    def w1(d, qr, qi):
        qr = qr.transpose(0, 2, 1)[..., None]
        qi = qi.transpose(0, 2, 1)[..., None]
        re = qr * bbr[d][:, :, None] - qi * bbi[d][:, :, None]
        im = qr * bbi[d][:, :, None] + qi * bbr[d][:, :, None]
        to = lambda v: v.reshape(g, p, fl)
        return to(re), to(im)
    f_re, f_im = w1(0, pr[0][:, L - 1::-1], pi[0][:, L - 1::-1])
    b_re_, b_im_ = w1(1, pr[1][:, :L], pi[1][:, :L])
    w1t = jnp.concatenate([f_re, f_im, b_re_, b_im_], axis=1)

    def w2(d, qr, qi):
        re = cr[d][:, None] * qr[:, :, None] - ci[d][:, None] * qi[:, :, None]
        im = -(cr[d][:, None] * qi[:, :, None] + ci[d][:, None] * qr[:, :, None])
        to = lambda v: v.reshape(g, fl, p)
        return to(re), to(im)
    rf_re, rf_im = w2(0, pr[0][:, 1:], pi[0][:, 1:])
    rb_re, rb_im = w2(1, pr[1][:, :0:-1], pi[1][:, :0:-1])
    w2t = jnp.concatenate([rf_re, rf_im, rb_re, rb_im], axis=2)

    a4 = jnp.stack([pr[0][:, L], pi[0][:, L], pr[1][:, L], pi[1][:, L]], axis=1)
    ap = a4.reshape(g // 2, 2, 4, p).transpose(0, 2, 1, 3).reshape(g // 2, 4, 2 * p)
    ap = jnp.pad(ap, ((0, 0), (0, 4), (0, 0)))
    return lag0, bb.astype(BF16), w1t.astype(BF16), w2t.astype(BF16), ap


def _cmul(ar, ai, br, bi):
    return ar * br - ai * bi, ar * bi + ai * br


def _ssm_kernel(ut_ref, w1_ref, lag0_ref, bb_ref, w2_ref, a_ref, yt_ref, mt_s, kk_s, zt_s, ht_s, *, nseg, L):
    p = SSM_STATE
    c = SSM_GROUP
    fl, n = ut_ref.shape[1:]
    nb = n // nseg
    nrow = nseg // 8
    st = [jnp.dot(w1_ref[q], ut_ref[q], preferred_element_type=F32) for q in range(2)]
    z = jnp.concatenate([st[q][k * p:(k + 1) * p] for k in range(4) for q in range(2)], axis=0)
    zt_s[...] = z.astype(BF16).T.astype(F32)
    per = LANES // c
    for q in range(2):
        kf = jnp.dot(w2_ref[q, :, 0:2 * p], bb_ref[q, 0:2 * p, :], preferred_element_type=F32)
        kb = jnp.dot(w2_ref[q, :, 2 * p:4 * p], bb_ref[q, 2 * p:4 * p, :], preferred_element_type=F32)
        tiles = [None] * (2 * L // per)
        for j in range(2 * L - 1):
            lag = L - 1 - j
            if lag > 0:
                blk = kf[(lag - 1) * c:lag * c, :]
            elif lag == 0:
                blk = lag0_ref[q]
            else:
                blk = kb[(L + lag) * c:(L + lag + 1) * c, :]
            sh = (j % per) * c
            blk = blk if sh == 0 else pltpu.roll(blk, sh, 1)
            tiles[j // per] = blk if tiles[j // per] is None else tiles[j // per] + blk
        kk_s[...] = jnp.concatenate(tiles, axis=1).astype(BF16)
        for t in range(L):
            mt_s[q, t * c:(t + 1) * c, :] = kk_s[:, (L - 1 - t) * c:(L - 1 - t) * c + fl]

    sub = lax.broadcasted_iota(I32, (8, LANES), 0)
    full = lambda v: jnp.broadcast_to(v, (8, LANES))
    one, zero = jnp.ones((8, LANES), F32), jnp.zeros((8, LANES), F32)

    def powers(row):
        out = [(full(a_ref[0, row:row + 1, :]), full(a_ref[0, row + 1:row + 2, :]))]
        for _ in range(3):
            out.append(_cmul(*out[-1], *out[-1]))
        return out

    def by_bits(pw, idx):
        qr, qi = one, zero
        for bit in range(3):
            on = (idx >> bit) & 1 == 1
            qr, qi = _cmul(qr, qi, jnp.where(on, pw[bit][0], one), jnp.where(on, pw[bit][1], zero))
        return qr, qi

    pw_f, pw_b = powers(0), powers(2)
    qf = by_bits(pw_f, sub)
    qb = by_bits(pw_b, 7 - sub)

    def local(xr, xi, pw, forward):
        for lvl, d in enumerate((1, 2, 4)):
            keep = (sub >= d) if forward else (sub < 8 - d)
            sh = d if forward else 8 - d
            tr = jnp.where(keep, pltpu.roll(xr, sh, 0), 0.0)
            ti = jnp.where(keep, pltpu.roll(xi, sh, 0), 0.0)
            mr, mi = _cmul(pw[lvl][0], pw[lvl][1], tr, ti)
            xr, xi = xr + mr, xi + mi
        return xr, xi

    def step(i, carry):
        new = []
        for b in range(nb):
            for fwd in (True, False):
                cr, ci = carry[2 * (2 * b + (0 if fwd else 1)):][:2]
                row = b * nseg + (i if fwd else nrow - 1 - i) * 8
                rows = pl.ds(pl.multiple_of(row, 8), 8)
                l0 = 0 if fwd else 2 * LANES
                pw, (qr, qi) = (pw_f, qf) if fwd else (pw_b, qb)
                lr, li = local(zt_s[rows, l0:l0 + LANES], zt_s[rows, l0 + LANES:l0 + 2 * LANES], pw, fwd)
                keep = (sub >= 1) if fwd else (sub < 7)
                er = jnp.where(keep, pltpu.roll(lr, 1 if fwd else 7, 0), 0.0)
                ei = jnp.where(keep, pltpu.roll(li, 1 if fwd else 7, 0), 0.0)
                mr, mi = _cmul(qr, qi, cr, ci)
                ht_s[rows, l0:l0 + LANES] = er + mr
                ht_s[rows, l0 + LANES:l0 + 2 * LANES] = ei + mi
                edge = 7 if fwd else 0
                mr, mi = _cmul(pw[3][0], pw[3][1], cr, ci)
                new += [full(lr[edge:edge + 1, :]) + mr, full(li[edge:edge + 1, :]) + mi]
        return tuple(new)

    lax.fori_loop(0, nrow, step, (zero,) * (4 * nb))

    h = ht_s[...].astype(BF16).T
    for q in range(2):
        hq = jnp.concatenate([h[(2 * k + q) * p:(2 * k + q + 1) * p] for k in range(4)], axis=0)
        y = jnp.dot(mt_s[q], ut_ref[q], preferred_element_type=F32) \
            + jnp.dot(w2_ref[q], hq, preferred_element_type=F32)
        yt_ref[q] = y.astype(BF16)


def _ssm(ut3, lag0, bb, w1t, w2t, ap, nseg, L):
    g, fl, n = ut3.shape
    sp = w1t.shape[1]
    return pl.pallas_call(
        functools.partial(_ssm_kernel, nseg=nseg, L=L),
        out_shape=jax.ShapeDtypeStruct((g, fl, n), BF16),
        grid=(g // 2,),
        in_specs=[pl.BlockSpec((2, fl, n), lambda i: (i, 0, 0)),
                  pl.BlockSpec((2, sp, fl), lambda i: (i, 0, 0)),
                  pl.BlockSpec((2,) + lag0.shape[1:], lambda i: (i, 0, 0)),
                  pl.BlockSpec((2,) + bb.shape[1:], lambda i: (i, 0, 0)),
                  pl.BlockSpec((2, fl, sp), lambda i: (i, 0, 0)),
                  pl.BlockSpec((1,) + ap.shape[1:], lambda i: (i, 0, 0))],
        out_specs=pl.BlockSpec((2, fl, n), lambda i: (i, 0, 0)),
        scratch_shapes=[pltpu.VMEM((2, fl, fl), BF16),
                        pltpu.VMEM((SSM_GROUP, 2 * fl), BF16),
                        pltpu.VMEM((n, 2 * sp), F32),
                        pltpu.VMEM((n, 2 * sp), F32)],
        compiler_params=_cparams(("arbitrary",)),
        name="ssm",
    )(ut3, w1t, lag0, bb, w2t, ap)


def _glu_kernel(yt_ref, w_ref, b_ref, out_ref, nat_ref, *, sub):
    g, L, c, r = yt_ref.shape
    nj = nat_ref.shape[0]
    gj = LANES // c
    for t in range(L):
        for j in range(nj):
            piece = yt_ref[j * gj:(j + 1) * gj, t, :, :].reshape(LANES, r).astype(F32)
            nat_ref[j, pl.ds(t, r, stride=L), :] = piece.T

    def body(k, carry):
        sl = pl.ds(pl.multiple_of(k * sub, sub), sub)
        y = jnp.concatenate([nat_ref[j, sl, :] for j in range(nj)], axis=1)
        ya = jax.nn.gelu(y, approximate=True)
        z = jnp.dot(ya.astype(BF16), w_ref[...], preferred_element_type=F32) + b_ref[...]
        out_ref[sl, :] = (ya * (1.0 / (1.0 + jnp.exp(-z)))).astype(BF16)
        return carry
    lax.fori_loop(0, nat_ref.shape[1] // sub, body, 0)


def _glu(yt4, glu_w, glu_b):
    g, L, c, n = yt4.shape
    ds = g * c
    r = LANES
    return pl.pallas_call(
        functools.partial(_glu_kernel, sub=512),
        out_shape=jax.ShapeDtypeStruct((n * L, ds), BF16),
        grid=(n // r,),
        in_specs=[pl.BlockSpec((g, L, c, r), lambda j: (0, 0, 0, j)),
                  pl.BlockSpec((ds, ds), lambda j: (0, 0)),
                  pl.BlockSpec((1, ds), lambda j: (0, 0))],
        out_specs=pl.BlockSpec((r * L, ds), lambda j: (j, 0)),
        scratch_shapes=[pltpu.VMEM((ds // LANES, r * L, LANES), F32)],
        compiler_params=_cparams(("arbitrary",)),
        name="glu",
    )(yt4, glu_w, glu_b)


def _layer_norm(r, g, b):
    mu = jnp.mean(r, axis=-1, keepdims=True)
    cen = r - mu
    var = jnp.mean(cen * cen, axis=-1, keepdims=True)
    return cen * lax.rsqrt(var + LN_EPS) * g + b


def _store_row_packed(ref, m, row0=0):
    rows, width = m.shape
    half = width // 2
    nc = half // LANES
    lo = lax.bitcast_convert_type(m[:, :half].astype(BF16).astype(F32), U32) >> 16
    hi = lax.bitcast_convert_type(m[:, half:].astype(BF16).astype(F32), U32) & jnp.uint32(0xFFFF0000)
    pk = lo | hi
    for c in range(nc):
        ref[pl.ds(row0 * nc + c, rows, stride=nc), :] = pk[:, c * LANES:(c + 1) * LANES]


def _load_row_packed(ref, rows, nc, row0=0):
    los, his = [], []
    for c in range(nc):
        u = ref[pl.ds(row0 * nc + c, rows, stride=nc), :]
        los.append(lax.bitcast_convert_type(u << 16, F32))
        his.append(lax.bitcast_convert_type(u & jnp.uint32(0xFFFF0000), F32))
    return los, his


def _outproj_kernel(yp_ref, ys_ref, x_ref, wo_ref, g_ref, b_ref, rw_ref, rb_ref,
                    hp_ref, lt_ref, *, alpha, dp, sub):
    for sb in range(x_ref.shape[0] // sub):
        sl = pl.ds(sb * sub, sub)
        mix = jnp.dot(yp_ref[sl, :], wo_ref[0:dp, :], preferred_element_type=F32) \
            + jnp.dot(ys_ref[sl, :], wo_ref[dp:, :], preferred_element_type=F32)
        h = _layer_norm(alpha * x_ref[sl, :] + mix, g_ref[...], b_ref[...])
        _store_row_packed(hp_ref, h, row0=sb * sub)
        hh = h.astype(BF16)
        hl = (h - hh.astype(F32)).astype(BF16)
        part = jnp.dot(hh, rw_ref[...], preferred_element_type=F32) \
            + jnp.dot(hl, rw_ref[...], preferred_element_type=F32)
        logits = part + pltpu.roll(part, ROUTER_LANES // 2, 1) + rb_ref[...]
        lt_ref[:, sb * sub:(sb + 1) * sub] = logits.T[0:ROUTER_ROWS, :]


def _outproj(y_pool, y_ssm, xt, w_out, ln_g, ln_b, rw, rb, alpha):
    t, d = xt.shape
    dp = y_pool.shape[1]
    tm = min(512, t)
    pr = d // 2 // LANES
    return pl.pallas_call(
        functools.partial(_outproj_kernel, alpha=alpha, dp=dp, sub=256),
        out_shape=(jax.ShapeDtypeStruct((t * pr, LANES), U32),
                   jax.ShapeDtypeStruct((ROUTER_ROWS, t), F32)),
        grid=(t // tm,),
        in_specs=[pl.BlockSpec((tm, dp), lambda i: (i, 0)),
                  pl.BlockSpec((tm, y_ssm.shape[1]), lambda i: (i, 0)),
                  pl.BlockSpec((tm, d), lambda i: (i, 0)),
                  pl.BlockSpec(w_out.shape, lambda i: (0, 0)),
                  pl.BlockSpec((1, d), lambda i: (0, 0)),
                  pl.BlockSpec((1, d), lambda i: (0, 0)),
                  pl.BlockSpec(rw.shape, lambda i: (0, 0)),
                  pl.BlockSpec((1, ROUTER_LANES), lambda i: (0, 0))],
        out_specs=(pl.BlockSpec((tm * pr, LANES), lambda i: (i, 0)),
                   pl.BlockSpec((ROUTER_ROWS, tm), lambda i: (0, i))),
        compiler_params=_cparams(("arbitrary",)),
        name="outproj",
    )(y_pool, y_ssm, xt, w_out, ln_g, ln_b, rw, rb)


def _router_kernel(lt_ref, eid_ref, rank_ref, gate_ref, cnt_ref, carry_ref):
    i = pl.program_id(0)
    ne, epg = N_EXPERTS, EXPERTS_PER_GROUP

    @pl.when(i == 0)
    def _():
        carry_ref[...] = jnp.zeros_like(carry_ref)

    lt = lt_ref[...]
    tt = lt.shape[1]
    gl = [lt[j:j + 1, :] for j in range(N_EXPERT_GROUPS)]
    gmax = jnp.maximum(jnp.maximum(gl[0], gl[1]), jnp.maximum(gl[2], gl[3]))
    grp = jnp.where(gl[0] == gmax, 0, jnp.where(gl[1] == gmax, 1, jnp.where(gl[2] == gmax, 2, 3)))
    p_grp = 1.0 / (jnp.exp(gl[0] - gmax) + jnp.exp(gl[1] - gmax)
                   + jnp.exp(gl[2] - gmax) + jnp.exp(gl[3] - gmax))
    eg = [lt[8 + epg * j: 8 + epg * (j + 1), :] for j in range(N_EXPERT_GROUPS)]
    el = jnp.where(grp == 0, eg[0], jnp.where(grp == 1, eg[1], jnp.where(grp == 2, eg[2], eg[3])))
    sub = lax.broadcasted_iota(I32, (epg, tt), 0)
    m1 = jnp.max(el, axis=0, keepdims=True)
    i1 = jnp.min(jnp.where(el == m1, sub, epg), axis=0, keepdims=True)
    rest = jnp.where(sub == i1, -jnp.inf, el)
    m2 = jnp.max(rest, axis=0, keepdims=True)
    i2 = jnp.min(jnp.where(rest == m2, sub, epg), axis=0, keepdims=True)
    r21 = jnp.exp(m2 - m1)
    g1 = p_grp / (1.0 + r21)
    g2 = g1 * r21
    e1 = grp * epg + i1
    e2 = grp * epg + i2

    rows = lax.broadcasted_iota(I32, (ne, tt), 0)
    oh1 = rows == e1
    oh2 = rows == e2
    oh = jnp.where(oh1 | oh2, 1.0, 0.0)
    tri = jnp.where(lax.broadcasted_iota(I32, (tt, tt), 0) < lax.broadcasted_iota(I32, (tt, tt), 1),
                    1.0, 0.0).astype(BF16)
    before = jnp.dot(oh.astype(BF16), tri, preferred_element_type=F32) + carry_ref[:, 0:1]
    r1 = jnp.sum(jnp.where(oh1, before, 0.0), axis=0, keepdims=True)
    r2 = jnp.sum(jnp.where(oh2, before, 0.0), axis=0, keepdims=True)
    carry_ref[...] = carry_ref[...] + jnp.sum(oh, axis=1, keepdims=True)

    eid_ref[...] = jnp.concatenate([e1, e2], axis=0)
    rank_ref[...] = jnp.concatenate([r1, r2], axis=0).astype(I32)
    gate_ref[...] = jnp.concatenate([g1, g2], axis=0)
    cnt_ref[...] = carry_ref[...].astype(I32)


def _router(lt):
    rr, t = lt.shape
    tt = min(512, t)
    return pl.pallas_call(
        _router_kernel,
        out_shape=(jax.ShapeDtypeStruct((2, t), I32),
                   jax.ShapeDtypeStruct((2, t), I32),
                   jax.ShapeDtypeStruct((2, t), F32),
                   jax.ShapeDtypeStruct((N_EXPERTS, 128), I32)),
        grid=(t // tt,),
        in_specs=[pl.BlockSpec((rr, tt), lambda i: (0, i))],
        out_specs=(pl.BlockSpec((2, tt), lambda i: (0, i)),
                   pl.BlockSpec((2, tt), lambda i: (0, i)),
                   pl.BlockSpec((2, tt), lambda i: (0, i)),
                   pl.BlockSpec((N_EXPERTS, 128), lambda i: (0, 0))),
        scratch_shapes=[pltpu.VMEM((N_EXPERTS, 128), F32)],
        compiler_params=_cparams(("arbitrary",)),
        name="router",
    )(lt)


def _invert_kernel(dest_ref, zeros_hbm, out_ref):
    pltpu.sync_copy(zeros_hbm, out_ref)
    t = dest_ref.shape[0] // 2

    def place(i, carry):
        out_ref[dest_ref[i]] = i
        out_ref[dest_ref[t + i]] = i
        return carry
    lax.fori_loop(0, t, place, 0, unroll=8)


def _invert(dest_flat, n_slots):
    return pl.pallas_call(
        _invert_kernel,
        out_shape=jax.ShapeDtypeStruct((n_slots,), I32),
        grid_spec=pltpu.PrefetchScalarGridSpec(
            num_scalar_prefetch=1, grid=(1,),
            in_specs=[pl.BlockSpec(memory_space=pl.ANY)],
            out_specs=pl.BlockSpec(memory_space=pltpu.SMEM)),
        compiler_params=_cparams(("arbitrary",)),
        name="invert",
    )(dest_flat, jnp.zeros((n_slots,), I32))


def _moe_kernel(be_ref, nu_ref, pc_ref, st_ref, hp_hbm, wg_hbm, wu_hbm, wd_hbm, y_ref,
                gbuf, wgf, wuf, wdf, wgb, wub, wdb, par_ref, gsem, wsem):
    for half in range(MOE_BLOCKS_PER_STEP):
        _moe_block(pl.program_id(0) * MOE_BLOCKS_PER_STEP + half, half * MOE_BLK,
                   be_ref, nu_ref, pc_ref, st_ref, hp_hbm, wg_hbm, wu_hbm, wd_hbm, y_ref,
                   gbuf, wgf, wuf, wdf, wgb, wub, wdb, par_ref, gsem, wsem)


def _moe_block(b, row0, be_ref, nu_ref, pc_ref, st_ref, hp_hbm, wg_hbm, wu_hbm, wd_hbm, y_ref,
               gbuf, wgf, wuf, wdf, wgb, wub, wdb, par_ref, gsem, wsem):
    nu = nu_ref[0]
    pr = gbuf.shape[1] // MOE_BLK
    blk = MOE_BLK
    nbuf = gbuf.shape[0]
    look = nbuf - 1

    def row_copy(tok, slot, r):
        src = hp_hbm.at[pl.ds(pl.multiple_of(tok * pr, pr), pr)]
        off = r * pr if isinstance(r, int) else pl.multiple_of(r * pr, pr)
        return pltpu.make_async_copy(src, gbuf.at[slot, pl.ds(off, pr)], gsem.at[slot])

    def issue(block, slot):
        base = block * blk

        def body(r, carry):
            row_copy(st_ref[base + r], slot, r).start()
            return carry
        lax.fori_loop(0, blk, body, 0, unroll=8)

    def weight_copies(e, p):
        return (pltpu.make_async_copy(wg_hbm.at[e], wgf.at[p], wsem.at[p]),
                pltpu.make_async_copy(wu_hbm.at[e], wuf.at[p], wsem.at[p]),
                pltpu.make_async_copy(wd_hbm.at[e], wdf.at[p], wsem.at[p]))

    @pl.when(b == 0)
    def _():
        par_ref[0] = 0
        for cp in weight_copies(be_ref[0], 0):
            cp.start(priority=1)
        issue(0, 0)
        for j in range(1, look):
            @pl.when(j < nu)
            def _():
                issue(j, j)

    def issue_part(k):
        base = (b + look) * blk
        for r in range(k * blk // 4, (k + 1) * blk // 4):
            row_copy(st_ref[base + r], (b + look) % nbuf, r).start(priority=1 if r % 3 == 2 else 0)

    def expert_mlp(slot, with_issue):
        los, his = _load_row_packed(gbuf.at[slot], blk, pr)
        x = jnp.concatenate([v.astype(BF16) for v in los + his], axis=1)
        if with_issue:
            issue_part(0)
        hg = jnp.dot(x, wgb[...], preferred_element_type=F32)
        if with_issue:
            issue_part(1)
        hu = jnp.dot(x, wub[...], preferred_element_type=F32)
        if with_issue:
            issue_part(2)
        hh = hg * (1.0 / (1.0 + jnp.exp(-hg))) * hu
        y = jnp.dot(hh.astype(BF16), wdb[...], preferred_element_type=F32)
        if with_issue:
            issue_part(3)
        _store_row_packed(y_ref, y, row0=row0)

    @pl.when(b < nu)
    def _():
        slot = b % nbuf
        e = be_ref[b]
        first = (b == 0) | (e != be_ref[jnp.maximum(b - 1, 0)])

        @pl.when(first)
        def _():
            p = par_ref[0]
            for cp in weight_copies(e, p):
                cp.wait()
            nxt = lax.while_loop(
                lambda c: (c < N_EXPERTS) & (pc_ref[jnp.minimum(c, N_EXPERTS - 1)] == 0),
                lambda c: c + 1, e + 1)

            @pl.when(nxt < N_EXPERTS)
            def _():
                for cp in weight_copies(nxt, 1 - p):
                    cp.start(priority=1)
            wgb[...] = wgf[p].astype(BF16)
            wub[...] = wuf[p].astype(BF16)
            wdb[...] = wdf[p].astype(BF16)
            par_ref[0] = 1 - p

        pltpu.make_async_copy(hp_hbm.at[pl.ds(0, blk * pr)], gbuf.at[slot], gsem.at[slot]).wait()

        @pl.when(b + look < nu)
        def _():
            expert_mlp(slot, True)

        @pl.when(b + look >= nu)
        def _():
            expert_mlp(slot, False)

    @pl.when(b >= nu)
    def _():
        y_ref[pl.ds(row0 * pr, blk * pr), :] = jnp.zeros((blk * pr, LANES), U32)


def _moe(block_e, n_used, pcounts, slot_tok, hp, w_gate, w_up, w_down, n_blocks):
    ne, d, de = w_gate.shape
    blk = MOE_BLK
    pr = d // 2 // LANES
    bps = MOE_BLOCKS_PER_STEP
    assert n_blocks % bps == 0
    grid_spec = pltpu.PrefetchScalarGridSpec(
        num_scalar_prefetch=4,
        grid=(n_blocks // bps,),
        in_specs=[pl.BlockSpec(memory_space=pl.ANY)] * 4,
        out_specs=pl.BlockSpec((bps * blk * pr, LANES), lambda b, *_: (b, 0)),
        scratch_shapes=[pltpu.VMEM((GATHER_BUFS, blk * pr, LANES), U32),
                        pltpu.VMEM((2, d, de), F32),
                        pltpu.VMEM((2, d, de), F32),
                        pltpu.VMEM((2, de, d), F32),
                        pltpu.VMEM((d, de), BF16),
                        pltpu.VMEM((d, de), BF16),
                        pltpu.VMEM((de, d), BF16),
                        pltpu.SMEM((1,), I32),
                        pltpu.SemaphoreType.DMA((GATHER_BUFS,)),
                        pltpu.SemaphoreType.DMA((2,))],
    )
    return pl.pallas_call(
        _moe_kernel,
        out_shape=jax.ShapeDtypeStruct((n_blocks * blk * pr, LANES), U32),
        grid_spec=grid_spec,
        compiler_params=_cparams(("arbitrary",)),
        name="moe",
    )(block_e, n_used, pcounts, slot_tok, hp, w_gate, w_up, w_down)


def _combine_kernel(d0_ref, d1_ref, hp_ref, gate_ref, yb_hbm, g_ref, b_ref, out_ref, buf, sem, *, alpha):
    i = pl.program_id(0)
    n = pl.num_programs(0)
    tm = out_ref.shape[0]
    pr = buf.shape[2] // tm

    def issue(tile, slot):
        base = tile * tm

        def body(r, carry):
            dst = pl.ds(pl.multiple_of(r * pr, pr), pr)
            for k, dref in enumerate((d0_ref, d1_ref)):
                src = yb_hbm.at[pl.ds(pl.multiple_of(dref[base + r] * pr, pr), pr)]
                pltpu.make_async_copy(src, buf.at[slot, k, dst], sem.at[slot]).start()
            return carry
        lax.fori_loop(0, tm, body, 0, unroll=8)

    nbuf = buf.shape[0]
    look = nbuf - 1

    @pl.when(i == 0)
    def _():
        issue(0, 0)
        for j in range(1, look):
            @pl.when(j < n)
            def _():
                issue(j, j)

    slot = i % nbuf
    for k in range(2):
        pltpu.make_async_copy(yb_hbm.at[pl.ds(0, tm * pr)], buf.at[slot, k], sem.at[slot]).wait()

    nq = 4
    sub = tm // nq

    def issue_part(q):
        base = (i + look) * tm
        nslot = (i + look) % nbuf
        for r in range(q * sub, (q + 1) * sub):
            for k, dref in enumerate((d0_ref, d1_ref)):
                src = yb_hbm.at[pl.ds(pl.multiple_of(dref[base + r] * pr, pr), pr)]
                pltpu.make_async_copy(src, buf.at[nslot, k, pl.ds(r * pr, pr)], sem.at[nslot]).start(priority=k)

    def finish(with_issue):
        for q in range(nq):
            if with_issue:
                issue_part(q)
            rows = pl.ds(q * sub, sub)
            lo0, hi0 = _load_row_packed(buf.at[slot, 0], sub, pr, row0=q * sub)
            lo1, hi1 = _load_row_packed(buf.at[slot, 1], sub, pr, row0=q * sub)
            loh, hih = _load_row_packed(hp_ref, sub, pr, row0=q * sub)
            g0 = gate_ref[rows, 0:1]
            g1 = gate_ref[rows, 1:2]
            r = jnp.concatenate([alpha * hh + g0 * a + g1 * c
                                 for hh, a, c in zip(loh + hih, lo0 + hi0, lo1 + hi1)], axis=1)
            out_ref[rows, :] = _layer_norm(r, g_ref[...], b_ref[...])

    @pl.when(i + look < n)
    def _():
        finish(True)

    @pl.when(i + look >= n)
    def _():
        finish(False)


def _combine(dest0, dest1, hp, gate_t, yb, ln_g, ln_b, alpha):
    d = ln_g.shape[1]
    pr = d // 2 // LANES
    t = hp.shape[0] // pr
    tm = min(256, t)
    grid_spec = pltpu.PrefetchScalarGridSpec(
        num_scalar_prefetch=2,
        grid=(t // tm,),
        in_specs=[pl.BlockSpec((tm * pr, LANES), lambda i, d0, d1: (i, 0)),
                  pl.BlockSpec((tm, 2), lambda i, d0, d1: (i, 0)),
                  pl.BlockSpec(memory_space=pl.ANY),
                  pl.BlockSpec((1, d), lambda i, d0, d1: (0, 0)),
                  pl.BlockSpec((1, d), lambda i, d0, d1: (0, 0))],
        out_specs=pl.BlockSpec((tm, d), lambda i, d0, d1: (i, 0)),
        scratch_shapes=[pltpu.VMEM((GATHER_BUFS, 2, tm * (d // 2 // LANES), LANES), U32),
                        pltpu.SemaphoreType.DMA((GATHER_BUFS,))],
    )
    return pl.pallas_call(
        functools.partial(_combine_kernel, alpha=alpha),
        out_shape=jax.ShapeDtypeStruct((t, d), F32),
        grid_spec=grid_spec,
        compiler_params=_cparams(("arbitrary",)),
        name="combine",
    )(dest0, dest1, hp, gate_t, yb, ln_g, ln_b)


def _layer(h, w_in, pool_w, pool_scale, a_re, a_im, log_dt, b_re, b_im, c_re, c_im, d_skip,
           glu_w, glu_b, w_out, ln1_g, ln1_b, rg_w, rg_b, re_w, re_b, w_gate, w_up, w_down,
           ln2_g, ln2_b, alpha):
    bsz, seq, d = h.shape
    t = bsz * seq
    L = CHUNK
    dp = pool_w.shape[0] * pool_w.shape[1]
    ds = w_in.shape[1] - dp
    g = ds // SSM_GROUP
    n = t // L

    lag0, bb, w1t, w2t, al = _ssm_tables(a_re, a_im, log_dt, b_re, b_im, c_re, c_im, d_skip, L)
    half = ROUTER_LANES // 2
    zpad = lambda k: jnp.zeros((d, k), F32)
    rw = jnp.concatenate([rg_w.astype(F32), zpad(8 - N_EXPERT_GROUPS), re_w.astype(F32),
                          zpad(half - ROUTER_ROWS)], axis=1)
    rw_hi = rw.astype(BF16)
    rw_lo = (rw - rw_hi.astype(F32)).astype(BF16)
    rw2 = jnp.concatenate([rw_hi, rw_lo], axis=1)
    rb = jnp.concatenate([rg_b.astype(F32), jnp.zeros((8 - N_EXPERT_GROUPS,), F32), re_b.astype(F32),
                          jnp.zeros((ROUTER_LANES - ROUTER_ROWS,), F32)]).reshape(1, ROUTER_LANES)

    xt = h.reshape(t, d)
    pool_p, ut = _proj(xt, w_in.astype(BF16), dp, L)
    y_pool = _pool(pool_p, pool_w.astype(BF16), pool_scale.reshape(1, dp).astype(F32), seq)
    yt = _ssm(ut.reshape(g, L * SSM_GROUP, n), lag0, bb, w1t, w2t, al, seq // L, L)
    y_ssm = _glu(yt.reshape(g, L, SSM_GROUP, n), glu_w.astype(BF16), glu_b.reshape(1, ds).astype(F32))
    hp, lt = _outproj(y_pool, y_ssm, xt, w_out.astype(BF16), ln1_g.reshape(1, d), ln1_b.reshape(1, d),
                          rw2, rb, alpha)
    eid, rank, gate, cnt = _router(lt)

    blk = MOE_BLK
    m = 2 * t
    n_blocks = -(-m // blk) + N_EXPERTS
    counts = cnt[:, 0]
    pcounts = (counts + blk - 1) // blk * blk
    pends = jnp.cumsum(pcounts)
    pstarts = pends - pcounts
    e_ids = jnp.arange(N_EXPERTS, dtype=I32)
    dest = jnp.sum(jnp.where(eid[..., None] == e_ids, pstarts, 0), axis=-1) + rank
    slot_tok = _invert(dest.reshape(-1), n_blocks * blk)
    n_used = (pends[-1] // blk).astype(I32)
    bidx = jnp.minimum(jnp.arange(n_blocks, dtype=I32), n_used - 1)
    block_e = jnp.minimum(jnp.sum((pends[None, :] <= (bidx * blk)[:, None]).astype(I32), axis=1),
                          N_EXPERTS - 1)

    yb = _moe(block_e, n_used.reshape(1), pcounts, slot_tok, hp, w_gate, w_up, w_down, n_blocks)
    out = _combine(dest[0], dest[1], hp, gate.T, yb, ln2_g.reshape(1, d), ln2_b.reshape(1, d), alpha)
    return out.reshape(bsz, seq, d)


def kernel(x, w_in, pool_w, pool_scale, ssm_a_re, ssm_a_im, ssm_log_dt, ssm_b_re, ssm_b_im, ssm_c_re, ssm_c_im, ssm_d, glu_w, glu_b, w_out, ln1_g, ln1_b, router_g_w, router_g_b, router_e_w, router_e_b, w_gate, w_up, w_down, ln2_g, ln2_b):
    depth = w_in.shape[0]
    alpha = (2.0 * depth) ** 0.25
    h = x
    for l in range(depth):
        h = _layer(h, w_in[l], pool_w[l], pool_scale[l], ssm_a_re[l], ssm_a_im[l], ssm_log_dt[l],
                   ssm_b_re[l], ssm_b_im[l], ssm_c_re[l], ssm_c_im[l], ssm_d[l], glu_w[l], glu_b[l],
                   w_out[l], ln1_g[l], ln1_b[l], router_g_w[l], router_g_b[l], router_e_w[l],
                   router_e_b[l], w_gate[l], w_up[l], w_down[l], ln2_g[l], ln2_b[l], alpha)
    return h
```

```python
import functools
import math

import numpy as np
import jax
import jax.numpy as jnp
from jax import lax
from jax.experimental import pallas as pl
from jax.experimental.pallas import tpu as pltpu

F32 = jnp.float32
BF16 = jnp.bfloat16
I32 = jnp.int32
U32 = jnp.uint32

POOL_WINDOWS = (2, 4, 8, 16)
POOL_GROUP = 256
SSM_GROUP = 16
SSM_STATE = 64
N_EXPERT_GROUPS = 4
EXPERTS_PER_GROUP = 8
N_EXPERTS = N_EXPERT_GROUPS * EXPERTS_PER_GROUP
LN_EPS = 1e-5

CHUNK = 16
MOE_BLK = 256
MOE_BLOCKS_PER_STEP = 4
GATHER_BUFS = 4
ROUTER_ROWS = 8 + N_EXPERTS
ROUTER_LANES = 128
LANES = 128
MXU_N = 256
HALO = 16
VMEM_LIMIT = 56 * 1024 * 1024


def _cparams(sem, vmem=VMEM_LIMIT):
    return pltpu.CompilerParams(dimension_semantics=sem, vmem_limit_bytes=vmem)


def _proj_kernel(x_ref, w_ref, pool_ref, ut_ref, accp_ref, accs_ref):
    kk = pl.program_id(1)
    rows, dp = accp_ref.shape
    nj = accs_ref.shape[0]

    xb = x_ref[...].astype(BF16)
    per = MXU_N // LANES

    def accumulate(first):
        for n in range(dp // MXU_N):
            cs = slice(n * MXU_N, (n + 1) * MXU_N)
            part = jnp.dot(xb, w_ref[:, cs], preferred_element_type=F32)
            accp_ref[:, cs] = part if first else accp_ref[:, cs] + part
        for n in range(nj // per):
            part = jnp.dot(xb, w_ref[:, dp + n * MXU_N: dp + (n + 1) * MXU_N], preferred_element_type=F32)
            for q in range(per):
                piece = part[:, q * LANES:(q + 1) * LANES]
                accs_ref[n * per + q] = piece if first else accs_ref[n * per + q] + piece

    @pl.when(kk == 0)
    def _():
        accumulate(True)

    @pl.when(kk > 0)
    def _():
        accumulate(False)

    @pl.when(kk == pl.num_programs(1) - 1)
    def _():
        pool_ref[...] = accp_ref[...].astype(BF16)
        g, L, c, r = ut_ref.shape
        gj = LANES // c
        for s in range(L):
            for j in range(nj):
                piece = accs_ref[j, pl.ds(s, r, stride=L), :]
                ut_ref[j * gj:(j + 1) * gj, s, :, :] = piece.astype(BF16).T.reshape(gj, c, r)


def _proj(xt, w_in, dp, L):
    t, d = xt.shape
    dm = w_in.shape[1]
    g = (dm - dp) // SSM_GROUP
    tm = LANES * L
    kb = 512
    return pl.pallas_call(
        _proj_kernel,
        out_shape=(jax.ShapeDtypeStruct((t, dp), BF16),
                   jax.ShapeDtypeStruct((g, L, SSM_GROUP, t // L), BF16)),
        grid=(t // tm, d // kb),
        in_specs=[pl.BlockSpec((tm, kb), lambda i, k: (i, k)),
                  pl.BlockSpec((kb, dm), lambda i, k: (k, 0))],
        out_specs=(pl.BlockSpec((tm, dp), lambda i, k: (i, 0)),
                   pl.BlockSpec((g, L, SSM_GROUP, LANES), lambda i, k: (0, 0, 0, i))),
        scratch_shapes=[pltpu.VMEM((tm, dp), F32),
                        pltpu.VMEM(((dm - dp) // LANES, tm, LANES), F32)],
        compiler_params=_cparams(("arbitrary", "arbitrary")),
        name="proj",
    )(xt, w_in)


def _pool_kernel(prev_ref, main_ref, next_ref, pw_ref, sc_ref, out_ref, *, seq, sub):
    i = pl.program_id(0)
    ts = main_ref.shape[0]
    ext = jnp.concatenate([prev_ref[...], main_ref[...], next_ref[...]], axis=0)
    k = sub + 2 * HALO
    row = lax.broadcasted_iota(I32, (sub, k), 0)
    col = lax.broadcasted_iota(I32, (sub, k), 1)
    off0 = col - HALO - row
    for sb in range(ts // sub):
        rows = slice(sb * sub, (sb + 1) * sub)
        base = (i * ts + sb * sub) % seq
        src = base + row + off0
        off = jnp.where((src >= 0) & (src < seq), off0, 2 * HALO)
        pos = base + lax.broadcasted_iota(I32, (sub, 1), 0)
        for g, w in enumerate(POOL_WINDOWS):
            sl = slice(g * POOL_GROUP, (g + 1) * POOL_GROUP)
            band = (off + w // 2).astype(U32) < w
            bm = jnp.where(band, 1.0, 0.0).astype(BF16)
            sums = jnp.dot(bm, ext[sb * sub:sb * sub + k, sl], preferred_element_type=F32)
            lo = jnp.maximum(pos - w // 2, 0)
            hi = jnp.minimum(pos - w // 2 + w, seq)
            inv = 1.0 / (hi - lo).astype(F32)
            dlt = sums * inv - main_ref[rows, sl].astype(F32)
            y = jnp.dot(dlt.astype(BF16), pw_ref[g], preferred_element_type=F32) * sc_ref[:, sl]
            out_ref[rows, sl] = y.astype(BF16)


def _pool(pp, pool_w, pool_scale, seq):
    t, dp = pp.shape
    ts = min(1024, seq)
    nh = t // HALO
    per = ts // HALO
    return pl.pallas_call(
        functools.partial(_pool_kernel, seq=seq, sub=256),
        out_shape=jax.ShapeDtypeStruct((t, dp), BF16),
        grid=(t // ts,),
        in_specs=[pl.BlockSpec((HALO, dp), lambda i: (jnp.maximum(i * per - 1, 0), 0)),
                  pl.BlockSpec((ts, dp), lambda i: (i, 0)),
                  pl.BlockSpec((HALO, dp), lambda i: (jnp.minimum((i + 1) * per, nh - 1), 0)),
                  pl.BlockSpec(pool_w.shape, lambda i: (0, 0, 0)),
                  pl.BlockSpec((1, dp), lambda i: (0, 0))],
        out_specs=pl.BlockSpec((ts, dp), lambda i: (i, 0)),
        compiler_params=_cparams(("arbitrary",)),
        name="pool",
    )(pp, pp, pp, pool_w, pool_scale)


def _ssm_tables(a_re, a_im, log_dt, b_re, b_im, c_re, c_im, d_skip, L):
    g = a_re.shape[1]
    p = a_re.shape[2]
    c = b_re.shape[3]
    fl = L * c
    lr = a_re.astype(F32)
    li = a_im.astype(F32)
    dt = jnp.exp(log_dt.astype(F32))[..., None]
    mag = jnp.exp(lr * dt)
    abr = mag * jnp.cos(li * dt)
    abi = mag * jnp.sin(li * dt)
    den = lr * lr + li * li
    zr = ((abr - 1.0) * lr + abi * li) / den
    zi = (abi * lr - (abr - 1.0) * li) / den
    br = b_re.astype(F32)
    bi = b_im.astype(F32)
    bbr = zr[..., None] * br - zi[..., None] * bi
    bbi = zr[..., None] * bi + zi[..., None] * br
    cr = c_re.astype(F32)
    ci = c_im.astype(F32)
    kk = jnp.arange(L + 1, dtype=F32)[None, None, :, None]
    pm = jnp.exp(kk * (lr * dt)[:, :, None, :])
    ang = kk * (li * dt)[:, :, None, :]
    pr = pm * jnp.cos(ang)
    pi = pm * jnp.sin(ang)

    skip = jnp.eye(c, dtype=F32)[None] * d_skip.astype(F32).reshape(g, c, 1)
    lag0 = jnp.einsum('dgop,dgpc->goc', cr, bbr, precision=lax.Precision.HIGHEST) \
        - jnp.einsum('dgop,dgpc->goc', ci, bbi, precision=lax.Precision.HIGHEST) + skip
    lag0 = jnp.pad(lag0, ((0, 0), (0, 0), (0, LANES - c)))
    bb = jnp.concatenate([bbr[0], bbi[0], bbr[1], bbi[1]], axis=1)
    bb = jnp.pad(bb, ((0, 0), (0, 0), (0, LANES - c)))

    def w1(d, qr, qi):
        qr = qr.transpose(0, 2, 1)[..., None]
        qi = qi.transpose(0, 2, 1)[..., None]
        re = qr * bbr[d][:, :, None] - qi * bbi[d][:, :, None]
        im = qr * bbi[d][:, :, None] + qi * bbr[d][:, :, None]
        to = lambda v: v.reshape(g, p, fl)
        return to(re), to(im)
    f_re, f_im = w1(0, pr[0][:, L - 1::-1], pi[0][:, L - 1::-1])
    b_re_, b_im_ = w1(1, pr[1][:, :L], pi[1][:, :L])
    w1t = jnp.concatenate([f_re, f_im, b_re_, b_im_], axis=1)

    def w2(d, qr, qi):
        re = cr[d][:, None] * qr[:, :, None] - ci[d][:, None] * qi[:, :, None]
        im = -(cr[d][:, None] * qi[:, :, None] + ci[d][:, None] * qr[:, :, None])
        to = lambda v: v.reshape(g, fl, p)
        return to(re), to(im)
    rf_re, rf_im = w2(0, pr[0][:, 1:], pi[0][:, 1:])
    rb_re, rb_im = w2(1, pr[1][:, :0:-1], pi[1][:, :0:-1])
    w2t = jnp.concatenate([rf_re, rf_im, rb_re, rb_im], axis=2)

    a4 = jnp.stack([pr[0][:, L], pi[0][:, L], pr[1][:, L], pi[1][:, L]], axis=1)
    ap = a4.reshape(g // 2, 2, 4, p).transpose(0, 2, 1, 3).reshape(g // 2, 4, 2 * p)
    ap = jnp.pad(ap, ((0, 0), (0, 4), (0, 0)))
    return lag0, bb.astype(BF16), w1t.astype(BF16), w2t.astype(BF16), ap


def _cmul(ar, ai, br, bi):
    return ar * br - ai * bi, ar * bi + ai * br


def _ssm_kernel(ut_ref, w1_ref, lag0_ref, bb_ref, w2_ref, a_ref, yt_ref, mt_s, kk_s, zt_s, ht_s, *, nseg, L):
    p = SSM_STATE
    c = SSM_GROUP
    fl, n = ut_ref.shape[1:]
    nb = n // nseg
    nrow = nseg // 8
    st = [jnp.dot(w1_ref[q], ut_ref[q], preferred_element_type=F32) for q in range(2)]
    z = jnp.concatenate([st[q][k * p:(k + 1) * p] for k in range(4) for q in range(2)], axis=0)
    zt_s[...] = z.astype(BF16).T.astype(F32)
    per = LANES // c
    for q in range(2):
        kf = jnp.dot(w2_ref[q, :, 0:2 * p], bb_ref[q, 0:2 * p, :], preferred_element_type=F32)
        kb = jnp.dot(w2_ref[q, :, 2 * p:4 * p], bb_ref[q, 2 * p:4 * p, :], preferred_element_type=F32)
        tiles = [None] * (2 * L // per)
        for j in range(2 * L - 1):
            lag = L - 1 - j
            if lag > 0:
                blk = kf[(lag - 1) * c:lag * c, :]
            elif lag == 0:
                blk = lag0_ref[q]
            else:
                blk = kb[(L + lag) * c:(L + lag + 1) * c, :]
            sh = (j % per) * c
            blk = blk if sh == 0 else pltpu.roll(blk, sh, 1)
            tiles[j // per] = blk if tiles[j // per] is None else tiles[j // per] + blk
        kk_s[...] = jnp.concatenate(tiles, axis=1).astype(BF16)
        for t in range(L):
            mt_s[q, t * c:(t + 1) * c, :] = kk_s[:, (L - 1 - t) * c:(L - 1 - t) * c + fl]

    sub = lax.broadcasted_iota(I32, (8, LANES), 0)
    full = lambda v: jnp.broadcast_to(v, (8, LANES))
    one, zero = jnp.ones((8, LANES), F32), jnp.zeros((8, LANES), F32)

    def powers(row):
        out = [(full(a_ref[0, row:row + 1, :]), full(a_ref[0, row + 1:row + 2, :]))]
        for _ in range(3):
            out.append(_cmul(*out[-1], *out[-1]))
        return out

    def by_bits(pw, idx):
        qr, qi = one, zero
        for bit in range(3):
            on = (idx >> bit) & 1 == 1
            qr, qi = _cmul(qr, qi, jnp.where(on, pw[bit][0], one), jnp.where(on, pw[bit][1], zero))
        return qr, qi

    pw_f, pw_b = powers(0), powers(2)
    qf = by_bits(pw_f, sub)
    qb = by_bits(pw_b, 7 - sub)

    def local(xr, xi, pw, forward):
        for lvl, d in enumerate((1, 2, 4)):
            keep = (sub >= d) if forward else (sub < 8 - d)
            sh = d if forward else 8 - d
            tr = jnp.where(keep, pltpu.roll(xr, sh, 0), 0.0)
            ti = jnp.where(keep, pltpu.roll(xi, sh, 0), 0.0)
            mr, mi = _cmul(pw[lvl][0], pw[lvl][1], tr, ti)
            xr, xi = xr + mr, xi + mi
        return xr, xi

    def step(i, carry):
        new = []
        for b in range(nb):
            for fwd in (True, False):
                cr, ci = carry[2 * (2 * b + (0 if fwd else 1)):][:2]
                row = b * nseg + (i if fwd else nrow - 1 - i) * 8
                rows = pl.ds(pl.multiple_of(row, 8), 8)
                l0 = 0 if fwd else 2 * LANES
                pw, (qr, qi) = (pw_f, qf) if fwd else (pw_b, qb)
                lr, li = local(zt_s[rows, l0:l0 + LANES], zt_s[rows, l0 + LANES:l0 + 2 * LANES], pw, fwd)
                keep = (sub >= 1) if fwd else (sub < 7)
                er = jnp.where(keep, pltpu.roll(lr, 1 if fwd else 7, 0), 0.0)
                ei = jnp.where(keep, pltpu.roll(li, 1 if fwd else 7, 0), 0.0)
                mr, mi = _cmul(qr, qi, cr, ci)
                ht_s[rows, l0:l0 + LANES] = er + mr
                ht_s[rows, l0 + LANES:l0 + 2 * LANES] = ei + mi
                edge = 7 if fwd else 0
                mr, mi = _cmul(pw[3][0], pw[3][1], cr, ci)
                new += [full(lr[edge:edge + 1, :]) + mr, full(li[edge:edge + 1, :]) + mi]
        return tuple(new)

    lax.fori_loop(0, nrow, step, (zero,) * (4 * nb))

    h = ht_s[...].astype(BF16).T
    for q in range(2):
        hq = jnp.concatenate([h[(2 * k + q) * p:(2 * k + q + 1) * p] for k in range(4)], axis=0)
        y = jnp.dot(mt_s[q], ut_ref[q], preferred_element_type=F32) \
            + jnp.dot(w2_ref[q], hq, preferred_element_type=F32)
        yt_ref[q] = y.astype(BF16)


def _ssm(ut3, lag0, bb, w1t, w2t, ap, nseg, L):
    g, fl, n = ut3.shape
    sp = w1t.shape[1]
    return pl.pallas_call(
        functools.partial(_ssm_kernel, nseg=nseg, L=L),
        out_shape=jax.ShapeDtypeStruct((g, fl, n), BF16),
        grid=(g // 2,),
        in_specs=[pl.BlockSpec((2, fl, n), lambda i: (i, 0, 0)),
                  pl.BlockSpec((2, sp, fl), lambda i: (i, 0, 0)),
                  pl.BlockSpec((2,) + lag0.shape[1:], lambda i: (i, 0, 0)),
                  pl.BlockSpec((2,) + bb.shape[1:], lambda i: (i, 0, 0)),
                  pl.BlockSpec((2, fl, sp), lambda i: (i, 0, 0)),
                  pl.BlockSpec((1,) + ap.shape[1:], lambda i: (i, 0, 0))],
        out_specs=pl.BlockSpec((2, fl, n), lambda i: (i, 0, 0)),
        scratch_shapes=[pltpu.VMEM((2, fl, fl), BF16),
                        pltpu.VMEM((SSM_GROUP, 2 * fl), BF16),
                        pltpu.VMEM((n, 2 * sp), F32),
                        pltpu.VMEM((n, 2 * sp), F32)],
        compiler_params=_cparams(("arbitrary",)),
        name="ssm",
    )(ut3, w1t, lag0, bb, w2t, ap)


def _glu_kernel(yt_ref, w_ref, b_ref, out_ref, nat_ref, *, sub):
    g, L, c, r = yt_ref.shape
    nj = nat_ref.shape[0]
    gj = LANES // c
    for t in range(L):
        for j in range(nj):
            piece = yt_ref[j * gj:(j + 1) * gj, t, :, :].reshape(LANES, r).astype(F32)
            nat_ref[j, pl.ds(t, r, stride=L), :] = piece.T

    def body(k, carry):
        sl = pl.ds(pl.multiple_of(k * sub, sub), sub)
        y = jnp.concatenate([nat_ref[j, sl, :] for j in range(nj)], axis=1)
        ya = jax.nn.gelu(y, approximate=True)
        z = jnp.dot(ya.astype(BF16), w_ref[...], preferred_element_type=F32) + b_ref[...]
        out_ref[sl, :] = (ya * (1.0 / (1.0 + jnp.exp(-z)))).astype(BF16)
        return carry
    lax.fori_loop(0, nat_ref.shape[1] // sub, body, 0)


def _glu(yt4, glu_w, glu_b):
    g, L, c, n = yt4.shape
    ds = g * c
    r = LANES
    return pl.pallas_call(
        functools.partial(_glu_kernel, sub=512),
        out_shape=jax.ShapeDtypeStruct((n * L, ds), BF16),
        grid=(n // r,),
        in_specs=[pl.BlockSpec((g, L, c, r), lambda j: (0, 0, 0, j)),
                  pl.BlockSpec((ds, ds), lambda j: (0, 0)),
                  pl.BlockSpec((1, ds), lambda j: (0, 0))],
        out_specs=pl.BlockSpec((r * L, ds), lambda j: (j, 0)),
        scratch_shapes=[pltpu.VMEM((ds // LANES, r * L, LANES), F32)],
        compiler_params=_cparams(("arbitrary",)),
        name="glu",
    )(yt4, glu_w, glu_b)


def _layer_norm(r, g, b):
    mu = jnp.mean(r, axis=-1, keepdims=True)
    cen = r - mu
    var = jnp.mean(cen * cen, axis=-1, keepdims=True)
    return cen * lax.rsqrt(var + LN_EPS) * g + b


def _store_row_packed(ref, m, row0=0):
    rows, width = m.shape
    half = width // 2
    nc = half // LANES
    lo = lax.bitcast_convert_type(m[:, :half].astype(BF16).astype(F32), U32) >> 16
    hi = lax.bitcast_convert_type(m[:, half:].astype(BF16).astype(F32), U32) & jnp.uint32(0xFFFF0000)
    pk = lo | hi
    for c in range(nc):
        ref[pl.ds(row0 * nc + c, rows, stride=nc), :] = pk[:, c * LANES:(c + 1) * LANES]


def _load_row_packed(ref, rows, nc, row0=0):
    los, his = [], []
    for c in range(nc):
        u = ref[pl.ds(row0 * nc + c, rows, stride=nc), :]
        los.append(lax.bitcast_convert_type(u << 16, F32))
        his.append(lax.bitcast_convert_type(u & jnp.uint32(0xFFFF0000), F32))
    return los, his


def _outproj_kernel(yp_ref, ys_ref, x_ref, wo_ref, g_ref, b_ref, rw_ref, rb_ref,
                    hp_ref, lt_ref, *, alpha, dp, sub):
    for sb in range(x_ref.shape[0] // sub):
        sl = pl.ds(sb * sub, sub)
        mix = jnp.dot(yp_ref[sl, :], wo_ref[0:dp, :], preferred_element_type=F32) \
            + jnp.dot(ys_ref[sl, :], wo_ref[dp:, :], preferred_element_type=F32)
        h = _layer_norm(alpha * x_ref[sl, :] + mix, g_ref[...], b_ref[...])
        _store_row_packed(hp_ref, h, row0=sb * sub)
        hh = h.astype(BF16)
        hl = (h - hh.astype(F32)).astype(BF16)
        part = jnp.dot(hh, rw_ref[...], preferred_element_type=F32) \
            + jnp.dot(hl, rw_ref[...], preferred_element_type=F32)
        logits = part + pltpu.roll(part, ROUTER_LANES // 2, 1) + rb_ref[...]
        lt_ref[:, sb * sub:(sb + 1) * sub] = logits.T[0:ROUTER_ROWS, :]


def _outproj(y_pool, y_ssm, xt, w_out, ln_g, ln_b, rw, rb, alpha):
    t, d = xt.shape
    dp = y_pool.shape[1]
    tm = min(512, t)
    pr = d // 2 // LANES
    return pl.pallas_call(
        functools.partial(_outproj_kernel, alpha=alpha, dp=dp, sub=256),
        out_shape=(jax.ShapeDtypeStruct((t * pr, LANES), U32),
                   jax.ShapeDtypeStruct((ROUTER_ROWS, t), F32)),
        grid=(t // tm,),
        in_specs=[pl.BlockSpec((tm, dp), lambda i: (i, 0)),
                  pl.BlockSpec((tm, y_ssm.shape[1]), lambda i: (i, 0)),
                  pl.BlockSpec((tm, d), lambda i: (i, 0)),
                  pl.BlockSpec(w_out.shape, lambda i: (0, 0)),
                  pl.BlockSpec((1, d), lambda i: (0, 0)),
                  pl.BlockSpec((1, d), lambda i: (0, 0)),
                  pl.BlockSpec(rw.shape, lambda i: (0, 0)),
                  pl.BlockSpec((1, ROUTER_LANES), lambda i: (0, 0))],
        out_specs=(pl.BlockSpec((tm * pr, LANES), lambda i: (i, 0)),
                   pl.BlockSpec((ROUTER_ROWS, tm), lambda i: (0, i))),
        compiler_params=_cparams(("arbitrary",)),
        name="outproj",
    )(y_pool, y_ssm, xt, w_out, ln_g, ln_b, rw, rb)


def _router_kernel(lt_ref, eid_ref, rank_ref, gate_ref, cnt_ref, carry_ref):
    i = pl.program_id(0)
    ne, epg = N_EXPERTS, EXPERTS_PER_GROUP

    @pl.when(i == 0)
    def _():
        carry_ref[...] = jnp.zeros_like(carry_ref)

    lt = lt_ref[...]
    tt = lt.shape[1]
    gl = [lt[j:j + 1, :] for j in range(N_EXPERT_GROUPS)]
    gmax = jnp.maximum(jnp.maximum(gl[0], gl[1]), jnp.maximum(gl[2], gl[3]))
    grp = jnp.where(gl[0] == gmax, 0, jnp.where(gl[1] == gmax, 1, jnp.where(gl[2] == gmax, 2, 3)))
    p_grp = 1.0 / (jnp.exp(gl[0] - gmax) + jnp.exp(gl[1] - gmax)
                   + jnp.exp(gl[2] - gmax) + jnp.exp(gl[3] - gmax))
    eg = [lt[8 + epg * j: 8 + epg * (j + 1), :] for j in range(N_EXPERT_GROUPS)]
    el = jnp.where(grp == 0, eg[0], jnp.where(grp == 1, eg[1], jnp.where(grp == 2, eg[2], eg[3])))
    sub = lax.broadcasted_iota(I32, (epg, tt), 0)
    m1 = jnp.max(el, axis=0, keepdims=True)
    i1 = jnp.min(jnp.where(el == m1, sub, epg), axis=0, keepdims=True)
    rest = jnp.where(sub == i1, -jnp.inf, el)
    m2 = jnp.max(rest, axis=0, keepdims=True)
    i2 = jnp.min(jnp.where(rest == m2, sub, epg), axis=0, keepdims=True)
    r21 = jnp.exp(m2 - m1)
    g1 = p_grp / (1.0 + r21)
    g2 = g1 * r21
    e1 = grp * epg + i1
    e2 = grp * epg + i2

    rows = lax.broadcasted_iota(I32, (ne, tt), 0)
    oh1 = rows == e1
    oh2 = rows == e2
    oh = jnp.where(oh1 | oh2, 1.0, 0.0)
    tri = jnp.where(lax.broadcasted_iota(I32, (tt, tt), 0) < lax.broadcasted_iota(I32, (tt, tt), 1),
                    1.0, 0.0).astype(BF16)
    before = jnp.dot(oh.astype(BF16), tri, preferred_element_type=F32) + carry_ref[:, 0:1]
    r1 = jnp.sum(jnp.where(oh1, before, 0.0), axis=0, keepdims=True)
    r2 = jnp.sum(jnp.where(oh2, before, 0.0), axis=0, keepdims=True)
    carry_ref[...] = carry_ref[...] + jnp.sum(oh, axis=1, keepdims=True)

    eid_ref[...] = jnp.concatenate([e1, e2], axis=0)
    rank_ref[...] = jnp.concatenate([r1, r2], axis=0).astype(I32)
    gate_ref[...] = jnp.concatenate([g1, g2], axis=0)
    cnt_ref[...] = carry_ref[...].astype(I32)


def _router(lt):
    rr, t = lt.shape
    tt = min(512, t)
    return pl.pallas_call(
        _router_kernel,
        out_shape=(jax.ShapeDtypeStruct((2, t), I32),
                   jax.ShapeDtypeStruct((2, t), I32),
                   jax.ShapeDtypeStruct((2, t), F32),
                   jax.ShapeDtypeStruct((N_EXPERTS, 128), I32)),
        grid=(t // tt,),
        in_specs=[pl.BlockSpec((rr, tt), lambda i: (0, i))],
        out_specs=(pl.BlockSpec((2, tt), lambda i: (0, i)),
                   pl.BlockSpec((2, tt), lambda i: (0, i)),
                   pl.BlockSpec((2, tt), lambda i: (0, i)),
                   pl.BlockSpec((N_EXPERTS, 128), lambda i: (0, 0))),
        scratch_shapes=[pltpu.VMEM((N_EXPERTS, 128), F32)],
        compiler_params=_cparams(("arbitrary",)),
        name="router",
    )(lt)


def _invert_kernel(dest_ref, cnt_ref, pc_ref, out_ref):
    t = dest_ref.shape[0] // 2

    def place(i, carry):
        out_ref[dest_ref[i]] = i
        out_ref[dest_ref[t + i]] = i
        return carry
    lax.fori_loop(0, t, place, 0, unroll=8)

    def fill(s, carry):
        out_ref[s] = 0
        return carry

    def pad(e, start):
        lax.fori_loop(start + cnt_ref[e], start + pc_ref[e], fill, 0)
        return start + pc_ref[e]
    used = lax.fori_loop(0, N_EXPERTS, pad, 0)
    lax.fori_loop(used, out_ref.shape[0], fill, 0)


def _invert(dest_flat, counts, pcounts, n_slots):
    return pl.pallas_call(
        _invert_kernel,
        out_shape=jax.ShapeDtypeStruct((n_slots,), I32),
        grid_spec=pltpu.PrefetchScalarGridSpec(
            num_scalar_prefetch=3, grid=(1,), in_specs=[],
            out_specs=pl.BlockSpec(memory_space=pltpu.SMEM)),
        compiler_params=_cparams(("arbitrary",)),
        name="invert",
    )(dest_flat, counts, pcounts)


def _moe_kernel(be_ref, nu_ref, pc_ref, st_ref, hp_hbm, wg_hbm, wu_hbm, wd_hbm, y_ref,
                gbuf, wgf, wuf, wdf, wgb, wub, wdb, par_ref, gsem, wsem):
    for half in range(MOE_BLOCKS_PER_STEP):
        _moe_block(pl.program_id(0) * MOE_BLOCKS_PER_STEP + half, half * MOE_BLK,
                   be_ref, nu_ref, pc_ref, st_ref, hp_hbm, wg_hbm, wu_hbm, wd_hbm, y_ref,
                   gbuf, wgf, wuf, wdf, wgb, wub, wdb, par_ref, gsem, wsem)


def _moe_block(b, row0, be_ref, nu_ref, pc_ref, st_ref, hp_hbm, wg_hbm, wu_hbm, wd_hbm, y_ref,
               gbuf, wgf, wuf, wdf, wgb, wub, wdb, par_ref, gsem, wsem):
    nu = nu_ref[0]
    pr = gbuf.shape[1] // MOE_BLK
    blk = MOE_BLK
    nbuf = gbuf.shape[0]
    look = nbuf - 1

    def row_copy(tok, slot, r):
        src = hp_hbm.at[pl.ds(pl.multiple_of(tok * pr, pr), pr)]
        off = r * pr if isinstance(r, int) else pl.multiple_of(r * pr, pr)
        return pltpu.make_async_copy(src, gbuf.at[slot, pl.ds(off, pr)], gsem.at[slot])

    def issue(block, slot):
        base = block * blk

        def body(r, carry):
            row_copy(st_ref[base + r], slot, r).start()
            return carry
        lax.fori_loop(0, blk, body, 0, unroll=8)

    def weight_copies(e, p):
        return (pltpu.make_async_copy(wg_hbm.at[e], wgf.at[p], wsem.at[p]),
                pltpu.make_async_copy(wu_hbm.at[e], wuf.at[p], wsem.at[p]),
                pltpu.make_async_copy(wd_hbm.at[e], wdf.at[p], wsem.at[p]))

    @pl.when(b == 0)
    def _():
        par_ref[0] = 0
        for cp in weight_copies(be_ref[0], 0):
            cp.start(priority=1)
        issue(0, 0)
        for j in range(1, look):
            @pl.when(j < nu)
            def _():
                issue(j, j)

    def issue_part(k):
        base = (b + look) * blk
        for r in range(k * blk // 4, (k + 1) * blk // 4):
            row_copy(st_ref[base + r], (b + look) % nbuf, r).start(priority=1 if r % 3 == 2 else 0)

    def expert_mlp(slot, with_issue):
        los, his = _load_row_packed(gbuf.at[slot], blk, pr)
        x = jnp.concatenate([v.astype(BF16) for v in los + his], axis=1)
        if with_issue:
            issue_part(0)
        hg = jnp.dot(x, wgb[...], preferred_element_type=F32)
        if with_issue:
            issue_part(1)
        hu = jnp.dot(x, wub[...], preferred_element_type=F32)
        if with_issue:
            issue_part(2)
        hh = hg * (1.0 / (1.0 + jnp.exp(-hg))) * hu
        y = jnp.dot(hh.astype(BF16), wdb[...], preferred_element_type=F32)
        if with_issue:
            issue_part(3)
        _store_row_packed(y_ref, y, row0=row0)

    @pl.when(b < nu)
    def _():
        slot = b % nbuf
        e = be_ref[b]
        first = (b == 0) | (e != be_ref[jnp.maximum(b - 1, 0)])

        @pl.when(first)
        def _():
            p = par_ref[0]
            for cp in weight_copies(e, p):
                cp.wait()
            nxt = lax.while_loop(
                lambda c: (c < N_EXPERTS) & (pc_ref[jnp.minimum(c, N_EXPERTS - 1)] == 0),
                lambda c: c + 1, e + 1)

            @pl.when(nxt < N_EXPERTS)
            def _():
                for cp in weight_copies(nxt, 1 - p):
                    cp.start(priority=1)
            wgb[...] = wgf[p].astype(BF16)
            wub[...] = wuf[p].astype(BF16)
            wdb[...] = wdf[p].astype(BF16)
            par_ref[0] = 1 - p

        pltpu.make_async_copy(hp_hbm.at[pl.ds(0, blk * pr)], gbuf.at[slot], gsem.at[slot]).wait()

        @pl.when(b + look < nu)
        def _():
            expert_mlp(slot, True)

        @pl.when(b + look >= nu)
        def _():
            expert_mlp(slot, False)

    @pl.when(b >= nu)
    def _():
        y_ref[pl.ds(row0 * pr, blk * pr), :] = jnp.zeros((blk * pr, LANES), U32)


def _moe(block_e, n_used, pcounts, slot_tok, hp, w_gate, w_up, w_down, n_blocks):
    ne, d, de = w_gate.shape
    blk = MOE_BLK
    pr = d // 2 // LANES
    bps = MOE_BLOCKS_PER_STEP
    assert n_blocks % bps == 0
    grid_spec = pltpu.PrefetchScalarGridSpec(
        num_scalar_prefetch=4,
        grid=(n_blocks // bps,),
        in_specs=[pl.BlockSpec(memory_space=pl.ANY)] * 4,
        out_specs=pl.BlockSpec((bps * blk * pr, LANES), lambda b, *_: (b, 0)),
        scratch_shapes=[pltpu.VMEM((GATHER_BUFS, blk * pr, LANES), U32),
                        pltpu.VMEM((2, d, de), F32),
                        pltpu.VMEM((2, d, de), F32),
                        pltpu.VMEM((2, de, d), F32),
                        pltpu.VMEM((d, de), BF16),
                        pltpu.VMEM((d, de), BF16),
                        pltpu.VMEM((de, d), BF16),
                        pltpu.SMEM((1,), I32),
                        pltpu.SemaphoreType.DMA((GATHER_BUFS,)),
                        pltpu.SemaphoreType.DMA((2,))],
    )
    return pl.pallas_call(
        _moe_kernel,
        out_shape=jax.ShapeDtypeStruct((n_blocks * blk * pr, LANES), U32),
        grid_spec=grid_spec,
        compiler_params=_cparams(("arbitrary",)),
        name="moe",
    )(block_e, n_used, pcounts, slot_tok, hp, w_gate, w_up, w_down)


def _combine_kernel(d0_ref, d1_ref, hp_ref, gate_ref, yb_hbm, g_ref, b_ref, out_ref, buf, sem, *, alpha):
    i = pl.program_id(0)
    n = pl.num_programs(0)
    tm = out_ref.shape[0]
    pr = buf.shape[2] // tm

    def issue(tile, slot):
        base = tile * tm

        def body(r, carry):
            dst = pl.ds(pl.multiple_of(r * pr, pr), pr)
            for k, dref in enumerate((d0_ref, d1_ref)):
                src = yb_hbm.at[pl.ds(pl.multiple_of(dref[base + r] * pr, pr), pr)]
                pltpu.make_async_copy(src, buf.at[slot, k, dst], sem.at[slot]).start()
            return carry
        lax.fori_loop(0, tm, body, 0, unroll=8)

    nbuf = buf.shape[0]
    look = nbuf - 1

    @pl.when(i == 0)
    def _():
        issue(0, 0)
        for j in range(1, look):
            @pl.when(j < n)
            def _():
                issue(j, j)

    slot = i % nbuf
    for k in range(2):
        pltpu.make_async_copy(yb_hbm.at[pl.ds(0, tm * pr)], buf.at[slot, k], sem.at[slot]).wait()

    nq = 4
    sub = tm // nq

    def issue_part(q):
        base = (i + look) * tm
        nslot = (i + look) % nbuf
        for r in range(q * sub, (q + 1) * sub):
            for k, dref in enumerate((d0_ref, d1_ref)):
                src = yb_hbm.at[pl.ds(pl.multiple_of(dref[base + r] * pr, pr), pr)]
                pltpu.make_async_copy(src, buf.at[nslot, k, pl.ds(r * pr, pr)], sem.at[nslot]).start(priority=k)

    def finish(with_issue):
        for q in range(nq):
            if with_issue:
                issue_part(q)
            rows = pl.ds(q * sub, sub)
            lo0, hi0 = _load_row_packed(buf.at[slot, 0], sub, pr, row0=q * sub)
            lo1, hi1 = _load_row_packed(buf.at[slot, 1], sub, pr, row0=q * sub)
            loh, hih = _load_row_packed(hp_ref, sub, pr, row0=q * sub)
            g0 = gate_ref[rows, 0:1]
            g1 = gate_ref[rows, 1:2]
            r = jnp.concatenate([alpha * hh + g0 * a + g1 * c
                                 for hh, a, c in zip(loh + hih, lo0 + hi0, lo1 + hi1)], axis=1)
            out_ref[rows, :] = _layer_norm(r, g_ref[...], b_ref[...])

    @pl.when(i + look < n)
    def _():
        finish(True)

    @pl.when(i + look >= n)
    def _():
        finish(False)


def _combine(dest0, dest1, hp, gate_t, yb, ln_g, ln_b, alpha):
    d = ln_g.shape[1]
    pr = d // 2 // LANES
    t = hp.shape[0] // pr
    tm = min(256, t)
    grid_spec = pltpu.PrefetchScalarGridSpec(
        num_scalar_prefetch=2,
        grid=(t // tm,),
        in_specs=[pl.BlockSpec((tm * pr, LANES), lambda i, d0, d1: (i, 0)),
                  pl.BlockSpec((tm, 2), lambda i, d0, d1: (i, 0)),
                  pl.BlockSpec(memory_space=pl.ANY),
                  pl.BlockSpec((1, d), lambda i, d0, d1: (0, 0)),
                  pl.BlockSpec((1, d), lambda i, d0, d1: (0, 0))],
        out_specs=pl.BlockSpec((tm, d), lambda i, d0, d1: (i, 0)),
        scratch_shapes=[pltpu.VMEM((GATHER_BUFS, 2, tm * (d // 2 // LANES), LANES), U32),
                        pltpu.SemaphoreType.DMA((GATHER_BUFS,))],
    )
    return pl.pallas_call(
        functools.partial(_combine_kernel, alpha=alpha),
        out_shape=jax.ShapeDtypeStruct((t, d), F32),
        grid_spec=grid_spec,
        compiler_params=_cparams(("arbitrary",)),
        name="combine",
    )(dest0, dest1, hp, gate_t, yb, ln_g, ln_b)


def _layer(h, w_in, pool_w, pool_scale, a_re, a_im, log_dt, b_re, b_im, c_re, c_im, d_skip,
           glu_w, glu_b, w_out, ln1_g, ln1_b, rg_w, rg_b, re_w, re_b, w_gate, w_up, w_down,
           ln2_g, ln2_b, alpha):
    bsz, seq, d = h.shape
    t = bsz * seq
    L = CHUNK
    dp = pool_w.shape[0] * pool_w.shape[1]
    ds = w_in.shape[1] - dp
    g = ds // SSM_GROUP
    n = t // L

    lag0, bb, w1t, w2t, al = _ssm_tables(a_re, a_im, log_dt, b_re, b_im, c_re, c_im, d_skip, L)
    half = ROUTER_LANES // 2
    zpad = lambda k: jnp.zeros((d, k), F32)
    rw = jnp.concatenate([rg_w.astype(F32), zpad(8 - N_EXPERT_GROUPS), re_w.astype(F32),
                          zpad(half - ROUTER_ROWS)], axis=1)
    rw_hi = rw.astype(BF16)
    rw_lo = (rw - rw_hi.astype(F32)).astype(BF16)
    rw2 = jnp.concatenate([rw_hi, rw_lo], axis=1)
    rb = jnp.concatenate([rg_b.astype(F32), jnp.zeros((8 - N_EXPERT_GROUPS,), F32), re_b.astype(F32),
                          jnp.zeros((ROUTER_LANES - ROUTER_ROWS,), F32)]).reshape(1, ROUTER_LANES)

    xt = h.reshape(t, d)
    pool_p, ut = _proj(xt, w_in.astype(BF16), dp, L)
    y_pool = _pool(pool_p, pool_w.astype(BF16), pool_scale.reshape(1, dp).astype(F32), seq)
    yt = _ssm(ut.reshape(g, L * SSM_GROUP, n), lag0, bb, w1t, w2t, al, seq // L, L)
    y_ssm = _glu(yt.reshape(g, L, SSM_GROUP, n), glu_w.astype(BF16), glu_b.reshape(1, ds).astype(F32))
    hp, lt = _outproj(y_pool, y_ssm, xt, w_out.astype(BF16), ln1_g.reshape(1, d), ln1_b.reshape(1, d),
                          rw2, rb, alpha)
    eid, rank, gate, cnt = _router(lt)

    blk = MOE_BLK
    m = 2 * t
    n_blocks = -(-m // blk) + N_EXPERTS
    counts = cnt[:, 0]
    pcounts = (counts + blk - 1) // blk * blk
    pends = jnp.cumsum(pcounts)
    pstarts = pends - pcounts
    e_ids = jnp.arange(N_EXPERTS, dtype=I32)
    dest = jnp.sum(jnp.where(eid[..., None] == e_ids, pstarts, 0), axis=-1) + rank
    slot_tok = _invert(dest.reshape(-1), counts, pcounts, n_blocks * blk)
    n_used = (pends[-1] // blk).astype(I32)
    bidx = jnp.minimum(jnp.arange(n_blocks, dtype=I32), n_used - 1)
    block_e = jnp.minimum(jnp.sum((pends[None, :] <= (bidx * blk)[:, None]).astype(I32), axis=1),
                          N_EXPERTS - 1)

    yb = _moe(block_e, n_used.reshape(1), pcounts, slot_tok, hp, w_gate, w_up, w_down, n_blocks)
    out = _combine(dest[0], dest[1], hp, gate.T, yb, ln2_g.reshape(1, d), ln2_b.reshape(1, d), alpha)
    return out.reshape(bsz, seq, d)


def kernel(x, w_in, pool_w, pool_scale, ssm_a_re, ssm_a_im, ssm_log_dt, ssm_b_re, ssm_b_im, ssm_c_re, ssm_c_im, ssm_d, glu_w, glu_b, w_out, ln1_g, ln1_b, router_g_w, router_g_b, router_e_w, router_e_b, w_gate, w_up, w_down, ln2_g, ln2_b):
    depth = w_in.shape[0]
    alpha = (2.0 * depth) ** 0.25
    h = x
    for l in range(depth):
        h = _layer(h, w_in[l], pool_w[l], pool_scale[l], ssm_a_re[l], ssm_a_im[l], ssm_log_dt[l],
                   ssm_b_re[l], ssm_b_im[l], ssm_c_re[l], ssm_c_im[l], ssm_d[l], glu_w[l], glu_b[l],
                   w_out[l], ln1_g[l], ln1_b[l], router_g_w[l], router_g_b[l], router_e_w[l],
                   router_e_b[l], w_gate[l], w_up[l], w_down[l], ln2_g[l], ln2_b[l], alpha)
    return h
```

```python
import functools
import math

import numpy as np
import jax
import jax.numpy as jnp
from jax import lax
from jax.experimental import pallas as pl
from jax.experimental.pallas import tpu as pltpu

F32 = jnp.float32
BF16 = jnp.bfloat16
I32 = jnp.int32
U32 = jnp.uint32

POOL_WINDOWS = (2, 4, 8, 16)
POOL_GROUP = 256
SSM_GROUP = 16
SSM_STATE = 64
N_EXPERT_GROUPS = 4
EXPERTS_PER_GROUP = 8
N_EXPERTS = N_EXPERT_GROUPS * EXPERTS_PER_GROUP
LN_EPS = 1e-5

CHUNK = 16
MOE_BLK = 256
MOE_BLOCKS_PER_STEP = 2
GATHER_BUFS = 4
ROUTER_ROWS = 8 + N_EXPERTS
ROUTER_LANES = 128
LANES = 128
MXU_N = 256
HALO = 16
VMEM_LIMIT = 56 * 1024 * 1024


def _cparams(sem, vmem=VMEM_LIMIT):
    return pltpu.CompilerParams(dimension_semantics=sem, vmem_limit_bytes=vmem)


def _proj_kernel(x_ref, w_ref, pool_ref, ut_ref, accp_ref, accs_ref):
    kk = pl.program_id(1)
    rows, dp = accp_ref.shape
    nj = accs_ref.shape[0]

    xb = x_ref[...].astype(BF16)
    per = MXU_N // LANES

    def accumulate(first):
        for n in range(dp // MXU_N):
            cs = slice(n * MXU_N, (n + 1) * MXU_N)
            part = jnp.dot(xb, w_ref[:, cs], preferred_element_type=F32)
            accp_ref[:, cs] = part if first else accp_ref[:, cs] + part
        for n in range(nj // per):
            part = jnp.dot(xb, w_ref[:, dp + n * MXU_N: dp + (n + 1) * MXU_N], preferred_element_type=F32)
            for q in range(per):
                piece = part[:, q * LANES:(q + 1) * LANES]
                accs_ref[n * per + q] = piece if first else accs_ref[n * per + q] + piece

    @pl.when(kk == 0)
    def _():
        accumulate(True)

    @pl.when(kk > 0)
    def _():
        accumulate(False)

    @pl.when(kk == pl.num_programs(1) - 1)
    def _():
        pool_ref[...] = accp_ref[...].astype(BF16)
        g, L, c, r = ut_ref.shape
        gj = LANES // c
        for j in range(nj):
            steps = pltpu.einshape("rsl->srl", accs_ref[j].reshape(r, L, LANES))
            for s in range(L):
                ut_ref[j * gj:(j + 1) * gj, s, :, :] = steps[s].astype(BF16).T.reshape(gj, c, r)


def _proj(xt, w_in, dp, L):
    t, d = xt.shape
    dm = w_in.shape[1]
    g = (dm - dp) // SSM_GROUP
    tm = LANES * L
    kb = 512
    return pl.pallas_call(
        _proj_kernel,
        out_shape=(jax.ShapeDtypeStruct((t, dp), BF16),
                   jax.ShapeDtypeStruct((g, L, SSM_GROUP, t // L), BF16)),
        grid=(t // tm, d // kb),
        in_specs=[pl.BlockSpec((tm, kb), lambda i, k: (i, k)),
                  pl.BlockSpec((kb, dm), lambda i, k: (k, 0))],
        out_specs=(pl.BlockSpec((tm, dp), lambda i, k: (i, 0)),
                   pl.BlockSpec((g, L, SSM_GROUP, LANES), lambda i, k: (0, 0, 0, i))),
        scratch_shapes=[pltpu.VMEM((tm, dp), F32),
                        pltpu.VMEM(((dm - dp) // LANES, tm, LANES), F32)],
        compiler_params=_cparams(("arbitrary", "arbitrary")),
        name="proj",
    )(xt, w_in)


def _pool_kernel(prev_ref, main_ref, next_ref, pw_ref, sc_ref, out_ref, *, seq, sub):
    i = pl.program_id(0)
    ts = main_ref.shape[0]
    ext = jnp.concatenate([prev_ref[...], main_ref[...], next_ref[...]], axis=0)
    k = sub + 2 * HALO
    row = lax.broadcasted_iota(I32, (sub, k), 0)
    col = lax.broadcasted_iota(I32, (sub, k), 1)
    off0 = col - HALO - row
    for sb in range(ts // sub):
        rows = slice(sb * sub, (sb + 1) * sub)
        base = (i * ts + sb * sub) % seq
        src = base + row + off0
        off = jnp.where((src >= 0) & (src < seq), off0, 2 * HALO)
        pos = base + lax.broadcasted_iota(I32, (sub, 1), 0)
        for g, w in enumerate(POOL_WINDOWS):
            sl = slice(g * POOL_GROUP, (g + 1) * POOL_GROUP)
            band = (off + w // 2).astype(U32) < w
            bm = jnp.where(band, 1.0, 0.0).astype(BF16)
            sums = jnp.dot(bm, ext[sb * sub:sb * sub + k, sl], preferred_element_type=F32)
            lo = jnp.maximum(pos - w // 2, 0)
            hi = jnp.minimum(pos - w // 2 + w, seq)
            inv = 1.0 / (hi - lo).astype(F32)
            dlt = sums * inv - main_ref[rows, sl].astype(F32)
            y = jnp.dot(dlt.astype(BF16), pw_ref[g], preferred_element_type=F32) * sc_ref[:, sl]
            out_ref[rows, sl] = y.astype(BF16)


def _pool(pp, pool_w, pool_scale, seq):
    t, dp = pp.shape
    ts = min(1024, seq)
    nh = t // HALO
    per = ts // HALO
    return pl.pallas_call(
        functools.partial(_pool_kernel, seq=seq, sub=256),
        out_shape=jax.ShapeDtypeStruct((t, dp), BF16),
        grid=(t // ts,),
        in_specs=[pl.BlockSpec((HALO, dp), lambda i: (jnp.maximum(i * per - 1, 0), 0)),
                  pl.BlockSpec((ts, dp), lambda i: (i, 0)),
                  pl.BlockSpec((HALO, dp), lambda i: (jnp.minimum((i + 1) * per, nh - 1), 0)),
                  pl.BlockSpec(pool_w.shape, lambda i: (0, 0, 0)),
                  pl.BlockSpec((1, dp), lambda i: (0, 0))],
        out_specs=pl.BlockSpec((ts, dp), lambda i: (i, 0)),
        compiler_params=_cparams(("arbitrary",)),
        name="pool",
    )(pp, pp, pp, pool_w, pool_scale)


def _ssm_tables(a_re, a_im, log_dt, b_re, b_im, c_re, c_im, d_skip, L):
    g = a_re.shape[1]
    p = a_re.shape[2]
    c = b_re.shape[3]
    fl = L * c
    lr = a_re.astype(F32)
    li = a_im.astype(F32)
    dt = jnp.exp(log_dt.astype(F32))[..., None]
    mag = jnp.exp(lr * dt)
    abr = mag * jnp.cos(li * dt)
    abi = mag * jnp.sin(li * dt)
    den = lr * lr + li * li
    zr = ((abr - 1.0) * lr + abi * li) / den
    zi = (abi * lr - (abr - 1.0) * li) / den
    br = b_re.astype(F32)
    bi = b_im.astype(F32)
    bbr = zr[..., None] * br - zi[..., None] * bi
    bbi = zr[..., None] * bi + zi[..., None] * br
    cr = c_re.astype(F32)
    ci = c_im.astype(F32)
    kk = jnp.arange(L + 1, dtype=F32)[None, None, :, None]
    pm = jnp.exp(kk * (lr * dt)[:, :, None, :])
    ang = kk * (li * dt)[:, :, None, :]
    pr = pm * jnp.cos(ang)
    pi = pm * jnp.sin(ang)

    skip = jnp.eye(c, dtype=F32)[None] * d_skip.astype(F32).reshape(g, c, 1)
    lag0 = jnp.einsum('dgop,dgpc->goc', cr, bbr, precision=lax.Precision.HIGHEST) \
        - jnp.einsum('dgop,dgpc->goc', ci, bbi, precision=lax.Precision.HIGHEST) + skip
    lag0 = jnp.pad(lag0, ((0, 0), (0, 0), (0, LANES - c)))
    bb = jnp.concatenate([bbr[0], bbi[0], bbr[1], bbi[1]], axis=1)
    bb = jnp.pad(bb, ((0, 0), (0, 0), (0, LANES - c)))

    def w1(d, qr, qi):
        qr = qr.transpose(0, 2, 1)[..., None]
        qi = qi.transpose(0, 2, 1)[..., None]
        re = qr * bbr[d][:, :, None] - qi * bbi[d][:, :, None]
        im = qr * bbi[d][:, :, None] + qi * bbr[d][:, :, None]
        to = lambda v: v.reshape(g, p, fl)
        return to(re), to(im)
    f_re, f_im = w1(0, pr[0][:, L - 1::-1], pi[0][:, L - 1::-1])
    b_re_, b_im_ = w1(1, pr[1][:, :L], pi[1][:, :L])
    w1t = jnp.concatenate([f_re, f_im, b_re_, b_im_], axis=1)

    def w2(d, qr, qi):
        re = cr[d][:, None] * qr[:, :, None] - ci[d][:, None] * qi[:, :, None]
        im = -(cr[d][:, None] * qi[:, :, None] + ci[d][:, None] * qr[:, :, None])
        to = lambda v: v.reshape(g, fl, p)
        return to(re), to(im)
    rf_re, rf_im = w2(0, pr[0][:, 1:], pi[0][:, 1:])
    rb_re, rb_im = w2(1, pr[1][:, :0:-1], pi[1][:, :0:-1])
    w2t = jnp.concatenate([rf_re, rf_im, rb_re, rb_im], axis=2)

    a4 = jnp.stack([pr[0][:, L], pi[0][:, L], pr[1][:, L], pi[1][:, L]], axis=1)
    ap = a4.reshape(g // 2, 2, 4, p).transpose(0, 2, 1, 3).reshape(g // 2, 4, 2 * p)
    ap = jnp.pad(ap, ((0, 0), (0, 4), (0, 0)))
    return lag0, bb.astype(BF16), w1t.astype(BF16), w2t.astype(BF16), ap


def _cmul(ar, ai, br, bi):
    return ar * br - ai * bi, ar * bi + ai * br


def _ssm_kernel(ut_ref, w1_ref, lag0_ref, bb_ref, w2_ref, a_ref, yt_ref, mt_s, kk_s, zt_s, ht_s, *, nseg, L):
    p = SSM_STATE
    c = SSM_GROUP
    fl, n = ut_ref.shape[1:]
    nb = n // nseg
    nrow = nseg // 8
    st = [jnp.dot(w1_ref[q], ut_ref[q], preferred_element_type=F32) for q in range(2)]
    z = jnp.concatenate([st[q][k * p:(k + 1) * p] for k in range(4) for q in range(2)], axis=0)
    zt_s[...] = z.astype(BF16).T.astype(F32)
    per = LANES // c
    for q in range(2):
        kf = jnp.dot(w2_ref[q, :, 0:2 * p], bb_ref[q, 0:2 * p, :], preferred_element_type=F32)
        kb = jnp.dot(w2_ref[q, :, 2 * p:4 * p], bb_ref[q, 2 * p:4 * p, :], preferred_element_type=F32)
        tiles = [None] * (2 * L // per)
        for j in range(2 * L - 1):
            lag = L - 1 - j
            if lag > 0:
                blk = kf[(lag - 1) * c:lag * c, :]
            elif lag == 0:
                blk = lag0_ref[q]
            else:
                blk = kb[(L + lag) * c:(L + lag + 1) * c, :]
            sh = (j % per) * c
            blk = blk if sh == 0 else pltpu.roll(blk, sh, 1)
            tiles[j // per] = blk if tiles[j // per] is None else tiles[j // per] + blk
        kk_s[...] = jnp.concatenate(tiles, axis=1).astype(BF16)
        for t in range(L):
            mt_s[q, t * c:(t + 1) * c, :] = kk_s[:, (L - 1 - t) * c:(L - 1 - t) * c + fl]

    sub = lax.broadcasted_iota(I32, (8, LANES), 0)
    full = lambda v: jnp.broadcast_to(v, (8, LANES))
    one, zero = jnp.ones((8, LANES), F32), jnp.zeros((8, LANES), F32)

    def powers(row):
        out = [(full(a_ref[0, row:row + 1, :]), full(a_ref[0, row + 1:row + 2, :]))]
        for _ in range(3):
            out.append(_cmul(*out[-1], *out[-1]))
        return out

    def by_bits(pw, idx):
        qr, qi = one, zero
        for bit in range(3):
            on = (idx >> bit) & 1 == 1
            qr, qi = _cmul(qr, qi, jnp.where(on, pw[bit][0], one), jnp.where(on, pw[bit][1], zero))
        return qr, qi

    pw_f, pw_b = powers(0), powers(2)
    qf = by_bits(pw_f, sub)
    qb = by_bits(pw_b, 7 - sub)

    def local(xr, xi, pw, forward):
        for lvl, d in enumerate((1, 2, 4)):
            keep = (sub >= d) if forward else (sub < 8 - d)
            sh = d if forward else 8 - d
            tr = jnp.where(keep, pltpu.roll(xr, sh, 0), 0.0)
            ti = jnp.where(keep, pltpu.roll(xi, sh, 0), 0.0)
            mr, mi = _cmul(pw[lvl][0], pw[lvl][1], tr, ti)
            xr, xi = xr + mr, xi + mi
        return xr, xi

    def step(i, carry):
        new = []
        for b in range(nb):
            for fwd in (True, False):
                cr, ci = carry[2 * (2 * b + (0 if fwd else 1)):][:2]
                row = b * nseg + (i if fwd else nrow - 1 - i) * 8
                rows = pl.ds(pl.multiple_of(row, 8), 8)
                l0 = 0 if fwd else 2 * LANES
                pw, (qr, qi) = (pw_f, qf) if fwd else (pw_b, qb)
                lr, li = local(zt_s[rows, l0:l0 + LANES], zt_s[rows, l0 + LANES:l0 + 2 * LANES], pw, fwd)
                keep = (sub >= 1) if fwd else (sub < 7)
                er = jnp.where(keep, pltpu.roll(lr, 1 if fwd else 7, 0), 0.0)
                ei = jnp.where(keep, pltpu.roll(li, 1 if fwd else 7, 0), 0.0)
                mr, mi = _cmul(qr, qi, cr, ci)
                ht_s[rows, l0:l0 + LANES] = er + mr
                ht_s[rows, l0 + LANES:l0 + 2 * LANES] = ei + mi
                edge = 7 if fwd else 0
                mr, mi = _cmul(pw[3][0], pw[3][1], cr, ci)
                new += [full(lr[edge:edge + 1, :]) + mr, full(li[edge:edge + 1, :]) + mi]
        return tuple(new)

    lax.fori_loop(0, nrow, step, (zero,) * (4 * nb))

    h = ht_s[...].astype(BF16).T
    for q in range(2):
        hq = jnp.concatenate([h[(2 * k + q) * p:(2 * k + q + 1) * p] for k in range(4)], axis=0)
        y = jnp.dot(mt_s[q], ut_ref[q], preferred_element_type=F32) \
            + jnp.dot(w2_ref[q], hq, preferred_element_type=F32)
        yt_ref[q] = y.astype(BF16)


def _ssm(ut3, lag0, bb, w1t, w2t, ap, nseg, L):
    g, fl, n = ut3.shape
    sp = w1t.shape[1]
    return pl.pallas_call(
        functools.partial(_ssm_kernel, nseg=nseg, L=L),
        out_shape=jax.ShapeDtypeStruct((g, fl, n), BF16),
        grid=(g // 2,),
        in_specs=[pl.BlockSpec((2, fl, n), lambda i: (i, 0, 0)),
                  pl.BlockSpec((2, sp, fl), lambda i: (i, 0, 0)),
                  pl.BlockSpec((2,) + lag0.shape[1:], lambda i: (i, 0, 0)),
                  pl.BlockSpec((2,) + bb.shape[1:], lambda i: (i, 0, 0)),
                  pl.BlockSpec((2, fl, sp), lambda i: (i, 0, 0)),
                  pl.BlockSpec((1,) + ap.shape[1:], lambda i: (i, 0, 0))],
        out_specs=pl.BlockSpec((2, fl, n), lambda i: (i, 0, 0)),
        scratch_shapes=[pltpu.VMEM((2, fl, fl), BF16),
                        pltpu.VMEM((SSM_GROUP, 2 * fl), BF16),
                        pltpu.VMEM((n, 2 * sp), F32),
                        pltpu.VMEM((n, 2 * sp), F32)],
        compiler_params=_cparams(("arbitrary",)),
        name="ssm",
    )(ut3, w1t, lag0, bb, w2t, ap)


def _glu_kernel(yt_ref, w_ref, b_ref, out_ref, nat_ref, *, sub):
    g, L, c, r = yt_ref.shape
    nj = nat_ref.shape[0]
    gj = LANES // c
    for j in range(nj):
        steps = jnp.stack([yt_ref[j * gj:(j + 1) * gj, t, :, :].reshape(LANES, r).astype(F32).T
                           for t in range(L)])
        nat_ref[j] = pltpu.einshape("srl->rsl", steps).reshape(r * L, LANES)

    def body(k, carry):
        sl = pl.ds(pl.multiple_of(k * sub, sub), sub)
        y = jnp.concatenate([nat_ref[j, sl, :] for j in range(nj)], axis=1)
        ya = jax.nn.gelu(y, approximate=True)
        z = jnp.dot(ya.astype(BF16), w_ref[...], preferred_element_type=F32) + b_ref[...]
        out_ref[sl, :] = (ya * (1.0 / (1.0 + jnp.exp(-z)))).astype(BF16)
        return carry
    lax.fori_loop(0, nat_ref.shape[1] // sub, body, 0)


def _glu(yt4, glu_w, glu_b):
    g, L, c, n = yt4.shape
    ds = g * c
    r = LANES
    return pl.pallas_call(
        functools.partial(_glu_kernel, sub=512),
        out_shape=jax.ShapeDtypeStruct((n * L, ds), BF16),
        grid=(n // r,),
        in_specs=[pl.BlockSpec((g, L, c, r), lambda j: (0, 0, 0, j)),
                  pl.BlockSpec((ds, ds), lambda j: (0, 0)),
                  pl.BlockSpec((1, ds), lambda j: (0, 0))],
        out_specs=pl.BlockSpec((r * L, ds), lambda j: (j, 0)),
        scratch_shapes=[pltpu.VMEM((ds // LANES, r * L, LANES), F32)],
        compiler_params=_cparams(("arbitrary",)),
        name="glu",
    )(yt4, glu_w, glu_b)


def _layer_norm(r, g, b):
    mu = jnp.mean(r, axis=-1, keepdims=True)
    cen = r - mu
    var = jnp.mean(cen * cen, axis=-1, keepdims=True)
    return cen * lax.rsqrt(var + LN_EPS) * g + b


def _store_row_packed(ref, m, row0=0):
    rows, width = m.shape
    half = width // 2
    nc = half // LANES
    lo = lax.bitcast_convert_type(m[:, :half].astype(BF16).astype(F32), U32) >> 16
    hi = lax.bitcast_convert_type(m[:, half:].astype(BF16).astype(F32), U32) & jnp.uint32(0xFFFF0000)
    pk = lo | hi
    for c in range(nc):
        ref[pl.ds(row0 * nc + c, rows, stride=nc), :] = pk[:, c * LANES:(c + 1) * LANES]


def _load_row_packed(ref, rows, nc, row0=0):
    los, his = [], []
    for c in range(nc):
        u = ref[pl.ds(row0 * nc + c, rows, stride=nc), :]
        los.append(lax.bitcast_convert_type(u << 16, F32))
        his.append(lax.bitcast_convert_type(u & jnp.uint32(0xFFFF0000), F32))
    return los, his


def _outproj_kernel(yp_ref, ys_ref, x_ref, wo_ref, g_ref, b_ref, rw_ref, rb_ref,
                    hp_ref, lt_ref, *, alpha, dp, sub):
    for sb in range(x_ref.shape[0] // sub):
        sl = pl.ds(sb * sub, sub)
        mix = jnp.dot(yp_ref[sl, :], wo_ref[0:dp, :], preferred_element_type=F32) \
            + jnp.dot(ys_ref[sl, :], wo_ref[dp:, :], preferred_element_type=F32)
        h = _layer_norm(alpha * x_ref[sl, :] + mix, g_ref[...], b_ref[...])
        _store_row_packed(hp_ref, h, row0=sb * sub)
        hh = h.astype(BF16)
        hl = (h - hh.astype(F32)).astype(BF16)
        part = jnp.dot(hh, rw_ref[...], preferred_element_type=F32) \
            + jnp.dot(hl, rw_ref[...], preferred_element_type=F32)
        logits = part + pltpu.roll(part, ROUTER_LANES // 2, 1) + rb_ref[...]
        lt_ref[:, sb * sub:(sb + 1) * sub] = logits.T[0:ROUTER_ROWS, :]


def _outproj(y_pool, y_ssm, xt, w_out, ln_g, ln_b, rw, rb, alpha):
    t, d = xt.shape
    dp = y_pool.shape[1]
    tm = min(512, t)
    pr = d // 2 // LANES
    return pl.pallas_call(
        functools.partial(_outproj_kernel, alpha=alpha, dp=dp, sub=256),
        out_shape=(jax.ShapeDtypeStruct((t * pr, LANES), U32),
                   jax.ShapeDtypeStruct((ROUTER_ROWS, t), F32)),
        grid=(t // tm,),
        in_specs=[pl.BlockSpec((tm, dp), lambda i: (i, 0)),
                  pl.BlockSpec((tm, y_ssm.shape[1]), lambda i: (i, 0)),
                  pl.BlockSpec((tm, d), lambda i: (i, 0)),
                  pl.BlockSpec(w_out.shape, lambda i: (0, 0)),
                  pl.BlockSpec((1, d), lambda i: (0, 0)),
                  pl.BlockSpec((1, d), lambda i: (0, 0)),
                  pl.BlockSpec(rw.shape, lambda i: (0, 0)),
                  pl.BlockSpec((1, ROUTER_LANES), lambda i: (0, 0))],
        out_specs=(pl.BlockSpec((tm * pr, LANES), lambda i: (i, 0)),
                   pl.BlockSpec((ROUTER_ROWS, tm), lambda i: (0, i))),
        compiler_params=_cparams(("arbitrary",)),
        name="outproj",
    )(y_pool, y_ssm, xt, w_out, ln_g, ln_b, rw, rb)


def _router_kernel(lt_ref, eid_ref, rank_ref, gate_ref, cnt_ref, carry_ref):
    i = pl.program_id(0)
    ne, epg = N_EXPERTS, EXPERTS_PER_GROUP

    @pl.when(i == 0)
    def _():
        carry_ref[...] = jnp.zeros_like(carry_ref)

    lt = lt_ref[...]
    tt = lt.shape[1]
    gl = [lt[j:j + 1, :] for j in range(N_EXPERT_GROUPS)]
    gmax = jnp.maximum(jnp.maximum(gl[0], gl[1]), jnp.maximum(gl[2], gl[3]))
    grp = jnp.where(gl[0] == gmax, 0, jnp.where(gl[1] == gmax, 1, jnp.where(gl[2] == gmax, 2, 3)))
    p_grp = 1.0 / (jnp.exp(gl[0] - gmax) + jnp.exp(gl[1] - gmax)
                   + jnp.exp(gl[2] - gmax) + jnp.exp(gl[3] - gmax))
    eg = [lt[8 + epg * j: 8 + epg * (j + 1), :] for j in range(N_EXPERT_GROUPS)]
    el = jnp.where(grp == 0, eg[0], jnp.where(grp == 1, eg[1], jnp.where(grp == 2, eg[2], eg[3])))
    sub = lax.broadcasted_iota(I32, (epg, tt), 0)
    m1 = jnp.max(el, axis=0, keepdims=True)
    i1 = jnp.min(jnp.where(el == m1, sub, epg), axis=0, keepdims=True)
    rest = jnp.where(sub == i1, -jnp.inf, el)
    m2 = jnp.max(rest, axis=0, keepdims=True)
    i2 = jnp.min(jnp.where(rest == m2, sub, epg), axis=0, keepdims=True)
    r21 = jnp.exp(m2 - m1)
    g1 = p_grp / (1.0 + r21)
    g2 = g1 * r21
    e1 = grp * epg + i1
    e2 = grp * epg + i2

    rows = lax.broadcasted_iota(I32, (ne, tt), 0)
    oh1 = rows == e1
    oh2 = rows == e2
    oh = jnp.where(oh1 | oh2, 1.0, 0.0)
    tri = jnp.where(lax.broadcasted_iota(I32, (tt, tt), 0) < lax.broadcasted_iota(I32, (tt, tt), 1),
                    1.0, 0.0).astype(BF16)
    before = jnp.dot(oh.astype(BF16), tri, preferred_element_type=F32) + carry_ref[:, 0:1]
    r1 = jnp.sum(jnp.where(oh1, before, 0.0), axis=0, keepdims=True)
    r2 = jnp.sum(jnp.where(oh2, before, 0.0), axis=0, keepdims=True)
    carry_ref[...] = carry_ref[...] + jnp.sum(oh, axis=1, keepdims=True)

    eid_ref[...] = jnp.concatenate([e1, e2], axis=0)
    rank_ref[...] = jnp.concatenate([r1, r2], axis=0).astype(I32)
    gate_ref[...] = jnp.concatenate([g1, g2], axis=0)
    cnt_ref[...] = carry_ref[...].astype(I32)


def _router(lt):
    rr, t = lt.shape
    tt = min(512, t)
    return pl.pallas_call(
        _router_kernel,
        out_shape=(jax.ShapeDtypeStruct((2, t), I32),
                   jax.ShapeDtypeStruct((2, t), I32),
                   jax.ShapeDtypeStruct((2, t), F32),
                   jax.ShapeDtypeStruct((N_EXPERTS, 128), I32)),
        grid=(t // tt,),
        in_specs=[pl.BlockSpec((rr, tt), lambda i: (0, i))],
        out_specs=(pl.BlockSpec((2, tt), lambda i: (0, i)),
                   pl.BlockSpec((2, tt), lambda i: (0, i)),
                   pl.BlockSpec((2, tt), lambda i: (0, i)),
                   pl.BlockSpec((N_EXPERTS, 128), lambda i: (0, 0))),
        scratch_shapes=[pltpu.VMEM((N_EXPERTS, 128), F32)],
        compiler_params=_cparams(("arbitrary",)),
        name="router",
    )(lt)


def _invert_kernel(dest_ref, zeros_hbm, out_ref):
    pltpu.sync_copy(zeros_hbm, out_ref)
    t = dest_ref.shape[0] // 2

    def place(i, carry):
        out_ref[dest_ref[i]] = i
        out_ref[dest_ref[t + i]] = i
        return carry
    lax.fori_loop(0, t, place, 0, unroll=8)


def _invert(dest_flat, n_slots):
    return pl.pallas_call(
        _invert_kernel,
        out_shape=jax.ShapeDtypeStruct((n_slots,), I32),
        grid_spec=pltpu.PrefetchScalarGridSpec(
            num_scalar_prefetch=1, grid=(1,),
            in_specs=[pl.BlockSpec(memory_space=pl.ANY)],
            out_specs=pl.BlockSpec(memory_space=pltpu.SMEM)),
        compiler_params=_cparams(("arbitrary",)),
        name="invert",
    )(dest_flat, jnp.zeros((n_slots,), I32))


def _moe_kernel(be_ref, nu_ref, pc_ref, st_ref, hp_hbm, wg_hbm, wu_hbm, wd_hbm, y_ref,
                gbuf, wgf, wuf, wdf, wgb, wub, wdb, par_ref, gsem, wsem):
    for half in range(MOE_BLOCKS_PER_STEP):
        _moe_block(pl.program_id(0) * MOE_BLOCKS_PER_STEP + half, half * MOE_BLK,
                   be_ref, nu_ref, pc_ref, st_ref, hp_hbm, wg_hbm, wu_hbm, wd_hbm, y_ref,
                   gbuf, wgf, wuf, wdf, wgb, wub, wdb, par_ref, gsem, wsem)


def _moe_block(b, row0, be_ref, nu_ref, pc_ref, st_ref, hp_hbm, wg_hbm, wu_hbm, wd_hbm, y_ref,
               gbuf, wgf, wuf, wdf, wgb, wub, wdb, par_ref, gsem, wsem):
    nu = nu_ref[0]
    pr = gbuf.shape[1] // MOE_BLK
    blk = MOE_BLK
    nbuf = gbuf.shape[0]
    look = nbuf - 1

    def row_copy(tok, slot, r):
        src = hp_hbm.at[pl.ds(pl.multiple_of(tok * pr, pr), pr)]
        off = r * pr if isinstance(r, int) else pl.multiple_of(r * pr, pr)
        return pltpu.make_async_copy(src, gbuf.at[slot, pl.ds(off, pr)], gsem.at[slot])

    def issue(block, slot):
        base = block * blk

        def body(r, carry):
            row_copy(st_ref[base + r], slot, r).start()
            return carry
        lax.fori_loop(0, blk, body, 0, unroll=8)

    def weight_copies(e, p):
        return (pltpu.make_async_copy(wg_hbm.at[e], wgf.at[p], wsem.at[p]),
                pltpu.make_async_copy(wu_hbm.at[e], wuf.at[p], wsem.at[p]),
                pltpu.make_async_copy(wd_hbm.at[e], wdf.at[p], wsem.at[p]))

    @pl.when(b == 0)
    def _():
        par_ref[0] = 0
        for cp in weight_copies(be_ref[0], 0):
            cp.start(priority=1)
        issue(0, 0)
        for j in range(1, look):
            @pl.when(j < nu)
            def _():
                issue(j, j)

    def issue_part(k):
        base = (b + look) * blk
        for r in range(k * blk // 4, (k + 1) * blk // 4):
            row_copy(st_ref[base + r], (b + look) % nbuf, r).start(priority=1 if r % 3 == 2 else 0)

    def expert_mlp(slot, with_issue):
        los, his = _load_row_packed(gbuf.at[slot], blk, pr)
        x = jnp.concatenate([v.astype(BF16) for v in los + his], axis=1)
        if with_issue:
            issue_part(0)
        hg = jnp.dot(x, wgb[...], preferred_element_type=F32)
        if with_issue:
            issue_part(1)
        hu = jnp.dot(x, wub[...], preferred_element_type=F32)
        if with_issue:
            issue_part(2)
        hh = hg * (1.0 / (1.0 + jnp.exp(-hg))) * hu
        y = jnp.dot(hh.astype(BF16), wdb[...], preferred_element_type=F32)
        if with_issue:
            issue_part(3)
        _store_row_packed(y_ref, y, row0=row0)

    @pl.when(b < nu)
    def _():
        slot = b % nbuf
        e = be_ref[b]
        first = (b == 0) | (e != be_ref[jnp.maximum(b - 1, 0)])

        @pl.when(first)
        def _():
            p = par_ref[0]
            for cp in weight_copies(e, p):
                cp.wait()
            nxt = lax.while_loop(
                lambda c: (c < N_EXPERTS) & (pc_ref[jnp.minimum(c, N_EXPERTS - 1)] == 0),
                lambda c: c + 1, e + 1)

            @pl.when(nxt < N_EXPERTS)
            def _():
                for cp in weight_copies(nxt, 1 - p):
                    cp.start(priority=1)
            wgb[...] = wgf[p].astype(BF16)
            wub[...] = wuf[p].astype(BF16)
            wdb[...] = wdf[p].astype(BF16)
            par_ref[0] = 1 - p

        pltpu.make_async_copy(hp_hbm.at[pl.ds(0, blk * pr)], gbuf.at[slot], gsem.at[slot]).wait()

        @pl.when(b + look < nu)
        def _():
            expert_mlp(slot, True)

        @pl.when(b + look >= nu)
        def _():
            expert_mlp(slot, False)

    @pl.when(b >= nu)
    def _():
        y_ref[pl.ds(row0 * pr, blk * pr), :] = jnp.zeros((blk * pr, LANES), U32)


def _moe(block_e, n_used, pcounts, slot_tok, hp, w_gate, w_up, w_down, n_blocks):
    ne, d, de = w_gate.shape
    blk = MOE_BLK
    pr = d // 2 // LANES
    bps = MOE_BLOCKS_PER_STEP
    assert n_blocks % bps == 0
    grid_spec = pltpu.PrefetchScalarGridSpec(
        num_scalar_prefetch=4,
        grid=(n_blocks // bps,),
        in_specs=[pl.BlockSpec(memory_space=pl.ANY)] * 4,
        out_specs=pl.BlockSpec((bps * blk * pr, LANES), lambda b, *_: (b, 0)),
        scratch_shapes=[pltpu.VMEM((GATHER_BUFS, blk * pr, LANES), U32),
                        pltpu.VMEM((2, d, de), F32),
                        pltpu.VMEM((2, d, de), F32),
                        pltpu.VMEM((2, de, d), F32),
                        pltpu.VMEM((d, de), BF16),
                        pltpu.VMEM((d, de), BF16),
                        pltpu.VMEM((de, d), BF16),
                        pltpu.SMEM((1,), I32),
                        pltpu.SemaphoreType.DMA((GATHER_BUFS,)),
                        pltpu.SemaphoreType.DMA((2,))],
    )
    return pl.pallas_call(
        _moe_kernel,
        out_shape=jax.ShapeDtypeStruct((n_blocks * blk * pr, LANES), U32),
        grid_spec=grid_spec,
        compiler_params=_cparams(("arbitrary",)),
        name="moe",
    )(block_e, n_used, pcounts, slot_tok, hp, w_gate, w_up, w_down)


def _combine_kernel(d0_ref, d1_ref, hp_ref, gate_ref, yb_hbm, g_ref, b_ref, out_ref, buf, sem, *, alpha):
    i = pl.program_id(0)
    n = pl.num_programs(0)
    tm = out_ref.shape[0]
    pr = buf.shape[2] // tm

    def issue(tile, slot):
        base = tile * tm

        def body(r, carry):
            dst = pl.ds(pl.multiple_of(r * pr, pr), pr)
            for k, dref in enumerate((d0_ref, d1_ref)):
                src = yb_hbm.at[pl.ds(pl.multiple_of(dref[base + r] * pr, pr), pr)]
                pltpu.make_async_copy(src, buf.at[slot, k, dst], sem.at[slot]).start()
            return carry
        lax.fori_loop(0, tm, body, 0, unroll=8)

    nbuf = buf.shape[0]
    look = nbuf - 1

    @pl.when(i == 0)
    def _():
        issue(0, 0)
        for j in range(1, look):
            @pl.when(j < n)
            def _():
                issue(j, j)

    slot = i % nbuf
    for k in range(2):
        pltpu.make_async_copy(yb_hbm.at[pl.ds(0, tm * pr)], buf.at[slot, k], sem.at[slot]).wait()

    nq = 4
    sub = tm // nq

    def issue_part(q):
        base = (i + look) * tm
        nslot = (i + look) % nbuf
        for r in range(q * sub, (q + 1) * sub):
            for k, dref in enumerate((d0_ref, d1_ref)):
                src = yb_hbm.at[pl.ds(pl.multiple_of(dref[base + r] * pr, pr), pr)]
                pltpu.make_async_copy(src, buf.at[nslot, k, pl.ds(r * pr, pr)], sem.at[nslot]).start(priority=k)

    def finish(with_issue):
        for q in range(nq):
            if with_issue:
                issue_part(q)
            rows = pl.ds(q * sub, sub)
            lo0, hi0 = _load_row_packed(buf.at[slot, 0], sub, pr, row0=q * sub)
            lo1, hi1 = _load_row_packed(buf.at[slot, 1], sub, pr, row0=q * sub)
            loh, hih = _load_row_packed(hp_ref, sub, pr, row0=q * sub)
            g0 = gate_ref[rows, 0:1]
            g1 = gate_ref[rows, 1:2]
            r = jnp.concatenate([alpha * hh + g0 * a + g1 * c
                                 for hh, a, c in zip(loh + hih, lo0 + hi0, lo1 + hi1)], axis=1)
            out_ref[rows, :] = _layer_norm(r, g_ref[...], b_ref[...])

    @pl.when(i + look < n)
    def _():
        finish(True)

    @pl.when(i + look >= n)
    def _():
        finish(False)


def _combine(dest0, dest1, hp, gate_t, yb, ln_g, ln_b, alpha):
    d = ln_g.shape[1]
    pr = d // 2 // LANES
    t = hp.shape[0] // pr
    tm = min(256, t)
    grid_spec = pltpu.PrefetchScalarGridSpec(
        num_scalar_prefetch=2,
        grid=(t // tm,),
        in_specs=[pl.BlockSpec((tm * pr, LANES), lambda i, d0, d1: (i, 0)),
                  pl.BlockSpec((tm, 2), lambda i, d0, d1: (i, 0)),
                  pl.BlockSpec(memory_space=pl.ANY),
                  pl.BlockSpec((1, d), lambda i, d0, d1: (0, 0)),
                  pl.BlockSpec((1, d), lambda i, d0, d1: (0, 0))],
        out_specs=pl.BlockSpec((tm, d), lambda i, d0, d1: (i, 0)),
        scratch_shapes=[pltpu.VMEM((GATHER_BUFS, 2, tm * (d // 2 // LANES), LANES), U32),
                        pltpu.SemaphoreType.DMA((GATHER_BUFS,))],
    )
    return pl.pallas_call(
        functools.partial(_combine_kernel, alpha=alpha),
        out_shape=jax.ShapeDtypeStruct((t, d), F32),
        grid_spec=grid_spec,
        compiler_params=_cparams(("arbitrary",)),
        name="combine",
    )(dest0, dest1, hp, gate_t, yb, ln_g, ln_b)


def _layer(h, w_in, pool_w, pool_scale, a_re, a_im, log_dt, b_re, b_im, c_re, c_im, d_skip,
           glu_w, glu_b, w_out, ln1_g, ln1_b, rg_w, rg_b, re_w, re_b, w_gate, w_up, w_down,
           ln2_g, ln2_b, alpha):
    bsz, seq, d = h.shape
    t = bsz * seq
    L = CHUNK
    dp = pool_w.shape[0] * pool_w.shape[1]
    ds = w_in.shape[1] - dp
    g = ds // SSM_GROUP
    n = t // L

    lag0, bb, w1t, w2t, al = _ssm_tables(a_re, a_im, log_dt, b_re, b_im, c_re, c_im, d_skip, L)
    half = ROUTER_LANES // 2
    zpad = lambda k: jnp.zeros((d, k), F32)
    rw = jnp.concatenate([rg_w.astype(F32), zpad(8 - N_EXPERT_GROUPS), re_w.astype(F32),
                          zpad(half - ROUTER_ROWS)], axis=1)
    rw_hi = rw.astype(BF16)
    rw_lo = (rw - rw_hi.astype(F32)).astype(BF16)
    rw2 = jnp.concatenate([rw_hi, rw_lo], axis=1)
    rb = jnp.concatenate([rg_b.astype(F32), jnp.zeros((8 - N_EXPERT_GROUPS,), F32), re_b.astype(F32),
                          jnp.zeros((ROUTER_LANES - ROUTER_ROWS,), F32)]).reshape(1, ROUTER_LANES)

    xt = h.reshape(t, d)
    pool_p, ut = _proj(xt, w_in.astype(BF16), dp, L)
    y_pool = _pool(pool_p, pool_w.astype(BF16), pool_scale.reshape(1, dp).astype(F32), seq)
    yt = _ssm(ut.reshape(g, L * SSM_GROUP, n), lag0, bb, w1t, w2t, al, seq // L, L)
    y_ssm = _glu(yt.reshape(g, L, SSM_GROUP, n), glu_w.astype(BF16), glu_b.reshape(1, ds).astype(F32))
    hp, lt = _outproj(y_pool, y_ssm, xt, w_out.astype(BF16), ln1_g.reshape(1, d), ln1_b.reshape(1, d),
                          rw2, rb, alpha)
    eid, rank, gate, cnt = _router(lt)

    blk = MOE_BLK
    m = 2 * t
    n_blocks = -(-m // blk) + N_EXPERTS
    counts = cnt[:, 0]
    pcounts = (counts + blk - 1) // blk * blk
    pends = jnp.cumsum(pcounts)
    pstarts = pends - pcounts
    e_ids = jnp.arange(N_EXPERTS, dtype=I32)
    dest = jnp.sum(jnp.where(eid[..., None] == e_ids, pstarts, 0), axis=-1) + rank
    slot_tok = _invert(dest.reshape(-1), n_blocks * blk)
    n_used = (pends[-1] // blk).astype(I32)
    bidx = jnp.minimum(jnp.arange(n_blocks, dtype=I32), n_used - 1)
    block_e = jnp.minimum(jnp.sum((pends[None, :] <= (bidx * blk)[:, None]).astype(I32), axis=1),
                          N_EXPERTS - 1)

    yb = _moe(block_e, n_used.reshape(1), pcounts, slot_tok, hp, w_gate, w_up, w_down, n_blocks)
    out = _combine(dest[0], dest[1], hp, gate.T, yb, ln2_g.reshape(1, d), ln2_b.reshape(1, d), alpha)
    return out.reshape(bsz, seq, d)


def kernel(x, w_in, pool_w, pool_scale, ssm_a_re, ssm_a_im, ssm_log_dt, ssm_b_re, ssm_b_im, ssm_c_re, ssm_c_im, ssm_d, glu_w, glu_b, w_out, ln1_g, ln1_b, router_g_w, router_g_b, router_e_w, router_e_b, w_gate, w_up, w_down, ln2_g, ln2_b):
    depth = w_in.shape[0]
    alpha = (2.0 * depth) ** 0.25
    h = x
    for l in range(depth):
        h = _layer(h, w_in[l], pool_w[l], pool_scale[l], ssm_a_re[l], ssm_a_im[l], ssm_log_dt[l],
                   ssm_b_re[l], ssm_b_im[l], ssm_c_re[l], ssm_c_im[l], ssm_d[l], glu_w[l], glu_b[l],
                   w_out[l], ln1_g[l], ln1_b[l], router_g_w[l], router_g_b[l], router_e_w[l],
                   router_e_b[l], w_gate[l], w_up[l], w_down[l], ln2_g[l], ln2_b[l], alpha)
    return h
```

```python
import functools
import math

import numpy as np
import jax
import jax.numpy as jnp
from jax import lax
from jax.experimental import pallas as pl
from jax.experimental.pallas import tpu as pltpu

F32 = jnp.float32
BF16 = jnp.bfloat16
I32 = jnp.int32
U32 = jnp.uint32

POOL_WINDOWS = (2, 4, 8, 16)
POOL_GROUP = 256
SSM_GROUP = 16
SSM_STATE = 64
N_EXPERT_GROUPS = 4
EXPERTS_PER_GROUP = 8
N_EXPERTS = N_EXPERT_GROUPS * EXPERTS_PER_GROUP
LN_EPS = 1e-5

CHUNK = 16
MOE_BLK = 256
MOE_BLOCKS_PER_STEP = 2
GATHER_BUFS = 4
ROUTER_ROWS = 8 + N_EXPERTS
ROUTER_LANES = 128
LANES = 128
MXU_N = 256
HALO = 16
VMEM_LIMIT = 56 * 1024 * 1024


def _cparams(sem, vmem=VMEM_LIMIT):
    return pltpu.CompilerParams(dimension_semantics=sem, vmem_limit_bytes=vmem)


def _proj_kernel(x_ref, w_ref, pool_ref, ut_ref, accp_ref, accs_ref):
    kk = pl.program_id(1)
    rows, dp = accp_ref.shape
    nj = accs_ref.shape[0]

    xb = x_ref[...].astype(BF16)
    per = MXU_N // LANES

    def accumulate(first):
        for n in range(dp // MXU_N):
            cs = slice(n * MXU_N, (n + 1) * MXU_N)
            part = jnp.dot(xb, w_ref[:, cs], preferred_element_type=F32)
            accp_ref[:, cs] = part if first else accp_ref[:, cs] + part
        for n in range(nj // per):
            part = jnp.dot(xb, w_ref[:, dp + n * MXU_N: dp + (n + 1) * MXU_N], preferred_element_type=F32)
            for q in range(per):
                piece = part[:, q * LANES:(q + 1) * LANES]
                accs_ref[n * per + q] = piece if first else accs_ref[n * per + q] + piece

    @pl.when(kk == 0)
    def _():
        accumulate(True)

    @pl.when(kk > 0)
    def _():
        accumulate(False)

    @pl.when(kk == pl.num_programs(1) - 1)
    def _():
        pool_ref[...] = accp_ref[...].astype(BF16)
        g, L, c, r = ut_ref.shape
        gj = LANES // c
        for j in range(nj):
            steps = pltpu.einshape("rsl->srl", accs_ref[j].reshape(r, L, LANES))
            for s in range(L):
                ut_ref[j * gj:(j + 1) * gj, s, :, :] = steps[s].astype(BF16).T.reshape(gj, c, r)


def _proj(xt, w_in, dp, L):
    t, d = xt.shape
    dm = w_in.shape[1]
    g = (dm - dp) // SSM_GROUP
    tm = LANES * L
    kb = 512
    return pl.pallas_call(
        _proj_kernel,
        out_shape=(jax.ShapeDtypeStruct((t, dp), BF16),
                   jax.ShapeDtypeStruct((g, L, SSM_GROUP, t // L), BF16)),
        grid=(t // tm, d // kb),
        in_specs=[pl.BlockSpec((tm, kb), lambda i, k: (i, k)),
                  pl.BlockSpec((kb, dm), lambda i, k: (k, 0))],
        out_specs=(pl.BlockSpec((tm, dp), lambda i, k: (i, 0)),
                   pl.BlockSpec((g, L, SSM_GROUP, LANES), lambda i, k: (0, 0, 0, i))),
        scratch_shapes=[pltpu.VMEM((tm, dp), F32),
                        pltpu.VMEM(((dm - dp) // LANES, tm, LANES), F32)],
        compiler_params=_cparams(("arbitrary", "arbitrary")),
        name="proj",
    )(xt, w_in)


def _pool_kernel(prev_ref, main_ref, next_ref, pw_ref, sc_ref, out_ref, *, seq, sub):
    i = pl.program_id(0)
    ts = main_ref.shape[0]
    ext = jnp.concatenate([prev_ref[...], main_ref[...], next_ref[...]], axis=0)
    k = sub + 2 * HALO
    row = lax.broadcasted_iota(I32, (sub, k), 0)
    col = lax.broadcasted_iota(I32, (sub, k), 1)
    off0 = col - HALO - row
    for sb in range(ts // sub):
        rows = slice(sb * sub, (sb + 1) * sub)
        base = (i * ts + sb * sub) % seq
        src = base + row + off0
        off = jnp.where((src >= 0) & (src < seq), off0, 2 * HALO)
        pos = base + lax.broadcasted_iota(I32, (sub, 1), 0)
        for g, w in enumerate(POOL_WINDOWS):
            sl = slice(g * POOL_GROUP, (g + 1) * POOL_GROUP)
            band = (off + w // 2).astype(U32) < w
            bm = jnp.where(band, 1.0, 0.0).astype(BF16)
            sums = jnp.dot(bm, ext[sb * sub:sb * sub + k, sl], preferred_element_type=F32)
            lo = jnp.maximum(pos - w // 2, 0)
            hi = jnp.minimum(pos - w // 2 + w, seq)
            inv = 1.0 / (hi - lo).astype(F32)
            dlt = sums * inv - main_ref[rows, sl].astype(F32)
            y = jnp.dot(dlt.astype(BF16), pw_ref[g], preferred_element_type=F32) * sc_ref[:, sl]
            out_ref[rows, sl] = y.astype(BF16)


def _pool(pp, pool_w, pool_scale, seq):
    t, dp = pp.shape
    ts = min(1024, seq)
    nh = t // HALO
    per = ts // HALO
    return pl.pallas_call(
        functools.partial(_pool_kernel, seq=seq, sub=256),
        out_shape=jax.ShapeDtypeStruct((t, dp), BF16),
        grid=(t // ts,),
        in_specs=[pl.BlockSpec((HALO, dp), lambda i: (jnp.maximum(i * per - 1, 0), 0)),
                  pl.BlockSpec((ts, dp), lambda i: (i, 0)),
                  pl.BlockSpec((HALO, dp), lambda i: (jnp.minimum((i + 1) * per, nh - 1), 0)),
                  pl.BlockSpec(pool_w.shape, lambda i: (0, 0, 0)),
                  pl.BlockSpec((1, dp), lambda i: (0, 0))],
        out_specs=pl.BlockSpec((ts, dp), lambda i: (i, 0)),
        compiler_params=_cparams(("arbitrary",)),
        name="pool",
    )(pp, pp, pp, pool_w, pool_scale)


def _ssm_tables(a_re, a_im, log_dt, b_re, b_im, c_re, c_im, d_skip, L):
    g = a_re.shape[1]
    p = a_re.shape[2]
    c = b_re.shape[3]
    fl = L * c
    lr = a_re.astype(F32)
    li = a_im.astype(F32)
    dt = jnp.exp(log_dt.astype(F32))[..., None]
    mag = jnp.exp(lr * dt)
    abr = mag * jnp.cos(li * dt)
    abi = mag * jnp.sin(li * dt)
    den = lr * lr + li * li
    zr = ((abr - 1.0) * lr + abi * li) / den
    zi = (abi * lr - (abr - 1.0) * li) / den
    br = b_re.astype(F32)
    bi = b_im.astype(F32)
    bbr = zr[..., None] * br - zi[..., None] * bi
    bbi = zr[..., None] * bi + zi[..., None] * br
    cr = c_re.astype(F32)
    ci = c_im.astype(F32)
    kk = jnp.arange(L + 1, dtype=F32)[None, None, :, None]
    pm = jnp.exp(kk * (lr * dt)[:, :, None, :])
    ang = kk * (li * dt)[:, :, None, :]
    pr = pm * jnp.cos(ang)
    pi = pm * jnp.sin(ang)

    skip = jnp.eye(c, dtype=F32)[None] * d_skip.astype(F32).reshape(g, c, 1)
    lag0 = jnp.einsum('dgop,dgpc->goc', cr, bbr, precision=lax.Precision.HIGHEST) \
        - jnp.einsum('dgop,dgpc->goc', ci, bbi, precision=lax.Precision.HIGHEST) + skip
    lag0 = jnp.pad(lag0, ((0, 0), (0, 0), (0, LANES - c)))
    bb = jnp.concatenate([bbr[0], bbi[0], bbr[1], bbi[1]], axis=1)
    bb = jnp.pad(bb, ((0, 0), (0, 0), (0, LANES - c)))

    def w1(d, qr, qi):
        qr = qr.transpose(0, 2, 1)[..., None]
        qi = qi.transpose(0, 2, 1)[..., None]
        re = qr * bbr[d][:, :, None] - qi * bbi[d][:, :, None]
        im = qr * bbi[d][:, :, None] + qi * bbr[d][:, :, None]
        to = lambda v: v.reshape(g, p, fl)
        return to(re), to(im)
    f_re, f_im = w1(0, pr[0][:, L - 1::-1], pi[0][:, L - 1::-1])
    b_re_, b_im_ = w1(1, pr[1][:, :L], pi[1][:, :L])
    w1t = jnp.concatenate([f_re, f_im, b_re_, b_im_], axis=1)

    def w2(d, qr, qi):
        re = cr[d][:, None] * qr[:, :, None] - ci[d][:, None] * qi[:, :, None]
        im = -(cr[d][:, None] * qi[:, :, None] + ci[d][:, None] * qr[:, :, None])
        to = lambda v: v.reshape(g, fl, p)
        return to(re), to(im)
    rf_re, rf_im = w2(0, pr[0][:, 1:], pi[0][:, 1:])
    rb_re, rb_im = w2(1, pr[1][:, :0:-1], pi[1][:, :0:-1])
    w2t = jnp.concatenate([rf_re, rf_im, rb_re, rb_im], axis=2)

    a4 = jnp.stack([pr[0][:, L], pi[0][:, L], pr[1][:, L], pi[1][:, L]], axis=1)
    ap = a4.reshape(g // 2, 2, 4, p).transpose(0, 2, 1, 3).reshape(g // 2, 4, 2 * p)
    ap = jnp.pad(ap, ((0, 0), (0, 4), (0, 0)))
    return lag0, bb.astype(BF16), w1t.astype(BF16), w2t.astype(BF16), ap


def _cmul(ar, ai, br, bi):
    return ar * br - ai * bi, ar * bi + ai * br


def _ssm_kernel(ut_ref, w1_ref, lag0_ref, bb_ref, w2_ref, a_ref, yt_ref, mt_s, kk_s, zt_s, ht_s, *, nseg, L):
    p = SSM_STATE
    c = SSM_GROUP
    fl, n = ut_ref.shape[1:]
    nb = n // nseg
    nrow = nseg // 8
    st = [jnp.dot(w1_ref[q], ut_ref[q], preferred_element_type=F32) for q in range(2)]
    z = jnp.concatenate([st[q][k * p:(k + 1) * p] for k in range(4) for q in range(2)], axis=0)
    zt_s[...] = z.astype(BF16).T.astype(F32)
    per = LANES // c
    for q in range(2):
        kf = jnp.dot(w2_ref[q, :, 0:2 * p], bb_ref[q, 0:2 * p, :], preferred_element_type=F32)
        kb = jnp.dot(w2_ref[q, :, 2 * p:4 * p], bb_ref[q, 2 * p:4 * p, :], preferred_element_type=F32)
        tiles = [None] * (2 * L // per)
        for j in range(2 * L - 1):
            lag = L - 1 - j
            if lag > 0:
                blk = kf[(lag - 1) * c:lag * c, :]
            elif lag == 0:
                blk = lag0_ref[q]
            else:
                blk = kb[(L + lag) * c:(L + lag + 1) * c, :]
            sh = (j % per) * c
            blk = blk if sh == 0 else pltpu.roll(blk, sh, 1)
            tiles[j // per] = blk if tiles[j // per] is None else tiles[j // per] + blk
        kk_s[...] = jnp.concatenate(tiles, axis=1).astype(BF16)
        for t in range(L):
            mt_s[q, t * c:(t + 1) * c, :] = kk_s[:, (L - 1 - t) * c:(L - 1 - t) * c + fl]

    sub = lax.broadcasted_iota(I32, (8, LANES), 0)
    full = lambda v: jnp.broadcast_to(v, (8, LANES))
    one, zero = jnp.ones((8, LANES), F32), jnp.zeros((8, LANES), F32)

    def powers(row):
        out = [(full(a_ref[0, row:row + 1, :]), full(a_ref[0, row + 1:row + 2, :]))]
        for _ in range(3):
            out.append(_cmul(*out[-1], *out[-1]))
        return out

    def by_bits(pw, idx):
        qr, qi = one, zero
        for bit in range(3):
            on = (idx >> bit) & 1 == 1
            qr, qi = _cmul(qr, qi, jnp.where(on, pw[bit][0], one), jnp.where(on, pw[bit][1], zero))
        return qr, qi

    pw_f, pw_b = powers(0), powers(2)
    qf = by_bits(pw_f, sub)
    qb = by_bits(pw_b, 7 - sub)

    def local(xr, xi, pw, forward):
        for lvl, d in enumerate((1, 2, 4)):
            keep = (sub >= d) if forward else (sub < 8 - d)
            sh = d if forward else 8 - d
            tr = jnp.where(keep, pltpu.roll(xr, sh, 0), 0.0)
            ti = jnp.where(keep, pltpu.roll(xi, sh, 0), 0.0)
            mr, mi = _cmul(pw[lvl][0], pw[lvl][1], tr, ti)
            xr, xi = xr + mr, xi + mi
        return xr, xi

    def step(i, carry):
        new = []
        for b in range(nb):
            for fwd in (True, False):
                cr, ci = carry[2 * (2 * b + (0 if fwd else 1)):][:2]
                row = b * nseg + (i if fwd else nrow - 1 - i) * 8
                rows = pl.ds(pl.multiple_of(row, 8), 8)
                l0 = 0 if fwd else 2 * LANES
                pw, (qr, qi) = (pw_f, qf) if fwd else (pw_b, qb)
                lr, li = local(zt_s[rows, l0:l0 + LANES], zt_s[rows, l0 + LANES:l0 + 2 * LANES], pw, fwd)
                keep = (sub >= 1) if fwd else (sub < 7)
                er = jnp.where(keep, pltpu.roll(lr, 1 if fwd else 7, 0), 0.0)
                ei = jnp.where(keep, pltpu.roll(li, 1 if fwd else 7, 0), 0.0)
                mr, mi = _cmul(qr, qi, cr, ci)
                ht_s[rows, l0:l0 + LANES] = er + mr
                ht_s[rows, l0 + LANES:l0 + 2 * LANES] = ei + mi
                edge = 7 if fwd else 0
                mr, mi = _cmul(pw[3][0], pw[3][1], cr, ci)
                new += [full(lr[edge:edge + 1, :]) + mr, full(li[edge:edge + 1, :]) + mi]
        return tuple(new)

    lax.fori_loop(0, nrow, step, (zero,) * (4 * nb))

    h = ht_s[...].astype(BF16).T
    for q in range(2):
        hq = jnp.concatenate([h[(2 * k + q) * p:(2 * k + q + 1) * p] for k in range(4)], axis=0)
        y = jnp.dot(mt_s[q], ut_ref[q], preferred_element_type=F32) \
            + jnp.dot(w2_ref[q], hq, preferred_element_type=F32)
        yt_ref[q] = y.astype(BF16)


def _ssm(ut3, lag0, bb, w1t, w2t, ap, nseg, L):
    g, fl, n = ut3.shape
    sp = w1t.shape[1]
    return pl.pallas_call(
        functools.partial(_ssm_kernel, nseg=nseg, L=L),
        out_shape=jax.ShapeDtypeStruct((g, fl, n), BF16),
        grid=(g // 2,),
        in_specs=[pl.BlockSpec((2, fl, n), lambda i: (i, 0, 0)),
                  pl.BlockSpec((2, sp, fl), lambda i: (i, 0, 0)),
                  pl.BlockSpec((2,) + lag0.shape[1:], lambda i: (i, 0, 0)),
                  pl.BlockSpec((2,) + bb.shape[1:], lambda i: (i, 0, 0)),
                  pl.BlockSpec((2, fl, sp), lambda i: (i, 0, 0)),
                  pl.BlockSpec((1,) + ap.shape[1:], lambda i: (i, 0, 0))],
        out_specs=pl.BlockSpec((2, fl, n), lambda i: (i, 0, 0)),
        scratch_shapes=[pltpu.VMEM((2, fl, fl), BF16),
                        pltpu.VMEM((SSM_GROUP, 2 * fl), BF16),
                        pltpu.VMEM((n, 2 * sp), F32),
                        pltpu.VMEM((n, 2 * sp), F32)],
        compiler_params=_cparams(("arbitrary",)),
        name="ssm",
    )(ut3, w1t, lag0, bb, w2t, ap)


def _glu_kernel(yt_ref, w_ref, b_ref, out_ref, nat_ref, *, sub):
    g, L, c, r = yt_ref.shape
    nj = nat_ref.shape[0]
    gj = LANES // c
    for j in range(nj):
        steps = jnp.stack([yt_ref[j * gj:(j + 1) * gj, t, :, :].reshape(LANES, r).astype(F32).T
                           for t in range(L)])
        nat_ref[j] = pltpu.einshape("srl->rsl", steps).reshape(r * L, LANES)

    def body(k, carry):
        sl = pl.ds(pl.multiple_of(k * sub, sub), sub)
        y = jnp.concatenate([nat_ref[j, sl, :] for j in range(nj)], axis=1)
        ya = jax.nn.gelu(y, approximate=True)
        z = jnp.dot(ya.astype(BF16), w_ref[...], preferred_element_type=F32) + b_ref[...]
        out_ref[sl, :] = (ya * (1.0 / (1.0 + jnp.exp(-z)))).astype(BF16)
        return carry
    lax.fori_loop(0, nat_ref.shape[1] // sub, body, 0)


def _glu(yt4, glu_w, glu_b):
    g, L, c, n = yt4.shape
    ds = g * c
    r = LANES
    return pl.pallas_call(
        functools.partial(_glu_kernel, sub=512),
        out_shape=jax.ShapeDtypeStruct((n * L, ds), BF16),
        grid=(n // r,),
        in_specs=[pl.BlockSpec((g, L, c, r), lambda j: (0, 0, 0, j)),
                  pl.BlockSpec((ds, ds), lambda j: (0, 0)),
                  pl.BlockSpec((1, ds), lambda j: (0, 0))],
        out_specs=pl.BlockSpec((r * L, ds), lambda j: (j, 0)),
        scratch_shapes=[pltpu.VMEM((ds // LANES, r * L, LANES), F32)],
        compiler_params=_cparams(("arbitrary",)),
        name="glu",
    )(yt4, glu_w, glu_b)


def _layer_norm(r, g, b):
    mu = jnp.mean(r, axis=-1, keepdims=True)
    cen = r - mu
    var = jnp.mean(cen * cen, axis=-1, keepdims=True)
    return cen * lax.rsqrt(var + LN_EPS) * g + b


def _store_row_packed(ref, m, row0=0):
    rows, width = m.shape
    half = width // 2
    nc = half // LANES
    lo = lax.bitcast_convert_type(m[:, :half].astype(BF16).astype(F32), U32) >> 16
    hi = lax.bitcast_convert_type(m[:, half:].astype(BF16).astype(F32), U32) & jnp.uint32(0xFFFF0000)
    pk = lo | hi
    for c in range(nc):
        ref[pl.ds(row0 * nc + c, rows, stride=nc), :] = pk[:, c * LANES:(c + 1) * LANES]


def _load_row_packed(ref, rows, nc, row0=0, regroup=False):
    if regroup:
        tiles = pltpu.einshape("rcl->crl", ref[pl.ds(row0 * nc, rows * nc), :].reshape(rows, nc, LANES))
    else:
        tiles = [ref[pl.ds(row0 * nc + c, rows, stride=nc), :] for c in range(nc)]
    los = [lax.bitcast_convert_type(tiles[c] << 16, F32) for c in range(nc)]
    his = [lax.bitcast_convert_type(tiles[c] & jnp.uint32(0xFFFF0000), F32) for c in range(nc)]
    return los, his


def _outproj_kernel(yp_ref, ys_ref, x_ref, wo_ref, g_ref, b_ref, rw_ref, rb_ref,
                    hp_ref, lt_ref, *, alpha, dp, sub):
    for sb in range(x_ref.shape[0] // sub):
        sl = pl.ds(sb * sub, sub)
        mix = jnp.dot(yp_ref[sl, :], wo_ref[0:dp, :], preferred_element_type=F32) \
            + jnp.dot(ys_ref[sl, :], wo_ref[dp:, :], preferred_element_type=F32)
        h = _layer_norm(alpha * x_ref[sl, :] + mix, g_ref[...], b_ref[...])
        _store_row_packed(hp_ref, h, row0=sb * sub)
        hh = h.astype(BF16)
        hl = (h - hh.astype(F32)).astype(BF16)
        part = jnp.dot(hh, rw_ref[...], preferred_element_type=F32) \
            + jnp.dot(hl, rw_ref[...], preferred_element_type=F32)
        logits = part + pltpu.roll(part, ROUTER_LANES // 2, 1) + rb_ref[...]
        lt_ref[:, sb * sub:(sb + 1) * sub] = logits.T[0:ROUTER_ROWS, :]


def _outproj(y_pool, y_ssm, xt, w_out, ln_g, ln_b, rw, rb, alpha):
    t, d = xt.shape
    dp = y_pool.shape[1]
    tm = min(512, t)
    pr = d // 2 // LANES
    return pl.pallas_call(
        functools.partial(_outproj_kernel, alpha=alpha, dp=dp, sub=256),
        out_shape=(jax.ShapeDtypeStruct((t * pr, LANES), U32),
                   jax.ShapeDtypeStruct((ROUTER_ROWS, t), F32)),
        grid=(t // tm,),
        in_specs=[pl.BlockSpec((tm, dp), lambda i: (i, 0)),
                  pl.BlockSpec((tm, y_ssm.shape[1]), lambda i: (i, 0)),
                  pl.BlockSpec((tm, d), lambda i: (i, 0)),
                  pl.BlockSpec(w_out.shape, lambda i: (0, 0)),
                  pl.BlockSpec((1, d), lambda i: (0, 0)),
                  pl.BlockSpec((1, d), lambda i: (0, 0)),
                  pl.BlockSpec(rw.shape, lambda i: (0, 0)),
                  pl.BlockSpec((1, ROUTER_LANES), lambda i: (0, 0))],
        out_specs=(pl.BlockSpec((tm * pr, LANES), lambda i: (i, 0)),
                   pl.BlockSpec((ROUTER_ROWS, tm), lambda i: (0, i))),
        compiler_params=_cparams(("arbitrary",)),
        name="outproj",
    )(y_pool, y_ssm, xt, w_out, ln_g, ln_b, rw, rb)


def _router_kernel(lt_ref, eid_ref, rank_ref, gate_ref, cnt_ref, carry_ref):
    i = pl.program_id(0)
    ne, epg = N_EXPERTS, EXPERTS_PER_GROUP

    @pl.when(i == 0)
    def _():
        carry_ref[...] = jnp.zeros_like(carry_ref)

    lt = lt_ref[...]
    tt = lt.shape[1]
    gl = [lt[j:j + 1, :] for j in range(N_EXPERT_GROUPS)]
    gmax = jnp.maximum(jnp.maximum(gl[0], gl[1]), jnp.maximum(gl[2], gl[3]))
    grp = jnp.where(gl[0] == gmax, 0, jnp.where(gl[1] == gmax, 1, jnp.where(gl[2] == gmax, 2, 3)))
    p_grp = 1.0 / (jnp.exp(gl[0] - gmax) + jnp.exp(gl[1] - gmax)
                   + jnp.exp(gl[2] - gmax) + jnp.exp(gl[3] - gmax))
    eg = [lt[8 + epg * j: 8 + epg * (j + 1), :] for j in range(N_EXPERT_GROUPS)]
    el = jnp.where(grp == 0, eg[0], jnp.where(grp == 1, eg[1], jnp.where(grp == 2, eg[2], eg[3])))
    sub = lax.broadcasted_iota(I32, (epg, tt), 0)
    m1 = jnp.max(el, axis=0, keepdims=True)
    i1 = jnp.min(jnp.where(el == m1, sub, epg), axis=0, keepdims=True)
    rest = jnp.where(sub == i1, -jnp.inf, el)
    m2 = jnp.max(rest, axis=0, keepdims=True)
    i2 = jnp.min(jnp.where(rest == m2, sub, epg), axis=0, keepdims=True)
    r21 = jnp.exp(m2 - m1)
    g1 = p_grp / (1.0 + r21)
    g2 = g1 * r21
    e1 = grp * epg + i1
    e2 = grp * epg + i2

    rows = lax.broadcasted_iota(I32, (ne, tt), 0)
    oh1 = rows == e1
    oh2 = rows == e2
    oh = jnp.where(oh1 | oh2, 1.0, 0.0)
    tri = jnp.where(lax.broadcasted_iota(I32, (tt, tt), 0) < lax.broadcasted_iota(I32, (tt, tt), 1),
                    1.0, 0.0).astype(BF16)
    before = jnp.dot(oh.astype(BF16), tri, preferred_element_type=F32) + carry_ref[:, 0:1]
    r1 = jnp.sum(jnp.where(oh1, before, 0.0), axis=0, keepdims=True)
    r2 = jnp.sum(jnp.where(oh2, before, 0.0), axis=0, keepdims=True)
    carry_ref[...] = carry_ref[...] + jnp.sum(oh, axis=1, keepdims=True)

    eid_ref[...] = jnp.concatenate([e1, e2], axis=0)
    rank_ref[...] = jnp.concatenate([r1, r2], axis=0).astype(I32)
    gate_ref[...] = jnp.concatenate([g1, g2], axis=0)
    cnt_ref[...] = carry_ref[...].astype(I32)


def _router(lt):
    rr, t = lt.shape
    tt = min(512, t)
    return pl.pallas_call(
        _router_kernel,
        out_shape=(jax.ShapeDtypeStruct((2, t), I32),
                   jax.ShapeDtypeStruct((2, t), I32),
                   jax.ShapeDtypeStruct((2, t), F32),
                   jax.ShapeDtypeStruct((N_EXPERTS, 128), I32)),
        grid=(t // tt,),
        in_specs=[pl.BlockSpec((rr, tt), lambda i: (0, i))],
        out_specs=(pl.BlockSpec((2, tt), lambda i: (0, i)),
                   pl.BlockSpec((2, tt), lambda i: (0, i)),
                   pl.BlockSpec((2, tt), lambda i: (0, i)),
                   pl.BlockSpec((N_EXPERTS, 128), lambda i: (0, 0))),
        scratch_shapes=[pltpu.VMEM((N_EXPERTS, 128), F32)],
        compiler_params=_cparams(("arbitrary",)),
        name="router",
    )(lt)


def _invert_kernel(dest_ref, zeros_hbm, out_ref):
    pltpu.sync_copy(zeros_hbm, out_ref)
    t = dest_ref.shape[0] // 2

    def place(i, carry):
        out_ref[dest_ref[i]] = i
        out_ref[dest_ref[t + i]] = i
        return carry
    lax.fori_loop(0, t, place, 0, unroll=8)


def _invert(dest_flat, n_slots):
    return pl.pallas_call(
        _invert_kernel,
        out_shape=jax.ShapeDtypeStruct((n_slots,), I32),
        grid_spec=pltpu.PrefetchScalarGridSpec(
            num_scalar_prefetch=1, grid=(1,),
            in_specs=[pl.BlockSpec(memory_space=pl.ANY)],
            out_specs=pl.BlockSpec(memory_space=pltpu.SMEM)),
        compiler_params=_cparams(("arbitrary",)),
        name="invert",
    )(dest_flat, jnp.zeros((n_slots,), I32))


def _moe_kernel(be_ref, nu_ref, pc_ref, st_ref, hp_hbm, wg_hbm, wu_hbm, wd_hbm, y_ref,
                gbuf, wgf, wuf, wdf, wgb, wub, wdb, par_ref, gsem, wsem):
    for half in range(MOE_BLOCKS_PER_STEP):
        _moe_block(pl.program_id(0) * MOE_BLOCKS_PER_STEP + half, half * MOE_BLK,
                   be_ref, nu_ref, pc_ref, st_ref, hp_hbm, wg_hbm, wu_hbm, wd_hbm, y_ref,
                   gbuf, wgf, wuf, wdf, wgb, wub, wdb, par_ref, gsem, wsem)


def _moe_block(b, row0, be_ref, nu_ref, pc_ref, st_ref, hp_hbm, wg_hbm, wu_hbm, wd_hbm, y_ref,
               gbuf, wgf, wuf, wdf, wgb, wub, wdb, par_ref, gsem, wsem):
    nu = nu_ref[0]
    pr = gbuf.shape[1] // MOE_BLK
    blk = MOE_BLK
    nbuf = gbuf.shape[0]
    look = nbuf - 1

    def row_copy(tok, slot, r):
        src = hp_hbm.at[pl.ds(pl.multiple_of(tok * pr, pr), pr)]
        off = r * pr if isinstance(r, int) else pl.multiple_of(r * pr, pr)
        return pltpu.make_async_copy(src, gbuf.at[slot, pl.ds(off, pr)], gsem.at[slot])

    def issue(block, slot):
        base = block * blk

        def body(r, carry):
            row_copy(st_ref[base + r], slot, r).start()
            return carry
        lax.fori_loop(0, blk, body, 0, unroll=8)

    def weight_copies(e, p):
        return (pltpu.make_async_copy(wg_hbm.at[e], wgf.at[p], wsem.at[p]),
                pltpu.make_async_copy(wu_hbm.at[e], wuf.at[p], wsem.at[p]),
                pltpu.make_async_copy(wd_hbm.at[e], wdf.at[p], wsem.at[p]))

    @pl.when(b == 0)
    def _():
        par_ref[0] = 0
        for cp in weight_copies(be_ref[0], 0):
            cp.start(priority=1)
        issue(0, 0)
        for j in range(1, look):
            @pl.when(j < nu)
            def _():
                issue(j, j)

    def issue_part(k):
        base = (b + look) * blk
        for r in range(k * blk // 4, (k + 1) * blk // 4):
            row_copy(st_ref[base + r], (b + look) % nbuf, r).start(priority=1 if r % 3 == 2 else 0)

    def expert_mlp(slot, with_issue):
        los, his = _load_row_packed(gbuf.at[slot], blk, pr)
        x = jnp.concatenate([v.astype(BF16) for v in los + his], axis=1)
        if with_issue:
            issue_part(0)
        hg = jnp.dot(x, wgb[...], preferred_element_type=F32)
        if with_issue:
            issue_part(1)
        hu = jnp.dot(x, wub[...], preferred_element_type=F32)
        if with_issue:
            issue_part(2)
        hh = hg * (1.0 / (1.0 + jnp.exp(-hg))) * hu
        y = jnp.dot(hh.astype(BF16), wdb[...], preferred_element_type=F32)
        if with_issue:
            issue_part(3)
        _store_row_packed(y_ref, y, row0=row0)

    @pl.when(b < nu)
    def _():
        slot = b % nbuf
        e = be_ref[b]
        first = (b == 0) | (e != be_ref[jnp.maximum(b - 1, 0)])

        @pl.when(first)
        def _():
            p = par_ref[0]
            for cp in weight_copies(e, p):
                cp.wait()
            nxt = lax.while_loop(
                lambda c: (c < N_EXPERTS) & (pc_ref[jnp.minimum(c, N_EXPERTS - 1)] == 0),
                lambda c: c + 1, e + 1)

            @pl.when(nxt < N_EXPERTS)
            def _():
                for cp in weight_copies(nxt, 1 - p):
                    cp.start(priority=1)
            wgb[...] = wgf[p].astype(BF16)
            wub[...] = wuf[p].astype(BF16)
            wdb[...] = wdf[p].astype(BF16)
            par_ref[0] = 1 - p

        pltpu.make_async_copy(hp_hbm.at[pl.ds(0, blk * pr)], gbuf.at[slot], gsem.at[slot]).wait()

        @pl.when(b + look < nu)
        def _():
            expert_mlp(slot, True)

        @pl.when(b + look >= nu)
        def _():
            expert_mlp(slot, False)

    @pl.when(b >= nu)
    def _():
        y_ref[pl.ds(row0 * pr, blk * pr), :] = jnp.zeros((blk * pr, LANES), U32)


def _moe(block_e, n_used, pcounts, slot_tok, hp, w_gate, w_up, w_down, n_blocks):
    ne, d, de = w_gate.shape
    blk = MOE_BLK
    pr = d // 2 // LANES
    bps = MOE_BLOCKS_PER_STEP
    assert n_blocks % bps == 0
    grid_spec = pltpu.PrefetchScalarGridSpec(
        num_scalar_prefetch=4,
        grid=(n_blocks // bps,),
        in_specs=[pl.BlockSpec(memory_space=pl.ANY)] * 4,
        out_specs=pl.BlockSpec((bps * blk * pr, LANES), lambda b, *_: (b, 0)),
        scratch_shapes=[pltpu.VMEM((GATHER_BUFS, blk * pr, LANES), U32),
                        pltpu.VMEM((2, d, de), F32),
                        pltpu.VMEM((2, d, de), F32),
                        pltpu.VMEM((2, de, d), F32),
                        pltpu.VMEM((d, de), BF16),
                        pltpu.VMEM((d, de), BF16),
                        pltpu.VMEM((de, d), BF16),
                        pltpu.SMEM((1,), I32),
                        pltpu.SemaphoreType.DMA((GATHER_BUFS,)),
                        pltpu.SemaphoreType.DMA((2,))],
    )
    return pl.pallas_call(
        _moe_kernel,
        out_shape=jax.ShapeDtypeStruct((n_blocks * blk * pr, LANES), U32),
        grid_spec=grid_spec,
        compiler_params=_cparams(("arbitrary",)),
        name="moe",
    )(block_e, n_used, pcounts, slot_tok, hp, w_gate, w_up, w_down)


def _combine_kernel(d0_ref, d1_ref, hp_ref, gate_ref, yb_hbm, g_ref, b_ref, out_ref, buf, sem, *, alpha):
    i = pl.program_id(0)
    n = pl.num_programs(0)
    tm = out_ref.shape[0]
    pr = buf.shape[2] // tm

    def issue(tile, slot):
        base = tile * tm

        def body(r, carry):
            dst = pl.ds(pl.multiple_of(r * pr, pr), pr)
            for k, dref in enumerate((d0_ref, d1_ref)):
                src = yb_hbm.at[pl.ds(pl.multiple_of(dref[base + r] * pr, pr), pr)]
                pltpu.make_async_copy(src, buf.at[slot, k, dst], sem.at[slot]).start()
            return carry
        lax.fori_loop(0, tm, body, 0, unroll=8)

    nbuf = buf.shape[0]
    look = nbuf - 1

    @pl.when(i == 0)
    def _():
        issue(0, 0)
        for j in range(1, look):
            @pl.when(j < n)
            def _():
                issue(j, j)

    slot = i % nbuf
    for k in range(2):
        pltpu.make_async_copy(yb_hbm.at[pl.ds(0, tm * pr)], buf.at[slot, k], sem.at[slot]).wait()

    nq = 4
    sub = tm // nq

    def issue_part(q):
        base = (i + look) * tm
        nslot = (i + look) % nbuf
        for r in range(q * sub, (q + 1) * sub):
            for k, dref in enumerate((d0_ref, d1_ref)):
                src = yb_hbm.at[pl.ds(pl.multiple_of(dref[base + r] * pr, pr), pr)]
                pltpu.make_async_copy(src, buf.at[nslot, k, pl.ds(r * pr, pr)], sem.at[nslot]).start(priority=k)

    def finish(with_issue):
        for q in range(nq):
            if with_issue:
                issue_part(q)
            rows = pl.ds(q * sub, sub)
            lo0, hi0 = _load_row_packed(buf.at[slot, 0], sub, pr, row0=q * sub, regroup=True)
            lo1, hi1 = _load_row_packed(buf.at[slot, 1], sub, pr, row0=q * sub, regroup=True)
            loh, hih = _load_row_packed(hp_ref, sub, pr, row0=q * sub, regroup=True)
            g0 = gate_ref[rows, 0:1]
            g1 = gate_ref[rows, 1:2]
            r = jnp.concatenate([alpha * hh + g0 * a + g1 * c
                                 for hh, a, c in zip(loh + hih, lo0 + hi0, lo1 + hi1)], axis=1)
            out_ref[rows, :] = _layer_norm(r, g_ref[...], b_ref[...])

    @pl.when(i + look < n)
    def _():
        finish(True)

    @pl.when(i + look >= n)
    def _():
        finish(False)


def _combine(dest0, dest1, hp, gate_t, yb, ln_g, ln_b, alpha):
    d = ln_g.shape[1]
    pr = d // 2 // LANES
    t = hp.shape[0] // pr
    tm = min(256, t)
    grid_spec = pltpu.PrefetchScalarGridSpec(
        num_scalar_prefetch=2,
        grid=(t // tm,),
        in_specs=[pl.BlockSpec((tm * pr, LANES), lambda i, d0, d1: (i, 0)),
                  pl.BlockSpec((tm, 2), lambda i, d0, d1: (i, 0)),
                  pl.BlockSpec(memory_space=pl.ANY),
                  pl.BlockSpec((1, d), lambda i, d0, d1: (0, 0)),
                  pl.BlockSpec((1, d), lambda i, d0, d1: (0, 0))],
        out_specs=pl.BlockSpec((tm, d), lambda i, d0, d1: (i, 0)),
        scratch_shapes=[pltpu.VMEM((GATHER_BUFS, 2, tm * (d // 2 // LANES), LANES), U32),
                        pltpu.SemaphoreType.DMA((GATHER_BUFS,))],
    )
    return pl.pallas_call(
        functools.partial(_combine_kernel, alpha=alpha),
        out_shape=jax.ShapeDtypeStruct((t, d), F32),
        grid_spec=grid_spec,
        compiler_params=_cparams(("arbitrary",)),
        name="combine",
    )(dest0, dest1, hp, gate_t, yb, ln_g, ln_b)


def _layer(h, w_in, pool_w, pool_scale, a_re, a_im, log_dt, b_re, b_im, c_re, c_im, d_skip,
           glu_w, glu_b, w_out, ln1_g, ln1_b, rg_w, rg_b, re_w, re_b, w_gate, w_up, w_down,
           ln2_g, ln2_b, alpha):
    bsz, seq, d = h.shape
    t = bsz * seq
    L = CHUNK
    dp = pool_w.shape[0] * pool_w.shape[1]
    ds = w_in.shape[1] - dp
    g = ds // SSM_GROUP
    n = t // L

    lag0, bb, w1t, w2t, al = _ssm_tables(a_re, a_im, log_dt, b_re, b_im, c_re, c_im, d_skip, L)
    half = ROUTER_LANES // 2
    zpad = lambda k: jnp.zeros((d, k), F32)
    rw = jnp.concatenate([rg_w.astype(F32), zpad(8 - N_EXPERT_GROUPS), re_w.astype(F32),
                          zpad(half - ROUTER_ROWS)], axis=1)
    rw_hi = rw.astype(BF16)
    rw_lo = (rw - rw_hi.astype(F32)).astype(BF16)
    rw2 = jnp.concatenate([rw_hi, rw_lo], axis=1)
    rb = jnp.concatenate([rg_b.astype(F32), jnp.zeros((8 - N_EXPERT_GROUPS,), F32), re_b.astype(F32),
                          jnp.zeros((ROUTER_LANES - ROUTER_ROWS,), F32)]).reshape(1, ROUTER_LANES)

    xt = h.reshape(t, d)
    pool_p, ut = _proj(xt, w_in.astype(BF16), dp, L)
    y_pool = _pool(pool_p, pool_w.astype(BF16), pool_scale.reshape(1, dp).astype(F32), seq)
    yt = _ssm(ut.reshape(g, L * SSM_GROUP, n), lag0, bb, w1t, w2t, al, seq // L, L)
    y_ssm = _glu(yt.reshape(g, L, SSM_GROUP, n), glu_w.astype(BF16), glu_b.reshape(1, ds).astype(F32))
    hp, lt = _outproj(y_pool, y_ssm, xt, w_out.astype(BF16), ln1_g.reshape(1, d), ln1_b.reshape(1, d),
                          rw2, rb, alpha)
    eid, rank, gate, cnt = _router(lt)

    blk = MOE_BLK
    m = 2 * t
    n_blocks = -(-m // blk) + N_EXPERTS
    counts = cnt[:, 0]
    pcounts = (counts + blk - 1) // blk * blk
    pends = jnp.cumsum(pcounts)
    pstarts = pends - pcounts
    e_ids = jnp.arange(N_EXPERTS, dtype=I32)
    dest = jnp.sum(jnp.where(eid[..., None] == e_ids, pstarts, 0), axis=-1) + rank
    slot_tok = _invert(dest.reshape(-1), n_blocks * blk)
    n_used = (pends[-1] // blk).astype(I32)
    bidx = jnp.minimum(jnp.arange(n_blocks, dtype=I32), n_used - 1)
    block_e = jnp.minimum(jnp.sum((pends[None, :] <= (bidx * blk)[:, None]).astype(I32), axis=1),
                          N_EXPERTS - 1)

    yb = _moe(block_e, n_used.reshape(1), pcounts, slot_tok, hp, w_gate, w_up, w_down, n_blocks)
    out = _combine(dest[0], dest[1], hp, gate.T, yb, ln2_g.reshape(1, d), ln2_b.reshape(1, d), alpha)
    return out.reshape(bsz, seq, d)


def kernel(x, w_in, pool_w, pool_scale, ssm_a_re, ssm_a_im, ssm_log_dt, ssm_b_re, ssm_b_im, ssm_c_re, ssm_c_im, ssm_d, glu_w, glu_b, w_out, ln1_g, ln1_b, router_g_w, router_g_b, router_e_w, router_e_b, w_gate, w_up, w_down, ln2_g, ln2_b):
    depth = w_in.shape[0]
    alpha = (2.0 * depth) ** 0.25
    h = x
    for l in range(depth):
        h = _layer(h, w_in[l], pool_w[l], pool_scale[l], ssm_a_re[l], ssm_a_im[l], ssm_log_dt[l],
                   ssm_b_re[l], ssm_b_im[l], ssm_c_re[l], ssm_c_im[l], ssm_d[l], glu_w[l], glu_b[l],
                   w_out[l], ln1_g[l], ln1_b[l], router_g_w[l], router_g_b[l], router_e_w[l],
                   router_e_b[l], w_gate[l], w_up[l], w_down[l], ln2_g[l], ln2_b[l], alpha)
    return h
```

```python
import functools
import math

import numpy as np
import jax
import jax.numpy as jnp
from jax import lax
from jax.experimental import pallas as pl
from jax.experimental.pallas import tpu as pltpu

F32 = jnp.float32
BF16 = jnp.bfloat16
I32 = jnp.int32
U32 = jnp.uint32

POOL_WINDOWS = (2, 4, 8, 16)
POOL_GROUP = 256
SSM_GROUP = 16
SSM_STATE = 64
N_EXPERT_GROUPS = 4
EXPERTS_PER_GROUP = 8
N_EXPERTS = N_EXPERT_GROUPS * EXPERTS_PER_GROUP
LN_EPS = 1e-5

CHUNK = 16
MOE_BLK = 256
MOE_BLOCKS_PER_STEP = 4
GATHER_BUFS = 4
ROUTER_ROWS = 8 + N_EXPERTS
ROUTER_LANES = 128
LANES = 128
MXU_N = 256
HALO = 16
VMEM_LIMIT = 56 * 1024 * 1024


def _cparams(sem, vmem=VMEM_LIMIT):
    return pltpu.CompilerParams(dimension_semantics=sem, vmem_limit_bytes=vmem)


def _proj_kernel(x_ref, w_ref, pool_ref, ut_ref, accp_ref, accs_ref):
    kk = pl.program_id(1)
    rows, dp = accp_ref.shape
    nj = accs_ref.shape[0]

    xb = x_ref[...].astype(BF16)
    per = MXU_N // LANES

    def accumulate(first):
        for n in range(dp // MXU_N):
            cs = slice(n * MXU_N, (n + 1) * MXU_N)
            part = jnp.dot(xb, w_ref[:, cs], preferred_element_type=F32)
            accp_ref[:, cs] = part if first else accp_ref[:, cs] + part
        for n in range(nj // per):
            part = jnp.dot(xb, w_ref[:, dp + n * MXU_N: dp + (n + 1) * MXU_N], preferred_element_type=F32)
            for q in range(per):
                piece = part[:, q * LANES:(q + 1) * LANES]
                accs_ref[n * per + q] = piece if first else accs_ref[n * per + q] + piece

    @pl.when(kk == 0)
    def _():
        accumulate(True)

    @pl.when(kk > 0)
    def _():
        accumulate(False)

    @pl.when(kk == pl.num_programs(1) - 1)
    def _():
        pool_ref[...] = accp_ref[...].astype(BF16)
        g, L, c, r = ut_ref.shape
        gj = LANES // c
        for j in range(nj):
            steps = pltpu.einshape("rsl->srl", accs_ref[j].reshape(r, L, LANES))
            for s in range(L):
                ut_ref[j * gj:(j + 1) * gj, s, :, :] = steps[s].astype(BF16).T.reshape(gj, c, r)


def _proj(xt, w_in, dp, L):
    t, d = xt.shape
    dm = w_in.shape[1]
    g = (dm - dp) // SSM_GROUP
    tm = LANES * L
    kb = 512
    return pl.pallas_call(
        _proj_kernel,
        out_shape=(jax.ShapeDtypeStruct((t, dp), BF16),
                   jax.ShapeDtypeStruct((g, L, SSM_GROUP, t // L), BF16)),
        grid=(t // tm, d // kb),
        in_specs=[pl.BlockSpec((tm, kb), lambda i, k: (i, k)),
                  pl.BlockSpec((kb, dm), lambda i, k: (k, 0))],
        out_specs=(pl.BlockSpec((tm, dp), lambda i, k: (i, 0)),
                   pl.BlockSpec((g, L, SSM_GROUP, LANES), lambda i, k: (0, 0, 0, i))),
        scratch_shapes=[pltpu.VMEM((tm, dp), F32),
                        pltpu.VMEM(((dm - dp) // LANES, tm, LANES), F32)],
        compiler_params=_cparams(("arbitrary", "arbitrary")),
        name="proj",
    )(xt, w_in)


def _pool_kernel(prev_ref, main_ref, next_ref, pw_ref, sc_ref, out_ref, *, seq, sub):
    i = pl.program_id(0)
    ts = main_ref.shape[0]
    ext = jnp.concatenate([prev_ref[...], main_ref[...], next_ref[...]], axis=0)
    k = sub + 2 * HALO
    row = lax.broadcasted_iota(I32, (sub, k), 0)
    col = lax.broadcasted_iota(I32, (sub, k), 1)
    off0 = col - HALO - row
    for sb in range(ts // sub):
        rows = slice(sb * sub, (sb + 1) * sub)
        base = (i * ts + sb * sub) % seq
        src = base + row + off0
        off = jnp.where((src >= 0) & (src < seq), off0, 2 * HALO)
        pos = base + lax.broadcasted_iota(I32, (sub, 1), 0)
        for g, w in enumerate(POOL_WINDOWS):
            sl = slice(g * POOL_GROUP, (g + 1) * POOL_GROUP)
            band = (off + w // 2).astype(U32) < w
            bm = jnp.where(band, 1.0, 0.0).astype(BF16)
            sums = jnp.dot(bm, ext[sb * sub:sb * sub + k, sl], preferred_element_type=F32)
            lo = jnp.maximum(pos - w // 2, 0)
            hi = jnp.minimum(pos - w // 2 + w, seq)
            inv = 1.0 / (hi - lo).astype(F32)
            dlt = sums * inv - main_ref[rows, sl].astype(F32)
            y = jnp.dot(dlt.astype(BF16), pw_ref[g], preferred_element_type=F32) * sc_ref[:, sl]
            out_ref[rows, sl] = y.astype(BF16)


def _pool(pp, pool_w, pool_scale, seq):
    t, dp = pp.shape
    ts = min(1024, seq)
    nh = t // HALO
    per = ts // HALO
    return pl.pallas_call(
        functools.partial(_pool_kernel, seq=seq, sub=256),
        out_shape=jax.ShapeDtypeStruct((t, dp), BF16),
        grid=(t // ts,),
        in_specs=[pl.BlockSpec((HALO, dp), lambda i: (jnp.maximum(i * per - 1, 0), 0)),
                  pl.BlockSpec((ts, dp), lambda i: (i, 0)),
                  pl.BlockSpec((HALO, dp), lambda i: (jnp.minimum((i + 1) * per, nh - 1), 0)),
                  pl.BlockSpec(pool_w.shape, lambda i: (0, 0, 0)),
                  pl.BlockSpec((1, dp), lambda i: (0, 0))],
        out_specs=pl.BlockSpec((ts, dp), lambda i: (i, 0)),
        compiler_params=_cparams(("arbitrary",)),
        name="pool",
    )(pp, pp, pp, pool_w, pool_scale)


def _ssm_tables(a_re, a_im, log_dt, b_re, b_im, c_re, c_im, d_skip, L):
    g = a_re.shape[1]
    p = a_re.shape[2]
    c = b_re.shape[3]
    fl = L * c
    lr = a_re.astype(F32)
    li = a_im.astype(F32)
    dt = jnp.exp(log_dt.astype(F32))[..., None]
    mag = jnp.exp(lr * dt)
    abr = mag * jnp.cos(li * dt)
    abi = mag * jnp.sin(li * dt)
    den = lr * lr + li * li
    zr = ((abr - 1.0) * lr + abi * li) / den
    zi = (abi * lr - (abr - 1.0) * li) / den
    br = b_re.astype(F32)
    bi = b_im.astype(F32)
    bbr = zr[..., None] * br - zi[..., None] * bi
    bbi = zr[..., None] * bi + zi[..., None] * br
    cr = c_re.astype(F32)
    ci = c_im.astype(F32)
    kk = jnp.arange(L + 1, dtype=F32)[None, None, :, None]
    pm = jnp.exp(kk * (lr * dt)[:, :, None, :])
    ang = kk * (li * dt)[:, :, None, :]
    pr = pm * jnp.cos(ang)
    pi = pm * jnp.sin(ang)

    skip = jnp.eye(c, dtype=F32)[None] * d_skip.astype(F32).reshape(g, c, 1)
    lag0 = jnp.einsum('dgop,dgpc->goc', cr, bbr, precision=lax.Precision.HIGHEST) \
        - jnp.einsum('dgop,dgpc->goc', ci, bbi, precision=lax.Precision.HIGHEST) + skip
    lag0 = jnp.pad(lag0, ((0, 0), (0, 0), (0, LANES - c)))
    bb = jnp.concatenate([bbr[0], bbi[0], bbr[1], bbi[1]], axis=1)
    bb = jnp.pad(bb, ((0, 0), (0, 0), (0, LANES - c)))

    def w1(d, qr, qi):
        qr = qr.transpose(0, 2, 1)[..., None]
        qi = qi.transpose(0, 2, 1)[..., None]
        re = qr * bbr[d][:, :, None] - qi * bbi[d][:, :, None]
        im = qr * bbi[d][:, :, None] + qi * bbr[d][:, :, None]
        to = lambda v: v.reshape(g, p, fl)
        return to(re), to(im)
    f_re, f_im = w1(0, pr[0][:, L - 1::-1], pi[0][:, L - 1::-1])
    b_re_, b_im_ = w1(1, pr[1][:, :L], pi[1][:, :L])
    w1t = jnp.concatenate([f_re, f_im, b_re_, b_im_], axis=1)

    def w2(d, qr, qi):
        re = cr[d][:, None] * qr[:, :, None] - ci[d][:, None] * qi[:, :, None]
        im = -(cr[d][:, None] * qi[:, :, None] + ci[d][:, None] * qr[:, :, None])
        to = lambda v: v.reshape(g, fl, p)
        return to(re), to(im)
    rf_re, rf_im = w2(0, pr[0][:, 1:], pi[0][:, 1:])
    rb_re, rb_im = w2(1, pr[1][:, :0:-1], pi[1][:, :0:-1])
    w2t = jnp.concatenate([rf_re, rf_im, rb_re, rb_im], axis=2)

    a4 = jnp.stack([pr[0][:, L], pi[0][:, L], pr[1][:, L], pi[1][:, L]], axis=1)
    ap = a4.reshape(g // 2, 2, 4, p).transpose(0, 2, 1, 3).reshape(g // 2, 4, 2 * p)
    ap = jnp.pad(ap, ((0, 0), (0, 4), (0, 0)))
    return lag0, bb.astype(BF16), w1t.astype(BF16), w2t.astype(BF16), ap


def _cmul(ar, ai, br, bi):
    return ar * br - ai * bi, ar * bi + ai * br


def _ssm_kernel(ut_ref, w1_ref, lag0_ref, bb_ref, w2_ref, a_ref, yt_ref, mt_s, kk_s, zt_s, ht_s, *, nseg, L):
    p = SSM_STATE
    c = SSM_GROUP
    fl, n = ut_ref.shape[1:]
    nb = n // nseg
    nrow = nseg // 8
    st = [jnp.dot(w1_ref[q], ut_ref[q], preferred_element_type=F32) for q in range(2)]
    z = jnp.concatenate([st[q][k * p:(k + 1) * p] for k in range(4) for q in range(2)], axis=0)
    zt_s[...] = z.astype(BF16).T.astype(F32)
    per = LANES // c
    for q in range(2):
        kf = jnp.dot(w2_ref[q, :, 0:2 * p], bb_ref[q, 0:2 * p, :], preferred_element_type=F32)
        kb = jnp.dot(w2_ref[q, :, 2 * p:4 * p], bb_ref[q, 2 * p:4 * p, :], preferred_element_type=F32)
        tiles = [None] * (2 * L // per)
        for j in range(2 * L - 1):
            lag = L - 1 - j
            if lag > 0:
                blk = kf[(lag - 1) * c:lag * c, :]
            elif lag == 0:
                blk = lag0_ref[q]
            else:
                blk = kb[(L + lag) * c:(L + lag + 1) * c, :]
            sh = (j % per) * c
            blk = blk if sh == 0 else pltpu.roll(blk, sh, 1)
            tiles[j // per] = blk if tiles[j // per] is None else tiles[j // per] + blk
        kk_s[...] = jnp.concatenate(tiles, axis=1).astype(BF16)
        for t in range(L):
            mt_s[q, t * c:(t + 1) * c, :] = kk_s[:, (L - 1 - t) * c:(L - 1 - t) * c + fl]

    sub = lax.broadcasted_iota(I32, (8, LANES), 0)
    full = lambda v: jnp.broadcast_to(v, (8, LANES))
    one, zero = jnp.ones((8, LANES), F32), jnp.zeros((8, LANES), F32)

    def powers(row):
        out = [(full(a_ref[0, row:row + 1, :]), full(a_ref[0, row + 1:row + 2, :]))]
        for _ in range(3):
            out.append(_cmul(*out[-1], *out[-1]))
        return out

    def by_bits(pw, idx):
        qr, qi = one, zero
        for bit in range(3):
            on = (idx >> bit) & 1 == 1
            qr, qi = _cmul(qr, qi, jnp.where(on, pw[bit][0], one), jnp.where(on, pw[bit][1], zero))
        return qr, qi

    pw_f, pw_b = powers(0), powers(2)
    qf = by_bits(pw_f, sub)
    qb = by_bits(pw_b, 7 - sub)

    def local(xr, xi, pw, forward):
        for lvl, d in enumerate((1, 2, 4)):
            keep = (sub >= d) if forward else (sub < 8 - d)
            sh = d if forward else 8 - d
            tr = jnp.where(keep, pltpu.roll(xr, sh, 0), 0.0)
            ti = jnp.where(keep, pltpu.roll(xi, sh, 0), 0.0)
            mr, mi = _cmul(pw[lvl][0], pw[lvl][1], tr, ti)
            xr, xi = xr + mr, xi + mi
        return xr, xi

    def step(i, carry):
        new = []
        for b in range(nb):
            for fwd in (True, False):
                cr, ci = carry[2 * (2 * b + (0 if fwd else 1)):][:2]
                row = b * nseg + (i if fwd else nrow - 1 - i) * 8
                rows = pl.ds(pl.multiple_of(row, 8), 8)
                l0 = 0 if fwd else 2 * LANES
                pw, (qr, qi) = (pw_f, qf) if fwd else (pw_b, qb)
                lr, li = local(zt_s[rows, l0:l0 + LANES], zt_s[rows, l0 + LANES:l0 + 2 * LANES], pw, fwd)
                keep = (sub >= 1) if fwd else (sub < 7)
                er = jnp.where(keep, pltpu.roll(lr, 1 if fwd else 7, 0), 0.0)
                ei = jnp.where(keep, pltpu.roll(li, 1 if fwd else 7, 0), 0.0)
                mr, mi = _cmul(qr, qi, cr, ci)
                ht_s[rows, l0:l0 + LANES] = er + mr
                ht_s[rows, l0 + LANES:l0 + 2 * LANES] = ei + mi
                edge = 7 if fwd else 0
                mr, mi = _cmul(pw[3][0], pw[3][1], cr, ci)
                new += [full(lr[edge:edge + 1, :]) + mr, full(li[edge:edge + 1, :]) + mi]
        return tuple(new)

    lax.fori_loop(0, nrow, step, (zero,) * (4 * nb))

    h = ht_s[...].astype(BF16).T
    for q in range(2):
        hq = jnp.concatenate([h[(2 * k + q) * p:(2 * k + q + 1) * p] for k in range(4)], axis=0)
        y = jnp.dot(mt_s[q], ut_ref[q], preferred_element_type=F32) \
            + jnp.dot(w2_ref[q], hq, preferred_element_type=F32)
        yt_ref[q] = y.astype(BF16)


def _ssm(ut3, lag0, bb, w1t, w2t, ap, nseg, L):
    g, fl, n = ut3.shape
    sp = w1t.shape[1]
    return pl.pallas_call(
        functools.partial(_ssm_kernel, nseg=nseg, L=L),
        out_shape=jax.ShapeDtypeStruct((g, fl, n), BF16),
        grid=(g // 2,),
        in_specs=[pl.BlockSpec((2, fl, n), lambda i: (i, 0, 0)),
                  pl.BlockSpec((2, sp, fl), lambda i: (i, 0, 0)),
                  pl.BlockSpec((2,) + lag0.shape[1:], lambda i: (i, 0, 0)),
                  pl.BlockSpec((2,) + bb.shape[1:], lambda i: (i, 0, 0)),
                  pl.BlockSpec((2, fl, sp), lambda i: (i, 0, 0)),
                  pl.BlockSpec((1,) + ap.shape[1:], lambda i: (i, 0, 0))],
        out_specs=pl.BlockSpec((2, fl, n), lambda i: (i, 0, 0)),
        scratch_shapes=[pltpu.VMEM((2, fl, fl), BF16),
                        pltpu.VMEM((SSM_GROUP, 2 * fl), BF16),
                        pltpu.VMEM((n, 2 * sp), F32),
                        pltpu.VMEM((n, 2 * sp), F32)],
        compiler_params=_cparams(("arbitrary",)),
        name="ssm",
    )(ut3, w1t, lag0, bb, w2t, ap)


def _glu_kernel(yt_ref, w_ref, b_ref, out_ref, nat_ref, *, sub):
    g, L, c, r = yt_ref.shape
    nj = nat_ref.shape[0]
    gj = LANES // c
    for j in range(nj):
        steps = jnp.stack([yt_ref[j * gj:(j + 1) * gj, t, :, :].reshape(LANES, r).astype(F32).T
                           for t in range(L)])
        nat_ref[j] = pltpu.einshape("srl->rsl", steps).reshape(r * L, LANES)

    def body(k, carry):
        sl = pl.ds(pl.multiple_of(k * sub, sub), sub)
        y = jnp.concatenate([nat_ref[j, sl, :] for j in range(nj)], axis=1)
        ya = jax.nn.gelu(y, approximate=True)
        z = jnp.dot(ya.astype(BF16), w_ref[...], preferred_element_type=F32) + b_ref[...]
        out_ref[sl, :] = (ya * (1.0 / (1.0 + jnp.exp(-z)))).astype(BF16)
        return carry
    lax.fori_loop(0, nat_ref.shape[1] // sub, body, 0)


def _glu(yt4, glu_w, glu_b):
    g, L, c, n = yt4.shape
    ds = g * c
    r = LANES
    return pl.pallas_call(
        functools.partial(_glu_kernel, sub=512),
        out_shape=jax.ShapeDtypeStruct((n * L, ds), BF16),
        grid=(n // r,),
        in_specs=[pl.BlockSpec((g, L, c, r), lambda j: (0, 0, 0, j)),
                  pl.BlockSpec((ds, ds), lambda j: (0, 0)),
                  pl.BlockSpec((1, ds), lambda j: (0, 0))],
        out_specs=pl.BlockSpec((r * L, ds), lambda j: (j, 0)),
        scratch_shapes=[pltpu.VMEM((ds // LANES, r * L, LANES), F32)],
        compiler_params=_cparams(("arbitrary",)),
        name="glu",
    )(yt4, glu_w, glu_b)


def _layer_norm(r, g, b):
    mu = jnp.mean(r, axis=-1, keepdims=True)
    cen = r - mu
    var = jnp.mean(cen * cen, axis=-1, keepdims=True)
    return cen * lax.rsqrt(var + LN_EPS) * g + b


def _store_row_packed(ref, m, row0=0):
    rows, width = m.shape
    half = width // 2
    nc = half // LANES
    lo = lax.bitcast_convert_type(m[:, :half].astype(BF16).astype(F32), U32) >> 16
    hi = lax.bitcast_convert_type(m[:, half:].astype(BF16).astype(F32), U32) & jnp.uint32(0xFFFF0000)
    pk = lo | hi
    for c in range(nc):
        ref[pl.ds(row0 * nc + c, rows, stride=nc), :] = pk[:, c * LANES:(c + 1) * LANES]


def _load_row_packed(ref, rows, nc, row0=0, regroup=False):
    if regroup:
        tiles = pltpu.einshape("rcl->crl", ref[pl.ds(row0 * nc, rows * nc), :].reshape(rows, nc, LANES))
    else:
        tiles = [ref[pl.ds(row0 * nc + c, rows, stride=nc), :] for c in range(nc)]
    los = [lax.bitcast_convert_type(tiles[c] << 16, F32) for c in range(nc)]
    his = [lax.bitcast_convert_type(tiles[c] & jnp.uint32(0xFFFF0000), F32) for c in range(nc)]
    return los, his


def _outproj_kernel(yp_ref, ys_ref, x_ref, wo_ref, g_ref, b_ref, rw_ref, rb_ref,
                    hp_ref, lt_ref, *, alpha, dp, sub):
    for sb in range(x_ref.shape[0] // sub):
        sl = pl.ds(sb * sub, sub)
        mix = jnp.dot(yp_ref[sl, :], wo_ref[0:dp, :], preferred_element_type=F32) \
            + jnp.dot(ys_ref[sl, :], wo_ref[dp:, :], preferred_element_type=F32)
        h = _layer_norm(alpha * x_ref[sl, :] + mix, g_ref[...], b_ref[...])
        _store_row_packed(hp_ref, h, row0=sb * sub)
        hh = h.astype(BF16)
        hl = (h - hh.astype(F32)).astype(BF16)
        part = jnp.dot(hh, rw_ref[...], preferred_element_type=F32) \
            + jnp.dot(hl, rw_ref[...], preferred_element_type=F32)
        logits = part + pltpu.roll(part, ROUTER_LANES // 2, 1) + rb_ref[...]
        lt_ref[:, sb * sub:(sb + 1) * sub] = logits.T[0:ROUTER_ROWS, :]


def _outproj(y_pool, y_ssm, xt, w_out, ln_g, ln_b, rw, rb, alpha):
    t, d = xt.shape
    dp = y_pool.shape[1]
    tm = min(512, t)
    pr = d // 2 // LANES
    return pl.pallas_call(
        functools.partial(_outproj_kernel, alpha=alpha, dp=dp, sub=256),
        out_shape=(jax.ShapeDtypeStruct((t * pr, LANES), U32),
                   jax.ShapeDtypeStruct((ROUTER_ROWS, t), F32)),
        grid=(t // tm,),
        in_specs=[pl.BlockSpec((tm, dp), lambda i: (i, 0)),
                  pl.BlockSpec((tm, y_ssm.shape[1]), lambda i: (i, 0)),
                  pl.BlockSpec((tm, d), lambda i: (i, 0)),
                  pl.BlockSpec(w_out.shape, lambda i: (0, 0)),
                  pl.BlockSpec((1, d), lambda i: (0, 0)),
                  pl.BlockSpec((1, d), lambda i: (0, 0)),
                  pl.BlockSpec(rw.shape, lambda i: (0, 0)),
                  pl.BlockSpec((1, ROUTER_LANES), lambda i: (0, 0))],
        out_specs=(pl.BlockSpec((tm * pr, LANES), lambda i: (i, 0)),
                   pl.BlockSpec((ROUTER_ROWS, tm), lambda i: (0, i))),
        compiler_params=_cparams(("arbitrary",)),
        name="outproj",
    )(y_pool, y_ssm, xt, w_out, ln_g, ln_b, rw, rb)


def _router_kernel(lt_ref, eid_ref, rank_ref, gate_ref, cnt_ref, carry_ref):
    i = pl.program_id(0)
    ne, epg = N_EXPERTS, EXPERTS_PER_GROUP

    @pl.when(i == 0)
    def _():
        carry_ref[...] = jnp.zeros_like(carry_ref)

    lt = lt_ref[...]
    tt = lt.shape[1]
    gl = [lt[j:j + 1, :] for j in range(N_EXPERT_GROUPS)]
    gmax = jnp.maximum(jnp.maximum(gl[0], gl[1]), jnp.maximum(gl[2], gl[3]))
    grp = jnp.where(gl[0] == gmax, 0, jnp.where(gl[1] == gmax, 1, jnp.where(gl[2] == gmax, 2, 3)))
    p_grp = 1.0 / (jnp.exp(gl[0] - gmax) + jnp.exp(gl[1] - gmax)
                   + jnp.exp(gl[2] - gmax) + jnp.exp(gl[3] - gmax))
    eg = [lt[8 + epg * j: 8 + epg * (j + 1), :] for j in range(N_EXPERT_GROUPS)]
    el = jnp.where(grp == 0, eg[0], jnp.where(grp == 1, eg[1], jnp.where(grp == 2, eg[2], eg[3])))
    sub = lax.broadcasted_iota(I32, (epg, tt), 0)
    m1 = jnp.max(el, axis=0, keepdims=True)
    i1 = jnp.min(jnp.where(el == m1, sub, epg), axis=0, keepdims=True)
    rest = jnp.where(sub == i1, -jnp.inf, el)
    m2 = jnp.max(rest, axis=0, keepdims=True)
    i2 = jnp.min(jnp.where(rest == m2, sub, epg), axis=0, keepdims=True)
    r21 = jnp.exp(m2 - m1)
    g1 = p_grp / (1.0 + r21)
    g2 = g1 * r21
    e1 = grp * epg + i1
    e2 = grp * epg + i2

    rows = lax.broadcasted_iota(I32, (ne, tt), 0)
    oh1 = rows == e1
    oh2 = rows == e2
    oh = jnp.where(oh1 | oh2, 1.0, 0.0)
    tri = jnp.where(lax.broadcasted_iota(I32, (tt, tt), 0) < lax.broadcasted_iota(I32, (tt, tt), 1),
                    1.0, 0.0).astype(BF16)
    before = jnp.dot(oh.astype(BF16), tri, preferred_element_type=F32) + carry_ref[:, 0:1]
    r1 = jnp.sum(jnp.where(oh1, before, 0.0), axis=0, keepdims=True)
    r2 = jnp.sum(jnp.where(oh2, before, 0.0), axis=0, keepdims=True)
    carry_ref[...] = carry_ref[...] + jnp.sum(oh, axis=1, keepdims=True)

    eid_ref[...] = jnp.concatenate([e1, e2], axis=0)
    rank_ref[...] = jnp.concatenate([r1, r2], axis=0).astype(I32)
    gate_ref[...] = jnp.concatenate([g1, g2], axis=0)
    cnt_ref[...] = carry_ref[...].astype(I32)


def _router(lt):
    rr, t = lt.shape
    tt = min(512, t)
    return pl.pallas_call(
        _router_kernel,
        out_shape=(jax.ShapeDtypeStruct((2, t), I32),
                   jax.ShapeDtypeStruct((2, t), I32),
                   jax.ShapeDtypeStruct((2, t), F32),
                   jax.ShapeDtypeStruct((N_EXPERTS, 128), I32)),
        grid=(t // tt,),
        in_specs=[pl.BlockSpec((rr, tt), lambda i: (0, i))],
        out_specs=(pl.BlockSpec((2, tt), lambda i: (0, i)),
                   pl.BlockSpec((2, tt), lambda i: (0, i)),
                   pl.BlockSpec((2, tt), lambda i: (0, i)),
                   pl.BlockSpec((N_EXPERTS, 128), lambda i: (0, 0))),
        scratch_shapes=[pltpu.VMEM((N_EXPERTS, 128), F32)],
        compiler_params=_cparams(("arbitrary",)),
        name="router",
    )(lt)


def _invert_kernel(dest_ref, zeros_hbm, out_ref):
    pltpu.sync_copy(zeros_hbm, out_ref)
    t = dest_ref.shape[0] // 2

    def place(i, carry):
        out_ref[dest_ref[i]] = i
        out_ref[dest_ref[t + i]] = i
        return carry
    lax.fori_loop(0, t, place, 0, unroll=8)


def _invert(dest_flat, n_slots):
    return pl.pallas_call(
        _invert_kernel,
        out_shape=jax.ShapeDtypeStruct((n_slots,), I32),
        grid_spec=pltpu.PrefetchScalarGridSpec(
            num_scalar_prefetch=1, grid=(1,),
            in_specs=[pl.BlockSpec(memory_space=pl.ANY)],
            out_specs=pl.BlockSpec(memory_space=pltpu.SMEM)),
        compiler_params=_cparams(("arbitrary",)),
        name="invert",
    )(dest_flat, jnp.zeros((n_slots,), I32))


def _moe_kernel(be_ref, nu_ref, pc_ref, st_ref, hp_hbm, wg_hbm, wu_hbm, wd_hbm, y_ref,
                gbuf, wgf, wuf, wdf, wgb, wub, wdb, par_ref, gsem, wsem):
    for half in range(MOE_BLOCKS_PER_STEP):
        _moe_block(pl.program_id(0) * MOE_BLOCKS_PER_STEP + half, half * MOE_BLK,
                   be_ref, nu_ref, pc_ref, st_ref, hp_hbm, wg_hbm, wu_hbm, wd_hbm, y_ref,
                   gbuf, wgf, wuf, wdf, wgb, wub, wdb, par_ref, gsem, wsem)


def _moe_block(b, row0, be_ref, nu_ref, pc_ref, st_ref, hp_hbm, wg_hbm, wu_hbm, wd_hbm, y_ref,
               gbuf, wgf, wuf, wdf, wgb, wub, wdb, par_ref, gsem, wsem):
    nu = nu_ref[0]
    pr = gbuf.shape[1] // MOE_BLK
    blk = MOE_BLK
    nbuf = gbuf.shape[0]
    look = nbuf - 1

    def row_copy(tok, slot, r):
        src = hp_hbm.at[pl.ds(pl.multiple_of(tok * pr, pr), pr)]
        off = r * pr if isinstance(r, int) else pl.multiple_of(r * pr, pr)
        return pltpu.make_async_copy(src, gbuf.at[slot, pl.ds(off, pr)], gsem.at[slot])

    def issue(block, slot):
        base = block * blk

        def body(r, carry):
            row_copy(st_ref[base + r], slot, r).start()
            return carry
        lax.fori_loop(0, blk, body, 0, unroll=8)

    def weight_copies(e, p):
        return (pltpu.make_async_copy(wg_hbm.at[e], wgf.at[p], wsem.at[p]),
                pltpu.make_async_copy(wu_hbm.at[e], wuf.at[p], wsem.at[p]),
                pltpu.make_async_copy(wd_hbm.at[e], wdf.at[p], wsem.at[p]))

    @pl.when(b == 0)
    def _():
        par_ref[0] = 0
        for cp in weight_copies(be_ref[0], 0):
            cp.start(priority=1)
        issue(0, 0)
        for j in range(1, look):
            @pl.when(j < nu)
            def _():
                issue(j, j)

    def issue_part(k):
        base = (b + look) * blk
        for r in range(k * blk // 4, (k + 1) * blk // 4):
            row_copy(st_ref[base + r], (b + look) % nbuf, r).start(priority=1 if r % 3 == 2 else 0)

    def expert_mlp(slot, with_issue):
        los, his = _load_row_packed(gbuf.at[slot], blk, pr)
        x = jnp.concatenate([v.astype(BF16) for v in los + his], axis=1)
        if with_issue:
            issue_part(0)
        hg = jnp.dot(x, wgb[...], preferred_element_type=F32)
        if with_issue:
            issue_part(1)
        hu = jnp.dot(x, wub[...], preferred_element_type=F32)
        if with_issue:
            issue_part(2)
        hh = hg * (1.0 / (1.0 + jnp.exp(-hg))) * hu
        y = jnp.dot(hh.astype(BF16), wdb[...], preferred_element_type=F32)
        if with_issue:
            issue_part(3)
        _store_row_packed(y_ref, y, row0=row0)

    @pl.when(b < nu)
    def _():
        slot = b % nbuf
        e = be_ref[b]
        first = (b == 0) | (e != be_ref[jnp.maximum(b - 1, 0)])

        @pl.when(first)
        def _():
            p = par_ref[0]
            for cp in weight_copies(e, p):
                cp.wait()
            nxt = lax.while_loop(
                lambda c: (c < N_EXPERTS) & (pc_ref[jnp.minimum(c, N_EXPERTS - 1)] == 0),
                lambda c: c + 1, e + 1)

            @pl.when(nxt < N_EXPERTS)
            def _():
                for cp in weight_copies(nxt, 1 - p):
                    cp.start(priority=1)
            wgb[...] = wgf[p].astype(BF16)
            wub[...] = wuf[p].astype(BF16)
            wdb[...] = wdf[p].astype(BF16)
            par_ref[0] = 1 - p

        pltpu.make_async_copy(hp_hbm.at[pl.ds(0, blk * pr)], gbuf.at[slot], gsem.at[slot]).wait()

        @pl.when(b + look < nu)
        def _():
            expert_mlp(slot, True)

        @pl.when(b + look >= nu)
        def _():
            expert_mlp(slot, False)

    @pl.when(b >= nu)
    def _():
        y_ref[pl.ds(row0 * pr, blk * pr), :] = jnp.zeros((blk * pr, LANES), U32)


def _moe(block_e, n_used, pcounts, slot_tok, hp, w_gate, w_up, w_down, n_blocks):
    ne, d, de = w_gate.shape
    blk = MOE_BLK
    pr = d // 2 // LANES
    bps = MOE_BLOCKS_PER_STEP
    assert n_blocks % bps == 0
    grid_spec = pltpu.PrefetchScalarGridSpec(
        num_scalar_prefetch=4,
        grid=(n_blocks // bps,),
        in_specs=[pl.BlockSpec(memory_space=pl.ANY)] * 4,
        out_specs=pl.BlockSpec((bps * blk * pr, LANES), lambda b, *_: (b, 0)),
        scratch_shapes=[pltpu.VMEM((GATHER_BUFS, blk * pr, LANES), U32),
                        pltpu.VMEM((2, d, de), F32),
                        pltpu.VMEM((2, d, de), F32),
                        pltpu.VMEM((2, de, d), F32),
                        pltpu.VMEM((d, de), BF16),
                        pltpu.VMEM((d, de), BF16),
                        pltpu.VMEM((de, d), BF16),
                        pltpu.SMEM((1,), I32),
                        pltpu.SemaphoreType.DMA((GATHER_BUFS,)),
                        pltpu.SemaphoreType.DMA((2,))],
    )
    return pl.pallas_call(
        _moe_kernel,
        out_shape=jax.ShapeDtypeStruct((n_blocks * blk * pr, LANES), U32),
        grid_spec=grid_spec,
        compiler_params=_cparams(("arbitrary",)),
        name="moe",
    )(block_e, n_used, pcounts, slot_tok, hp, w_gate, w_up, w_down)


def _combine_kernel(d0_ref, d1_ref, hp_ref, gate_ref, yb_hbm, g_ref, b_ref, out_ref, buf, sem, *, alpha):
    i = pl.program_id(0)
    n = pl.num_programs(0)
    tm = out_ref.shape[0]
    pr = buf.shape[2] // tm

    def issue(tile, slot):
        base = tile * tm

        def body(r, carry):
            dst = pl.ds(pl.multiple_of(r * pr, pr), pr)
            for k, dref in enumerate((d0_ref, d1_ref)):
                src = yb_hbm.at[pl.ds(pl.multiple_of(dref[base + r] * pr, pr), pr)]
                pltpu.make_async_copy(src, buf.at[slot, k, dst], sem.at[slot]).start()
            return carry
        lax.fori_loop(0, tm, body, 0, unroll=8)

    nbuf = buf.shape[0]
    look = nbuf - 1

    @pl.when(i == 0)
    def _():
        issue(0, 0)
        for j in range(1, look):
            @pl.when(j < n)
            def _():
                issue(j, j)

    slot = i % nbuf
    for k in range(2):
        pltpu.make_async_copy(yb_hbm.at[pl.ds(0, tm * pr)], buf.at[slot, k], sem.at[slot]).wait()

    nq = 4
    sub = tm // nq

    def issue_part(q):
        base = (i + look) * tm
        nslot = (i + look) % nbuf
        for r in range(q * sub, (q + 1) * sub):
            for k, dref in enumerate((d0_ref, d1_ref)):
                src = yb_hbm.at[pl.ds(pl.multiple_of(dref[base + r] * pr, pr), pr)]
                pltpu.make_async_copy(src, buf.at[nslot, k, pl.ds(r * pr, pr)], sem.at[nslot]).start(priority=k)

    def finish(with_issue):
        for q in range(nq):
            if with_issue:
                issue_part(q)
            rows = pl.ds(q * sub, sub)
            lo0, hi0 = _load_row_packed(buf.at[slot, 0], sub, pr, row0=q * sub, regroup=True)
            lo1, hi1 = _load_row_packed(buf.at[slot, 1], sub, pr, row0=q * sub, regroup=True)
            loh, hih = _load_row_packed(hp_ref, sub, pr, row0=q * sub, regroup=True)
            g0 = gate_ref[rows, 0:1]
            g1 = gate_ref[rows, 1:2]
            r = jnp.concatenate([alpha * hh + g0 * a + g1 * c
                                 for hh, a, c in zip(loh + hih, lo0 + hi0, lo1 + hi1)], axis=1)
            out_ref[rows, :] = _layer_norm(r, g_ref[...], b_ref[...])

    @pl.when(i + look < n)
    def _():
        finish(True)

    @pl.when(i + look >= n)
    def _():
        finish(False)


def _combine(dest0, dest1, hp, gate_t, yb, ln_g, ln_b, alpha):
    d = ln_g.shape[1]
    pr = d // 2 // LANES
    t = hp.shape[0] // pr
    tm = min(256, t)
    grid_spec = pltpu.PrefetchScalarGridSpec(
        num_scalar_prefetch=2,
        grid=(t // tm,),
        in_specs=[pl.BlockSpec((tm * pr, LANES), lambda i, d0, d1: (i, 0)),
                  pl.BlockSpec((tm, 2), lambda i, d0, d1: (i, 0)),
                  pl.BlockSpec(memory_space=pl.ANY),
                  pl.BlockSpec((1, d), lambda i, d0, d1: (0, 0)),
                  pl.BlockSpec((1, d), lambda i, d0, d1: (0, 0))],
        out_specs=pl.BlockSpec((tm, d), lambda i, d0, d1: (i, 0)),
        scratch_shapes=[pltpu.VMEM((GATHER_BUFS, 2, tm * (d // 2 // LANES), LANES), U32),
                        pltpu.SemaphoreType.DMA((GATHER_BUFS,))],
    )
    return pl.pallas_call(
        functools.partial(_combine_kernel, alpha=alpha),
        out_shape=jax.ShapeDtypeStruct((t, d), F32),
        grid_spec=grid_spec,
        compiler_params=_cparams(("arbitrary",)),
        name="combine",
    )(dest0, dest1, hp, gate_t, yb, ln_g, ln_b)


def _layer(h, w_in, pool_w, pool_scale, a_re, a_im, log_dt, b_re, b_im, c_re, c_im, d_skip,
           glu_w, glu_b, w_out, ln1_g, ln1_b, rg_w, rg_b, re_w, re_b, w_gate, w_up, w_down,
           ln2_g, ln2_b, alpha):
    bsz, seq, d = h.shape
    t = bsz * seq
    L = CHUNK
    dp = pool_w.shape[0] * pool_w.shape[1]
    ds = w_in.shape[1] - dp
    g = ds // SSM_GROUP
    n = t // L

    lag0, bb, w1t, w2t, al = _ssm_tables(a_re, a_im, log_dt, b_re, b_im, c_re, c_im, d_skip, L)
    half = ROUTER_LANES // 2
    zpad = lambda k: jnp.zeros((d, k), F32)
    rw = jnp.concatenate([rg_w.astype(F32), zpad(8 - N_EXPERT_GROUPS), re_w.astype(F32),
                          zpad(half - ROUTER_ROWS)], axis=1)
    rw_hi = rw.astype(BF16)
    rw_lo = (rw - rw_hi.astype(F32)).astype(BF16)
    rw2 = jnp.concatenate([rw_hi, rw_lo], axis=1)
    rb = jnp.concatenate([rg_b.astype(F32), jnp.zeros((8 - N_EXPERT_GROUPS,), F32), re_b.astype(F32),
                          jnp.zeros((ROUTER_LANES - ROUTER_ROWS,), F32)]).reshape(1, ROUTER_LANES)

    xt = h.reshape(t, d)
    pool_p, ut = _proj(xt, w_in.astype(BF16), dp, L)
    y_pool = _pool(pool_p, pool_w.astype(BF16), pool_scale.reshape(1, dp).astype(F32), seq)
    yt = _ssm(ut.reshape(g, L * SSM_GROUP, n), lag0, bb, w1t, w2t, al, seq // L, L)
    y_ssm = _glu(yt.reshape(g, L, SSM_GROUP, n), glu_w.astype(BF16), glu_b.reshape(1, ds).astype(F32))
    hp, lt = _outproj(y_pool, y_ssm, xt, w_out.astype(BF16), ln1_g.reshape(1, d), ln1_b.reshape(1, d),
                          rw2, rb, alpha)
    eid, rank, gate, cnt = _router(lt)

    blk = MOE_BLK
    m = 2 * t
    n_blocks = -(-m // blk) + N_EXPERTS
    counts = cnt[:, 0]
    pcounts = (counts + blk - 1) // blk * blk
    pends = jnp.cumsum(pcounts)
    pstarts = pends - pcounts
    e_ids = jnp.arange(N_EXPERTS, dtype=I32)
    dest = jnp.sum(jnp.where(eid[..., None] == e_ids, pstarts, 0), axis=-1) + rank
    slot_tok = _invert(dest.reshape(-1), n_blocks * blk)
    n_used = (pends[-1] // blk).astype(I32)
    bidx = jnp.minimum(jnp.arange(n_blocks, dtype=I32), n_used - 1)
    block_e = jnp.minimum(jnp.sum((pends[None, :] <= (bidx * blk)[:, None]).astype(I32), axis=1),
                          N_EXPERTS - 1)

    yb = _moe(block_e, n_used.reshape(1), pcounts, slot_tok, hp, w_gate, w_up, w_down, n_blocks)
    out = _combine(dest[0], dest[1], hp, gate.T, yb, ln2_g.reshape(1, d), ln2_b.reshape(1, d), alpha)
    return out.reshape(bsz, seq, d)


def kernel(x, w_in, pool_w, pool_scale, ssm_a_re, ssm_a_im, ssm_log_dt, ssm_b_re, ssm_b_im, ssm_c_re, ssm_c_im, ssm_d, glu_w, glu_b, w_out, ln1_g, ln1_b, router_g_w, router_g_b, router_e_w, router_e_b, w_gate, w_up, w_down, ln2_g, ln2_b):
    depth = w_in.shape[0]
    alpha = (2.0 * depth) ** 0.25
    h = x
    for l in range(depth):
        h = _layer(h, w_in[l], pool_w[l], pool_scale[l], ssm_a_re[l], ssm_a_im[l], ssm_log_dt[l],
                   ssm_b_re[l], ssm_b_im[l], ssm_c_re[l], ssm_c_im[l], ssm_d[l], glu_w[l], glu_b[l],
                   w_out[l], ln1_g[l], ln1_b[l], router_g_w[l], router_g_b[l], router_e_w[l],
                   router_e_b[l], w_gate[l], w_up[l], w_down[l], ln2_g[l], ln2_b[l], alpha)
    return h
```
